```python
import math
import jax, jax.numpy as jnp
from jax import lax
import numpy as np

D_MODEL = 1024
BATCH = 32
SEQ = 2048
DEPTH = 1

CHUNK = 64
CONV_DIM = D_MODEL // 2
CONV_WIDTH = 3
ATTN_HEADS = 8
HEAD_DIM = 64
ATTN_DIM = ATTN_HEADS * HEAD_DIM
Q_BLOCK = 128
IN_SPLITS = (CONV_DIM, CONV_DIM, CONV_DIM, ATTN_DIM, ATTN_DIM, ATTN_DIM, ATTN_HEADS, D_MODEL, D_MODEL)
IN_COLS = sum(IN_SPLITS)
N_EXPERTS = 32
TOP_K = 4
D_FF = D_MODEL
SWIGLU_ALPHA = 1.702
SWIGLU_LIMIT = 7.0
ROUTE_BLOCK = 256
RMS_EPS = 1e-5

kernel_name = "hybrid_gated_conv_fox_moe_block"


def rmsnorm(x, g):
    xf = x.astype(jnp.float32)
    y = xf * lax.rsqrt(jnp.mean(xf * xf, axis=-1, keepdims=True) + RMS_EPS)
    return y.astype(x.dtype) * g


def split_cols(u):
    idx = np.cumsum(IN_SPLITS)[:-1].tolist()
    return jnp.split(u, idx, axis=-1)


def short_gated_conv(b_gate, c_gate, xc, conv_w):
    u = c_gate * xc
    S = u.shape[1]
    up = jnp.pad(u, ((0, 0), (CONV_WIDTH - 1, 0), (0, 0)))
    conv = sum(conv_w[j] * up[:, j:j + S] for j in range(CONV_WIDTH))
    return b_gate * conv


def forgetting_attention(q, k, v, logf):
    Bsz, S, H, Dh = q.shape
    nb = S // Q_BLOCK
    scale = 1.0 / math.sqrt(Dh)
    c = jnp.cumsum(logf, axis=1).transpose(0, 2, 1)
    kh = k.transpose(0, 2, 1, 3)
    vh = v.transpose(0, 2, 1, 3)
    qb = q.reshape(Bsz, nb, Q_BLOCK, H, Dh).transpose(1, 0, 3, 2, 4)
    cb = c.reshape(Bsz, H, nb, Q_BLOCK).transpose(2, 0, 1, 3)
    key_pos = jnp.arange(S)

    def block(args):
        qi, ci, i = args
        q_pos = i * Q_BLOCK + jnp.arange(Q_BLOCK)
        s = jnp.einsum('bhqd,bhkd->bhqk', qi, kh).astype(jnp.float32) * scale
        s = s + (ci[..., :, None] - c[:, :, None, :])
        s = jnp.where(key_pos[None, :] <= q_pos[:, None], s, -jnp.inf)
        p = jax.nn.softmax(s, axis=-1).astype(vh.dtype)
        return jnp.einsum('bhqk,bhkd->bhqd', p, vh)

    o = lax.map(block, (qb, cb, jnp.arange(nb)))
    return o.transpose(1, 0, 3, 2, 4).reshape(Bsz, S, H * Dh)


def moe_ffn(h, w_router, b_router, w_gate_up, b_gate_up, w_down, b_down):
    N, D = h.shape
    logits = (h @ w_router + b_router).astype(jnp.float32)
    top_val, top_idx = lax.top_k(logits, TOP_K)
    gates = jax.nn.softmax(top_val, axis=-1)
    M = N * TOP_K
    flat_e = top_idx.reshape(-1)
    flat_tok = (jnp.arange(M) // TOP_K).astype(jnp.int32)
    flat_w = gates.reshape(-1)
    order = jnp.argsort(flat_e, stable=True)
    sorted_e = flat_e[order]
    counts = jnp.bincount(flat_e, length=N_EXPERTS)
    padded = ((counts + ROUTE_BLOCK - 1) // ROUTE_BLOCK) * ROUTE_BLOCK
    start = jnp.cumsum(counts) - counts
    pend = jnp.cumsum(padded)
    pstart = pend - padded
    dest = pstart[sorted_e] + (jnp.arange(M) - start[sorted_e])
    n_blocks = -(-M // ROUTE_BLOCK) + N_EXPERTS
    P = n_blocks * ROUTE_BLOCK
    slot_tok = jnp.full((P,), N, jnp.int32).at[dest].set(flat_tok[order])
    slot_w = jnp.zeros((P,), h.dtype).at[dest].set(flat_w[order].astype(h.dtype))
    block_e = jnp.minimum(jnp.searchsorted(pend, jnp.arange(n_blocks) * ROUTE_BLOCK, side='right'), N_EXPERTS - 1)
    h_pad = jnp.concatenate([h, jnp.zeros((1, D), h.dtype)], axis=0)
    xs = h_pad[slot_tok].reshape(n_blocks, ROUTE_BLOCK, D)

    def expert_block(args):
        xb, e = args
        gu = xb @ w_gate_up[e] + b_gate_up[e]
        x_glu, x_lin = jnp.split(gu, 2, axis=-1)
        x_glu = jnp.minimum(x_glu, SWIGLU_LIMIT)
        x_lin = jnp.clip(x_lin, -SWIGLU_LIMIT, SWIGLU_LIMIT)
        act = (x_glu * jax.nn.sigmoid(SWIGLU_ALPHA * x_glu)) * (x_lin + 1.0)
        return act @ w_down[e] + b_down[e]

    ys = lax.map(expert_block, (xs, block_e)).reshape(P, D)
    out = jnp.zeros((N + 1, D), h.dtype).at[slot_tok].add(ys * slot_w[:, None])
    return out[:N]


def setup_inputs(seed: int = 0) -> dict:
    key = jax.random.key(seed)
    ks = jax.random.split(key, 16)
    f32 = jnp.float32
    nrm = lambda k, shp, s: jax.random.normal(k, shp, f32) * s
    return {
        "x": nrm(ks[0], (BATCH, SEQ, D_MODEL), 1.0),
        "g_mix": 1.0 + nrm(ks[1], (D_MODEL,), 0.02),
        "w_in": nrm(ks[2], (D_MODEL, IN_COLS), D_MODEL ** -0.5),
        "conv_w": nrm(ks[3], (CONV_WIDTH, CONV_DIM), CONV_WIDTH ** -0.5),
        "b_f": 3.0 + nrm(ks[4], (ATTN_HEADS,), 0.5),
        "w_conv_o": nrm(ks[5], (CONV_DIM, D_MODEL), CONV_DIM ** -0.5),
        "w_attn_o": nrm(ks[6], (ATTN_DIM, D_MODEL), ATTN_DIM ** -0.5),
        "w_out": nrm(ks[7], (D_MODEL, D_MODEL), D_MODEL ** -0.5),
        "g_ffn": 1.0 + nrm(ks[8], (D_MODEL,), 0.02),
        "w_router": nrm(ks[9], (D_MODEL, N_EXPERTS), D_MODEL ** -0.5),
        "b_router": nrm(ks[10], (N_EXPERTS,), 0.01),
        "w_gate_up": nrm(ks[11], (N_EXPERTS, D_MODEL, 2 * D_FF), D_MODEL ** -0.5),
        "b_gate_up": nrm(ks[12], (N_EXPERTS, 2 * D_FF), 0.01),
        "w_down": nrm(ks[13], (N_EXPERTS, D_FF, D_MODEL), D_FF ** -0.5),
        "b_down": nrm(ks[14], (N_EXPERTS, D_MODEL), 0.01),
        "g_final": 1.0 + nrm(ks[15], (D_MODEL,), 0.02),
    }


def reference(x, g_mix, w_in, conv_w, b_f, w_conv_o, w_attn_o, w_out, g_ffn, w_router, b_router, w_gate_up, b_gate_up, w_down, b_down, g_final):
    Bsz, S, D = x.shape
    for _ in range(DEPTH):
        h = rmsnorm(x, g_mix)
        u = h @ w_in
        cb, cc, cx, q, k, v, fl, gc, ga = split_cols(u)
        y_conv = short_gated_conv(cb, cc, cx, conv_w) @ w_conv_o
        logf = jax.nn.log_sigmoid((fl + b_f).astype(jnp.float32))
        hs = (Bsz, S, ATTN_HEADS, HEAD_DIM)
        o = forgetting_attention(q.reshape(hs), k.reshape(hs), v.reshape(hs), logf)
        y_attn = o @ w_attn_o
        mixed = jax.nn.sigmoid(gc) * y_conv + jax.nn.sigmoid(ga) * y_attn
        x = x + mixed @ w_out
        h2 = rmsnorm(x, g_ffn).reshape(Bsz * S, D)
        x = x + moe_ffn(h2, w_router, b_router, w_gate_up, b_gate_up, w_down, b_down).reshape(Bsz, S, D)
    return rmsnorm(x, g_final)
```

```python
import functools

import jax
import jax.numpy as jnp
from jax import lax
from jax.experimental import pallas as pl
from jax.experimental.pallas import tpu as pltpu

TOP_K = 4
TOP_K_SHIFT = 2
RMS_EPS = 1e-5
SWIGLU_ALPHA = 1.702
SWIGLU_LIMIT = 7.0

LANES = 128
BF16_SUBLANES = 16
VMEM_LIMIT_BYTES = 56 * 1024 * 1024

F32 = jnp.float32
BF16 = jnp.bfloat16
U32 = jnp.uint32
I32 = jnp.int32
HI_MASK = 0xFFFF0000


def _params(sem):
    return pltpu.CompilerParams(dimension_semantics=sem, vmem_limit_bytes=VMEM_LIMIT_BYTES)


def _pack_bf16_pair(lo_f32, hi_f32):
    lo = lax.bitcast_convert_type(lo_f32, U32)
    hi = lax.bitcast_convert_type(hi_f32, U32)
    return (lo >> 16) | (hi & U32(HI_MASK))


def _unpack_bf16_pair(w):
    lo = lax.bitcast_convert_type(w << 16, F32)
    hi = lax.bitcast_convert_type(w & U32(HI_MASK), F32)
    return lo, hi


def _inproj_kernel(x_ref, g_ref, wa_ref, wqkv_ref, wg_ref, wf_ref, bf_ref,
                   cb_ref, ucx_ref, q_ref, k_ref, v_ref, sgc_ref, sga_ref, crow_ref,
                   carry_ref, *, tiles_per_batch, conv_dim, attn_dim, d_model, heads, tk):
    tm = x_ref.shape[0]
    t = lax.rem(pl.program_id(0), tiles_per_batch)
    x = x_ref[...]
    ms = jnp.mean(x * x, axis=-1, keepdims=True)
    h = ((x * lax.rsqrt(ms + RMS_EPS)) * g_ref[...]).astype(BF16)

    def mm(w_ref, c0, n):
        return jnp.dot(h, w_ref[:, c0:c0 + n], preferred_element_type=F32)

    cb_ref[...] = mm(wa_ref, 0, conv_dim).astype(BF16)
    ucx_ref[...] = (mm(wa_ref, conv_dim, conv_dim) * mm(wa_ref, 2 * conv_dim, conv_dim)).astype(BF16)
    q_ref[...] = mm(wqkv_ref, 0, attn_dim).astype(BF16)
    k_ref[...] = mm(wqkv_ref, attn_dim, attn_dim).astype(BF16)
    v_ref[...] = mm(wqkv_ref, 2 * attn_dim, attn_dim).astype(BF16)
    half = d_model // 2
    for c in range(2):
        sgc_ref[:, c * half:(c + 1) * half] = jax.nn.sigmoid(mm(wg_ref, c * half, half)).astype(BF16)
        sga_ref[:, c * half:(c + 1) * half] = jax.nn.sigmoid(mm(wg_ref, d_model + c * half, half)).astype(BF16)

    z = mm(wf_ref, 0, LANES) + bf_ref[...]
    lane = lax.broadcasted_iota(I32, z.shape, 1)
    logf = jnp.minimum(z, 0.0) - jnp.log1p(jnp.exp(-jnp.abs(z)))
    logf = jnp.where(lane < heads, logf, 0.0)
    p1 = logf.astype(BF16).astype(F32)
    r1 = logf - p1
    p2 = r1.astype(BF16).astype(F32)
    p3 = (r1 - p2).astype(BF16).astype(F32)
    packed = (p1 + pltpu.roll(p2, heads, 1) + pltpu.roll(p3, 2 * heads, 1)).astype(BF16)
    row = lax.broadcasted_iota(I32, (tm, tm), 0)
    col = lax.broadcasted_iota(I32, (tm, tm), 1)
    tri = jnp.where(col <= row, 1.0, 0.0).astype(BF16)
    r = jnp.dot(tri, packed, preferred_element_type=F32)
    local = r + pltpu.roll(r, LANES - heads, 1) + pltpu.roll(r, LANES - 2 * heads, 1)

    @pl.when(t == 0)
    def _():
        carry_ref[...] = jnp.zeros_like(carry_ref)

    c = jnp.where(lane < heads, local + carry_ref[0:1, :], 0.0)
    carry_ref[0:1, :] = c[tm - 1:tm, :]
    ct = c.T
    per_tile = tm // tk
    nk = tiles_per_batch * per_tile
    for hh in range(heads):
        for u in range(per_tile):
            crow_ref[0, pl.ds(hh * nk + t * per_tile + u, 1), :] = ct[hh:hh + 1, u * tk:(u + 1) * tk]


def _inproj(x2d, g_mix, wa, wqkv, wg, wf, bf_pad, *, batch, seq, heads, tm, tk):
    n, d_model = x2d.shape
    conv_dim = wa.shape[1] // 3
    attn_dim = wqkv.shape[1] // 3
    tiles_per_batch = seq // tm
    nk = seq // tk
    grid = (n // tm,)
    full = lambda shape: pl.BlockSpec(shape, lambda i: (0,) * len(shape))
    rows = lambda w: pl.BlockSpec((tm, w), lambda i: (i, 0))
    kern = functools.partial(_inproj_kernel, tiles_per_batch=tiles_per_batch, conv_dim=conv_dim,
                             attn_dim=attn_dim, d_model=d_model, heads=heads, tk=tk)
    out_shape = (
        jax.ShapeDtypeStruct((n, conv_dim), BF16),
        jax.ShapeDtypeStruct((n, conv_dim), BF16),
        jax.ShapeDtypeStruct((n, attn_dim), BF16),
        jax.ShapeDtypeStruct((n, attn_dim), BF16),
        jax.ShapeDtypeStruct((n, attn_dim), BF16),
        jax.ShapeDtypeStruct((n, d_model), BF16),
        jax.ShapeDtypeStruct((n, d_model), BF16),
        jax.ShapeDtypeStruct((batch, heads * nk, tk), F32),
    )
    out_specs = (
        rows(conv_dim), rows(conv_dim), rows(attn_dim), rows(attn_dim), rows(attn_dim),
        rows(d_model), rows(d_model),
        pl.BlockSpec((1, heads * nk, tk), lambda i: (i // tiles_per_batch, 0, 0)),
    )
    return pl.pallas_call(
        kern,
        grid=grid,
        in_specs=[rows(d_model), full((1, d_model)), full(wa.shape), full(wqkv.shape), full(wg.shape),
                  full(wf.shape), full((1, LANES))],
        out_specs=out_specs,
        out_shape=out_shape,
        scratch_shapes=[pltpu.VMEM((8, LANES), F32)],
        compiler_params=_params(("arbitrary",)),
        name="inproj",
    )(x2d, g_mix, wa, wqkv, wg, wf, bf_pad)


def _attn_kernel(q_ref, k_ref, v_ref, crow_ref, o_ref, *, tq, tk, nk, head_dim):
    hp = pl.program_id(1)
    qi = pl.program_id(2)
    q = q_ref[...]
    lane = lax.broadcasted_iota(I32, (tq, LANES), 1)
    row = lax.broadcasted_iota(I32, (tq, tk), 0)
    col = lax.broadcasted_iota(I32, (tq, tk), 1)
    outs = []
    for hh in range(2):
        head = hp * 2 + hh
        keep = (lane < head_dim) if hh == 0 else (lane >= head_dim)
        qm = jnp.where(keep, q, jnp.zeros_like(q))

        def block(j, carry, masked, qm=qm, head=head):
            m, l, acc = carry
            off = pl.multiple_of(j * tk, tk)
            kj = k_ref[pl.ds(off, tk), :]
            vj = v_ref[pl.ds(off, tk), :]
            s = lax.dot_general(qm, kj, (((1,), (1,)), ((), ())), preferred_element_type=F32)
            s = s - crow_ref[0, pl.ds(head * nk + j, 1), :]
            if masked:
                s = jnp.where(col <= row, s, -jnp.inf)
            m_new = jnp.maximum(m, jnp.max(s, axis=1, keepdims=True))
            alpha = jnp.exp(m - m_new)
            p = jnp.exp(s - m_new)
            l_new = alpha * l + jnp.sum(p, axis=1, keepdims=True)
            acc_new = alpha * acc + jnp.dot(p.astype(BF16), vj, preferred_element_type=F32)
            return m_new, l_new, acc_new

        init = (jnp.full((tq, 1), -jnp.inf, F32), jnp.zeros((tq, 1), F32), jnp.zeros((tq, LANES), F32))
        carry = lax.fori_loop(0, qi, lambda j, c: block(j, c, False), init)
        _, l, acc = block(qi, carry, True)
        outs.append(acc / l)
    o_ref[...] = jnp.where(lane < head_dim, outs[0], outs[1]).astype(BF16)


def _attention(q, k, v, crow, *, batch, seq, heads, tq):
    n, attn_dim = q.shape
    head_dim = attn_dim // heads
    assert 2 * head_dim == LANES, "attention kernel packs two heads per 128-lane group"
    tk = crow.shape[2]
    assert tq == tk
    nq = seq // tq
    nk = seq // tk
    kern = functools.partial(_attn_kernel, tq=tq, tk=tk, nk=nk, head_dim=head_dim)
    return pl.pallas_call(
        kern,
        grid=(batch, heads // 2, nq),
        in_specs=[
            pl.BlockSpec((tq, LANES), lambda b, hp, qi: (b * nq + qi, hp)),
            pl.BlockSpec((seq, LANES), lambda b, hp, qi: (b, hp)),
            pl.BlockSpec((seq, LANES), lambda b, hp, qi: (b, hp)),
            pl.BlockSpec((1, heads * nk, tk), lambda b, hp, qi: (b, 0, 0)),
        ],
        out_specs=pl.BlockSpec((tq, LANES), lambda b, hp, qi: (b * nq + qi, hp)),
        out_shape=jax.ShapeDtypeStruct((n, attn_dim), BF16),
        compiler_params=_params(("arbitrary", "arbitrary", "arbitrary")),
        name="attn",
    )(q, k, v, crow)


def _mix_kernel(x_ref, cb_ref, u_ref, uh_ref, o_ref, sgc_ref, sga_ref, cw_ref, wco_ref, wao_ref, wout_ref,
                gffn_ref, wr_ref, br_ref,
                x1_ref, h2p_ref, topi_ref, gate_ref, rank_ref, cnt_ref, carry_ref,
                *, tiles_per_batch, n_experts):
    tm, d_model = x_ref.shape
    i = pl.program_id(0)
    first = lax.rem(i, tiles_per_batch) == 0

    u = u_ref[...].astype(F32)
    halo = jnp.where(first, 0.0, uh_ref[...].astype(F32))
    ext = jnp.concatenate([halo, u], axis=0)
    hs = halo.shape[0]
    u1 = pltpu.roll(ext, 1, 0)[hs:]
    u2 = pltpu.roll(ext, 2, 0)[hs:]
    cw = cw_ref[...]
    conv = cw[0:1, :] * u2 + cw[1:2, :] * u1 + cw[2:3, :] * u
    yc = (cb_ref[...].astype(F32) * conv).astype(BF16)
    y_conv = jnp.dot(yc, wco_ref[...], preferred_element_type=F32)
    y_attn = jnp.dot(o_ref[...], wao_ref[...], preferred_element_type=F32)
    mixed = (sgc_ref[...].astype(F32) * y_conv + sga_ref[...].astype(F32) * y_attn).astype(BF16)
    x1 = x_ref[...] + jnp.dot(mixed, wout_ref[...], preferred_element_type=F32)
    x1_ref[...] = x1

    ms = jnp.mean(x1 * x1, axis=-1, keepdims=True)
    h2 = (x1 * lax.rsqrt(ms + RMS_EPS)) * gffn_ref[...]
    hb = h2.astype(BF16)
    hbf = hb.astype(F32)
    half = d_model // 2
    h2p_ref[...] = _pack_bf16_pair(hbf[:, :half], hbf[:, half:])

    hlo = (h2 - hbf).astype(BF16)
    r = jnp.dot(hb, wr_ref[...], preferred_element_type=F32) + jnp.dot(hlo, wr_ref[...], preferred_element_type=F32)
    lane = lax.broadcasted_iota(I32, (tm, LANES), 1)
    logits = r + pltpu.roll(r, LANES - n_experts, 1) + br_ref[...]
    lg = jnp.where(lane < n_experts, logits, -jnp.inf)

    vals, idxs = [], []
    for _ in range(TOP_K):
        m = jnp.max(lg, axis=1, keepdims=True)
        idx = jnp.min(jnp.where(lg == m, lane, LANES), axis=1, keepdims=True)
        vals.append(m)
        idxs.append(idx)
        lg = jnp.where(lane == idx, -jnp.inf, lg)
    es = [jnp.exp(vk - vals[0]) for vk in vals]
    denom = es[0] + es[1] + es[2] + es[3]

    chosen = jnp.zeros((tm, LANES), F32)
    topi = jnp.zeros((tm, LANES), I32)
    gates = jnp.zeros((tm, LANES), F32)
    for kk in range(TOP_K):
        chosen = jnp.where(lane == idxs[kk], 1.0, chosen)
        topi = jnp.where(lane == kk, idxs[kk], topi)
        gates = jnp.where(lane == kk, es[kk] / denom, gates)
    topi_ref[...] = topi
    gate_ref[...] = gates

    @pl.when(i == 0)
    def _():
        carry_ref[...] = jnp.zeros_like(carry_ref)

    row = lax.broadcasted_iota(I32, (tm, tm), 0)
    col = lax.broadcasted_iota(I32, (tm, tm), 1)
    tri = jnp.where(col < row, 1.0, 0.0).astype(BF16)
    before = jnp.dot(tri, chosen.astype(BF16), preferred_element_type=F32) + carry_ref[0:1, :]
    rank = jnp.zeros((tm, LANES), F32)
    for kk in range(TOP_K):
        rk = jnp.sum(jnp.where(lane == idxs[kk], before, 0.0), axis=1, keepdims=True)
        rank = jnp.where(lane == kk, rk, rank)
    rank_ref[...] = rank.astype(I32)
    total = carry_ref[0:1, :] + jnp.sum(chosen, axis=0, keepdims=True)
    carry_ref[0:1, :] = total
    cnt_ref[...] = jnp.broadcast_to(total.reshape(1, 1, LANES), cnt_ref.shape)


def _mix(x2d, cb, ucx, o, sgc, sga, conv_w, wco, wao, wout, g_ffn, wr, br, *, seq, n_experts, tm):
    n, d_model = x2d.shape
    conv_dim = cb.shape[1]
    attn_dim = o.shape[1]
    tiles_per_batch = seq // tm
    nt = n // tm
    hb = tm // BF16_SUBLANES
    full = lambda shape: pl.BlockSpec(shape, lambda i: (0,) * len(shape))
    rows = lambda w: pl.BlockSpec((tm, w), lambda i: (i, 0))
    kern = functools.partial(_mix_kernel, tiles_per_batch=tiles_per_batch, n_experts=n_experts)
    out_shape = (
        jax.ShapeDtypeStruct((n, d_model), F32),
        jax.ShapeDtypeStruct((n, d_model // 2), U32),
        jax.ShapeDtypeStruct((n, LANES), I32),
        jax.ShapeDtypeStruct((n, LANES), F32),
        jax.ShapeDtypeStruct((n, LANES), I32),
        jax.ShapeDtypeStruct((nt, 8, LANES), F32),
    )
    out_specs = (rows(d_model), rows(d_model // 2), rows(LANES), rows(LANES), rows(LANES),
                 pl.BlockSpec((1, 8, LANES), lambda i: (i, 0, 0)))
    return pl.pallas_call(
        kern,
        grid=(nt,),
        in_specs=[rows(d_model), rows(conv_dim), rows(conv_dim),
                  pl.BlockSpec((BF16_SUBLANES, conv_dim), lambda i: (jnp.maximum(i * hb - 1, 0), 0)),
                  rows(attn_dim), rows(d_model), rows(d_model),
                  full(conv_w.shape), full(wco.shape), full(wao.shape), full(wout.shape),
                  full((1, d_model)), full(wr.shape), full((1, LANES))],
        out_specs=out_specs,
        out_shape=out_shape,
        scratch_shapes=[pltpu.VMEM((8, LANES), F32)],
        compiler_params=_params(("arbitrary",)),
        name="mix",
    )(x2d, cb, ucx, ucx, o, sgc, sga, conv_w, wco, wao, wout, g_ffn, wr, br)


def _rowcopy_kernel(fill_start_ref, fill_n_ref, idx_ref, src_ref, dst_ref, sem, *, scatter, ch, n_fill):
    g = pl.program_id(0)
    last = pl.num_programs(0) - 1
    slot = lax.rem(g, 2)

    width = src_ref.shape[1]

    def row_copy(s_row, d_row, d_col, s):
        dst = dst_ref.at[pl.ds(d_row, 1), pl.ds(pl.multiple_of(d_col * width, width), width)]
        return pltpu.make_async_copy(src_ref.at[pl.ds(s_row, 1)], dst, s)

    def issue(i, _):
        a = g * ch + i
        d = idx_ref[0, 0, i]
        tok = lax.shift_right_logical(a, TOP_K_SHIFT)
        if scatter:
            row_copy(tok, d, 0, sem.at[slot]).start()
        else:
            row_copy(d, tok, a & (TOP_K - 1), sem.at[slot]).start()
        return 0

    lax.fori_loop(0, ch, issue, 0, unroll=8)

    def drain(s):
        def body(i, _):
            row_copy(0, 0, 0, s).wait()
            return 0
        lax.fori_loop(0, ch, body, 0, unroll=8)

    @pl.when(g > 0)
    def _():
        drain(sem.at[1 - slot])

    @pl.when(g == last)
    def _():
        drain(sem.at[slot])
        for e in range(n_fill):
            start = fill_start_ref[e]
            cnt = fill_n_ref[e]

            def fill(p, _, start=start):
                row_copy(0, start + p, 0, sem.at[2]).start()
                return 0

            def fill_wait(p, _):
                row_copy(0, 0, 0, sem.at[2]).wait()
                return 0

            lax.fori_loop(0, cnt, fill, 0)
            lax.fori_loop(0, cnt, fill_wait, 0)


def _rowcopy(idx, src, n_out, fill_start, fill_n, *, scatter, ch):
    m = idx.shape[0]
    nchunks = m // ch
    width = src.shape[1] if scatter else TOP_K * src.shape[1]
    kern = functools.partial(_rowcopy_kernel, scatter=scatter, ch=ch, n_fill=fill_start.shape[0] if scatter else 0)
    grid_spec = pltpu.PrefetchScalarGridSpec(
        num_scalar_prefetch=2,
        grid=(nchunks,),
        in_specs=[pl.BlockSpec((1, 1, ch), lambda g, *_: (g, 0, 0), memory_space=pltpu.SMEM),
                  pl.BlockSpec(memory_space=pl.ANY)],
        out_specs=pl.BlockSpec(memory_space=pl.ANY),
        scratch_shapes=[pltpu.SemaphoreType.DMA((3,))],
    )
    return pl.pallas_call(
        kern,
        grid_spec=grid_spec,
        out_shape=jax.ShapeDtypeStruct((n_out, width), src.dtype),
        compiler_params=pltpu.CompilerParams(dimension_semantics=("arbitrary",), has_side_effects=True),
        name="scatter_rows" if scatter else "gather_rows",
    )(fill_start, fill_n, idx.reshape(nchunks, 1, ch), src)


def _expert_kernel(be_ref, nu_ref, xs_ref, wgu_ref, bgu_ref, wd_ref, bd_ref, ys_ref, *, d_ff):
    i = pl.program_id(0)

    @pl.when(i < nu_ref[0])
    def _():
        lo, hi = _unpack_bf16_pair(xs_ref[...])
        xb = jnp.concatenate([lo.astype(BF16), hi.astype(BF16)], axis=1)
        gu = jnp.dot(xb, wgu_ref[0], preferred_element_type=F32) + bgu_ref[0]
        glu = jnp.minimum(gu[:, :d_ff], SWIGLU_LIMIT)
        lin = jnp.clip(gu[:, d_ff:], -SWIGLU_LIMIT, SWIGLU_LIMIT)
        act = (glu * jax.nn.sigmoid(SWIGLU_ALPHA * glu)) * (lin + 1.0)
        y = jnp.dot(act.astype(BF16), wd_ref[0], preferred_element_type=F32) + bd_ref[0]
        yb = y.astype(BF16).astype(F32)
        half = y.shape[1] // 2
        ys_ref[...] = _pack_bf16_pair(yb[:, :half], yb[:, half:])


def _experts(xs, block_e, n_used, wgu, bgu, wd, bd, *, tm):
    p, half = xs.shape
    n_blocks = p // tm
    e, d_model, two_ff = wgu.shape
    d_ff = two_ff // 2
    kern = functools.partial(_expert_kernel, d_ff=d_ff)
    blk = lambda i, be, nu: (jnp.minimum(i, nu[0] - 1), 0)
    grid_spec = pltpu.PrefetchScalarGridSpec(
        num_scalar_prefetch=2,
        grid=(n_blocks,),
        in_specs=[
            pl.BlockSpec((tm, half), blk),
            pl.BlockSpec((1, d_model, two_ff), lambda i, be, nu: (be[i], 0, 0)),
            pl.BlockSpec((1, 1, two_ff), lambda i, be, nu: (be[i], 0, 0)),
            pl.BlockSpec((1, d_ff, d_model), lambda i, be, nu: (be[i], 0, 0)),
            pl.BlockSpec((1, 1, d_model), lambda i, be, nu: (be[i], 0, 0)),
        ],
        out_specs=pl.BlockSpec((tm, half), blk),
    )
    return pl.pallas_call(
        kern,
        grid_spec=grid_spec,
        out_shape=jax.ShapeDtypeStruct((p, half), U32),
        compiler_params=_params(("arbitrary",)),
        name="experts",
    )(block_e, n_used, xs, wgu, bgu.reshape(e, 1, two_ff), wd, bd.reshape(e, 1, d_model))


def _final_kernel(yk_ref, gate_ref, x1_ref, g_ref, out_ref):
    tm, d_model = x1_ref.shape
    half = d_model // 2
    gates = gate_ref[...]
    acc_lo = jnp.zeros((tm, half), F32)
    acc_hi = jnp.zeros((tm, half), F32)
    for kk in range(TOP_K):
        lo, hi = _unpack_bf16_pair(yk_ref[:, kk * half:(kk + 1) * half])
        gk = gates[:, kk:kk + 1]
        acc_lo = acc_lo + gk * lo
        acc_hi = acc_hi + gk * hi
    x2 = x1_ref[...] + jnp.concatenate([acc_lo, acc_hi], axis=1)
    ms = jnp.mean(x2 * x2, axis=-1, keepdims=True)
    out_ref[...] = (x2 * lax.rsqrt(ms + RMS_EPS)) * g_ref[...]


def _final(yk, gates, x1, g_final, *, tm):
    n, d_model = x1.shape
    rows = lambda w: pl.BlockSpec((tm, w), lambda i: (i, 0))
    return pl.pallas_call(
        _final_kernel,
        grid=(n // tm,),
        in_specs=[rows(yk.shape[1]), rows(LANES), rows(d_model), pl.BlockSpec((1, d_model), lambda i: (0, 0))],
        out_specs=rows(d_model),
        out_shape=jax.ShapeDtypeStruct((n, d_model), F32),
        compiler_params=_params(("arbitrary",)),
        name="final",
    )(yk, gates, x1, g_final)


def _tiles(seq):
    tm = min(512, seq)
    return dict(tm_proj=tm, t_attn=min(256, seq), tm_mix=tm, tm_expert=256, tm_final=tm, copy_chunk=2048)


def _forward(x, g_mix, w_in, conv_w, b_f, w_conv_o, w_attn_o, w_out, g_ffn, w_router, b_router,
             w_gate_up, b_gate_up, w_down, b_down, g_final, tiles):
    batch, seq, d_model = x.shape
    n = batch * seq
    conv_dim = conv_w.shape[1]
    attn_dim = w_attn_o.shape[0]
    heads = b_f.shape[0]
    head_dim = attn_dim // heads
    n_experts = w_router.shape[1]
    x2d = x.reshape(n, d_model)

    c0 = 3 * conv_dim
    a0 = c0 + 3 * attn_dim
    wa = w_in[:, :c0].astype(BF16)
    scale = 1.0 / (head_dim ** 0.5)
    wqkv = jnp.concatenate([w_in[:, c0:c0 + attn_dim] * scale, w_in[:, c0 + attn_dim:a0]], axis=1).astype(BF16)
    wf = jnp.pad(w_in[:, a0:a0 + heads], ((0, 0), (0, LANES - heads))).astype(BF16)
    wg = w_in[:, a0 + heads:].astype(BF16)
    bf_pad = jnp.pad(b_f, (0, LANES - heads)).reshape(1, LANES)
    wr_hi = w_router.astype(BF16)
    wr_lo = (w_router - wr_hi.astype(F32)).astype(BF16)
    wr = jnp.pad(jnp.concatenate([wr_hi, wr_lo], axis=1), ((0, 0), (0, LANES - 2 * n_experts)))
    br = jnp.pad(b_router, (0, LANES - n_experts)).reshape(1, LANES)

    cb, ucx, q, k, v, sgc, sga, crow = _inproj(
        x2d, g_mix.reshape(1, d_model), wa, wqkv, wg, wf, bf_pad,
        batch=batch, seq=seq, heads=heads, tm=tiles["tm_proj"], tk=tiles["t_attn"])
    o = _attention(q, k, v, crow, batch=batch, seq=seq, heads=heads, tq=tiles["t_attn"])
    x1, h2p, topi, gates, rank, cnt = _mix(
        x2d, cb, ucx, o, sgc, sga, conv_w, w_conv_o.astype(BF16), w_attn_o.astype(BF16), w_out.astype(BF16),
        g_ffn.reshape(1, d_model), wr, br, seq=seq, n_experts=n_experts, tm=tiles["tm_mix"])

    tme = tiles["tm_expert"]
    m = n * TOP_K
    n_blocks = m // tme + n_experts
    counts = cnt[-1, 0, :n_experts].astype(I32)
    padded = ((counts + tme - 1) // tme) * tme
    pend = jnp.cumsum(padded)
    pstart = pend - padded
    n_used = (pend[-1] // tme).astype(I32).reshape(1)
    blk_start = jnp.arange(n_blocks, dtype=I32) * tme
    block_e = jnp.minimum(jnp.searchsorted(pend, blk_start, side="right"), n_experts - 1).astype(I32)
    last_e = block_e[jnp.maximum(n_used[0] - 1, 0)]
    block_e = jnp.where(jnp.arange(n_blocks) < n_used[0], block_e, last_e)
    e_sel = topi[:, :TOP_K]
    dest = (jnp.take(pstart, e_sel) + rank[:, :TOP_K]).reshape(m).astype(I32)

    xs = _rowcopy(dest, h2p, n_blocks * tme, (pstart + counts).astype(I32), (padded - counts).astype(I32),
                  scatter=True, ch=tiles["copy_chunk"])
    ys = _experts(xs, block_e, n_used, w_gate_up.astype(BF16), b_gate_up, w_down.astype(BF16), b_down, tm=tme)
    yk = _rowcopy(dest, ys, n, jnp.zeros((1,), I32), jnp.zeros((1,), I32), scatter=False, ch=tiles["copy_chunk"])
    out = _final(yk, gates, x1, g_final.reshape(1, d_model), tm=tiles["tm_final"])
    return out.reshape(batch, seq, d_model)


def kernel(x, g_mix, w_in, conv_w, b_f, w_conv_o, w_attn_o, w_out, g_ffn, w_router, b_router, w_gate_up,
           b_gate_up, w_down, b_down, g_final):
    return _forward(x, g_mix, w_in, conv_w, b_f, w_conv_o, w_attn_o, w_out, g_ffn, w_router, b_router,
                    w_gate_up, b_gate_up, w_down, b_down, g_final, _tiles(x.shape[1]))
```

```python
import functools

import jax
import jax.numpy as jnp
from jax import lax
from jax.experimental import pallas as pl
from jax.experimental.pallas import tpu as pltpu
from jax.experimental.pallas import tpu_sc as plsc

TOP_K = 4
RMS_EPS = 1e-5
SWIGLU_ALPHA = 1.702
SWIGLU_LIMIT = 7.0

LANES = 128
BF16_SUBLANES = 16
VMEM_LIMIT_BYTES = 56 * 1024 * 1024

F32 = jnp.float32
BF16 = jnp.bfloat16
U32 = jnp.uint32
I32 = jnp.int32
HI_MASK = 0xFFFF0000


def _params(sem):
    return pltpu.CompilerParams(dimension_semantics=sem, vmem_limit_bytes=VMEM_LIMIT_BYTES)


def _pack_bf16_pair(lo_f32, hi_f32):
    lo = lax.bitcast_convert_type(lo_f32, U32)
    hi = lax.bitcast_convert_type(hi_f32, U32)
    return (lo >> 16) | (hi & U32(HI_MASK))


def _unpack_bf16_pair(w):
    lo = lax.bitcast_convert_type(w << 16, F32)
    hi = lax.bitcast_convert_type(w & U32(HI_MASK), F32)
    return lo, hi


def _inproj_kernel(x_ref, g_ref, wa_ref, wqkv_ref, wg_ref, wf_ref, bf_ref,
                   cb_ref, ucx_ref, q_ref, k_ref, v_ref, sgc_ref, sga_ref, crow_ref,
                   carry_ref, *, tiles_per_batch, conv_dim, attn_dim, d_model, heads, tk):
    tm = x_ref.shape[0]
    t = lax.rem(pl.program_id(0), tiles_per_batch)
    x = x_ref[...]
    ms = jnp.mean(x * x, axis=-1, keepdims=True)
    h = ((x * lax.rsqrt(ms + RMS_EPS)) * g_ref[...]).astype(BF16)

    def mm(w_ref, c0, n):
        return jnp.dot(h, w_ref[:, c0:c0 + n], preferred_element_type=F32)

    cb_ref[...] = mm(wa_ref, 0, conv_dim).astype(BF16)
    ucx_ref[...] = (mm(wa_ref, conv_dim, conv_dim) * mm(wa_ref, 2 * conv_dim, conv_dim)).astype(BF16)
    q_ref[...] = mm(wqkv_ref, 0, attn_dim).astype(BF16)
    k_ref[...] = mm(wqkv_ref, attn_dim, attn_dim).astype(BF16)
    v_ref[...] = mm(wqkv_ref, 2 * attn_dim, attn_dim).astype(BF16)
    half = d_model // 2
    for c in range(2):
        sgc_ref[:, c * half:(c + 1) * half] = jax.nn.sigmoid(mm(wg_ref, c * half, half)).astype(BF16)
        sga_ref[:, c * half:(c + 1) * half] = jax.nn.sigmoid(mm(wg_ref, d_model + c * half, half)).astype(BF16)

    z = mm(wf_ref, 0, LANES) + bf_ref[...]
    lane = lax.broadcasted_iota(I32, z.shape, 1)
    logf = jnp.minimum(z, 0.0) - jnp.log1p(jnp.exp(-jnp.abs(z)))
    logf = jnp.where(lane < heads, logf, 0.0)
    p1 = logf.astype(BF16).astype(F32)
    r1 = logf - p1
    p2 = r1.astype(BF16).astype(F32)
    p3 = (r1 - p2).astype(BF16).astype(F32)
    packed = (p1 + pltpu.roll(p2, heads, 1) + pltpu.roll(p3, 2 * heads, 1)).astype(BF16)
    row = lax.broadcasted_iota(I32, (tm, tm), 0)
    col = lax.broadcasted_iota(I32, (tm, tm), 1)
    tri = jnp.where(col <= row, 1.0, 0.0).astype(BF16)
    r = jnp.dot(tri, packed, preferred_element_type=F32)
    local = r + pltpu.roll(r, LANES - heads, 1) + pltpu.roll(r, LANES - 2 * heads, 1)

    @pl.when(t == 0)
    def _():
        carry_ref[...] = jnp.zeros_like(carry_ref)

    c = jnp.where(lane < heads, local + carry_ref[0:1, :], 0.0)
    carry_ref[0:1, :] = c[tm - 1:tm, :]
    ct = c.T
    per_tile = tm // tk
    nk = tiles_per_batch * per_tile
    for hh in range(heads):
        for u in range(per_tile):
            crow_ref[0, pl.ds(hh * nk + t * per_tile + u, 1), :] = ct[hh:hh + 1, u * tk:(u + 1) * tk]


def _inproj(x2d, g_mix, wa, wqkv, wg, wf, bf_pad, *, batch, seq, heads, tm, tk):
    n, d_model = x2d.shape
    conv_dim = wa.shape[1] // 3
    attn_dim = wqkv.shape[1] // 3
    tiles_per_batch = seq // tm
    nk = seq // tk
    grid = (n // tm,)
    full = lambda shape: pl.BlockSpec(shape, lambda i: (0,) * len(shape))
    rows = lambda w: pl.BlockSpec((tm, w), lambda i: (i, 0))
    kern = functools.partial(_inproj_kernel, tiles_per_batch=tiles_per_batch, conv_dim=conv_dim,
                             attn_dim=attn_dim, d_model=d_model, heads=heads, tk=tk)
    out_shape = (
        jax.ShapeDtypeStruct((n, conv_dim), BF16),
        jax.ShapeDtypeStruct((n, conv_dim), BF16),
        jax.ShapeDtypeStruct((n, attn_dim), BF16),
        jax.ShapeDtypeStruct((n, attn_dim), BF16),
        jax.ShapeDtypeStruct((n, attn_dim), BF16),
        jax.ShapeDtypeStruct((n, d_model), BF16),
        jax.ShapeDtypeStruct((n, d_model), BF16),
        jax.ShapeDtypeStruct((batch, heads * nk, tk), F32),
    )
    out_specs = (
        rows(conv_dim), rows(conv_dim), rows(attn_dim), rows(attn_dim), rows(attn_dim),
        rows(d_model), rows(d_model),
        pl.BlockSpec((1, heads * nk, tk), lambda i: (i // tiles_per_batch, 0, 0)),
    )
    return pl.pallas_call(
        kern,
        grid=grid,
        in_specs=[rows(d_model), full((1, d_model)), full(wa.shape), full(wqkv.shape), full(wg.shape),
                  full(wf.shape), full((1, LANES))],
        out_specs=out_specs,
        out_shape=out_shape,
        scratch_shapes=[pltpu.VMEM((8, LANES), F32)],
        compiler_params=_params(("arbitrary",)),
        name="inproj",
    )(x2d, g_mix, wa, wqkv, wg, wf, bf_pad)


def _attn_kernel(q_ref, k_ref, v_ref, crow_ref, o_ref, *, tq, tk, nk, head_dim):
    hp = pl.program_id(1)
    qi = pl.program_id(2)
    q = q_ref[...]
    lane = lax.broadcasted_iota(I32, (tq, LANES), 1)
    row = lax.broadcasted_iota(I32, (tq, tk), 0)
    col = lax.broadcasted_iota(I32, (tq, tk), 1)
    outs = []
    for hh in range(2):
        head = hp * 2 + hh
        keep = (lane < head_dim) if hh == 0 else (lane >= head_dim)
        qm = jnp.where(keep, q, jnp.zeros_like(q))

        def block(j, carry, masked, qm=qm, head=head):
            m, l, acc = carry
            off = pl.multiple_of(j * tk, tk)
            kj = k_ref[pl.ds(off, tk), :]
            vj = v_ref[pl.ds(off, tk), :]
            s = lax.dot_general(qm, kj, (((1,), (1,)), ((), ())), preferred_element_type=F32)
            s = s - crow_ref[0, pl.ds(head * nk + j, 1), :]
            if masked:
                s = jnp.where(col <= row, s, -jnp.inf)
            m_new = jnp.maximum(m, jnp.max(s, axis=1, keepdims=True))
            alpha = jnp.exp(m - m_new)
            p = jnp.exp(s - m_new)
            l_new = alpha * l + jnp.sum(p, axis=1, keepdims=True)
            acc_new = alpha * acc + jnp.dot(p.astype(BF16), vj, preferred_element_type=F32)
            return m_new, l_new, acc_new

        init = (jnp.full((tq, 1), -jnp.inf, F32), jnp.zeros((tq, 1), F32), jnp.zeros((tq, LANES), F32))
        carry = lax.fori_loop(0, qi, lambda j, c: block(j, c, False), init)
        _, l, acc = block(qi, carry, True)
        outs.append(acc / l)
    o_ref[...] = jnp.where(lane < head_dim, outs[0], outs[1]).astype(BF16)


def _attention(q, k, v, crow, *, batch, seq, heads, tq):
    n, attn_dim = q.shape
    head_dim = attn_dim // heads
    assert 2 * head_dim == LANES, "attention kernel packs two heads per 128-lane group"
    tk = crow.shape[2]
    assert tq == tk
    nq = seq // tq
    nk = seq // tk
    kern = functools.partial(_attn_kernel, tq=tq, tk=tk, nk=nk, head_dim=head_dim)
    return pl.pallas_call(
        kern,
        grid=(batch, heads // 2, nq),
        in_specs=[
            pl.BlockSpec((tq, LANES), lambda b, hp, qi: (b * nq + qi, hp)),
            pl.BlockSpec((seq, LANES), lambda b, hp, qi: (b, hp)),
            pl.BlockSpec((seq, LANES), lambda b, hp, qi: (b, hp)),
            pl.BlockSpec((1, heads * nk, tk), lambda b, hp, qi: (b, 0, 0)),
        ],
        out_specs=pl.BlockSpec((tq, LANES), lambda b, hp, qi: (b * nq + qi, hp)),
        out_shape=jax.ShapeDtypeStruct((n, attn_dim), BF16),
        compiler_params=_params(("arbitrary", "arbitrary", "arbitrary")),
        name="attn",
    )(q, k, v, crow)


def _mix_kernel(x_ref, cb_ref, u_ref, uh_ref, o_ref, sgc_ref, sga_ref, cw_ref, wco_ref, wao_ref, wout_ref,
                gffn_ref, wr_ref, br_ref,
                x1_ref, h2p_ref, topi_ref, gate_ref, rank_ref, cnt_ref, carry_ref,
                *, tiles_per_batch, n_experts):
    tm, d_model = x_ref.shape
    i = pl.program_id(0)
    first = lax.rem(i, tiles_per_batch) == 0

    u = u_ref[...].astype(F32)
    halo = jnp.where(first, 0.0, uh_ref[...].astype(F32))
    ext = jnp.concatenate([halo, u], axis=0)
    hs = halo.shape[0]
    u1 = pltpu.roll(ext, 1, 0)[hs:]
    u2 = pltpu.roll(ext, 2, 0)[hs:]
    cw = cw_ref[...]
    conv = cw[0:1, :] * u2 + cw[1:2, :] * u1 + cw[2:3, :] * u
    yc = (cb_ref[...].astype(F32) * conv).astype(BF16)
    y_conv = jnp.dot(yc, wco_ref[...], preferred_element_type=F32)
    y_attn = jnp.dot(o_ref[...], wao_ref[...], preferred_element_type=F32)
    mixed = (sgc_ref[...].astype(F32) * y_conv + sga_ref[...].astype(F32) * y_attn).astype(BF16)
    x1 = x_ref[...] + jnp.dot(mixed, wout_ref[...], preferred_element_type=F32)
    x1_ref[...] = x1

    ms = jnp.mean(x1 * x1, axis=-1, keepdims=True)
    h2 = (x1 * lax.rsqrt(ms + RMS_EPS)) * gffn_ref[...]
    hb = h2.astype(BF16)
    hbf = hb.astype(F32)
    half = d_model // 2
    h2p_ref[...] = _pack_bf16_pair(hbf[:, :half], hbf[:, half:])

    hlo = (h2 - hbf).astype(BF16)
    r = jnp.dot(hb, wr_ref[...], preferred_element_type=F32) + jnp.dot(hlo, wr_ref[...], preferred_element_type=F32)
    lane = lax.broadcasted_iota(I32, (tm, LANES), 1)
    logits = r + pltpu.roll(r, LANES - n_experts, 1) + br_ref[...]
    lg = jnp.where(lane < n_experts, logits, -jnp.inf)

    vals, idxs = [], []
    for _ in range(TOP_K):
        m = jnp.max(lg, axis=1, keepdims=True)
        idx = jnp.min(jnp.where(lg == m, lane, LANES), axis=1, keepdims=True)
        vals.append(m)
        idxs.append(idx)
        lg = jnp.where(lane == idx, -jnp.inf, lg)
    es = [jnp.exp(vk - vals[0]) for vk in vals]
    denom = es[0] + es[1] + es[2] + es[3]

    chosen = jnp.zeros((tm, LANES), F32)
    topi = jnp.zeros((tm, LANES), I32)
    gates = jnp.zeros((tm, LANES), F32)
    for kk in range(TOP_K):
        chosen = jnp.where(lane == idxs[kk], 1.0, chosen)
        topi = jnp.where(lane == kk, idxs[kk], topi)
        gates = jnp.where(lane == kk, es[kk] / denom, gates)
    topi_ref[...] = topi
    gate_ref[...] = gates

    @pl.when(i == 0)
    def _():
        carry_ref[...] = jnp.zeros_like(carry_ref)

    row = lax.broadcasted_iota(I32, (tm, tm), 0)
    col = lax.broadcasted_iota(I32, (tm, tm), 1)
    tri = jnp.where(col < row, 1.0, 0.0).astype(BF16)
    before = jnp.dot(tri, chosen.astype(BF16), preferred_element_type=F32) + carry_ref[0:1, :]
    rank = jnp.zeros((tm, LANES), F32)
    for kk in range(TOP_K):
        rk = jnp.sum(jnp.where(lane == idxs[kk], before, 0.0), axis=1, keepdims=True)
        rank = jnp.where(lane == kk, rk, rank)
    rank_ref[...] = rank.astype(I32)
    total = carry_ref[0:1, :] + jnp.sum(chosen, axis=0, keepdims=True)
    carry_ref[0:1, :] = total
    cnt_ref[...] = jnp.broadcast_to(total.reshape(1, 1, LANES), cnt_ref.shape)


def _mix(x2d, cb, ucx, o, sgc, sga, conv_w, wco, wao, wout, g_ffn, wr, br, *, seq, n_experts, tm):
    n, d_model = x2d.shape
    conv_dim = cb.shape[1]
    attn_dim = o.shape[1]
    tiles_per_batch = seq // tm
    nt = n // tm
    hb = tm // BF16_SUBLANES
    full = lambda shape: pl.BlockSpec(shape, lambda i: (0,) * len(shape))
    rows = lambda w: pl.BlockSpec((tm, w), lambda i: (i, 0))
    kern = functools.partial(_mix_kernel, tiles_per_batch=tiles_per_batch, n_experts=n_experts)
    out_shape = (
        jax.ShapeDtypeStruct((n, d_model), F32),
        jax.ShapeDtypeStruct((n, d_model // 2), U32),
        jax.ShapeDtypeStruct((n, LANES), I32),
        jax.ShapeDtypeStruct((n, LANES), F32),
        jax.ShapeDtypeStruct((n, LANES), I32),
        jax.ShapeDtypeStruct((nt, 8, LANES), F32),
    )
    out_specs = (rows(d_model), rows(d_model // 2), rows(LANES), rows(LANES), rows(LANES),
                 pl.BlockSpec((1, 8, LANES), lambda i: (i, 0, 0)))
    return pl.pallas_call(
        kern,
        grid=(nt,),
        in_specs=[rows(d_model), rows(conv_dim), rows(conv_dim),
                  pl.BlockSpec((BF16_SUBLANES, conv_dim), lambda i: (jnp.maximum(i * hb - 1, 0), 0)),
                  rows(attn_dim), rows(d_model), rows(d_model),
                  full(conv_w.shape), full(wco.shape), full(wao.shape), full(wout.shape),
                  full((1, d_model)), full(wr.shape), full((1, LANES))],
        out_specs=out_specs,
        out_shape=out_shape,
        scratch_shapes=[pltpu.VMEM((8, LANES), F32)],
        compiler_params=_params(("arbitrary",)),
        name="mix",
    )(x2d, cb, ucx, ucx, o, sgc, sga, conv_w, wco, wao, wout, g_ffn, wr, br)


SC_CORES = 2
SC_SUBCORES = 16
SC_WORKERS = SC_CORES * SC_SUBCORES
SC_CHUNK = 64


def _sc_mesh():
    return plsc.VectorSubcoreMesh(core_axis_name="c", subcore_axis_name="s",
                                  num_cores=SC_CORES, num_subcores=SC_SUBCORES)


def _sc_worker():
    return lax.axis_index("s") * SC_CORES + lax.axis_index("c")


def _sc_scatter_rows(src, dest4, pad_idx, n_out):
    n, width = src.shape
    nchunks = dest4.shape[1]
    npad = pad_idx.shape[1]

    @functools.partial(
        pl.kernel, mesh=_sc_mesh(),
        out_type=jax.ShapeDtypeStruct((n_out, width), src.dtype),
        scratch_types=[pltpu.VMEM((TOP_K, SC_CHUNK), I32), pltpu.VMEM((SC_CHUNK, width), src.dtype),
                       pltpu.VMEM((npad, SC_CHUNK), I32)],
        name="sc_scatter_rows",
    )
    def k(src_hbm, dest_hbm, pad_hbm, out_hbm, idx_v, rows_v, pad_v):
        wid = _sc_worker()
        base = wid * (nchunks * SC_CHUNK)

        @pl.loop(0, nchunks)
        def _(j):
            pltpu.sync_copy(src_hbm.at[pl.ds(base + j * SC_CHUNK, SC_CHUNK)], rows_v)
            pltpu.sync_copy(dest_hbm.at[wid, j], idx_v)
            for kk in range(TOP_K):
                pltpu.sync_copy(rows_v, out_hbm.at[idx_v.at[kk]])

        pltpu.sync_copy(pad_hbm.at[wid], pad_v)
        for p in range(npad):
            pltpu.sync_copy(rows_v, out_hbm.at[pad_v.at[p]])

    return k(src, dest4, pad_idx)


def _sc_gather_rows(src, dest4, n):
    width = src.shape[1]
    nchunks = dest4.shape[1]

    @functools.partial(
        pl.kernel, mesh=_sc_mesh(),
        out_type=jax.ShapeDtypeStruct((TOP_K, n, width), src.dtype),
        scratch_types=[pltpu.VMEM((TOP_K, SC_CHUNK), I32), pltpu.VMEM((SC_CHUNK, width), src.dtype)],
        name="sc_gather_rows",
    )
    def k(src_hbm, dest_hbm, out_hbm, idx_v, rows_v):
        wid = _sc_worker()
        base = wid * (nchunks * SC_CHUNK)

        @pl.loop(0, nchunks)
        def _(j):
            pltpu.sync_copy(dest_hbm.at[wid, j], idx_v)
            for kk in range(TOP_K):
                pltpu.sync_copy(src_hbm.at[idx_v.at[kk]], rows_v)
                pltpu.sync_copy(rows_v, out_hbm.at[kk, pl.ds(base + j * SC_CHUNK, SC_CHUNK)])

    return k(src, dest4)


def _expert_kernel(be_ref, nu_ref, xs_ref, wgu_ref, bgu_ref, wd_ref, bd_ref, ys_ref, *, d_ff):
    i = pl.program_id(0)

    @pl.when(i < nu_ref[0])
    def _():
        lo, hi = _unpack_bf16_pair(xs_ref[...])
        xb = jnp.concatenate([lo.astype(BF16), hi.astype(BF16)], axis=1)
        gu = jnp.dot(xb, wgu_ref[0], preferred_element_type=F32) + bgu_ref[0]
        glu = jnp.minimum(gu[:, :d_ff], SWIGLU_LIMIT)
        lin = jnp.clip(gu[:, d_ff:], -SWIGLU_LIMIT, SWIGLU_LIMIT)
        act = (glu * jax.nn.sigmoid(SWIGLU_ALPHA * glu)) * (lin + 1.0)
        y = jnp.dot(act.astype(BF16), wd_ref[0], preferred_element_type=F32) + bd_ref[0]
        yb = y.astype(BF16).astype(F32)
        half = y.shape[1] // 2
        ys_ref[...] = _pack_bf16_pair(yb[:, :half], yb[:, half:])


def _experts(xs, block_e, n_used, wgu, bgu, wd, bd, *, tm, n_blocks):
    half = xs.shape[1]
    p = n_blocks * tm
    e, d_model, two_ff = wgu.shape
    d_ff = two_ff // 2
    kern = functools.partial(_expert_kernel, d_ff=d_ff)
    blk = lambda i, be, nu: (jnp.minimum(i, nu[0] - 1), 0)
    grid_spec = pltpu.PrefetchScalarGridSpec(
        num_scalar_prefetch=2,
        grid=(n_blocks,),
        in_specs=[
            pl.BlockSpec((tm, half), blk),
            pl.BlockSpec((1, d_model, two_ff), lambda i, be, nu: (be[i], 0, 0)),
            pl.BlockSpec((1, 1, two_ff), lambda i, be, nu: (be[i], 0, 0)),
            pl.BlockSpec((1, d_ff, d_model), lambda i, be, nu: (be[i], 0, 0)),
            pl.BlockSpec((1, 1, d_model), lambda i, be, nu: (be[i], 0, 0)),
        ],
        out_specs=pl.BlockSpec((tm, half), blk),
    )
    return pl.pallas_call(
        kern,
        grid_spec=grid_spec,
        out_shape=jax.ShapeDtypeStruct((p, half), U32),
        compiler_params=_params(("arbitrary",)),
        name="experts",
    )(block_e, n_used, xs, wgu, bgu.reshape(e, 1, two_ff), wd, bd.reshape(e, 1, d_model))


def _final_kernel(yk_ref, gate_ref, x1_ref, g_ref, out_ref):
    tm, d_model = x1_ref.shape
    half = d_model // 2
    gates = gate_ref[...]
    acc_lo = jnp.zeros((tm, half), F32)
    acc_hi = jnp.zeros((tm, half), F32)
    for kk in range(TOP_K):
        lo, hi = _unpack_bf16_pair(yk_ref[kk])
        gk = gates[:, kk:kk + 1]
        acc_lo = acc_lo + gk * lo
        acc_hi = acc_hi + gk * hi
    x2 = x1_ref[...] + jnp.concatenate([acc_lo, acc_hi], axis=1)
    ms = jnp.mean(x2 * x2, axis=-1, keepdims=True)
    out_ref[...] = (x2 * lax.rsqrt(ms + RMS_EPS)) * g_ref[...]


def _final(yk, gates, x1, g_final, *, tm):
    n, d_model = x1.shape
    rows = lambda w: pl.BlockSpec((tm, w), lambda i: (i, 0))
    return pl.pallas_call(
        _final_kernel,
        grid=(n // tm,),
        in_specs=[pl.BlockSpec((TOP_K, tm, d_model // 2), lambda i: (0, i, 0)), rows(LANES), rows(d_model),
                  pl.BlockSpec((1, d_model), lambda i: (0, 0))],
        out_specs=rows(d_model),
        out_shape=jax.ShapeDtypeStruct((n, d_model), F32),
        compiler_params=_params(("arbitrary",)),
        name="final",
    )(yk, gates, x1, g_final)


def _tiles(seq):
    tm = min(512, seq)
    return dict(tm_proj=tm, t_attn=min(256, seq), tm_mix=tm, tm_expert=256, tm_final=tm)


def _forward(x, g_mix, w_in, conv_w, b_f, w_conv_o, w_attn_o, w_out, g_ffn, w_router, b_router,
             w_gate_up, b_gate_up, w_down, b_down, g_final, tiles):
    batch, seq, d_model = x.shape
    n = batch * seq
    conv_dim = conv_w.shape[1]
    attn_dim = w_attn_o.shape[0]
    heads = b_f.shape[0]
    head_dim = attn_dim // heads
    n_experts = w_router.shape[1]
    x2d = x.reshape(n, d_model)

    c0 = 3 * conv_dim
    a0 = c0 + 3 * attn_dim
    wa = w_in[:, :c0].astype(BF16)
    scale = 1.0 / (head_dim ** 0.5)
    wqkv = jnp.concatenate([w_in[:, c0:c0 + attn_dim] * scale, w_in[:, c0 + attn_dim:a0]], axis=1).astype(BF16)
    wf = jnp.pad(w_in[:, a0:a0 + heads], ((0, 0), (0, LANES - heads))).astype(BF16)
    wg = w_in[:, a0 + heads:].astype(BF16)
    bf_pad = jnp.pad(b_f, (0, LANES - heads)).reshape(1, LANES)
    wr_hi = w_router.astype(BF16)
    wr_lo = (w_router - wr_hi.astype(F32)).astype(BF16)
    wr = jnp.pad(jnp.concatenate([wr_hi, wr_lo], axis=1), ((0, 0), (0, LANES - 2 * n_experts)))
    br = jnp.pad(b_router, (0, LANES - n_experts)).reshape(1, LANES)

    cb, ucx, q, k, v, sgc, sga, crow = _inproj(
        x2d, g_mix.reshape(1, d_model), wa, wqkv, wg, wf, bf_pad,
        batch=batch, seq=seq, heads=heads, tm=tiles["tm_proj"], tk=tiles["t_attn"])
    o = _attention(q, k, v, crow, batch=batch, seq=seq, heads=heads, tq=tiles["t_attn"])
    x1, h2p, topi, gates, rank, cnt = _mix(
        x2d, cb, ucx, o, sgc, sga, conv_w, w_conv_o.astype(BF16), w_attn_o.astype(BF16), w_out.astype(BF16),
        g_ffn.reshape(1, d_model), wr, br, seq=seq, n_experts=n_experts, tm=tiles["tm_mix"])

    tme = tiles["tm_expert"]
    m = n * TOP_K
    n_blocks = m // tme + n_experts
    counts = cnt[-1, 0, :n_experts].astype(I32)
    padded = ((counts + tme - 1) // tme) * tme
    pend = jnp.cumsum(padded)
    pstart = pend - padded
    n_used = (pend[-1] // tme).astype(I32).reshape(1)
    blk_start = jnp.minimum(jnp.arange(n_blocks, dtype=I32), n_used[0] - 1) * tme
    block_e = jnp.minimum(jnp.sum((pend[None, :] <= blk_start[:, None]).astype(I32), axis=1), n_experts - 1)
    e_sel = topi[:, :TOP_K]
    onehot = e_sel[:, :, None] == jnp.arange(n_experts, dtype=I32)[None, None, :]
    dest = jnp.sum(jnp.where(onehot, pstart[None, None, :], 0), axis=2) + rank[:, :TOP_K]
    nchunks = n // (SC_WORKERS * SC_CHUNK)
    dest4 = dest.reshape(SC_WORKERS, nchunks, SC_CHUNK, TOP_K).transpose(0, 1, 3, 2)
    p_rows = n_blocks * tme
    jj = jnp.arange(tme, dtype=I32)[None, :]
    spare = p_rows + jnp.arange(n_experts, dtype=I32)[:, None] * tme + jj
    pad_idx = jnp.where(jj < (padded - counts)[:, None], (pstart + counts)[:, None] + jj, spare)
    pad_idx = pad_idx.reshape(SC_WORKERS, (n_experts * tme) // (SC_WORKERS * SC_CHUNK), SC_CHUNK)

    xs = _sc_scatter_rows(h2p, dest4, pad_idx, p_rows + n_experts * tme)
    ys = _experts(xs, block_e, n_used, w_gate_up.astype(BF16), b_gate_up, w_down.astype(BF16), b_down,
                  tm=tme, n_blocks=n_blocks)
    yk = _sc_gather_rows(ys, dest4, n)
    out = _final(yk, gates, x1, g_final.reshape(1, d_model), tm=tiles["tm_final"])
    return out.reshape(batch, seq, d_model)


def kernel(x, g_mix, w_in, conv_w, b_f, w_conv_o, w_attn_o, w_out, g_ffn, w_router, b_router, w_gate_up,
           b_gate_up, w_down, b_down, g_final):
    return _forward(x, g_mix, w_in, conv_w, b_f, w_conv_o, w_attn_o, w_out, g_ffn, w_router, b_router,
                    w_gate_up, b_gate_up, w_down, b_down, g_final, _tiles(x.shape[1]))
```

```python
import functools

import jax
import jax.numpy as jnp
from jax import lax
from jax.experimental import pallas as pl
from jax.experimental.pallas import tpu as pltpu
from jax.experimental.pallas import tpu_sc as plsc

TOP_K = 4
RMS_EPS = 1e-5
SWIGLU_ALPHA = 1.702
SWIGLU_LIMIT = 7.0
LOG2E = 1.4426950408889634

LANES = 128
BF16_SUBLANES = 16
VMEM_LIMIT_BYTES = 56 * 1024 * 1024

F32 = jnp.float32
BF16 = jnp.bfloat16
U32 = jnp.uint32
I32 = jnp.int32
HI_MASK = 0xFFFF0000


def _params(sem):
    return pltpu.CompilerParams(dimension_semantics=sem, vmem_limit_bytes=VMEM_LIMIT_BYTES)


def _pack_bf16_pair(lo_f32, hi_f32):
    lo = lax.bitcast_convert_type(lo_f32, U32)
    hi = lax.bitcast_convert_type(hi_f32, U32)
    return (lo >> 16) | (hi & U32(HI_MASK))


def _unpack_bf16_pair(w):
    lo = lax.bitcast_convert_type(w << 16, F32)
    hi = lax.bitcast_convert_type(w & U32(HI_MASK), F32)
    return lo, hi


def _inproj_kernel(x_ref, g_ref, wa_ref, wqk_ref, wvt_ref, wg_ref, wf_ref, bf_ref,
                   cb_ref, ucx_ref, qa_ref, ka_ref, vt_ref, sgc_ref, sga_ref,
                   carry_ref, *, tiles_per_batch, conv_dim, attn_dim, d_model, heads, tk):
    tm = x_ref.shape[0]
    t = lax.rem(pl.program_id(0), tiles_per_batch)
    x = x_ref[...]
    ms = jnp.mean(x * x, axis=-1, keepdims=True)
    h = ((x * lax.rsqrt(ms + RMS_EPS)) * g_ref[...]).astype(BF16)

    def mm(w_ref, c0, n):
        return jnp.dot(h, w_ref[:, c0:c0 + n], preferred_element_type=F32)

    cb_ref[...] = mm(wa_ref, 0, conv_dim).astype(BF16)
    ucx_ref[...] = (mm(wa_ref, conv_dim, conv_dim) * mm(wa_ref, 2 * conv_dim, conv_dim)).astype(BF16)
    half = d_model // 2
    for c in range(2):
        sgc_ref[:, c * half:(c + 1) * half] = jax.nn.sigmoid(mm(wg_ref, c * half, half)).astype(BF16)
        sga_ref[:, c * half:(c + 1) * half] = jax.nn.sigmoid(mm(wg_ref, d_model + c * half, half)).astype(BF16)

    vt = lax.dot_general(wvt_ref[...], h, (((1,), (1,)), ((), ())), preferred_element_type=F32).astype(BF16)
    for u in range(tm // tk):
        vt_ref[u] = vt[:, u * tk:(u + 1) * tk]

    z = mm(wf_ref, 0, LANES) + bf_ref[...]
    lane = lax.broadcasted_iota(I32, z.shape, 1)
    logf = jnp.minimum(z, 0.0) - jnp.log1p(jnp.exp(-jnp.abs(z)))
    logf = jnp.where(lane < heads, logf, 0.0)
    p1 = logf.astype(BF16).astype(F32)
    r1 = logf - p1
    p2 = r1.astype(BF16).astype(F32)
    p3 = (r1 - p2).astype(BF16).astype(F32)
    packed = (p1 + pltpu.roll(p2, heads, 1) + pltpu.roll(p3, 2 * heads, 1)).astype(BF16)
    row = lax.broadcasted_iota(I32, (tm, tm), 0)
    col = lax.broadcasted_iota(I32, (tm, tm), 1)
    tri = jnp.where(col <= row, 1.0, 0.0).astype(BF16)
    r = jnp.dot(tri, packed, preferred_element_type=F32)
    local = r + pltpu.roll(r, LANES - heads, 1) + pltpu.roll(r, LANES - 2 * heads, 1)

    @pl.when(t == 0)
    def _():
        carry_ref[...] = jnp.zeros_like(carry_ref)

    c = jnp.where(lane < heads, local + carry_ref[0:1, :], 0.0)
    carry_ref[0:1, :] = c[tm - 1:tm, :]

    c2 = c * LOG2E
    c_hi = c2.astype(BF16).astype(F32)
    c_r = c2 - c_hi
    c_mid = c_r.astype(BF16).astype(F32)
    c_lo = (c_r - c_mid).astype(BF16).astype(F32)
    head_dim = attn_dim // heads
    ext = LANES - head_dim
    lane_e = lax.broadcasted_iota(I32, (tm, ext), 1)
    ones3 = jnp.where(lane_e < 3, 1.0, 0.0)
    qf = mm(wqk_ref, 0, attn_dim)
    kf = mm(wqk_ref, attn_dim, attn_dim)
    for hh in range(heads):
        kext = jnp.where(lane_e == 0, -c_hi[:, hh:hh + 1],
                         jnp.where(lane_e == 1, -c_mid[:, hh:hh + 1],
                                   jnp.where(lane_e == 2, -c_lo[:, hh:hh + 1], 0.0)))
        sl = slice(hh * head_dim, (hh + 1) * head_dim)
        qa_ref[:, hh * LANES:(hh + 1) * LANES] = jnp.concatenate([qf[:, sl], ones3], axis=1).astype(BF16)
        ka_ref[:, hh * LANES:(hh + 1) * LANES] = jnp.concatenate([kf[:, sl], kext], axis=1).astype(BF16)


def _inproj(x2d, g_mix, wa, wqk, wvt, wg, wf, bf_pad, *, batch, seq, heads, tm, tk):
    n, d_model = x2d.shape
    conv_dim = wa.shape[1] // 3
    attn_dim = wvt.shape[0]
    tiles_per_batch = seq // tm
    grid = (n // tm,)
    full = lambda shape: pl.BlockSpec(shape, lambda i: (0,) * len(shape))
    rows = lambda w: pl.BlockSpec((tm, w), lambda i: (i, 0))
    kern = functools.partial(_inproj_kernel, tiles_per_batch=tiles_per_batch, conv_dim=conv_dim,
                             attn_dim=attn_dim, d_model=d_model, heads=heads, tk=tk)
    out_shape = (
        jax.ShapeDtypeStruct((n, conv_dim), BF16),
        jax.ShapeDtypeStruct((n, conv_dim), BF16),
        jax.ShapeDtypeStruct((n, heads * LANES), BF16),
        jax.ShapeDtypeStruct((n, heads * LANES), BF16),
        jax.ShapeDtypeStruct((n // tk, attn_dim, tk), BF16),
        jax.ShapeDtypeStruct((n, d_model), BF16),
        jax.ShapeDtypeStruct((n, d_model), BF16),
    )
    out_specs = (
        rows(conv_dim), rows(conv_dim), rows(heads * LANES), rows(heads * LANES),
        pl.BlockSpec((tm // tk, attn_dim, tk), lambda i: (i, 0, 0)),
        rows(d_model), rows(d_model),
    )
    return pl.pallas_call(
        kern,
        grid=grid,
        in_specs=[rows(d_model), full((1, d_model)), full(wa.shape), full(wqk.shape), full(wvt.shape),
                  full(wg.shape), full(wf.shape), full((1, LANES))],
        out_specs=out_specs,
        out_shape=out_shape,
        scratch_shapes=[pltpu.VMEM((8, LANES), F32)],
        compiler_params=_params(("arbitrary",)),
        name="inproj",
    )(x2d, g_mix, wa, wqk, wvt, wg, wf, bf_pad)


def _attn_kernel(q_ref, k_ref, v_ref, o_ref, *, tq, tk, head_dim, heads):
    qi = pl.program_id(1)
    row = lax.broadcasted_iota(I32, (tk, tq), 0)
    col = lax.broadcasted_iota(I32, (tk, tq), 1)

    def block(j, carry, masked):
        off = pl.multiple_of(j * tk, tk)
        scores = []
        for hh in range(heads):
            qa = q_ref[:, hh * LANES:(hh + 1) * LANES]
            ka = k_ref[pl.ds(off, tk), hh * LANES:(hh + 1) * LANES]
            scores.append(lax.dot_general(ka, qa, (((1,), (1,)), ((), ())), preferred_element_type=F32))
        stats = []
        for hh in range(heads):
            m, l, _ = carry[hh]
            s = scores[hh]
            if masked:
                s = jnp.where(row <= col, s, -jnp.inf)
            m_new = jnp.maximum(m, jnp.max(s, axis=0, keepdims=True))
            alpha = jnp.exp2(m - m_new)
            p = jnp.exp2(s - m_new)
            l_new = alpha * l + jnp.sum(p, axis=0, keepdims=True)
            stats.append((m_new, l_new, alpha, p.astype(BF16)))
        out = []
        for hh in range(heads):
            m_new, l_new, alpha, p = stats[hh]
            vth = v_ref[j, hh * head_dim:(hh + 1) * head_dim, :]
            acc_new = alpha * carry[hh][2] + jnp.dot(vth, p, preferred_element_type=F32)
            out.append((m_new, l_new, acc_new))
        return tuple(out)

    init = tuple((jnp.full((1, tq), -jnp.inf, F32), jnp.zeros((1, tq), F32), jnp.zeros((head_dim, tq), F32))
                 for _ in range(heads))
    carry = lax.fori_loop(0, qi, lambda j, c: block(j, c, False), init)
    final = block(qi, carry, True)
    per_group = LANES // head_dim
    for g in range(heads // per_group):
        ot = jnp.concatenate([final[g * per_group + u][2] / final[g * per_group + u][1] for u in range(per_group)],
                             axis=0)
        o_ref[:, g * LANES:(g + 1) * LANES] = ot.T.astype(BF16)


def _attention(qa, ka, vt, *, batch, seq, heads, tq):
    n = qa.shape[0]
    _, attn_dim, tk = vt.shape
    head_dim = attn_dim // heads
    assert LANES % head_dim == 0 and tq == tk
    nq = seq // tq
    nk = seq // tk
    kern = functools.partial(_attn_kernel, tq=tq, tk=tk, head_dim=head_dim, heads=heads)
    return pl.pallas_call(
        kern,
        grid=(batch, nq),
        in_specs=[
            pl.BlockSpec((tq, heads * LANES), lambda b, qi: (b * nq + qi, 0)),
            pl.BlockSpec((seq, heads * LANES), lambda b, qi: (b, 0)),
            pl.BlockSpec((nk, attn_dim, tk), lambda b, qi: (b, 0, 0)),
        ],
        out_specs=pl.BlockSpec((tq, attn_dim), lambda b, qi: (b * nq + qi, 0)),
        out_shape=jax.ShapeDtypeStruct((n, attn_dim), BF16),
        compiler_params=_params(("arbitrary", "arbitrary")),
        name="attn",
    )(qa, ka, vt)


def _mix_kernel(x_ref, cb_ref, u_ref, uh_ref, o_ref, sgc_ref, sga_ref, cw_ref, wco_ref, wao_ref, wout_ref,
                gffn_ref, wr_ref, br_ref,
                x1_ref, h2p_ref, topi_ref, gate_ref, rank_ref, cnt_ref, carry_ref,
                *, tiles_per_batch, n_experts):
    tm, d_model = x_ref.shape
    i = pl.program_id(0)
    first = lax.rem(i, tiles_per_batch) == 0

    u = u_ref[...].astype(F32)
    halo = jnp.where(first, 0.0, uh_ref[...].astype(F32))
    ext = jnp.concatenate([halo, u], axis=0)
    hs = halo.shape[0]
    u1 = pltpu.roll(ext, 1, 0)[hs:]
    u2 = pltpu.roll(ext, 2, 0)[hs:]
    cw = cw_ref[...]
    conv = cw[0:1, :] * u2 + cw[1:2, :] * u1 + cw[2:3, :] * u
    yc = (cb_ref[...].astype(F32) * conv).astype(BF16)
    y_conv = jnp.dot(yc, wco_ref[...], preferred_element_type=F32)
    y_attn = jnp.dot(o_ref[...], wao_ref[...], preferred_element_type=F32)
    mixed = (sgc_ref[...].astype(F32) * y_conv + sga_ref[...].astype(F32) * y_attn).astype(BF16)
    x1 = x_ref[...] + jnp.dot(mixed, wout_ref[...], preferred_element_type=F32)
    x1_ref[...] = x1

    ms = jnp.mean(x1 * x1, axis=-1, keepdims=True)
    h2 = (x1 * lax.rsqrt(ms + RMS_EPS)) * gffn_ref[...]
    hb = h2.astype(BF16)
    hbf = hb.astype(F32)
    half = d_model // 2
    h2p_ref[...] = _pack_bf16_pair(hbf[:, :half], hbf[:, half:])

    hlo = (h2 - hbf).astype(BF16)
    r = jnp.dot(hb, wr_ref[...], preferred_element_type=F32) + jnp.dot(hlo, wr_ref[...], preferred_element_type=F32)
    lane = lax.broadcasted_iota(I32, (tm, LANES), 1)
    logits = r + pltpu.roll(r, LANES - n_experts, 1) + br_ref[...]
    lg = jnp.where(lane < n_experts, logits, -jnp.inf)

    vals, idxs = [], []
    for _ in range(TOP_K):
        m = jnp.max(lg, axis=1, keepdims=True)
        idx = jnp.min(jnp.where(lg == m, lane, LANES), axis=1, keepdims=True)
        vals.append(m)
        idxs.append(idx)
        lg = jnp.where(lane == idx, -jnp.inf, lg)
    es = [jnp.exp(vk - vals[0]) for vk in vals]
    denom = es[0] + es[1] + es[2] + es[3]

    chosen = jnp.zeros((tm, LANES), F32)
    topi = jnp.zeros((tm, LANES), I32)
    gates = jnp.zeros((tm, LANES), F32)
    for kk in range(TOP_K):
        chosen = jnp.where(lane == idxs[kk], 1.0, chosen)
        topi = jnp.where(lane == kk, idxs[kk], topi)
        gates = jnp.where(lane == kk, es[kk] / denom, gates)
    topi_ref[...] = topi
    gate_ref[...] = gates

    @pl.when(i == 0)
    def _():
        carry_ref[...] = jnp.zeros_like(carry_ref)

    row = lax.broadcasted_iota(I32, (tm, tm), 0)
    col = lax.broadcasted_iota(I32, (tm, tm), 1)
    tri = jnp.where(col < row, 1.0, 0.0).astype(BF16)
    before = jnp.dot(tri, chosen.astype(BF16), preferred_element_type=F32) + carry_ref[0:1, :]
    rank = jnp.zeros((tm, LANES), F32)
    for kk in range(TOP_K):
        rk = jnp.sum(jnp.where(lane == idxs[kk], before, 0.0), axis=1, keepdims=True)
        rank = jnp.where(lane == kk, rk, rank)
    rank_ref[...] = rank.astype(I32)
    total = carry_ref[0:1, :] + jnp.sum(chosen, axis=0, keepdims=True)
    carry_ref[0:1, :] = total
    cnt_ref[...] = jnp.broadcast_to(total.reshape(1, 1, LANES), cnt_ref.shape)


def _mix(x2d, cb, ucx, o, sgc, sga, conv_w, wco, wao, wout, g_ffn, wr, br, *, seq, n_experts, tm):
    n, d_model = x2d.shape
    conv_dim = cb.shape[1]
    attn_dim = o.shape[1]
    tiles_per_batch = seq // tm
    nt = n // tm
    hb = tm // BF16_SUBLANES
    full = lambda shape: pl.BlockSpec(shape, lambda i: (0,) * len(shape))
    rows = lambda w: pl.BlockSpec((tm, w), lambda i: (i, 0))
    kern = functools.partial(_mix_kernel, tiles_per_batch=tiles_per_batch, n_experts=n_experts)
    out_shape = (
        jax.ShapeDtypeStruct((n, d_model), F32),
        jax.ShapeDtypeStruct((n, d_model // 2), U32),
        jax.ShapeDtypeStruct((n, LANES), I32),
        jax.ShapeDtypeStruct((n, LANES), F32),
        jax.ShapeDtypeStruct((n, LANES), I32),
        jax.ShapeDtypeStruct((nt, 8, LANES), F32),
    )
    out_specs = (rows(d_model), rows(d_model // 2), rows(LANES), rows(LANES), rows(LANES),
                 pl.BlockSpec((1, 8, LANES), lambda i: (i, 0, 0)))
    return pl.pallas_call(
        kern,
        grid=(nt,),
        in_specs=[rows(d_model), rows(conv_dim), rows(conv_dim),
                  pl.BlockSpec((BF16_SUBLANES, conv_dim), lambda i: (jnp.maximum(i * hb - 1, 0), 0)),
                  rows(attn_dim), rows(d_model), rows(d_model),
                  full(conv_w.shape), full(wco.shape), full(wao.shape), full(wout.shape),
                  full((1, d_model)), full(wr.shape), full((1, LANES))],
        out_specs=out_specs,
        out_shape=out_shape,
        scratch_shapes=[pltpu.VMEM((8, LANES), F32)],
        compiler_params=_params(("arbitrary",)),
        name="mix",
    )(x2d, cb, ucx, ucx, o, sgc, sga, conv_w, wco, wao, wout, g_ffn, wr, br)


SC_CORES = 2
SC_SUBCORES = 16
SC_WORKERS = SC_CORES * SC_SUBCORES
SC_CHUNK = 64


def _sc_mesh():
    return plsc.VectorSubcoreMesh(core_axis_name="c", subcore_axis_name="s",
                                  num_cores=SC_CORES, num_subcores=SC_SUBCORES)


def _sc_worker():
    return lax.axis_index("s") * SC_CORES + lax.axis_index("c")


def _sc_scatter_rows(src, dest4, pad_idx, n_out):
    n, width = src.shape
    nchunks = dest4.shape[1]
    npad = pad_idx.shape[1]

    @functools.partial(
        pl.kernel, mesh=_sc_mesh(),
        out_type=jax.ShapeDtypeStruct((n_out, width), src.dtype),
        scratch_types=[pltpu.VMEM((TOP_K, SC_CHUNK), I32), pltpu.VMEM((SC_CHUNK, width), src.dtype),
                       pltpu.VMEM((npad, SC_CHUNK), I32)],
        name="sc_scatter_rows",
    )
    def k(src_hbm, dest_hbm, pad_hbm, out_hbm, idx_v, rows_v, pad_v):
        wid = _sc_worker()
        base = wid * (nchunks * SC_CHUNK)

        @pl.loop(0, nchunks)
        def _(j):
            pltpu.sync_copy(src_hbm.at[pl.ds(base + j * SC_CHUNK, SC_CHUNK)], rows_v)
            pltpu.sync_copy(dest_hbm.at[wid, j], idx_v)
            for kk in range(TOP_K):
                pltpu.sync_copy(rows_v, out_hbm.at[idx_v.at[kk]])

        pltpu.sync_copy(pad_hbm.at[wid], pad_v)
        for p in range(npad):
            pltpu.sync_copy(rows_v, out_hbm.at[pad_v.at[p]])

    return k(src, dest4, pad_idx)


def _sc_gather_rows(src, dest4, n):
    width = src.shape[1]
    nchunks = dest4.shape[1]

    @functools.partial(
        pl.kernel, mesh=_sc_mesh(),
        out_type=jax.ShapeDtypeStruct((TOP_K, n, width), src.dtype),
        scratch_types=[pltpu.VMEM((TOP_K, SC_CHUNK), I32), pltpu.VMEM((SC_CHUNK, width), src.dtype)],
        name="sc_gather_rows",
    )
    def k(src_hbm, dest_hbm, out_hbm, idx_v, rows_v):
        wid = _sc_worker()
        base = wid * (nchunks * SC_CHUNK)

        @pl.loop(0, nchunks)
        def _(j):
            pltpu.sync_copy(dest_hbm.at[wid, j], idx_v)
            for kk in range(TOP_K):
                pltpu.sync_copy(src_hbm.at[idx_v.at[kk]], rows_v)
                pltpu.sync_copy(rows_v, out_hbm.at[kk, pl.ds(base + j * SC_CHUNK, SC_CHUNK)])

    return k(src, dest4)


def _expert_kernel(be_ref, nu_ref, xs_ref, wgu_ref, bgu_ref, wd_ref, bd_ref, ys_ref, *, d_ff):
    i = pl.program_id(0)

    @pl.when(i < nu_ref[0])
    def _():
        lo, hi = _unpack_bf16_pair(xs_ref[...])
        xb = jnp.concatenate([lo.astype(BF16), hi.astype(BF16)], axis=1)
        gu = jnp.dot(xb, wgu_ref[0], preferred_element_type=F32) + bgu_ref[0]
        glu = jnp.minimum(gu[:, :d_ff], SWIGLU_LIMIT)
        lin = jnp.clip(gu[:, d_ff:], -SWIGLU_LIMIT, SWIGLU_LIMIT)
        act = (glu * jax.nn.sigmoid(SWIGLU_ALPHA * glu)) * (lin + 1.0)
        y = jnp.dot(act.astype(BF16), wd_ref[0], preferred_element_type=F32) + bd_ref[0]
        yb = y.astype(BF16).astype(F32)
        half = y.shape[1] // 2
        ys_ref[...] = _pack_bf16_pair(yb[:, :half], yb[:, half:])


def _experts(xs, block_e, n_used, wgu, bgu, wd, bd, *, tm, n_blocks):
    half = xs.shape[1]
    p = n_blocks * tm
    e, d_model, two_ff = wgu.shape
    d_ff = two_ff // 2
    kern = functools.partial(_expert_kernel, d_ff=d_ff)
    blk = lambda i, be, nu: (jnp.minimum(i, nu[0] - 1), 0)
    grid_spec = pltpu.PrefetchScalarGridSpec(
        num_scalar_prefetch=2,
        grid=(n_blocks,),
        in_specs=[
            pl.BlockSpec((tm, half), blk),
            pl.BlockSpec((1, d_model, two_ff), lambda i, be, nu: (be[i], 0, 0)),
            pl.BlockSpec((1, 1, two_ff), lambda i, be, nu: (be[i], 0, 0)),
            pl.BlockSpec((1, d_ff, d_model), lambda i, be, nu: (be[i], 0, 0)),
            pl.BlockSpec((1, 1, d_model), lambda i, be, nu: (be[i], 0, 0)),
        ],
        out_specs=pl.BlockSpec((tm, half), blk),
    )
    return pl.pallas_call(
        kern,
        grid_spec=grid_spec,
        out_shape=jax.ShapeDtypeStruct((p, half), U32),
        compiler_params=_params(("arbitrary",)),
        name="experts",
    )(block_e, n_used, xs, wgu, bgu.reshape(e, 1, two_ff), wd, bd.reshape(e, 1, d_model))


def _final_kernel(yk_ref, gate_ref, x1_ref, g_ref, out_ref):
    tm, d_model = x1_ref.shape
    half = d_model // 2
    gates = gate_ref[...]
    acc_lo = jnp.zeros((tm, half), F32)
    acc_hi = jnp.zeros((tm, half), F32)
    for kk in range(TOP_K):
        lo, hi = _unpack_bf16_pair(yk_ref[kk])
        gk = gates[:, kk:kk + 1]
        acc_lo = acc_lo + gk * lo
        acc_hi = acc_hi + gk * hi
    x2 = x1_ref[...] + jnp.concatenate([acc_lo, acc_hi], axis=1)
    ms = jnp.mean(x2 * x2, axis=-1, keepdims=True)
    out_ref[...] = (x2 * lax.rsqrt(ms + RMS_EPS)) * g_ref[...]


def _final(yk, gates, x1, g_final, *, tm):
    n, d_model = x1.shape
    rows = lambda w: pl.BlockSpec((tm, w), lambda i: (i, 0))
    return pl.pallas_call(
        _final_kernel,
        grid=(n // tm,),
        in_specs=[pl.BlockSpec((TOP_K, tm, d_model // 2), lambda i: (0, i, 0)), rows(LANES), rows(d_model),
                  pl.BlockSpec((1, d_model), lambda i: (0, 0))],
        out_specs=rows(d_model),
        out_shape=jax.ShapeDtypeStruct((n, d_model), F32),
        compiler_params=_params(("arbitrary",)),
        name="final",
    )(yk, gates, x1, g_final)


def _tiles(seq):
    tm = min(512, seq)
    return dict(tm_proj=tm, t_attn=min(256, seq), tm_mix=tm, tm_expert=256, tm_final=tm)


def _forward(x, g_mix, w_in, conv_w, b_f, w_conv_o, w_attn_o, w_out, g_ffn, w_router, b_router,
             w_gate_up, b_gate_up, w_down, b_down, g_final, tiles):
    batch, seq, d_model = x.shape
    n = batch * seq
    conv_dim = conv_w.shape[1]
    attn_dim = w_attn_o.shape[0]
    heads = b_f.shape[0]
    head_dim = attn_dim // heads
    n_experts = w_router.shape[1]
    x2d = x.reshape(n, d_model)

    c0 = 3 * conv_dim
    a0 = c0 + 3 * attn_dim
    wa = w_in[:, :c0].astype(BF16)
    scale = LOG2E / (head_dim ** 0.5)
    wqk = jnp.concatenate([w_in[:, c0:c0 + attn_dim] * scale, w_in[:, c0 + attn_dim:c0 + 2 * attn_dim]],
                          axis=1).astype(BF16)
    wvt = w_in[:, c0 + 2 * attn_dim:a0].T.astype(BF16)
    wf = jnp.pad(w_in[:, a0:a0 + heads], ((0, 0), (0, LANES - heads))).astype(BF16)
    wg = w_in[:, a0 + heads:].astype(BF16)
    bf_pad = jnp.pad(b_f, (0, LANES - heads)).reshape(1, LANES)
    wr_hi = w_router.astype(BF16)
    wr_lo = (w_router - wr_hi.astype(F32)).astype(BF16)
    wr = jnp.pad(jnp.concatenate([wr_hi, wr_lo], axis=1), ((0, 0), (0, LANES - 2 * n_experts)))
    br = jnp.pad(b_router, (0, LANES - n_experts)).reshape(1, LANES)

    cb, ucx, qa, ka, vt, sgc, sga = _inproj(
        x2d, g_mix.reshape(1, d_model), wa, wqk, wvt, wg, wf, bf_pad,
        batch=batch, seq=seq, heads=heads, tm=tiles["tm_proj"], tk=tiles["t_attn"])
    o = _attention(qa, ka, vt, batch=batch, seq=seq, heads=heads, tq=tiles["t_attn"])
    x1, h2p, topi, gates, rank, cnt = _mix(
        x2d, cb, ucx, o, sgc, sga, conv_w, w_conv_o.astype(BF16), w_attn_o.astype(BF16), w_out.astype(BF16),
        g_ffn.reshape(1, d_model), wr, br, seq=seq, n_experts=n_experts, tm=tiles["tm_mix"])

    tme = tiles["tm_expert"]
    m = n * TOP_K
    n_blocks = m // tme + n_experts
    counts = cnt[-1, 0, :n_experts].astype(I32)
    padded = ((counts + tme - 1) // tme) * tme
    pend = jnp.cumsum(padded)
    pstart = pend - padded
    n_used = (pend[-1] // tme).astype(I32).reshape(1)
    blk_start = jnp.minimum(jnp.arange(n_blocks, dtype=I32), n_used[0] - 1) * tme
    block_e = jnp.minimum(jnp.sum((pend[None, :] <= blk_start[:, None]).astype(I32), axis=1), n_experts - 1)
    e_sel = topi[:, :TOP_K]
    onehot = e_sel[:, :, None] == jnp.arange(n_experts, dtype=I32)[None, None, :]
    dest = jnp.sum(jnp.where(onehot, pstart[None, None, :], 0), axis=2) + rank[:, :TOP_K]
    nchunks = n // (SC_WORKERS * SC_CHUNK)
    dest4 = dest.reshape(SC_WORKERS, nchunks, SC_CHUNK, TOP_K).transpose(0, 1, 3, 2)
    p_rows = n_blocks * tme
    jj = jnp.arange(tme, dtype=I32)[None, :]
    spare = p_rows + jnp.arange(n_experts, dtype=I32)[:, None] * tme + jj
    pad_idx = jnp.where(jj < (padded - counts)[:, None], (pstart + counts)[:, None] + jj, spare)
    pad_idx = pad_idx.reshape(SC_WORKERS, (n_experts * tme) // (SC_WORKERS * SC_CHUNK), SC_CHUNK)

    xs = _sc_scatter_rows(h2p, dest4, pad_idx, p_rows + n_experts * tme)
    ys = _experts(xs, block_e, n_used, w_gate_up.astype(BF16), b_gate_up, w_down.astype(BF16), b_down,
                  tm=tme, n_blocks=n_blocks)
    yk = _sc_gather_rows(ys, dest4, n)
    out = _final(yk, gates, x1, g_final.reshape(1, d_model), tm=tiles["tm_final"])
    return out.reshape(batch, seq, d_model)


def kernel(x, g_mix, w_in, conv_w, b_f, w_conv_o, w_attn_o, w_out, g_ffn, w_router, b_router, w_gate_up,
           b_gate_up, w_down, b_down, g_final):
    return _forward(x, g_mix, w_in, conv_w, b_f, w_conv_o, w_attn_o, w_out, g_ffn, w_router, b_router,
                    w_gate_up, b_gate_up, w_down, b_down, g_final, _tiles(x.shape[1]))
```

```python
import functools

import jax
import jax.numpy as jnp
from jax import lax
from jax.experimental import pallas as pl
from jax.experimental.pallas import tpu as pltpu
from jax.experimental.pallas import tpu_sc as plsc

TOP_K = 4
RMS_EPS = 1e-5
SWIGLU_ALPHA = 1.702
SWIGLU_LIMIT = 7.0
LOG2E = 1.4426950408889634

LANES = 128
BF16_SUBLANES = 16
VMEM_LIMIT_BYTES = 56 * 1024 * 1024

F32 = jnp.float32
BF16 = jnp.bfloat16
U32 = jnp.uint32
I32 = jnp.int32
HI_MASK = 0xFFFF0000


def _params(sem):
    return pltpu.CompilerParams(dimension_semantics=sem, vmem_limit_bytes=VMEM_LIMIT_BYTES)


def _pack_bf16_pair(lo_f32, hi_f32):
    lo = lax.bitcast_convert_type(lo_f32, U32)
    hi = lax.bitcast_convert_type(hi_f32, U32)
    return (lo >> 16) | (hi & U32(HI_MASK))


def _unpack_bf16_pair(w):
    lo = lax.bitcast_convert_type(w << 16, F32)
    hi = lax.bitcast_convert_type(w & U32(HI_MASK), F32)
    return lo, hi


def _inproj_kernel(x_ref, g_ref, wa_ref, wqk_ref, wvt_ref, wg_ref, wf_ref, bf_ref,
                   cb_ref, ucx_ref, qa_ref, ka_ref, vt_ref, sgc_ref, sga_ref,
                   carry_ref, *, tiles_per_batch, conv_dim, attn_dim, d_model, heads, tk):
    tm = x_ref.shape[0]
    t = lax.rem(pl.program_id(0), tiles_per_batch)
    x = x_ref[...]
    ms = jnp.mean(x * x, axis=-1, keepdims=True)
    h = ((x * lax.rsqrt(ms + RMS_EPS)) * g_ref[...]).astype(BF16)

    def mm(w_ref, c0, n):
        return jnp.dot(h, w_ref[:, c0:c0 + n], preferred_element_type=F32)

    cb_ref[...] = mm(wa_ref, 0, conv_dim).astype(BF16)
    ucx_ref[...] = (mm(wa_ref, conv_dim, conv_dim) * mm(wa_ref, 2 * conv_dim, conv_dim)).astype(BF16)
    half = d_model // 2
    for c in range(2):
        sgc_ref[:, c * half:(c + 1) * half] = jax.nn.sigmoid(mm(wg_ref, c * half, half)).astype(BF16)
        sga_ref[:, c * half:(c + 1) * half] = jax.nn.sigmoid(mm(wg_ref, d_model + c * half, half)).astype(BF16)

    vt = lax.dot_general(wvt_ref[...], h, (((1,), (1,)), ((), ())), preferred_element_type=F32).astype(BF16)
    for u in range(tm // tk):
        vt_ref[u] = vt[:, u * tk:(u + 1) * tk]

    z = mm(wf_ref, 0, LANES) + bf_ref[...]
    lane = lax.broadcasted_iota(I32, z.shape, 1)
    logf = jnp.minimum(z, 0.0) - jnp.log1p(jnp.exp(-jnp.abs(z)))
    logf = jnp.where(lane < heads, logf, 0.0)
    p1 = logf.astype(BF16).astype(F32)
    r1 = logf - p1
    p2 = r1.astype(BF16).astype(F32)
    p3 = (r1 - p2).astype(BF16).astype(F32)
    packed = (p1 + pltpu.roll(p2, heads, 1) + pltpu.roll(p3, 2 * heads, 1)).astype(BF16)
    row = lax.broadcasted_iota(I32, (tm, tm), 0)
    col = lax.broadcasted_iota(I32, (tm, tm), 1)
    tri = jnp.where(col <= row, 1.0, 0.0).astype(BF16)
    r = jnp.dot(tri, packed, preferred_element_type=F32)
    local = r + pltpu.roll(r, LANES - heads, 1) + pltpu.roll(r, LANES - 2 * heads, 1)

    @pl.when(t == 0)
    def _():
        carry_ref[...] = jnp.zeros_like(carry_ref)

    c = jnp.where(lane < heads, local + carry_ref[0:1, :], 0.0)
    carry_ref[0:1, :] = c[tm - 1:tm, :]

    c2 = c * LOG2E
    c_hi = c2.astype(BF16).astype(F32)
    c_r = c2 - c_hi
    c_mid = c_r.astype(BF16).astype(F32)
    c_lo = (c_r - c_mid).astype(BF16).astype(F32)
    head_dim = attn_dim // heads
    ext = LANES - head_dim
    lane_e = lax.broadcasted_iota(I32, (tm, ext), 1)
    ones3 = jnp.where(lane_e < 3, 1.0, 0.0)
    qf = mm(wqk_ref, 0, attn_dim)
    kf = mm(wqk_ref, attn_dim, attn_dim)
    for hh in range(heads):
        kext = jnp.where(lane_e == 0, -c_hi[:, hh:hh + 1],
                         jnp.where(lane_e == 1, -c_mid[:, hh:hh + 1],
                                   jnp.where(lane_e == 2, -c_lo[:, hh:hh + 1], 0.0)))
        sl = slice(hh * head_dim, (hh + 1) * head_dim)
        qa_ref[:, hh * LANES:(hh + 1) * LANES] = jnp.concatenate([qf[:, sl], ones3], axis=1).astype(BF16)
        ka_ref[:, hh * LANES:(hh + 1) * LANES] = jnp.concatenate([kf[:, sl], kext], axis=1).astype(BF16)


def _inproj(x2d, g_mix, wa, wqk, wvt, wg, wf, bf_pad, *, batch, seq, heads, tm, tk):
    n, d_model = x2d.shape
    conv_dim = wa.shape[1] // 3
    attn_dim = wvt.shape[0]
    tiles_per_batch = seq // tm
    grid = (n // tm,)
    full = lambda shape: pl.BlockSpec(shape, lambda i: (0,) * len(shape))
    rows = lambda w: pl.BlockSpec((tm, w), lambda i: (i, 0))
    kern = functools.partial(_inproj_kernel, tiles_per_batch=tiles_per_batch, conv_dim=conv_dim,
                             attn_dim=attn_dim, d_model=d_model, heads=heads, tk=tk)
    out_shape = (
        jax.ShapeDtypeStruct((n, conv_dim), BF16),
        jax.ShapeDtypeStruct((n, conv_dim), BF16),
        jax.ShapeDtypeStruct((n, heads * LANES), BF16),
        jax.ShapeDtypeStruct((n, heads * LANES), BF16),
        jax.ShapeDtypeStruct((n // tk, attn_dim, tk), BF16),
        jax.ShapeDtypeStruct((n, d_model), BF16),
        jax.ShapeDtypeStruct((n, d_model), BF16),
    )
    out_specs = (
        rows(conv_dim), rows(conv_dim), rows(heads * LANES), rows(heads * LANES),
        pl.BlockSpec((tm // tk, attn_dim, tk), lambda i: (i, 0, 0)),
        rows(d_model), rows(d_model),
    )
    return pl.pallas_call(
        kern,
        grid=grid,
        in_specs=[rows(d_model), full((1, d_model)), full(wa.shape), full(wqk.shape), full(wvt.shape),
                  full(wg.shape), full(wf.shape), full((1, LANES))],
        out_specs=out_specs,
        out_shape=out_shape,
        scratch_shapes=[pltpu.VMEM((8, LANES), F32)],
        compiler_params=_params(("arbitrary",)),
        name="inproj",
    )(x2d, g_mix, wa, wqk, wvt, wg, wf, bf_pad)


def _attn_kernel(q_ref, k_ref, v_ref, o_ref, *, tq, tk, head_dim, heads):
    qi = pl.program_id(1)
    row = lax.broadcasted_iota(I32, (tk, tq), 0)
    col = lax.broadcasted_iota(I32, (tk, tq), 1)

    def block(j, carry, masked):
        off = pl.multiple_of(j * tk, tk)
        scores = []
        for hh in range(heads):
            qa = q_ref[:, hh * LANES:(hh + 1) * LANES]
            ka = k_ref[pl.ds(off, tk), hh * LANES:(hh + 1) * LANES]
            scores.append(lax.dot_general(ka, qa, (((1,), (1,)), ((), ())), preferred_element_type=F32))
        stats = []
        for hh in range(heads):
            m, l, _ = carry[hh]
            s = scores[hh]
            if masked:
                s = jnp.where(row <= col, s, -jnp.inf)
            m_new = jnp.maximum(m, jnp.max(s, axis=0, keepdims=True))
            alpha = jnp.exp2(m - m_new)
            p = jnp.exp2(s - m_new)
            l_new = alpha * l + jnp.sum(p, axis=0, keepdims=True)
            stats.append((m_new, l_new, alpha, p.astype(BF16)))
        out = []
        for hh in range(heads):
            m_new, l_new, alpha, p = stats[hh]
            vth = v_ref[j, hh * head_dim:(hh + 1) * head_dim, :]
            acc_new = alpha * carry[hh][2] + jnp.dot(vth, p, preferred_element_type=F32)
            out.append((m_new, l_new, acc_new))
        return tuple(out)

    init = tuple((jnp.full((1, tq), -jnp.inf, F32), jnp.zeros((1, tq), F32), jnp.zeros((head_dim, tq), F32))
                 for _ in range(heads))
    carry = lax.fori_loop(0, qi, lambda j, c: block(j, c, False), init)
    final = block(qi, carry, True)
    per_group = LANES // head_dim
    for g in range(heads // per_group):
        ot = jnp.concatenate([final[g * per_group + u][2] / final[g * per_group + u][1] for u in range(per_group)],
                             axis=0)
        o_ref[:, g * LANES:(g + 1) * LANES] = ot.T.astype(BF16)


def _attention(qa, ka, vt, *, batch, seq, heads, tq):
    n = qa.shape[0]
    _, attn_dim, tk = vt.shape
    head_dim = attn_dim // heads
    assert LANES % head_dim == 0 and tq == tk
    nq = seq // tq
    nk = seq // tk
    kern = functools.partial(_attn_kernel, tq=tq, tk=tk, head_dim=head_dim, heads=heads)
    return pl.pallas_call(
        kern,
        grid=(batch, nq),
        in_specs=[
            pl.BlockSpec((tq, heads * LANES), lambda b, qi: (b * nq + qi, 0)),
            pl.BlockSpec((seq, heads * LANES), lambda b, qi: (b, 0)),
            pl.BlockSpec((nk, attn_dim, tk), lambda b, qi: (b, 0, 0)),
        ],
        out_specs=pl.BlockSpec((tq, attn_dim), lambda b, qi: (b * nq + qi, 0)),
        out_shape=jax.ShapeDtypeStruct((n, attn_dim), BF16),
        compiler_params=_params(("arbitrary", "arbitrary")),
        name="attn",
    )(qa, ka, vt)


def _mix_kernel(x_ref, cb_ref, u_ref, uh_ref, o_ref, sgc_ref, sga_ref, cw_ref, wco_ref, wao_ref, wout_ref,
                gffn_ref, wr_ref, br_ref,
                x1_ref, h2p_ref, topi_ref, gate_ref, rank_ref, cnt_ref, carry_ref,
                *, tiles_per_batch, n_experts):
    tm, d_model = x_ref.shape
    i = pl.program_id(0)
    first = lax.rem(i, tiles_per_batch) == 0

    u = u_ref[...].astype(F32)
    halo = jnp.where(first, 0.0, uh_ref[...].astype(F32))
    ext = jnp.concatenate([halo, u], axis=0)
    hs = halo.shape[0]
    u1 = pltpu.roll(ext, 1, 0)[hs:]
    u2 = pltpu.roll(ext, 2, 0)[hs:]
    cw = cw_ref[...]
    conv = cw[0:1, :] * u2 + cw[1:2, :] * u1 + cw[2:3, :] * u
    yc = (cb_ref[...].astype(F32) * conv).astype(BF16)
    y_conv = jnp.dot(yc, wco_ref[...], preferred_element_type=F32)
    y_attn = jnp.dot(o_ref[...], wao_ref[...], preferred_element_type=F32)
    mixed = (sgc_ref[...].astype(F32) * y_conv + sga_ref[...].astype(F32) * y_attn).astype(BF16)
    x1 = x_ref[...] + jnp.dot(mixed, wout_ref[...], preferred_element_type=F32)
    x1_ref[...] = x1

    ms = jnp.mean(x1 * x1, axis=-1, keepdims=True)
    h2 = (x1 * lax.rsqrt(ms + RMS_EPS)) * gffn_ref[...]
    hb = h2.astype(BF16)
    hbf = hb.astype(F32)
    half = d_model // 2
    h2p_ref[...] = _pack_bf16_pair(hbf[:, :half], hbf[:, half:])

    hlo = (h2 - hbf).astype(BF16)
    nt_dims = (((1,), (1,)), ((), ()))
    rt = (lax.dot_general(wr_ref[...], hb, nt_dims, preferred_element_type=F32)
          + lax.dot_general(wr_ref[...], hlo, nt_dims, preferred_element_type=F32))
    lg = rt[:n_experts] + rt[n_experts:] + br_ref[...]
    sub = lax.broadcasted_iota(I32, (n_experts, tm), 0)

    vals, idxs = [], []
    for _ in range(TOP_K):
        m = jnp.max(lg, axis=0, keepdims=True)
        idx = jnp.min(jnp.where(lg == m, sub, n_experts), axis=0, keepdims=True)
        vals.append(m)
        idxs.append(idx)
        lg = jnp.where(sub == idx, -jnp.inf, lg)
    es = [jnp.exp(vk - vals[0]) for vk in vals]
    denom = es[0] + es[1] + es[2] + es[3]

    sub8 = lax.broadcasted_iota(I32, (8, tm), 0)
    subl = lax.broadcasted_iota(I32, (LANES, tm), 0)
    chosen = jnp.zeros((n_experts, tm), F32)
    topi = jnp.zeros((8, tm), I32)
    gates_t = jnp.zeros((LANES, tm), F32)
    for kk in range(TOP_K):
        chosen = jnp.where(sub == idxs[kk], 1.0, chosen)
        topi = jnp.where(sub8 == kk, idxs[kk], topi)
        gates_t = jnp.where(subl == kk, es[kk] / denom, gates_t)
    topi_ref[...] = topi
    gate_ref[...] = gates_t.T

    @pl.when(i == 0)
    def _():
        carry_ref[...] = jnp.zeros_like(carry_ref)

    row = lax.broadcasted_iota(I32, (tm, tm), 0)
    col = lax.broadcasted_iota(I32, (tm, tm), 1)
    tri = jnp.where(row < col, 1.0, 0.0).astype(BF16)
    before = jnp.dot(chosen.astype(BF16), tri, preferred_element_type=F32) + carry_ref[:, 0:1]
    rank = jnp.zeros((8, tm), F32)
    for kk in range(TOP_K):
        rk = jnp.sum(jnp.where(sub == idxs[kk], before, 0.0), axis=0, keepdims=True)
        rank = jnp.where(sub8 == kk, rk, rank)
    rank_ref[...] = rank.astype(I32)
    total = carry_ref[...] + jnp.sum(chosen, axis=1, keepdims=True)
    carry_ref[...] = total
    cnt_ref[0] = total


def _mix(x2d, cb, ucx, o, sgc, sga, conv_w, wco, wao, wout, g_ffn, wr, br, *, seq, n_experts, tm):
    n, d_model = x2d.shape
    conv_dim = cb.shape[1]
    attn_dim = o.shape[1]
    tiles_per_batch = seq // tm
    nt = n // tm
    hb = tm // BF16_SUBLANES
    full = lambda shape: pl.BlockSpec(shape, lambda i: (0,) * len(shape))
    rows = lambda w: pl.BlockSpec((tm, w), lambda i: (i, 0))
    kern = functools.partial(_mix_kernel, tiles_per_batch=tiles_per_batch, n_experts=n_experts)
    out_shape = (
        jax.ShapeDtypeStruct((n, d_model), F32),
        jax.ShapeDtypeStruct((n, d_model // 2), U32),
        jax.ShapeDtypeStruct((8, n), I32),
        jax.ShapeDtypeStruct((n, LANES), F32),
        jax.ShapeDtypeStruct((8, n), I32),
        jax.ShapeDtypeStruct((nt, n_experts, LANES), F32),
    )
    cols = pl.BlockSpec((8, tm), lambda i: (0, i))
    out_specs = (rows(d_model), rows(d_model // 2), cols, rows(LANES), cols,
                 pl.BlockSpec((1, n_experts, LANES), lambda i: (i, 0, 0)))
    return pl.pallas_call(
        kern,
        grid=(nt,),
        in_specs=[rows(d_model), rows(conv_dim), rows(conv_dim),
                  pl.BlockSpec((BF16_SUBLANES, conv_dim), lambda i: (jnp.maximum(i * hb - 1, 0), 0)),
                  rows(attn_dim), rows(d_model), rows(d_model),
                  full(conv_w.shape), full(wco.shape), full(wao.shape), full(wout.shape),
                  full((1, d_model)), full(wr.shape), full((n_experts, 1))],
        out_specs=out_specs,
        out_shape=out_shape,
        scratch_shapes=[pltpu.VMEM((n_experts, LANES), F32)],
        compiler_params=_params(("arbitrary",)),
        name="mix",
    )(x2d, cb, ucx, ucx, o, sgc, sga, conv_w, wco, wao, wout, g_ffn, wr, br)


SC_CORES = 2
SC_SUBCORES = 16
SC_WORKERS = SC_CORES * SC_SUBCORES
SC_CHUNK = 64


def _sc_mesh():
    return plsc.VectorSubcoreMesh(core_axis_name="c", subcore_axis_name="s",
                                  num_cores=SC_CORES, num_subcores=SC_SUBCORES)


def _sc_worker():
    return lax.axis_index("s") * SC_CORES + lax.axis_index("c")


def _sc_scatter_rows(src, dest4, pad_idx, n_out):
    n, width = src.shape
    nchunks = dest4.shape[1]
    npad = pad_idx.shape[1]

    @functools.partial(
        pl.kernel, mesh=_sc_mesh(),
        out_type=jax.ShapeDtypeStruct((n_out, width), src.dtype),
        scratch_types=[pltpu.VMEM((TOP_K, SC_CHUNK), I32), pltpu.VMEM((SC_CHUNK, width), src.dtype),
                       pltpu.VMEM((npad, SC_CHUNK), I32), pltpu.SemaphoreType.DMA],
        name="sc_scatter_rows",
    )
    def k(src_hbm, dest_hbm, pad_hbm, out_hbm, idx_v, rows_v, pad_v, sem):
        wid = _sc_worker()
        base = wid * (nchunks * SC_CHUNK)

        @pl.loop(0, nchunks)
        def _(j):
            pltpu.sync_copy(src_hbm.at[pl.ds(base + j * SC_CHUNK, SC_CHUNK)], rows_v)
            pltpu.sync_copy(dest_hbm.at[wid, j], idx_v)
            copies = [pltpu.async_copy(rows_v, out_hbm.at[idx_v.at[kk]], sem) for kk in range(TOP_K)]
            for cp in copies:
                cp.wait()

        pltpu.sync_copy(pad_hbm.at[wid], pad_v)
        for p in range(npad):
            pltpu.sync_copy(rows_v, out_hbm.at[pad_v.at[p]])

    return k(src, dest4, pad_idx)


def _sc_gather_rows(src, dest4, n):
    width = src.shape[1]
    nchunks = dest4.shape[1]

    @functools.partial(
        pl.kernel, mesh=_sc_mesh(),
        out_type=jax.ShapeDtypeStruct((TOP_K, n, width), src.dtype),
        scratch_types=[pltpu.VMEM((TOP_K, SC_CHUNK), I32), pltpu.VMEM((SC_CHUNK, width), src.dtype),
                       pltpu.VMEM((SC_CHUNK, width), src.dtype), pltpu.SemaphoreType.DMA, pltpu.SemaphoreType.DMA],
        name="sc_gather_rows",
    )
    def k(src_hbm, dest_hbm, out_hbm, idx_v, rows_a, rows_b, sem_a, sem_b):
        wid = _sc_worker()
        base = wid * (nchunks * SC_CHUNK)
        bufs = ((rows_a, sem_a), (rows_b, sem_b))

        @pl.loop(0, nchunks)
        def _(j):
            pltpu.sync_copy(dest_hbm.at[wid, j], idx_v)
            pending = pltpu.async_copy(src_hbm.at[idx_v.at[0]], rows_a, sem_a)
            for kk in range(TOP_K):
                buf, _ = bufs[kk % 2]
                pending.wait()
                if kk + 1 < TOP_K:
                    nbuf, nsem = bufs[(kk + 1) % 2]
                    pending = pltpu.async_copy(src_hbm.at[idx_v.at[kk + 1]], nbuf, nsem)
                pltpu.sync_copy(buf, out_hbm.at[kk, pl.ds(base + j * SC_CHUNK, SC_CHUNK)])

    return k(src, dest4)


def _expert_kernel(be_ref, nu_ref, xs_ref, wgu_ref, bgu_ref, wd_ref, bd_ref, ys_ref, wgu_bf, wd_bf, *, d_ff, ff_chunk):
    i = pl.program_id(0)

    @pl.when((i == 0) | (be_ref[i] != be_ref[jnp.maximum(i - 1, 0)]))
    def _():
        wgu_bf[...] = wgu_ref[0].astype(BF16)
        wd_bf[...] = wd_ref[0].astype(BF16)

    @pl.when(i < nu_ref[0])
    def _():
        lo, hi = _unpack_bf16_pair(xs_ref[...])
        xb = jnp.concatenate([lo.astype(BF16), hi.astype(BF16)], axis=1)
        y = bd_ref[0]
        for c0 in range(0, d_ff, ff_chunk):
            glu = jnp.dot(xb, wgu_bf[:, c0:c0 + ff_chunk], preferred_element_type=F32) + bgu_ref[0, :, c0:c0 + ff_chunk]
            lin = (jnp.dot(xb, wgu_bf[:, d_ff + c0:d_ff + c0 + ff_chunk], preferred_element_type=F32)
                   + bgu_ref[0, :, d_ff + c0:d_ff + c0 + ff_chunk])
            glu = jnp.minimum(glu, SWIGLU_LIMIT)
            lin = jnp.clip(lin, -SWIGLU_LIMIT, SWIGLU_LIMIT)
            act = (glu * jax.nn.sigmoid(SWIGLU_ALPHA * glu)) * (lin + 1.0)
            y = y + jnp.dot(act.astype(BF16), wd_bf[c0:c0 + ff_chunk, :], preferred_element_type=F32)
        yb = y.astype(BF16).astype(F32)
        half = y.shape[1] // 2
        ys_ref[...] = _pack_bf16_pair(yb[:, :half], yb[:, half:])


def _experts(xs, block_e, n_used, wgu, bgu, wd, bd, *, tm, n_blocks):
    half = xs.shape[1]
    p = n_blocks * tm
    e, d_model, two_ff = wgu.shape
    d_ff = two_ff // 2
    kern = functools.partial(_expert_kernel, d_ff=d_ff, ff_chunk=min(512, d_ff))
    blk = lambda i, be, nu: (jnp.minimum(i, nu[0] - 1), 0)
    grid_spec = pltpu.PrefetchScalarGridSpec(
        num_scalar_prefetch=2,
        grid=(n_blocks,),
        in_specs=[
            pl.BlockSpec((tm, half), blk),
            pl.BlockSpec((1, d_model, two_ff), lambda i, be, nu: (be[i], 0, 0)),
            pl.BlockSpec((1, 1, two_ff), lambda i, be, nu: (be[i], 0, 0)),
            pl.BlockSpec((1, d_ff, d_model), lambda i, be, nu: (be[i], 0, 0)),
            pl.BlockSpec((1, 1, d_model), lambda i, be, nu: (be[i], 0, 0)),
        ],
        out_specs=pl.BlockSpec((tm, half), blk),
        scratch_shapes=[pltpu.VMEM((d_model, two_ff), BF16), pltpu.VMEM((d_ff, d_model), BF16)],
    )
    return pl.pallas_call(
        kern,
        grid_spec=grid_spec,
        out_shape=jax.ShapeDtypeStruct((p, half), U32),
        compiler_params=_params(("arbitrary",)),
        name="experts",
    )(block_e, n_used, xs, wgu, bgu.reshape(e, 1, two_ff), wd, bd.reshape(e, 1, d_model))


def _final_kernel(yk_ref, gate_ref, x1_ref, g_ref, out_ref):
    tm, d_model = x1_ref.shape
    half = d_model // 2
    gates = gate_ref[...]
    acc_lo = jnp.zeros((tm, half), F32)
    acc_hi = jnp.zeros((tm, half), F32)
    for kk in range(TOP_K):
        lo, hi = _unpack_bf16_pair(yk_ref[kk])
        gk = gates[:, kk:kk + 1]
        acc_lo = acc_lo + gk * lo
        acc_hi = acc_hi + gk * hi
    x2 = x1_ref[...] + jnp.concatenate([acc_lo, acc_hi], axis=1)
    ms = jnp.mean(x2 * x2, axis=-1, keepdims=True)
    out_ref[...] = (x2 * lax.rsqrt(ms + RMS_EPS)) * g_ref[...]


def _final(yk, gates, x1, g_final, *, tm):
    n, d_model = x1.shape
    rows = lambda w: pl.BlockSpec((tm, w), lambda i: (i, 0))
    return pl.pallas_call(
        _final_kernel,
        grid=(n // tm,),
        in_specs=[pl.BlockSpec((TOP_K, tm, d_model // 2), lambda i: (0, i, 0)), rows(LANES), rows(d_model),
                  pl.BlockSpec((1, d_model), lambda i: (0, 0))],
        out_specs=rows(d_model),
        out_shape=jax.ShapeDtypeStruct((n, d_model), F32),
        compiler_params=_params(("arbitrary",)),
        name="final",
    )(yk, gates, x1, g_final)


def _tiles(seq):
    tm = min(512, seq)
    return dict(tm_proj=tm, t_attn=min(256, seq), tm_mix=tm, tm_expert=512, tm_final=tm)


def _forward(x, g_mix, w_in, conv_w, b_f, w_conv_o, w_attn_o, w_out, g_ffn, w_router, b_router,
             w_gate_up, b_gate_up, w_down, b_down, g_final, tiles):
    batch, seq, d_model = x.shape
    n = batch * seq
    conv_dim = conv_w.shape[1]
    attn_dim = w_attn_o.shape[0]
    heads = b_f.shape[0]
    head_dim = attn_dim // heads
    n_experts = w_router.shape[1]
    x2d = x.reshape(n, d_model)

    c0 = 3 * conv_dim
    a0 = c0 + 3 * attn_dim
    wa = w_in[:, :c0].astype(BF16)
    scale = LOG2E / (head_dim ** 0.5)
    wqk = jnp.concatenate([w_in[:, c0:c0 + attn_dim] * scale, w_in[:, c0 + attn_dim:c0 + 2 * attn_dim]],
                          axis=1).astype(BF16)
    wvt = w_in[:, c0 + 2 * attn_dim:a0].T.astype(BF16)
    wf = jnp.pad(w_in[:, a0:a0 + heads], ((0, 0), (0, LANES - heads))).astype(BF16)
    wg = w_in[:, a0 + heads:].astype(BF16)
    bf_pad = jnp.pad(b_f, (0, LANES - heads)).reshape(1, LANES)
    wr_hi = w_router.astype(BF16)
    wr_lo = (w_router - wr_hi.astype(F32)).astype(BF16)
    wr = jnp.concatenate([wr_hi, wr_lo], axis=1).T
    br = b_router.reshape(n_experts, 1)

    cb, ucx, qa, ka, vt, sgc, sga = _inproj(
        x2d, g_mix.reshape(1, d_model), wa, wqk, wvt, wg, wf, bf_pad,
        batch=batch, seq=seq, heads=heads, tm=tiles["tm_proj"], tk=tiles["t_attn"])
    o = _attention(qa, ka, vt, batch=batch, seq=seq, heads=heads, tq=tiles["t_attn"])
    x1, h2p, topi, gates, rank, cnt = _mix(
        x2d, cb, ucx, o, sgc, sga, conv_w, w_conv_o.astype(BF16), w_attn_o.astype(BF16), w_out.astype(BF16),
        g_ffn.reshape(1, d_model), wr, br, seq=seq, n_experts=n_experts, tm=tiles["tm_mix"])

    tme = tiles["tm_expert"]
    m = n * TOP_K
    n_blocks = m // tme + n_experts
    counts = cnt[-1, :, 0].astype(I32)
    padded = ((counts + tme - 1) // tme) * tme
    pend = jnp.cumsum(padded)
    pstart = pend - padded
    n_used = (pend[-1] // tme).astype(I32).reshape(1)
    blk_start = jnp.minimum(jnp.arange(n_blocks, dtype=I32), n_used[0] - 1) * tme
    block_e = jnp.minimum(jnp.sum((pend[None, :] <= blk_start[:, None]).astype(I32), axis=1), n_experts - 1)
    e_sel = topi[:TOP_K]
    onehot = e_sel[:, :, None] == jnp.arange(n_experts, dtype=I32)[None, None, :]
    dest = jnp.sum(jnp.where(onehot, pstart[None, None, :], 0), axis=2) + rank[:TOP_K]
    nchunks = n // (SC_WORKERS * SC_CHUNK)
    dest4 = dest.reshape(TOP_K, SC_WORKERS, nchunks, SC_CHUNK).transpose(1, 2, 0, 3)
    p_rows = n_blocks * tme
    jj = jnp.arange(tme, dtype=I32)[None, :]
    spare = p_rows + jnp.arange(n_experts, dtype=I32)[:, None] * tme + jj
    pad_idx = jnp.where(jj < (padded - counts)[:, None], (pstart + counts)[:, None] + jj, spare)
    pad_idx = pad_idx.reshape(SC_WORKERS, (n_experts * tme) // (SC_WORKERS * SC_CHUNK), SC_CHUNK)

    xs = _sc_scatter_rows(h2p, dest4, pad_idx, p_rows + n_experts * tme)
    ys = _experts(xs, block_e, n_used, w_gate_up, b_gate_up, w_down, b_down, tm=tme, n_blocks=n_blocks)
    yk = _sc_gather_rows(ys, dest4, n)
    out = _final(yk, gates, x1, g_final.reshape(1, d_model), tm=tiles["tm_final"])
    return out.reshape(batch, seq, d_model)


def kernel(x, g_mix, w_in, conv_w, b_f, w_conv_o, w_attn_o, w_out, g_ffn, w_router, b_router, w_gate_up,
           b_gate_up, w_down, b_down, g_final):
    return _forward(x, g_mix, w_in, conv_w, b_f, w_conv_o, w_attn_o, w_out, g_ffn, w_router, b_router,
                    w_gate_up, b_gate_up, w_down, b_down, g_final, _tiles(x.shape[1]))
```

```python
import functools

import jax
import jax.numpy as jnp
from jax import lax
from jax.experimental import pallas as pl
from jax.experimental.pallas import tpu as pltpu
from jax.experimental.pallas import tpu_sc as plsc

TOP_K = 4
RMS_EPS = 1e-5
SWIGLU_ALPHA = 1.702
SWIGLU_LIMIT = 7.0
LOG2E = 1.4426950408889634

LANES = 128
BF16_SUBLANES = 16
VMEM_LIMIT_BYTES = 56 * 1024 * 1024

F32 = jnp.float32
BF16 = jnp.bfloat16
U32 = jnp.uint32
I32 = jnp.int32
HI_MASK = 0xFFFF0000


def _params(sem):
    return pltpu.CompilerParams(dimension_semantics=sem, vmem_limit_bytes=VMEM_LIMIT_BYTES)


def _pack_bf16_pair(lo_f32, hi_f32):
    lo = lax.bitcast_convert_type(lo_f32, U32)
    hi = lax.bitcast_convert_type(hi_f32, U32)
    return (lo >> 16) | (hi & U32(HI_MASK))


def _unpack_bf16_pair(w):
    lo = lax.bitcast_convert_type(w << 16, F32)
    hi = lax.bitcast_convert_type(w & U32(HI_MASK), F32)
    return lo, hi


def _inproj_kernel(x_ref, g_ref, wa_ref, wqk_ref, wvt_ref, wg_ref, wf_ref, bf_ref,
                   cb_ref, ucx_ref, qa_ref, ka_ref, vt_ref, sgc_ref, sga_ref,
                   carry_ref, *, tiles_per_batch, conv_dim, attn_dim, d_model, heads, tk):
    tm = x_ref.shape[0]
    t = lax.rem(pl.program_id(0), tiles_per_batch)
    x = x_ref[...]
    ms = jnp.mean(x * x, axis=-1, keepdims=True)
    h = ((x * lax.rsqrt(ms + RMS_EPS)) * g_ref[...]).astype(BF16)

    def mm(w_ref, c0, n):
        return jnp.dot(h, w_ref[:, c0:c0 + n], preferred_element_type=F32)

    cb_ref[...] = mm(wa_ref, 0, conv_dim).astype(BF16)
    ucx_ref[...] = (mm(wa_ref, conv_dim, conv_dim) * mm(wa_ref, 2 * conv_dim, conv_dim)).astype(BF16)
    half = d_model // 2
    for c in range(2):
        sgc_ref[:, c * half:(c + 1) * half] = jax.nn.sigmoid(mm(wg_ref, c * half, half)).astype(BF16)
        sga_ref[:, c * half:(c + 1) * half] = jax.nn.sigmoid(mm(wg_ref, d_model + c * half, half)).astype(BF16)

    vt = lax.dot_general(wvt_ref[...], h, (((1,), (1,)), ((), ())), preferred_element_type=F32).astype(BF16)
    for u in range(tm // tk):
        vt_ref[u] = vt[:, u * tk:(u + 1) * tk]

    z = mm(wf_ref, 0, LANES) + bf_ref[...]
    lane = lax.broadcasted_iota(I32, z.shape, 1)
    logf = jnp.minimum(z, 0.0) - jnp.log1p(jnp.exp(-jnp.abs(z)))
    logf = jnp.where(lane < heads, logf, 0.0)
    p1 = logf.astype(BF16).astype(F32)
    r1 = logf - p1
    p2 = r1.astype(BF16).astype(F32)
    p3 = (r1 - p2).astype(BF16).astype(F32)
    packed = (p1 + pltpu.roll(p2, heads, 1) + pltpu.roll(p3, 2 * heads, 1)).astype(BF16)
    row = lax.broadcasted_iota(I32, (tm, tm), 0)
    col = lax.broadcasted_iota(I32, (tm, tm), 1)
    tri = jnp.where(col <= row, 1.0, 0.0).astype(BF16)
    r = jnp.dot(tri, packed, preferred_element_type=F32)
    local = r + pltpu.roll(r, LANES - heads, 1) + pltpu.roll(r, LANES - 2 * heads, 1)

    @pl.when(t == 0)
    def _():
        carry_ref[...] = jnp.zeros_like(carry_ref)

    c = jnp.where(lane < heads, local + carry_ref[0:1, :], 0.0)
    carry_ref[0:1, :] = c[tm - 1:tm, :]

    c2 = c * LOG2E
    c_hi = c2.astype(BF16).astype(F32)
    c_r = c2 - c_hi
    c_mid = c_r.astype(BF16).astype(F32)
    c_lo = (c_r - c_mid).astype(BF16).astype(F32)
    head_dim = attn_dim // heads
    ext = LANES - head_dim
    lane_e = lax.broadcasted_iota(I32, (tm, ext), 1)
    ones3 = jnp.where(lane_e < 3, 1.0, 0.0)
    qf = mm(wqk_ref, 0, attn_dim)
    kf = mm(wqk_ref, attn_dim, attn_dim)
    for hh in range(heads):
        kext = jnp.where(lane_e == 0, -c_hi[:, hh:hh + 1],
                         jnp.where(lane_e == 1, -c_mid[:, hh:hh + 1],
                                   jnp.where(lane_e == 2, -c_lo[:, hh:hh + 1], 0.0)))
        sl = slice(hh * head_dim, (hh + 1) * head_dim)
        qa_ref[:, hh * LANES:(hh + 1) * LANES] = jnp.concatenate([qf[:, sl], ones3], axis=1).astype(BF16)
        ka_ref[:, hh * LANES:(hh + 1) * LANES] = jnp.concatenate([kf[:, sl], kext], axis=1).astype(BF16)


def _inproj(x2d, g_mix, wa, wqk, wvt, wg, wf, bf_pad, *, batch, seq, heads, tm, tk):
    n, d_model = x2d.shape
    conv_dim = wa.shape[1] // 3
    attn_dim = wvt.shape[0]
    tiles_per_batch = seq // tm
    grid = (n // tm,)
    full = lambda shape: pl.BlockSpec(shape, lambda i: (0,) * len(shape))
    rows = lambda w: pl.BlockSpec((tm, w), lambda i: (i, 0))
    kern = functools.partial(_inproj_kernel, tiles_per_batch=tiles_per_batch, conv_dim=conv_dim,
                             attn_dim=attn_dim, d_model=d_model, heads=heads, tk=tk)
    out_shape = (
        jax.ShapeDtypeStruct((n, conv_dim), BF16),
        jax.ShapeDtypeStruct((n, conv_dim), BF16),
        jax.ShapeDtypeStruct((n, heads * LANES), BF16),
        jax.ShapeDtypeStruct((n, heads * LANES), BF16),
        jax.ShapeDtypeStruct((n // tk, attn_dim, tk), BF16),
        jax.ShapeDtypeStruct((n, d_model), BF16),
        jax.ShapeDtypeStruct((n, d_model), BF16),
    )
    out_specs = (
        rows(conv_dim), rows(conv_dim), rows(heads * LANES), rows(heads * LANES),
        pl.BlockSpec((tm // tk, attn_dim, tk), lambda i: (i, 0, 0)),
        rows(d_model), rows(d_model),
    )
    return pl.pallas_call(
        kern,
        grid=grid,
        in_specs=[rows(d_model), full((1, d_model)), full(wa.shape), full(wqk.shape), full(wvt.shape),
                  full(wg.shape), full(wf.shape), full((1, LANES))],
        out_specs=out_specs,
        out_shape=out_shape,
        scratch_shapes=[pltpu.VMEM((8, LANES), F32)],
        compiler_params=_params(("arbitrary",)),
        name="inproj",
    )(x2d, g_mix, wa, wqk, wvt, wg, wf, bf_pad)


def _attn_kernel(q_ref, k_ref, v_ref, o_ref, s_ref, *, tq, tk, head_dim, heads):
    qi = pl.program_id(1)
    row = lax.broadcasted_iota(I32, (tk, tq), 0)
    col = lax.broadcasted_iota(I32, (tk, tq), 1)

    def score_tile(j, slot):
        off = pl.multiple_of(j * tk, tk)
        for hh in range(heads):
            qa = q_ref[:, hh * LANES:(hh + 1) * LANES]
            ka = k_ref[pl.ds(off, tk), hh * LANES:(hh + 1) * LANES]
            s_ref[slot, hh] = lax.dot_general(ka, qa, (((1,), (1,)), ((), ())), preferred_element_type=F32)

    def block(j, slot, carry, masked):
        if not masked:
            score_tile(j + 1, 1 - slot)
        stats = []
        for hh in range(heads):
            m, l, _ = carry[hh]
            s = s_ref[slot, hh]
            if masked:
                s = jnp.where(row <= col, s, -jnp.inf)
            m_new = jnp.maximum(m, jnp.max(s, axis=0, keepdims=True))
            alpha = jnp.exp2(m - m_new)
            p = jnp.exp2(s - m_new)
            l_new = alpha * l + jnp.sum(p, axis=0, keepdims=True)
            stats.append((m_new, l_new, alpha, p.astype(BF16)))
        out = []
        for hh in range(heads):
            m_new, l_new, alpha, p = stats[hh]
            vth = v_ref[j, hh * head_dim:(hh + 1) * head_dim, :]
            acc_new = alpha * carry[hh][2] + jnp.dot(vth, p, preferred_element_type=F32)
            out.append((m_new, l_new, acc_new))
        return tuple(out)

    init = tuple((jnp.full((1, tq), -jnp.inf, F32), jnp.zeros((1, tq), F32), jnp.zeros((head_dim, tq), F32))
                 for _ in range(heads))
    score_tile(0, 0)

    def pair(p, c):
        return block(2 * p + 1, 1, block(2 * p, 0, c, False), False)

    carry = lax.fori_loop(0, qi // 2, pair, init)
    final = lax.cond(
        lax.rem(qi, 2) == 0,
        lambda c: block(qi, 0, c, True),
        lambda c: block(qi, 1, block(qi - 1, 0, c, False), True),
        carry)
    per_group = LANES // head_dim
    for g in range(heads // per_group):
        ot = jnp.concatenate([final[g * per_group + u][2] / final[g * per_group + u][1] for u in range(per_group)],
                             axis=0)
        o_ref[:, g * LANES:(g + 1) * LANES] = ot.T.astype(BF16)


def _attention(qa, ka, vt, *, batch, seq, heads, tq):
    n = qa.shape[0]
    _, attn_dim, tk = vt.shape
    head_dim = attn_dim // heads
    assert LANES % head_dim == 0 and tq == tk
    nq = seq // tq
    nk = seq // tk
    kern = functools.partial(_attn_kernel, tq=tq, tk=tk, head_dim=head_dim, heads=heads)
    return pl.pallas_call(
        kern,
        grid=(batch, nq),
        in_specs=[
            pl.BlockSpec((tq, heads * LANES), lambda b, qi: (b * nq + qi, 0)),
            pl.BlockSpec((seq, heads * LANES), lambda b, qi: (b, 0)),
            pl.BlockSpec((nk, attn_dim, tk), lambda b, qi: (b, 0, 0)),
        ],
        out_specs=pl.BlockSpec((tq, attn_dim), lambda b, qi: (b * nq + qi, 0)),
        out_shape=jax.ShapeDtypeStruct((n, attn_dim), BF16),
        scratch_shapes=[pltpu.VMEM((2, heads, tk, tq), F32)],
        compiler_params=_params(("arbitrary", "arbitrary")),
        name="attn",
    )(qa, ka, vt)


def _mix_kernel(x_ref, cb_ref, u_ref, uh_ref, o_ref, sgc_ref, sga_ref, cw_ref, wco_ref, wao_ref, wout_ref,
                gffn_ref, wr_ref, br_ref,
                x1_ref, h2p_ref, topi_ref, gate_ref, rank_ref, cnt_ref, carry_ref,
                *, tiles_per_batch, n_experts):
    tm, d_model = x_ref.shape
    i = pl.program_id(0)
    first = lax.rem(i, tiles_per_batch) == 0

    u = u_ref[...].astype(F32)
    halo = jnp.where(first, 0.0, uh_ref[...].astype(F32))
    ext = jnp.concatenate([halo, u], axis=0)
    hs = halo.shape[0]
    u1 = pltpu.roll(ext, 1, 0)[hs:]
    u2 = pltpu.roll(ext, 2, 0)[hs:]
    cw = cw_ref[...]
    conv = cw[0:1, :] * u2 + cw[1:2, :] * u1 + cw[2:3, :] * u
    yc = (cb_ref[...].astype(F32) * conv).astype(BF16)
    y_conv = jnp.dot(yc, wco_ref[...], preferred_element_type=F32)
    y_attn = jnp.dot(o_ref[...], wao_ref[...], preferred_element_type=F32)
    mixed = (sgc_ref[...].astype(F32) * y_conv + sga_ref[...].astype(F32) * y_attn).astype(BF16)
    x1 = x_ref[...] + jnp.dot(mixed, wout_ref[...], preferred_element_type=F32)
    x1_ref[...] = x1

    ms = jnp.mean(x1 * x1, axis=-1, keepdims=True)
    h2 = (x1 * lax.rsqrt(ms + RMS_EPS)) * gffn_ref[...]
    hb = h2.astype(BF16)
    hbf = hb.astype(F32)
    half = d_model // 2
    h2p_ref[...] = _pack_bf16_pair(hbf[:, :half], hbf[:, half:])

    hlo = (h2 - hbf).astype(BF16)
    nt_dims = (((1,), (1,)), ((), ()))
    rt = (lax.dot_general(wr_ref[...], hb, nt_dims, preferred_element_type=F32)
          + lax.dot_general(wr_ref[...], hlo, nt_dims, preferred_element_type=F32))
    lg = rt[:n_experts] + rt[n_experts:] + br_ref[...]
    sub = lax.broadcasted_iota(I32, (n_experts, tm), 0)

    vals, idxs = [], []
    for _ in range(TOP_K):
        m = jnp.max(lg, axis=0, keepdims=True)
        idx = jnp.min(jnp.where(lg == m, sub, n_experts), axis=0, keepdims=True)
        vals.append(m)
        idxs.append(idx)
        lg = jnp.where(sub == idx, -jnp.inf, lg)
    es = [jnp.exp(vk - vals[0]) for vk in vals]
    denom = es[0] + es[1] + es[2] + es[3]

    sub8 = lax.broadcasted_iota(I32, (8, tm), 0)
    subl = lax.broadcasted_iota(I32, (LANES, tm), 0)
    chosen = jnp.zeros((n_experts, tm), F32)
    topi = jnp.zeros((8, tm), I32)
    gates_t = jnp.zeros((LANES, tm), F32)
    for kk in range(TOP_K):
        chosen = jnp.where(sub == idxs[kk], 1.0, chosen)
        topi = jnp.where(sub8 == kk, idxs[kk], topi)
        gates_t = jnp.where(subl == kk, es[kk] / denom, gates_t)
    topi_ref[...] = topi
    gate_ref[...] = gates_t.T

    @pl.when(i == 0)
    def _():
        carry_ref[...] = jnp.zeros_like(carry_ref)

    row = lax.broadcasted_iota(I32, (tm, tm), 0)
    col = lax.broadcasted_iota(I32, (tm, tm), 1)
    tri = jnp.where(row < col, 1.0, 0.0).astype(BF16)
    before = jnp.dot(chosen.astype(BF16), tri, preferred_element_type=F32) + carry_ref[:, 0:1]
    rank = jnp.zeros((8, tm), F32)
    for kk in range(TOP_K):
        rk = jnp.sum(jnp.where(sub == idxs[kk], before, 0.0), axis=0, keepdims=True)
        rank = jnp.where(sub8 == kk, rk, rank)
    rank_ref[...] = rank.astype(I32)
    total = carry_ref[...] + jnp.sum(chosen, axis=1, keepdims=True)
    carry_ref[...] = total
    cnt_ref[0] = total


def _mix(x2d, cb, ucx, o, sgc, sga, conv_w, wco, wao, wout, g_ffn, wr, br, *, seq, n_experts, tm, tile0, nt):
    d_model = x2d.shape[1]
    n = nt * tm
    conv_dim = cb.shape[1]
    attn_dim = o.shape[1]
    tiles_per_batch = seq // tm
    assert tile0 % tiles_per_batch == 0
    hb = tm // BF16_SUBLANES
    full = lambda shape: pl.BlockSpec(shape, lambda i: (0,) * len(shape))
    rows_in = lambda w: pl.BlockSpec((tm, w), lambda i: (i + tile0, 0))
    rows = lambda w: pl.BlockSpec((tm, w), lambda i: (i, 0))
    kern = functools.partial(_mix_kernel, tiles_per_batch=tiles_per_batch, n_experts=n_experts)
    out_shape = (
        jax.ShapeDtypeStruct((n, d_model), F32),
        jax.ShapeDtypeStruct((n, d_model // 2), U32),
        jax.ShapeDtypeStruct((8, n), I32),
        jax.ShapeDtypeStruct((n, LANES), F32),
        jax.ShapeDtypeStruct((8, n), I32),
        jax.ShapeDtypeStruct((nt, n_experts, LANES), F32),
    )
    cols = pl.BlockSpec((8, tm), lambda i: (0, i))
    out_specs = (rows(d_model), rows(d_model // 2), cols, rows(LANES), cols,
                 pl.BlockSpec((1, n_experts, LANES), lambda i: (i, 0, 0)))
    return pl.pallas_call(
        kern,
        grid=(nt,),
        in_specs=[rows_in(d_model), rows_in(conv_dim), rows_in(conv_dim),
                  pl.BlockSpec((BF16_SUBLANES, conv_dim), lambda i: (jnp.maximum((i + tile0) * hb - 1, 0), 0)),
                  rows_in(attn_dim), rows_in(d_model), rows_in(d_model),
                  full(conv_w.shape), full(wco.shape), full(wao.shape), full(wout.shape),
                  full((1, d_model)), full(wr.shape), full((n_experts, 1))],
        out_specs=out_specs,
        out_shape=out_shape,
        scratch_shapes=[pltpu.VMEM((n_experts, LANES), F32)],
        compiler_params=_params(("arbitrary",)),
        name="mix",
    )(x2d, cb, ucx, ucx, o, sgc, sga, conv_w, wco, wao, wout, g_ffn, wr, br)


SC_CORES = 2
SC_SUBCORES = 16
SC_WORKERS = SC_CORES * SC_SUBCORES
SC_CHUNK = 64


def _sc_mesh():
    return plsc.VectorSubcoreMesh(core_axis_name="c", subcore_axis_name="s",
                                  num_cores=SC_CORES, num_subcores=SC_SUBCORES)


def _sc_worker():
    return lax.axis_index("s") * SC_CORES + lax.axis_index("c")


def _sc_scatter_rows(src, dest4, pad_idx, n_out):
    n, width = src.shape
    nchunks = dest4.shape[1]
    npad = pad_idx.shape[1]

    @functools.partial(
        pl.kernel, mesh=_sc_mesh(),
        out_type=jax.ShapeDtypeStruct((n_out, width), src.dtype),
        scratch_types=[pltpu.VMEM((TOP_K, SC_CHUNK), I32), pltpu.VMEM((SC_CHUNK, width), src.dtype),
                       pltpu.VMEM((npad, SC_CHUNK), I32), pltpu.SemaphoreType.DMA],
        name="sc_scatter_rows",
    )
    def k(src_hbm, dest_hbm, pad_hbm, out_hbm, idx_v, rows_v, pad_v, sem):
        wid = _sc_worker()
        base = wid * (nchunks * SC_CHUNK)

        @pl.loop(0, nchunks)
        def _(j):
            pltpu.sync_copy(src_hbm.at[pl.ds(base + j * SC_CHUNK, SC_CHUNK)], rows_v)
            pltpu.sync_copy(dest_hbm.at[wid, j], idx_v)
            copies = [pltpu.async_copy(rows_v, out_hbm.at[idx_v.at[kk]], sem) for kk in range(TOP_K)]
            for cp in copies:
                cp.wait()

        pltpu.sync_copy(pad_hbm.at[wid], pad_v)
        fills = [pltpu.async_copy(rows_v, out_hbm.at[pad_v.at[p]], sem) for p in range(npad)]
        for cp in fills:
            cp.wait()

    return k(src, dest4, pad_idx)


def _sc_gather_rows(src, dest4, n):
    width = src.shape[1]
    nchunks = dest4.shape[1]

    @functools.partial(
        pl.kernel, mesh=_sc_mesh(),
        out_type=jax.ShapeDtypeStruct((TOP_K, n, width), src.dtype),
        scratch_types=[pltpu.VMEM((TOP_K, SC_CHUNK), I32), pltpu.VMEM((SC_CHUNK, width), src.dtype),
                       pltpu.VMEM((SC_CHUNK, width), src.dtype), pltpu.SemaphoreType.DMA, pltpu.SemaphoreType.DMA],
        name="sc_gather_rows",
    )
    def k(src_hbm, dest_hbm, out_hbm, idx_v, rows_a, rows_b, sem_a, sem_b):
        wid = _sc_worker()
        base = wid * (nchunks * SC_CHUNK)
        bufs = ((rows_a, sem_a), (rows_b, sem_b))

        @pl.loop(0, nchunks)
        def _(j):
            pltpu.sync_copy(dest_hbm.at[wid, j], idx_v)
            pending = pltpu.async_copy(src_hbm.at[idx_v.at[0]], rows_a, sem_a)
            for kk in range(TOP_K):
                buf, _ = bufs[kk % 2]
                pending.wait()
                if kk + 1 < TOP_K:
                    nbuf, nsem = bufs[(kk + 1) % 2]
                    pending = pltpu.async_copy(src_hbm.at[idx_v.at[kk + 1]], nbuf, nsem)
                pltpu.sync_copy(buf, out_hbm.at[kk, pl.ds(base + j * SC_CHUNK, SC_CHUNK)])

    return k(src, dest4)


def _expert_kernel(be_ref, nu_ref, xs_ref, wgu_ref, bgu_ref, wd_ref, bd_ref, ys_ref, wgu_bf, wd_bf, *, d_ff, ff_chunk):
    i = pl.program_id(0)

    @pl.when((i == 0) | (be_ref[i] != be_ref[jnp.maximum(i - 1, 0)]))
    def _():
        wgu_bf[...] = wgu_ref[0].astype(BF16)
        wd_bf[...] = wd_ref[0].astype(BF16)

    @pl.when(i < nu_ref[0])
    def _():
        lo, hi = _unpack_bf16_pair(xs_ref[...])
        xb = jnp.concatenate([lo.astype(BF16), hi.astype(BF16)], axis=1)
        y = bd_ref[0]
        for c0 in range(0, d_ff, ff_chunk):
            glu = jnp.dot(xb, wgu_bf[:, c0:c0 + ff_chunk], preferred_element_type=F32) + bgu_ref[0, :, c0:c0 + ff_chunk]
            lin = (jnp.dot(xb, wgu_bf[:, d_ff + c0:d_ff + c0 + ff_chunk], preferred_element_type=F32)
                   + bgu_ref[0, :, d_ff + c0:d_ff + c0 + ff_chunk])
            glu = jnp.minimum(glu, SWIGLU_LIMIT)
            lin = jnp.clip(lin, -SWIGLU_LIMIT, SWIGLU_LIMIT)
            act = (glu * jax.nn.sigmoid(SWIGLU_ALPHA * glu)) * (lin + 1.0)
            y = y + jnp.dot(act.astype(BF16), wd_bf[c0:c0 + ff_chunk, :], preferred_element_type=F32)
        yb = y.astype(BF16).astype(F32)
        half = y.shape[1] // 2
        ys_ref[...] = _pack_bf16_pair(yb[:, :half], yb[:, half:])


def _experts(xs, block_e, n_used, wgu, bgu, wd, bd, *, tm, n_blocks):
    half = xs.shape[1]
    p = n_blocks * tm
    e, d_model, two_ff = wgu.shape
    d_ff = two_ff // 2
    kern = functools.partial(_expert_kernel, d_ff=d_ff, ff_chunk=min(512, d_ff))
    blk = lambda i, be, nu: (jnp.minimum(i, nu[0] - 1), 0)
    grid_spec = pltpu.PrefetchScalarGridSpec(
        num_scalar_prefetch=2,
        grid=(n_blocks,),
        in_specs=[
            pl.BlockSpec((tm, half), blk),
            pl.BlockSpec((1, d_model, two_ff), lambda i, be, nu: (be[i], 0, 0)),
            pl.BlockSpec((1, 1, two_ff), lambda i, be, nu: (be[i], 0, 0)),
            pl.BlockSpec((1, d_ff, d_model), lambda i, be, nu: (be[i], 0, 0)),
            pl.BlockSpec((1, 1, d_model), lambda i, be, nu: (be[i], 0, 0)),
        ],
        out_specs=pl.BlockSpec((tm, half), blk),
        scratch_shapes=[pltpu.VMEM((d_model, two_ff), BF16), pltpu.VMEM((d_ff, d_model), BF16)],
    )
    return pl.pallas_call(
        kern,
        grid_spec=grid_spec,
        out_shape=jax.ShapeDtypeStruct((p, half), U32),
        compiler_params=_params(("arbitrary",)),
        name="experts",
    )(block_e, n_used, xs, wgu, bgu.reshape(e, 1, two_ff), wd, bd.reshape(e, 1, d_model))


def _final_kernel(yk_ref, gate_ref, x1_ref, g_ref, out_ref):
    tm, d_model = x1_ref.shape
    half = d_model // 2
    gates = gate_ref[...]
    acc_lo = jnp.zeros((tm, half), F32)
    acc_hi = jnp.zeros((tm, half), F32)
    for kk in range(TOP_K):
        lo, hi = _unpack_bf16_pair(yk_ref[kk])
        gk = gates[:, kk:kk + 1]
        acc_lo = acc_lo + gk * lo
        acc_hi = acc_hi + gk * hi
    x2 = x1_ref[...] + jnp.concatenate([acc_lo, acc_hi], axis=1)
    ms = jnp.mean(x2 * x2, axis=-1, keepdims=True)
    out_ref[...] = (x2 * lax.rsqrt(ms + RMS_EPS)) * g_ref[...]


def _final(yk, gates, x1, g_final, *, tm, tile0):
    n, d_model = x1.shape
    rows = lambda w: pl.BlockSpec((tm, w), lambda i: (i + tile0, 0))
    return pl.pallas_call(
        _final_kernel,
        grid=(yk.shape[1] // tm,),
        in_specs=[pl.BlockSpec((TOP_K, tm, d_model // 2), lambda i: (0, i, 0)), rows(LANES), rows(d_model),
                  pl.BlockSpec((1, d_model), lambda i: (0, 0))],
        out_specs=rows(d_model),
        out_shape=jax.ShapeDtypeStruct((n, d_model), F32),
        input_output_aliases={2: 0},
        compiler_params=_params(("arbitrary",)),
        name="final",
    )(yk, gates, x1, g_final)


def _tiles(seq):
    tm = min(512, seq)
    return dict(tm_proj=tm, t_attn=min(256, seq), tm_mix=tm, tm_expert=512, tm_final=tm, combine_groups=4)


def _forward(x, g_mix, w_in, conv_w, b_f, w_conv_o, w_attn_o, w_out, g_ffn, w_router, b_router,
             w_gate_up, b_gate_up, w_down, b_down, g_final, tiles):
    batch, seq, d_model = x.shape
    n = batch * seq
    conv_dim = conv_w.shape[1]
    attn_dim = w_attn_o.shape[0]
    heads = b_f.shape[0]
    head_dim = attn_dim // heads
    n_experts = w_router.shape[1]
    x2d = x.reshape(n, d_model)

    c0 = 3 * conv_dim
    a0 = c0 + 3 * attn_dim
    wa = w_in[:, :c0].astype(BF16)
    scale = LOG2E / (head_dim ** 0.5)
    wqk = jnp.concatenate([w_in[:, c0:c0 + attn_dim] * scale, w_in[:, c0 + attn_dim:c0 + 2 * attn_dim]],
                          axis=1).astype(BF16)
    wvt = w_in[:, c0 + 2 * attn_dim:a0].T.astype(BF16)
    wf = jnp.pad(w_in[:, a0:a0 + heads], ((0, 0), (0, LANES - heads))).astype(BF16)
    wg = w_in[:, a0 + heads:].astype(BF16)
    bf_pad = jnp.pad(b_f, (0, LANES - heads)).reshape(1, LANES)
    wr_hi = w_router.astype(BF16)
    wr_lo = (w_router - wr_hi.astype(F32)).astype(BF16)
    wr = jnp.concatenate([wr_hi, wr_lo], axis=1).T
    br = b_router.reshape(n_experts, 1)

    cb, ucx, qa, ka, vt, sgc, sga = _inproj(
        x2d, g_mix.reshape(1, d_model), wa, wqk, wvt, wg, wf, bf_pad,
        batch=batch, seq=seq, heads=heads, tm=tiles["tm_proj"], tk=tiles["t_attn"])
    o = _attention(qa, ka, vt, batch=batch, seq=seq, heads=heads, tq=tiles["t_attn"])
    x1, h2p, topi, gates, rank, cnt = _mix(
        x2d, cb, ucx, o, sgc, sga, conv_w, w_conv_o.astype(BF16), w_attn_o.astype(BF16), w_out.astype(BF16),
        g_ffn.reshape(1, d_model), wr, br, seq=seq, n_experts=n_experts, tm=tiles["tm_mix"],
        tile0=0, nt=n // tiles["tm_mix"])

    tme = tiles["tm_expert"]
    m = n * TOP_K
    n_blocks = m // tme + n_experts
    counts = cnt[-1, :, 0].astype(I32)
    padded = ((counts + tme - 1) // tme) * tme
    pend = jnp.cumsum(padded)
    pstart = pend - padded
    n_used = (pend[-1] // tme).astype(I32).reshape(1)
    blk_start = jnp.minimum(jnp.arange(n_blocks, dtype=I32), n_used[0] - 1) * tme
    block_e = jnp.minimum(jnp.sum((pend[None, :] <= blk_start[:, None]).astype(I32), axis=1), n_experts - 1)
    e_sel = topi[:TOP_K]
    onehot = e_sel[:, :, None] == jnp.arange(n_experts, dtype=I32)[None, None, :]
    dest = jnp.sum(jnp.where(onehot, pstart[None, None, :], 0), axis=2) + rank[:TOP_K]
    nchunks = n // (SC_WORKERS * SC_CHUNK)
    dest4 = dest.reshape(TOP_K, SC_WORKERS, nchunks, SC_CHUNK).transpose(1, 2, 0, 3)
    p_rows = n_blocks * tme
    jj = jnp.arange(tme, dtype=I32)[None, :]
    spare = p_rows + jnp.arange(n_experts, dtype=I32)[:, None] * tme + jj
    pad_idx = jnp.where(jj < (padded - counts)[:, None], (pstart + counts)[:, None] + jj, spare)
    pad_idx = pad_idx.reshape(SC_WORKERS, (n_experts * tme) // (SC_WORKERS * SC_CHUNK), SC_CHUNK)

    xs = _sc_scatter_rows(h2p, dest4, pad_idx, p_rows + n_experts * tme)
    ys = _experts(xs, block_e, n_used, w_gate_up, b_gate_up, w_down, b_down, tm=tme, n_blocks=n_blocks)
    groups = tiles["combine_groups"]
    ng = n // groups
    out = x1
    for g in range(groups):
        dest_g = dest[:, g * ng:(g + 1) * ng]
        dest4_g = dest_g.reshape(TOP_K, SC_WORKERS, ng // (SC_WORKERS * SC_CHUNK), SC_CHUNK).transpose(1, 2, 0, 3)
        yk = _sc_gather_rows(ys, dest4_g, ng)
        out = _final(yk, gates, out, g_final.reshape(1, d_model), tm=tiles["tm_final"],
                     tile0=g * (ng // tiles["tm_final"]))
    return out.reshape(batch, seq, d_model)


def kernel(x, g_mix, w_in, conv_w, b_f, w_conv_o, w_attn_o, w_out, g_ffn, w_router, b_router, w_gate_up,
           b_gate_up, w_down, b_down, g_final):
    return _forward(x, g_mix, w_in, conv_w, b_f, w_conv_o, w_attn_o, w_out, g_ffn, w_router, b_router,
                    w_gate_up, b_gate_up, w_down, b_down, g_final, _tiles(x.shape[1]))
```

```python
import functools

import jax
import jax.numpy as jnp
from jax import lax
from jax.experimental import pallas as pl
from jax.experimental.pallas import tpu as pltpu
from jax.experimental.pallas import tpu_sc as plsc

TOP_K = 4
RMS_EPS = 1e-5
SWIGLU_ALPHA = 1.702
SWIGLU_LIMIT = 7.0
LOG2E = 1.4426950408889634

LANES = 128
BF16_SUBLANES = 16
V_ONES_ROWS = BF16_SUBLANES
VMEM_LIMIT_BYTES = 56 * 1024 * 1024

F32 = jnp.float32
BF16 = jnp.bfloat16
U32 = jnp.uint32
I32 = jnp.int32
HI_MASK = 0xFFFF0000


def _params(sem):
    return pltpu.CompilerParams(dimension_semantics=sem, vmem_limit_bytes=VMEM_LIMIT_BYTES)


def _pack_bf16_pair(lo_f32, hi_f32):
    lo = lax.bitcast_convert_type(lo_f32, U32)
    hi = lax.bitcast_convert_type(hi_f32, U32)
    return (lo >> 16) | (hi & U32(HI_MASK))


def _unpack_bf16_pair(w):
    lo = lax.bitcast_convert_type(w << 16, F32)
    hi = lax.bitcast_convert_type(w & U32(HI_MASK), F32)
    return lo, hi


def _inproj_kernel(x_ref, g_ref, wa_ref, wqk_ref, wvt_ref, wg_ref, wf_ref, bf_ref,
                   cb_ref, ucx_ref, qa_ref, ka_ref, vt_ref, sgc_ref, sga_ref,
                   carry_ref, *, tiles_per_batch, conv_dim, attn_dim, d_model, heads, tk):
    tm = x_ref.shape[0]
    t = lax.rem(pl.program_id(0), tiles_per_batch)
    x = x_ref[...]
    ms = jnp.mean(x * x, axis=-1, keepdims=True)
    h = ((x * lax.rsqrt(ms + RMS_EPS)) * g_ref[...]).astype(BF16)

    def mm(w_ref, c0, n):
        return jnp.dot(h, w_ref[:, c0:c0 + n], preferred_element_type=F32)

    z = mm(wf_ref, 0, LANES) + bf_ref[...]
    qf = mm(wqk_ref, 0, attn_dim)
    kf = mm(wqk_ref, attn_dim, attn_dim)
    cb = mm(wa_ref, 0, conv_dim)
    ucx = mm(wa_ref, conv_dim, conv_dim) * mm(wa_ref, 2 * conv_dim, conv_dim)

    lane = lax.broadcasted_iota(I32, z.shape, 1)
    logf = jnp.minimum(z, 0.0) - jnp.log1p(jnp.exp(-jnp.abs(z)))
    logf = jnp.where(lane < heads, logf, 0.0)
    p1 = logf.astype(BF16).astype(F32)
    r1 = logf - p1
    p2 = r1.astype(BF16).astype(F32)
    p3 = (r1 - p2).astype(BF16).astype(F32)
    packed = (p1 + pltpu.roll(p2, heads, 1) + pltpu.roll(p3, 2 * heads, 1)).astype(BF16)
    row = lax.broadcasted_iota(I32, (tm, tm), 0)
    col = lax.broadcasted_iota(I32, (tm, tm), 1)
    tri = jnp.where(col <= row, 1.0, 0.0).astype(BF16)
    r = jnp.dot(tri, packed, preferred_element_type=F32)
    local = r + pltpu.roll(r, LANES - heads, 1) + pltpu.roll(r, LANES - 2 * heads, 1)

    @pl.when(t == 0)
    def _():
        carry_ref[...] = jnp.zeros_like(carry_ref)

    c = jnp.where(lane < heads, local + carry_ref[0:1, :], 0.0)
    carry_ref[0:1, :] = c[tm - 1:tm, :]

    c2 = c * LOG2E
    c_hi = c2.astype(BF16).astype(F32)
    c_r = c2 - c_hi
    c_mid = c_r.astype(BF16).astype(F32)
    c_lo = (c_r - c_mid).astype(BF16).astype(F32)
    head_dim = attn_dim // heads
    ext = LANES - head_dim
    lane_e = lax.broadcasted_iota(I32, (tm, ext), 1)
    ones3 = jnp.where(lane_e < 3, 1.0, 0.0)
    for hh in range(heads):
        kext = jnp.where(lane_e == 0, -c_hi[:, hh:hh + 1],
                         jnp.where(lane_e == 1, -c_mid[:, hh:hh + 1],
                                   jnp.where(lane_e == 2, -c_lo[:, hh:hh + 1], 0.0)))
        sl = slice(hh * head_dim, (hh + 1) * head_dim)
        qa_ref[:, hh * LANES:(hh + 1) * LANES] = jnp.concatenate([qf[:, sl], ones3], axis=1).astype(BF16)
        ka_ref[:, hh * LANES:(hh + 1) * LANES] = jnp.concatenate([kf[:, sl], kext], axis=1).astype(BF16)

    cb_ref[...] = cb.astype(BF16)
    ucx_ref[...] = ucx.astype(BF16)
    half = d_model // 2
    for c in range(2):
        sgc_ref[:, c * half:(c + 1) * half] = jax.nn.sigmoid(mm(wg_ref, c * half, half)).astype(BF16)
        sga_ref[:, c * half:(c + 1) * half] = jax.nn.sigmoid(mm(wg_ref, d_model + c * half, half)).astype(BF16)

    vt = lax.dot_general(wvt_ref[...], h, (((1,), (1,)), ((), ())), preferred_element_type=F32).astype(BF16)
    ones_rows = jnp.ones((V_ONES_ROWS, tm), BF16)
    vt = jnp.concatenate([piece for hh in range(heads)
                          for piece in (vt[hh * head_dim:(hh + 1) * head_dim], ones_rows)], axis=0)
    for u in range(tm // tk):
        vt_ref[u] = vt[:, u * tk:(u + 1) * tk]


def _inproj(x2d, g_mix, wa, wqk, wvt, wg, wf, bf_pad, *, batch, seq, heads, tm, tk):
    n, d_model = x2d.shape
    conv_dim = wa.shape[1] // 3
    attn_dim = wvt.shape[0]
    tiles_per_batch = seq // tm
    grid = (n // tm,)
    full = lambda shape: pl.BlockSpec(shape, lambda i: (0,) * len(shape))
    rows = lambda w: pl.BlockSpec((tm, w), lambda i: (i, 0))
    kern = functools.partial(_inproj_kernel, tiles_per_batch=tiles_per_batch, conv_dim=conv_dim,
                             attn_dim=attn_dim, d_model=d_model, heads=heads, tk=tk)
    vt_rows = attn_dim + heads * V_ONES_ROWS
    out_shape = (
        jax.ShapeDtypeStruct((n, conv_dim), BF16),
        jax.ShapeDtypeStruct((n, conv_dim), BF16),
        jax.ShapeDtypeStruct((n, heads * LANES), BF16),
        jax.ShapeDtypeStruct((n, heads * LANES), BF16),
        jax.ShapeDtypeStruct((n // tk, vt_rows, tk), BF16),
        jax.ShapeDtypeStruct((n, d_model), BF16),
        jax.ShapeDtypeStruct((n, d_model), BF16),
    )
    out_specs = (
        rows(conv_dim), rows(conv_dim), rows(heads * LANES), rows(heads * LANES),
        pl.BlockSpec((tm // tk, vt_rows, tk), lambda i: (i, 0, 0)),
        rows(d_model), rows(d_model),
    )
    return pl.pallas_call(
        kern,
        grid=grid,
        in_specs=[rows(d_model), full((1, d_model)), full(wa.shape), full(wqk.shape), full(wvt.shape),
                  full(wg.shape), full(wf.shape), full((1, LANES))],
        out_specs=out_specs,
        out_shape=out_shape,
        scratch_shapes=[pltpu.VMEM((8, LANES), F32)],
        compiler_params=_params(("arbitrary",)),
        name="inproj",
    )(x2d, g_mix, wa, wqk, wvt, wg, wf, bf_pad)


def _attn_kernel(q_ref, k_ref, v_ref, o_ref, s_ref, *, tq, tk, head_dim, heads):
    qi = pl.program_id(1)
    row = lax.broadcasted_iota(I32, (tk, tq), 0)
    col = lax.broadcasted_iota(I32, (tk, tq), 1)

    def score_tile(j, slot):
        off = pl.multiple_of(j * tk, tk)
        for hh in range(heads):
            qa = q_ref[:, hh * LANES:(hh + 1) * LANES]
            ka = k_ref[pl.ds(off, tk), hh * LANES:(hh + 1) * LANES]
            s_ref[slot, hh] = lax.dot_general(ka, qa, (((1,), (1,)), ((), ())), preferred_element_type=F32)

    def block(j, slot, carry, masked):
        if not masked:
            score_tile(j + 1, 1 - slot)
        stats = []
        for hh in range(heads):
            m, _ = carry[hh]
            s = s_ref[slot, hh]
            if masked:
                s = jnp.where(row <= col, s, -jnp.inf)
            m_new = jnp.maximum(m, jnp.max(s, axis=0, keepdims=True))
            stats.append((m_new, jnp.exp2(m - m_new), jnp.exp2(s - m_new).astype(BF16)))
        out = []
        for hh in range(heads):
            m_new, alpha, p = stats[hh]
            vth = v_ref[j, hh * vrows:(hh + 1) * vrows, :]
            out.append((m_new, alpha * carry[hh][1] + jnp.dot(vth, p, preferred_element_type=F32)))
        return tuple(out)

    vrows = head_dim + V_ONES_ROWS
    init = tuple((jnp.full((1, tq), -jnp.inf, F32), jnp.zeros((vrows, tq), F32)) for _ in range(heads))
    score_tile(0, 0)

    def pair(p, c):
        return block(2 * p + 1, 1, block(2 * p, 0, c, False), False)

    carry = lax.fori_loop(0, qi // 2, pair, init)
    final = lax.cond(
        lax.rem(qi, 2) == 0,
        lambda c: block(qi, 0, c, True),
        lambda c: block(qi, 1, block(qi - 1, 0, c, False), True),
        carry)
    per_group = LANES // head_dim
    for g in range(heads // per_group):
        accs = [final[g * per_group + u][1] for u in range(per_group)]
        ot = jnp.concatenate([a[:head_dim] / a[head_dim:head_dim + 1] for a in accs],
                             axis=0)
        o_ref[:, g * LANES:(g + 1) * LANES] = ot.T.astype(BF16)


def _attention(qa, ka, vt, *, batch, seq, heads, tq):
    n = qa.shape[0]
    _, vt_rows, tk = vt.shape
    head_dim = vt_rows // heads - V_ONES_ROWS
    attn_dim = heads * head_dim
    assert LANES % head_dim == 0 and tq == tk
    nq = seq // tq
    nk = seq // tk
    kern = functools.partial(_attn_kernel, tq=tq, tk=tk, head_dim=head_dim, heads=heads)
    return pl.pallas_call(
        kern,
        grid=(batch, nq),
        in_specs=[
            pl.BlockSpec((tq, heads * LANES), lambda b, qi: (b * nq + qi, 0)),
            pl.BlockSpec((seq, heads * LANES), lambda b, qi: (b, 0)),
            pl.BlockSpec((nk, vt_rows, tk), lambda b, qi: (b, 0, 0)),
        ],
        out_specs=pl.BlockSpec((tq, attn_dim), lambda b, qi: (b * nq + qi, 0)),
        out_shape=jax.ShapeDtypeStruct((n, attn_dim), BF16),
        scratch_shapes=[pltpu.VMEM((2, heads, tk, tq), F32)],
        compiler_params=_params(("arbitrary", "arbitrary")),
        name="attn",
    )(qa, ka, vt)


def _mix_kernel(x_ref, cb_ref, u_ref, uh_ref, o_ref, sgc_ref, sga_ref, cw_ref, wco_ref, wao_ref, wout_ref,
                gffn_ref, wr_ref, br_ref,
                x1_ref, h2p_ref, topi_ref, gate_ref, rank_ref, cnt_ref, carry_ref,
                *, tiles_per_batch, n_experts):
    tm, d_model = x_ref.shape
    i = pl.program_id(0)
    first = lax.rem(i, tiles_per_batch) == 0

    u = u_ref[...].astype(F32)
    halo = jnp.where(first, 0.0, uh_ref[...].astype(F32))
    ext = jnp.concatenate([halo, u], axis=0)
    hs = halo.shape[0]
    u1 = pltpu.roll(ext, 1, 0)[hs:]
    u2 = pltpu.roll(ext, 2, 0)[hs:]
    cw = cw_ref[...]
    conv = cw[0:1, :] * u2 + cw[1:2, :] * u1 + cw[2:3, :] * u
    yc = (cb_ref[...].astype(F32) * conv).astype(BF16)
    y_conv = jnp.dot(yc, wco_ref[...], preferred_element_type=F32)
    y_attn = jnp.dot(o_ref[...], wao_ref[...], preferred_element_type=F32)
    mixed = (sgc_ref[...].astype(F32) * y_conv + sga_ref[...].astype(F32) * y_attn).astype(BF16)
    x1 = x_ref[...] + jnp.dot(mixed, wout_ref[...], preferred_element_type=F32)
    x1_ref[...] = x1

    ms = jnp.mean(x1 * x1, axis=-1, keepdims=True)
    h2 = (x1 * lax.rsqrt(ms + RMS_EPS)) * gffn_ref[...]
    hb = h2.astype(BF16)
    hbf = hb.astype(F32)
    half = d_model // 2
    h2p_ref[...] = _pack_bf16_pair(hbf[:, :half], hbf[:, half:])

    hlo = (h2 - hbf).astype(BF16)
    nt_dims = (((1,), (1,)), ((), ()))
    rt = (lax.dot_general(wr_ref[...], hb, nt_dims, preferred_element_type=F32)
          + lax.dot_general(wr_ref[...], hlo, nt_dims, preferred_element_type=F32))
    lg = rt[:n_experts] + rt[n_experts:] + br_ref[...]
    sub = lax.broadcasted_iota(I32, (n_experts, tm), 0)

    vals, idxs = [], []
    for _ in range(TOP_K):
        m = jnp.max(lg, axis=0, keepdims=True)
        idx = jnp.min(jnp.where(lg == m, sub, n_experts), axis=0, keepdims=True)
        vals.append(m)
        idxs.append(idx)
        lg = jnp.where(sub == idx, -jnp.inf, lg)
    es = [jnp.exp(vk - vals[0]) for vk in vals]
    denom = es[0] + es[1] + es[2] + es[3]

    sub8 = lax.broadcasted_iota(I32, (8, tm), 0)
    subl = lax.broadcasted_iota(I32, (LANES, tm), 0)
    chosen = jnp.zeros((n_experts, tm), F32)
    topi = jnp.zeros((8, tm), I32)
    gates_t = jnp.zeros((LANES, tm), F32)
    for kk in range(TOP_K):
        chosen = jnp.where(sub == idxs[kk], 1.0, chosen)
        topi = jnp.where(sub8 == kk, idxs[kk], topi)
        gates_t = jnp.where(subl == kk, es[kk] / denom, gates_t)
    topi_ref[...] = topi
    gate_ref[...] = gates_t.T

    @pl.when(i == 0)
    def _():
        carry_ref[...] = jnp.zeros_like(carry_ref)

    row = lax.broadcasted_iota(I32, (tm, tm), 0)
    col = lax.broadcasted_iota(I32, (tm, tm), 1)
    tri = jnp.where(row < col, 1.0, 0.0).astype(BF16)
    before = jnp.dot(chosen.astype(BF16), tri, preferred_element_type=F32) + carry_ref[:, 0:1]
    rank = jnp.zeros((8, tm), F32)
    for kk in range(TOP_K):
        rk = jnp.sum(jnp.where(sub == idxs[kk], before, 0.0), axis=0, keepdims=True)
        rank = jnp.where(sub8 == kk, rk, rank)
    rank_ref[...] = rank.astype(I32)
    total = carry_ref[...] + jnp.sum(chosen, axis=1, keepdims=True)
    carry_ref[...] = total
    cnt_ref[0] = total


def _mix(x2d, cb, ucx, o, sgc, sga, conv_w, wco, wao, wout, g_ffn, wr, br, *, seq, n_experts, tm, tile0, nt):
    d_model = x2d.shape[1]
    n = nt * tm
    conv_dim = cb.shape[1]
    attn_dim = o.shape[1]
    tiles_per_batch = seq // tm
    assert tile0 % tiles_per_batch == 0
    hb = tm // BF16_SUBLANES
    full = lambda shape: pl.BlockSpec(shape, lambda i: (0,) * len(shape))
    rows_in = lambda w: pl.BlockSpec((tm, w), lambda i: (i + tile0, 0))
    rows = lambda w: pl.BlockSpec((tm, w), lambda i: (i, 0))
    kern = functools.partial(_mix_kernel, tiles_per_batch=tiles_per_batch, n_experts=n_experts)
    out_shape = (
        jax.ShapeDtypeStruct((n, d_model), F32),
        jax.ShapeDtypeStruct((n, d_model // 2), U32),
        jax.ShapeDtypeStruct((8, n), I32),
        jax.ShapeDtypeStruct((n, LANES), F32),
        jax.ShapeDtypeStruct((8, n), I32),
        jax.ShapeDtypeStruct((nt, n_experts, LANES), F32),
    )
    cols = pl.BlockSpec((8, tm), lambda i: (0, i))
    out_specs = (rows(d_model), rows(d_model // 2), cols, rows(LANES), cols,
                 pl.BlockSpec((1, n_experts, LANES), lambda i: (i, 0, 0)))
    return pl.pallas_call(
        kern,
        grid=(nt,),
        in_specs=[rows_in(d_model), rows_in(conv_dim), rows_in(conv_dim),
                  pl.BlockSpec((BF16_SUBLANES, conv_dim), lambda i: (jnp.maximum((i + tile0) * hb - 1, 0), 0)),
                  rows_in(attn_dim), rows_in(d_model), rows_in(d_model),
                  full(conv_w.shape), full(wco.shape), full(wao.shape), full(wout.shape),
                  full((1, d_model)), full(wr.shape), full((n_experts, 1))],
        out_specs=out_specs,
        out_shape=out_shape,
        scratch_shapes=[pltpu.VMEM((n_experts, LANES), F32)],
        compiler_params=_params(("arbitrary",)),
        name="mix",
    )(x2d, cb, ucx, ucx, o, sgc, sga, conv_w, wco, wao, wout, g_ffn, wr, br)


SC_CORES = 2
SC_SUBCORES = 16
SC_WORKERS = SC_CORES * SC_SUBCORES
SC_CHUNK = 64


def _sc_mesh():
    return plsc.VectorSubcoreMesh(core_axis_name="c", subcore_axis_name="s",
                                  num_cores=SC_CORES, num_subcores=SC_SUBCORES)


def _sc_worker():
    return lax.axis_index("s") * SC_CORES + lax.axis_index("c")


def _sc_scatter_rows(src, dest4, pad_idx, n_out):
    n, width = src.shape
    nchunks = dest4.shape[1]
    npad = pad_idx.shape[1]

    @functools.partial(
        pl.kernel, mesh=_sc_mesh(),
        out_type=jax.ShapeDtypeStruct((n_out, width), src.dtype),
        scratch_types=[pltpu.VMEM((TOP_K, SC_CHUNK), I32), pltpu.VMEM((SC_CHUNK, width), src.dtype),
                       pltpu.VMEM((npad, SC_CHUNK), I32), pltpu.SemaphoreType.DMA],
        name="sc_scatter_rows",
    )
    def k(src_hbm, dest_hbm, pad_hbm, out_hbm, idx_v, rows_v, pad_v, sem):
        wid = _sc_worker()
        base = wid * (nchunks * SC_CHUNK)

        @pl.loop(0, nchunks)
        def _(j):
            pltpu.sync_copy(src_hbm.at[pl.ds(base + j * SC_CHUNK, SC_CHUNK)], rows_v)
            pltpu.sync_copy(dest_hbm.at[wid, j], idx_v)
            copies = [pltpu.async_copy(rows_v, out_hbm.at[idx_v.at[kk]], sem) for kk in range(TOP_K)]
            for cp in copies:
                cp.wait()

        pltpu.sync_copy(pad_hbm.at[wid], pad_v)
        fills = [pltpu.async_copy(rows_v, out_hbm.at[pad_v.at[p]], sem) for p in range(npad)]
        for cp in fills:
            cp.wait()

    return k(src, dest4, pad_idx)


def _sc_gather_rows(src, dest4, n):
    width = src.shape[1]
    nchunks = dest4.shape[1]

    @functools.partial(
        pl.kernel, mesh=_sc_mesh(),
        out_type=jax.ShapeDtypeStruct((TOP_K, n, width), src.dtype),
        scratch_types=[pltpu.VMEM((TOP_K, SC_CHUNK), I32), pltpu.VMEM((SC_CHUNK, width), src.dtype),
                       pltpu.VMEM((SC_CHUNK, width), src.dtype), pltpu.SemaphoreType.DMA, pltpu.SemaphoreType.DMA],
        name="sc_gather_rows",
    )
    def k(src_hbm, dest_hbm, out_hbm, idx_v, rows_a, rows_b, sem_a, sem_b):
        wid = _sc_worker()
        base = wid * (nchunks * SC_CHUNK)
        bufs = ((rows_a, sem_a), (rows_b, sem_b))

        @pl.loop(0, nchunks)
        def _(j):
            pltpu.sync_copy(dest_hbm.at[wid, j], idx_v)
            pending = pltpu.async_copy(src_hbm.at[idx_v.at[0]], rows_a, sem_a)
            for kk in range(TOP_K):
                buf, _ = bufs[kk % 2]
                pending.wait()
                if kk + 1 < TOP_K:
                    nbuf, nsem = bufs[(kk + 1) % 2]
                    pending = pltpu.async_copy(src_hbm.at[idx_v.at[kk + 1]], nbuf, nsem)
                pltpu.sync_copy(buf, out_hbm.at[kk, pl.ds(base + j * SC_CHUNK, SC_CHUNK)])

    return k(src, dest4)


def _expert_kernel(be_ref, nu_ref, xs_ref, wgu_ref, bgu_ref, wd_ref, bd_ref, ys_ref, wgu_bf, wd_bf, *, d_ff, ff_chunk):
    i = pl.program_id(0)

    @pl.when((i == 0) | (be_ref[i] != be_ref[jnp.maximum(i - 1, 0)]))
    def _():
        wgu_bf[...] = wgu_ref[0].astype(BF16)
        wd_bf[...] = wd_ref[0].astype(BF16)

    @pl.when(i < nu_ref[0])
    def _():
        lo, hi = _unpack_bf16_pair(xs_ref[...])
        xb = jnp.concatenate([lo.astype(BF16), hi.astype(BF16)], axis=1)
        y = bd_ref[0]
        for c0 in range(0, d_ff, ff_chunk):
            glu = jnp.dot(xb, wgu_bf[:, c0:c0 + ff_chunk], preferred_element_type=F32) + bgu_ref[0, :, c0:c0 + ff_chunk]
            lin = (jnp.dot(xb, wgu_bf[:, d_ff + c0:d_ff + c0 + ff_chunk], preferred_element_type=F32)
                   + bgu_ref[0, :, d_ff + c0:d_ff + c0 + ff_chunk])
            glu = jnp.minimum(glu, SWIGLU_LIMIT)
            lin = jnp.clip(lin, -SWIGLU_LIMIT, SWIGLU_LIMIT)
            act = (glu * jax.nn.sigmoid(SWIGLU_ALPHA * glu)) * (lin + 1.0)
            y = y + jnp.dot(act.astype(BF16), wd_bf[c0:c0 + ff_chunk, :], preferred_element_type=F32)
        yb = y.astype(BF16).astype(F32)
        half = y.shape[1] // 2
        ys_ref[...] = _pack_bf16_pair(yb[:, :half], yb[:, half:])


def _experts(xs, block_e, n_used, wgu, bgu, wd, bd, *, tm, n_blocks):
    half = xs.shape[1]
    p = n_blocks * tm
    e, d_model, two_ff = wgu.shape
    d_ff = two_ff // 2
    kern = functools.partial(_expert_kernel, d_ff=d_ff, ff_chunk=min(512, d_ff))
    blk = lambda i, be, nu: (jnp.minimum(i, nu[0] - 1), 0)
    grid_spec = pltpu.PrefetchScalarGridSpec(
        num_scalar_prefetch=2,
        grid=(n_blocks,),
        in_specs=[
            pl.BlockSpec((tm, half), blk),
            pl.BlockSpec((1, d_model, two_ff), lambda i, be, nu: (be[i], 0, 0)),
            pl.BlockSpec((1, 1, two_ff), lambda i, be, nu: (be[i], 0, 0)),
            pl.BlockSpec((1, d_ff, d_model), lambda i, be, nu: (be[i], 0, 0)),
            pl.BlockSpec((1, 1, d_model), lambda i, be, nu: (be[i], 0, 0)),
        ],
        out_specs=pl.BlockSpec((tm, half), blk),
        scratch_shapes=[pltpu.VMEM((d_model, two_ff), BF16), pltpu.VMEM((d_ff, d_model), BF16)],
    )
    return pl.pallas_call(
        kern,
        grid_spec=grid_spec,
        out_shape=jax.ShapeDtypeStruct((p, half), U32),
        compiler_params=_params(("arbitrary",)),
        name="experts",
    )(block_e, n_used, xs, wgu, bgu.reshape(e, 1, two_ff), wd, bd.reshape(e, 1, d_model))


def _final_kernel(yk_ref, gate_ref, x1_ref, g_ref, out_ref):
    tm, d_model = x1_ref.shape
    half = d_model // 2
    gates = gate_ref[...]
    acc_lo = jnp.zeros((tm, half), F32)
    acc_hi = jnp.zeros((tm, half), F32)
    for kk in range(TOP_K):
        lo, hi = _unpack_bf16_pair(yk_ref[kk])
        gk = gates[:, kk:kk + 1]
        acc_lo = acc_lo + gk * lo
        acc_hi = acc_hi + gk * hi
    x2 = x1_ref[...] + jnp.concatenate([acc_lo, acc_hi], axis=1)
    ms = jnp.mean(x2 * x2, axis=-1, keepdims=True)
    out_ref[...] = (x2 * lax.rsqrt(ms + RMS_EPS)) * g_ref[...]


def _final(yk, gates, x1, g_final, *, tm, tile0):
    n, d_model = x1.shape
    rows = lambda w: pl.BlockSpec((tm, w), lambda i: (i + tile0, 0))
    return pl.pallas_call(
        _final_kernel,
        grid=(yk.shape[1] // tm,),
        in_specs=[pl.BlockSpec((TOP_K, tm, d_model // 2), lambda i: (0, i, 0)), rows(LANES), rows(d_model),
                  pl.BlockSpec((1, d_model), lambda i: (0, 0))],
        out_specs=rows(d_model),
        out_shape=jax.ShapeDtypeStruct((n, d_model), F32),
        input_output_aliases={2: 0},
        compiler_params=_params(("arbitrary",)),
        name="final",
    )(yk, gates, x1, g_final)


def _tiles(seq):
    tm = min(512, seq)
    return dict(tm_proj=tm, t_attn=min(256, seq), tm_mix=tm, tm_expert=512, tm_final=tm, combine_groups=4)


def _forward(x, g_mix, w_in, conv_w, b_f, w_conv_o, w_attn_o, w_out, g_ffn, w_router, b_router,
             w_gate_up, b_gate_up, w_down, b_down, g_final, tiles):
    batch, seq, d_model = x.shape
    n = batch * seq
    conv_dim = conv_w.shape[1]
    attn_dim = w_attn_o.shape[0]
    heads = b_f.shape[0]
    head_dim = attn_dim // heads
    n_experts = w_router.shape[1]
    x2d = x.reshape(n, d_model)

    c0 = 3 * conv_dim
    a0 = c0 + 3 * attn_dim
    wa = w_in[:, :c0].astype(BF16)
    scale = LOG2E / (head_dim ** 0.5)
    wqk = jnp.concatenate([w_in[:, c0:c0 + attn_dim] * scale, w_in[:, c0 + attn_dim:c0 + 2 * attn_dim]],
                          axis=1).astype(BF16)
    wvt = w_in[:, c0 + 2 * attn_dim:a0].T.astype(BF16)
    wf = jnp.pad(w_in[:, a0:a0 + heads], ((0, 0), (0, LANES - heads))).astype(BF16)
    wg = w_in[:, a0 + heads:].astype(BF16)
    bf_pad = jnp.pad(b_f, (0, LANES - heads)).reshape(1, LANES)
    wr_hi = w_router.astype(BF16)
    wr_lo = (w_router - wr_hi.astype(F32)).astype(BF16)
    wr = jnp.concatenate([wr_hi, wr_lo], axis=1).T
    br = b_router.reshape(n_experts, 1)

    cb, ucx, qa, ka, vt, sgc, sga = _inproj(
        x2d, g_mix.reshape(1, d_model), wa, wqk, wvt, wg, wf, bf_pad,
        batch=batch, seq=seq, heads=heads, tm=tiles["tm_proj"], tk=tiles["t_attn"])
    o = _attention(qa, ka, vt, batch=batch, seq=seq, heads=heads, tq=tiles["t_attn"])
    x1, h2p, topi, gates, rank, cnt = _mix(
        x2d, cb, ucx, o, sgc, sga, conv_w, w_conv_o.astype(BF16), w_attn_o.astype(BF16), w_out.astype(BF16),
        g_ffn.reshape(1, d_model), wr, br, seq=seq, n_experts=n_experts, tm=tiles["tm_mix"],
        tile0=0, nt=n // tiles["tm_mix"])

    tme = tiles["tm_expert"]
    m = n * TOP_K
    n_blocks = m // tme + n_experts
    counts = cnt[-1, :, 0].astype(I32)
    padded = ((counts + tme - 1) // tme) * tme
    pend = jnp.cumsum(padded)
    pstart = pend - padded
    n_used = (pend[-1] // tme).astype(I32).reshape(1)
    blk_start = jnp.minimum(jnp.arange(n_blocks, dtype=I32), n_used[0] - 1) * tme
    block_e = jnp.minimum(jnp.sum((pend[None, :] <= blk_start[:, None]).astype(I32), axis=1), n_experts - 1)
    e_sel = topi[:TOP_K]
    onehot = e_sel[:, :, None] == jnp.arange(n_experts, dtype=I32)[None, None, :]
    dest = jnp.sum(jnp.where(onehot, pstart[None, None, :], 0), axis=2) + rank[:TOP_K]
    nchunks = n // (SC_WORKERS * SC_CHUNK)
    dest4 = dest.reshape(TOP_K, SC_WORKERS, nchunks, SC_CHUNK).transpose(1, 2, 0, 3)
    p_rows = n_blocks * tme
    jj = jnp.arange(tme, dtype=I32)[None, :]
    spare = p_rows + jnp.arange(n_experts, dtype=I32)[:, None] * tme + jj
    pad_idx = jnp.where(jj < (padded - counts)[:, None], (pstart + counts)[:, None] + jj, spare)
    pad_idx = pad_idx.reshape(SC_WORKERS, (n_experts * tme) // (SC_WORKERS * SC_CHUNK), SC_CHUNK)

    xs = _sc_scatter_rows(h2p, dest4, pad_idx, p_rows + n_experts * tme)
    ys = _experts(xs, block_e, n_used, w_gate_up, b_gate_up, w_down, b_down, tm=tme, n_blocks=n_blocks)
    groups = tiles["combine_groups"]
    ng = n // groups
    out = x1
    for g in range(groups):
        dest_g = dest[:, g * ng:(g + 1) * ng]
        dest4_g = dest_g.reshape(TOP_K, SC_WORKERS, ng // (SC_WORKERS * SC_CHUNK), SC_CHUNK).transpose(1, 2, 0, 3)
        yk = _sc_gather_rows(ys, dest4_g, ng)
        out = _final(yk, gates, out, g_final.reshape(1, d_model), tm=tiles["tm_final"],
                     tile0=g * (ng // tiles["tm_final"]))
    return out.reshape(batch, seq, d_model)


def kernel(x, g_mix, w_in, conv_w, b_f, w_conv_o, w_attn_o, w_out, g_ffn, w_router, b_router, w_gate_up,
           b_gate_up, w_down, b_down, g_final):
    return _forward(x, g_mix, w_in, conv_w, b_f, w_conv_o, w_attn_o, w_out, g_ffn, w_router, b_router,
                    w_gate_up, b_gate_up, w_down, b_down, g_final, _tiles(x.shape[1]))
```

```python
import functools

import jax
import jax.numpy as jnp
from jax import lax
from jax.experimental import pallas as pl
from jax.experimental.pallas import tpu as pltpu
from jax.experimental.pallas import tpu_sc as plsc

TOP_K = 4
RMS_EPS = 1e-5
SWIGLU_ALPHA = 1.702
SWIGLU_LIMIT = 7.0
LOG2E = 1.4426950408889634

LANES = 128
BF16_SUBLANES = 16
V_ONES_ROWS = BF16_SUBLANES
VMEM_LIMIT_BYTES = 56 * 1024 * 1024

F32 = jnp.float32
BF16 = jnp.bfloat16
U32 = jnp.uint32
I32 = jnp.int32
HI_MASK = 0xFFFF0000


def _params(sem):
    return pltpu.CompilerParams(dimension_semantics=sem, vmem_limit_bytes=VMEM_LIMIT_BYTES)


def _pack_bf16_pair(lo_f32, hi_f32):
    lo = lax.bitcast_convert_type(lo_f32, U32)
    hi = lax.bitcast_convert_type(hi_f32, U32)
    return (lo >> 16) | (hi & U32(HI_MASK))


def _unpack_bf16_pair(w):
    lo = lax.bitcast_convert_type(w << 16, F32)
    hi = lax.bitcast_convert_type(w & U32(HI_MASK), F32)
    return lo, hi


def _inproj_kernel(x_ref, g_ref, wa_ref, wqk_ref, wvt_ref, wg_ref, wf_ref, bf_ref,
                   cb_ref, ucx_ref, qa_ref, ka_ref, vt_ref, sgc_ref, sga_ref,
                   carry_ref, *, tiles_per_batch, conv_dim, attn_dim, d_model, heads, tk):
    tm = x_ref.shape[0]
    t = lax.rem(pl.program_id(0), tiles_per_batch)
    x = x_ref[...]
    ms = jnp.mean(x * x, axis=-1, keepdims=True)
    h = ((x * lax.rsqrt(ms + RMS_EPS)) * g_ref[...]).astype(BF16)

    def mm(w_ref, c0, n):
        return jnp.dot(h, w_ref[:, c0:c0 + n], preferred_element_type=F32)

    z = mm(wf_ref, 0, LANES) + bf_ref[...]
    qf = mm(wqk_ref, 0, attn_dim)
    kf = mm(wqk_ref, attn_dim, attn_dim)
    cb = mm(wa_ref, 0, conv_dim)
    ucx = mm(wa_ref, conv_dim, conv_dim) * mm(wa_ref, 2 * conv_dim, conv_dim)

    lane = lax.broadcasted_iota(I32, z.shape, 1)
    logf = jnp.minimum(z, 0.0) - jnp.log1p(jnp.exp(-jnp.abs(z)))
    logf = jnp.where(lane < heads, logf, 0.0)
    p1 = logf.astype(BF16).astype(F32)
    r1 = logf - p1
    p2 = r1.astype(BF16).astype(F32)
    p3 = (r1 - p2).astype(BF16).astype(F32)
    packed = (p1 + pltpu.roll(p2, heads, 1) + pltpu.roll(p3, 2 * heads, 1)).astype(BF16)
    row = lax.broadcasted_iota(I32, (tm, tm), 0)
    col = lax.broadcasted_iota(I32, (tm, tm), 1)
    tri = jnp.where(col <= row, 1.0, 0.0).astype(BF16)
    r = jnp.dot(tri, packed, preferred_element_type=F32)
    local = r + pltpu.roll(r, LANES - heads, 1) + pltpu.roll(r, LANES - 2 * heads, 1)

    @pl.when(t == 0)
    def _():
        carry_ref[...] = jnp.zeros_like(carry_ref)

    c = jnp.where(lane < heads, local + carry_ref[0:1, :], 0.0)
    carry_ref[0:1, :] = c[tm - 1:tm, :]

    c2 = c * LOG2E
    c_hi = c2.astype(BF16).astype(F32)
    c_r = c2 - c_hi
    c_mid = c_r.astype(BF16).astype(F32)
    c_lo = (c_r - c_mid).astype(BF16).astype(F32)
    head_dim = attn_dim // heads
    ext = LANES - head_dim
    lane_e = lax.broadcasted_iota(I32, (tm, ext), 1)
    ones3 = jnp.where(lane_e < 3, 1.0, 0.0)
    for hh in range(heads):
        kext = jnp.where(lane_e == 0, -c_hi[:, hh:hh + 1],
                         jnp.where(lane_e == 1, -c_mid[:, hh:hh + 1],
                                   jnp.where(lane_e == 2, -c_lo[:, hh:hh + 1], 0.0)))
        sl = slice(hh * head_dim, (hh + 1) * head_dim)
        qa_ref[:, hh * LANES:(hh + 1) * LANES] = jnp.concatenate([qf[:, sl], ones3], axis=1).astype(BF16)
        ka_ref[:, hh * LANES:(hh + 1) * LANES] = jnp.concatenate([kf[:, sl], kext], axis=1).astype(BF16)

    cb_ref[...] = cb.astype(BF16)
    ucx_ref[...] = ucx.astype(BF16)
    half = d_model // 2
    for c in range(2):
        sgc_ref[:, c * half:(c + 1) * half] = jax.nn.sigmoid(mm(wg_ref, c * half, half)).astype(BF16)
        sga_ref[:, c * half:(c + 1) * half] = jax.nn.sigmoid(mm(wg_ref, d_model + c * half, half)).astype(BF16)

    vt = lax.dot_general(wvt_ref[...], h, (((1,), (1,)), ((), ())), preferred_element_type=F32).astype(BF16)
    ones_rows = jnp.ones((V_ONES_ROWS, tm), BF16)
    vt = jnp.concatenate([piece for hh in range(heads)
                          for piece in (vt[hh * head_dim:(hh + 1) * head_dim], ones_rows)], axis=0)
    for u in range(tm // tk):
        vt_ref[u] = vt[:, u * tk:(u + 1) * tk]


def _inproj(x2d, g_mix, wa, wqk, wvt, wg, wf, bf_pad, *, batch, seq, heads, tm, tk):
    n, d_model = x2d.shape
    conv_dim = wa.shape[1] // 3
    attn_dim = wvt.shape[0]
    tiles_per_batch = seq // tm
    grid = (n // tm,)
    full = lambda shape: pl.BlockSpec(shape, lambda i: (0,) * len(shape))
    rows = lambda w: pl.BlockSpec((tm, w), lambda i: (i, 0))
    kern = functools.partial(_inproj_kernel, tiles_per_batch=tiles_per_batch, conv_dim=conv_dim,
                             attn_dim=attn_dim, d_model=d_model, heads=heads, tk=tk)
    vt_rows = attn_dim + heads * V_ONES_ROWS
    out_shape = (
        jax.ShapeDtypeStruct((n, conv_dim), BF16),
        jax.ShapeDtypeStruct((n, conv_dim), BF16),
        jax.ShapeDtypeStruct((n, heads * LANES), BF16),
        jax.ShapeDtypeStruct((n, heads * LANES), BF16),
        jax.ShapeDtypeStruct((n // tk, vt_rows, tk), BF16),
        jax.ShapeDtypeStruct((n, d_model), BF16),
        jax.ShapeDtypeStruct((n, d_model), BF16),
    )
    out_specs = (
        rows(conv_dim), rows(conv_dim), rows(heads * LANES), rows(heads * LANES),
        pl.BlockSpec((tm // tk, vt_rows, tk), lambda i: (i, 0, 0)),
        rows(d_model), rows(d_model),
    )
    return pl.pallas_call(
        kern,
        grid=grid,
        in_specs=[rows(d_model), full((1, d_model)), full(wa.shape), full(wqk.shape), full(wvt.shape),
                  full(wg.shape), full(wf.shape), full((1, LANES))],
        out_specs=out_specs,
        out_shape=out_shape,
        scratch_shapes=[pltpu.VMEM((8, LANES), F32)],
        compiler_params=_params(("arbitrary",)),
        name="inproj",
    )(x2d, g_mix, wa, wqk, wvt, wg, wf, bf_pad)


def _attn_kernel(q_ref, k_ref, v_ref, o_ref, s_ref, *, tq, tk, head_dim, heads):
    qi = pl.program_id(1)
    row = lax.broadcasted_iota(I32, (tk, tq), 0)
    col = lax.broadcasted_iota(I32, (tk, tq), 1)

    def score_tile(j, slot):
        off = pl.multiple_of(j * tk, tk)
        for hh in range(heads):
            qa = q_ref[:, hh * LANES:(hh + 1) * LANES]
            ka = k_ref[pl.ds(off, tk), hh * LANES:(hh + 1) * LANES]
            s_ref[slot, hh] = lax.dot_general(ka, qa, (((1,), (1,)), ((), ())), preferred_element_type=F32)

    def block(j, slot, carry, masked):
        if not masked:
            score_tile(j + 1, 1 - slot)
        stats = []
        for hh in range(heads):
            m, _ = carry[hh]
            s = s_ref[slot, hh]
            if masked:
                s = jnp.where(row <= col, s, -jnp.inf)
            m_new = jnp.maximum(m, jnp.max(s, axis=0, keepdims=True))
            stats.append((m_new, jnp.exp2(m - m_new), jnp.exp2(s - m_new).astype(BF16)))
        out = []
        for hh in range(heads):
            m_new, alpha, p = stats[hh]
            vth = v_ref[j, hh * vrows:(hh + 1) * vrows, :]
            out.append((m_new, alpha * carry[hh][1] + jnp.dot(vth, p, preferred_element_type=F32)))
        return tuple(out)

    vrows = head_dim + V_ONES_ROWS
    init = tuple((jnp.full((1, tq), -jnp.inf, F32), jnp.zeros((vrows, tq), F32)) for _ in range(heads))
    score_tile(0, 0)

    def pair(p, c):
        return block(2 * p + 1, 1, block(2 * p, 0, c, False), False)

    carry = lax.fori_loop(0, qi // 2, pair, init)
    final = lax.cond(
        lax.rem(qi, 2) == 0,
        lambda c: block(qi, 0, c, True),
        lambda c: block(qi, 1, block(qi - 1, 0, c, False), True),
        carry)
    per_group = LANES // head_dim
    for g in range(heads // per_group):
        accs = [final[g * per_group + u][1] for u in range(per_group)]
        ot = jnp.concatenate([a[:head_dim] / a[head_dim:head_dim + 1] for a in accs],
                             axis=0)
        o_ref[:, g * LANES:(g + 1) * LANES] = ot.T.astype(BF16)


def _attention(qa, ka, vt, *, batch, seq, heads, tq):
    n = qa.shape[0]
    _, vt_rows, tk = vt.shape
    head_dim = vt_rows // heads - V_ONES_ROWS
    attn_dim = heads * head_dim
    assert LANES % head_dim == 0 and tq == tk
    nq = seq // tq
    nk = seq // tk
    kern = functools.partial(_attn_kernel, tq=tq, tk=tk, head_dim=head_dim, heads=heads)
    return pl.pallas_call(
        kern,
        grid=(batch, nq),
        in_specs=[
            pl.BlockSpec((tq, heads * LANES), lambda b, qi: (b * nq + qi, 0)),
            pl.BlockSpec((seq, heads * LANES), lambda b, qi: (b, 0)),
            pl.BlockSpec((nk, vt_rows, tk), lambda b, qi: (b, 0, 0)),
        ],
        out_specs=pl.BlockSpec((tq, attn_dim), lambda b, qi: (b * nq + qi, 0)),
        out_shape=jax.ShapeDtypeStruct((n, attn_dim), BF16),
        scratch_shapes=[pltpu.VMEM((2, heads, tk, tq), F32)],
        compiler_params=_params(("arbitrary", "arbitrary")),
        name="attn",
    )(qa, ka, vt)


def _mix_kernel(x_ref, cb_ref, u_ref, uh_ref, o_ref, sgc_ref, sga_ref, cw_ref, wco_ref, wao_ref, wout_ref,
                gffn_ref, wr_ref, br_ref,
                x1_ref, h2p_ref, topi_ref, gate_ref, rank_ref, cnt_ref, carry_ref,
                *, tiles_per_batch, tiles_per_group, n_experts):
    tm, d_model = x_ref.shape
    i = pl.program_id(0)
    first = lax.rem(i, tiles_per_batch) == 0

    u = u_ref[...].astype(F32)
    halo = jnp.where(first, 0.0, uh_ref[...].astype(F32))
    ext = jnp.concatenate([halo, u], axis=0)
    hs = halo.shape[0]
    u1 = pltpu.roll(ext, 1, 0)[hs:]
    u2 = pltpu.roll(ext, 2, 0)[hs:]
    cw = cw_ref[...]
    conv = cw[0:1, :] * u2 + cw[1:2, :] * u1 + cw[2:3, :] * u
    yc = (cb_ref[...].astype(F32) * conv).astype(BF16)
    y_conv = jnp.dot(yc, wco_ref[...], preferred_element_type=F32)
    y_attn = jnp.dot(o_ref[...], wao_ref[...], preferred_element_type=F32)
    mixed = (sgc_ref[...].astype(F32) * y_conv + sga_ref[...].astype(F32) * y_attn).astype(BF16)
    x1 = x_ref[...] + jnp.dot(mixed, wout_ref[...], preferred_element_type=F32)
    x1_ref[...] = x1

    ms = jnp.mean(x1 * x1, axis=-1, keepdims=True)
    h2 = (x1 * lax.rsqrt(ms + RMS_EPS)) * gffn_ref[...]
    hb = h2.astype(BF16)
    hbf = hb.astype(F32)
    half = d_model // 2
    h2p_ref[...] = _pack_bf16_pair(hbf[:, :half], hbf[:, half:])

    hlo = (h2 - hbf).astype(BF16)
    nt_dims = (((1,), (1,)), ((), ()))
    rt = (lax.dot_general(wr_ref[...], hb, nt_dims, preferred_element_type=F32)
          + lax.dot_general(wr_ref[...], hlo, nt_dims, preferred_element_type=F32))
    lg = rt[:n_experts] + rt[n_experts:] + br_ref[...]
    sub = lax.broadcasted_iota(I32, (n_experts, tm), 0)

    vals, idxs = [], []
    for _ in range(TOP_K):
        m = jnp.max(lg, axis=0, keepdims=True)
        idx = jnp.min(jnp.where(lg == m, sub, n_experts), axis=0, keepdims=True)
        vals.append(m)
        idxs.append(idx)
        lg = jnp.where(sub == idx, -jnp.inf, lg)
    es = [jnp.exp(vk - vals[0]) for vk in vals]
    denom = es[0] + es[1] + es[2] + es[3]

    sub8 = lax.broadcasted_iota(I32, (8, tm), 0)
    subl = lax.broadcasted_iota(I32, (LANES, tm), 0)
    chosen = jnp.zeros((n_experts, tm), F32)
    topi = jnp.zeros((8, tm), I32)
    gates_t = jnp.zeros((LANES, tm), F32)
    for kk in range(TOP_K):
        chosen = jnp.where(sub == idxs[kk], 1.0, chosen)
        topi = jnp.where(sub8 == kk, idxs[kk], topi)
        gates_t = jnp.where(subl == kk, es[kk] / denom, gates_t)
    topi_ref[...] = topi
    gate_ref[...] = gates_t.T

    @pl.when(lax.rem(i, tiles_per_group) == 0)
    def _():
        carry_ref[...] = jnp.zeros_like(carry_ref)

    row = lax.broadcasted_iota(I32, (tm, tm), 0)
    col = lax.broadcasted_iota(I32, (tm, tm), 1)
    tri = jnp.where(row < col, 1.0, 0.0).astype(BF16)
    before = jnp.dot(chosen.astype(BF16), tri, preferred_element_type=F32) + carry_ref[:, 0:1]
    rank = jnp.zeros((8, tm), F32)
    for kk in range(TOP_K):
        rk = jnp.sum(jnp.where(sub == idxs[kk], before, 0.0), axis=0, keepdims=True)
        rank = jnp.where(sub8 == kk, rk, rank)
    rank_ref[...] = rank.astype(I32)
    total = carry_ref[...] + jnp.sum(chosen, axis=1, keepdims=True)
    carry_ref[...] = total
    cnt_ref[0] = total


def _mix(x2d, cb, ucx, o, sgc, sga, conv_w, wco, wao, wout, g_ffn, wr, br, *, seq, n_experts, tm, tile0, nt,
         tiles_per_group):
    d_model = x2d.shape[1]
    n = nt * tm
    conv_dim = cb.shape[1]
    attn_dim = o.shape[1]
    tiles_per_batch = seq // tm
    assert tile0 % tiles_per_batch == 0
    hb = tm // BF16_SUBLANES
    full = lambda shape: pl.BlockSpec(shape, lambda i: (0,) * len(shape))
    rows_in = lambda w: pl.BlockSpec((tm, w), lambda i: (i + tile0, 0))
    rows = lambda w: pl.BlockSpec((tm, w), lambda i: (i, 0))
    kern = functools.partial(_mix_kernel, tiles_per_batch=tiles_per_batch, tiles_per_group=tiles_per_group,
                             n_experts=n_experts)
    out_shape = (
        jax.ShapeDtypeStruct((n, d_model), F32),
        jax.ShapeDtypeStruct((n, d_model // 2), U32),
        jax.ShapeDtypeStruct((8, n), I32),
        jax.ShapeDtypeStruct((n, LANES), F32),
        jax.ShapeDtypeStruct((8, n), I32),
        jax.ShapeDtypeStruct((nt, n_experts, LANES), F32),
    )
    cols = pl.BlockSpec((8, tm), lambda i: (0, i))
    out_specs = (rows(d_model), rows(d_model // 2), cols, rows(LANES), cols,
                 pl.BlockSpec((1, n_experts, LANES), lambda i: (i, 0, 0)))
    return pl.pallas_call(
        kern,
        grid=(nt,),
        in_specs=[rows_in(d_model), rows_in(conv_dim), rows_in(conv_dim),
                  pl.BlockSpec((BF16_SUBLANES, conv_dim), lambda i: (jnp.maximum((i + tile0) * hb - 1, 0), 0)),
                  rows_in(attn_dim), rows_in(d_model), rows_in(d_model),
                  full(conv_w.shape), full(wco.shape), full(wao.shape), full(wout.shape),
                  full((1, d_model)), full(wr.shape), full((n_experts, 1))],
        out_specs=out_specs,
        out_shape=out_shape,
        scratch_shapes=[pltpu.VMEM((n_experts, LANES), F32)],
        compiler_params=_params(("arbitrary",)),
        name="mix",
    )(x2d, cb, ucx, ucx, o, sgc, sga, conv_w, wco, wao, wout, g_ffn, wr, br)


SC_CORES = 2
SC_SUBCORES = 16
SC_WORKERS = SC_CORES * SC_SUBCORES
SC_CHUNK = 64


def _sc_mesh():
    return plsc.VectorSubcoreMesh(core_axis_name="c", subcore_axis_name="s",
                                  num_cores=SC_CORES, num_subcores=SC_SUBCORES)


def _sc_worker():
    return lax.axis_index("s") * SC_CORES + lax.axis_index("c")


def _sc_scatter_rows(src, dest4, pad_idx, n_out, row0):
    width = src.shape[1]
    nchunks = dest4.shape[1]
    npad = pad_idx.shape[1]

    @functools.partial(
        pl.kernel, mesh=_sc_mesh(),
        out_type=jax.ShapeDtypeStruct((n_out, width), src.dtype),
        scratch_types=[pltpu.VMEM((TOP_K, SC_CHUNK), I32), pltpu.VMEM((SC_CHUNK, width), src.dtype),
                       pltpu.VMEM((npad, SC_CHUNK), I32), pltpu.SemaphoreType.DMA],
        name="sc_scatter_rows",
    )
    def k(src_hbm, dest_hbm, pad_hbm, out_hbm, idx_v, rows_v, pad_v, sem):
        wid = _sc_worker()
        base = row0 + wid * (nchunks * SC_CHUNK)

        @pl.loop(0, nchunks)
        def _(j):
            pltpu.sync_copy(src_hbm.at[pl.ds(base + j * SC_CHUNK, SC_CHUNK)], rows_v)
            pltpu.sync_copy(dest_hbm.at[wid, j], idx_v)
            copies = [pltpu.async_copy(rows_v, out_hbm.at[idx_v.at[kk]], sem) for kk in range(TOP_K)]
            for cp in copies:
                cp.wait()

        pltpu.sync_copy(pad_hbm.at[wid], pad_v)
        fills = [pltpu.async_copy(rows_v, out_hbm.at[pad_v.at[p]], sem) for p in range(npad)]
        for cp in fills:
            cp.wait()

    return k(src, dest4, pad_idx)


def _sc_gather_rows(src, dest4, n):
    width = src.shape[1]
    nchunks = dest4.shape[1]

    @functools.partial(
        pl.kernel, mesh=_sc_mesh(),
        out_type=jax.ShapeDtypeStruct((TOP_K, n, width), src.dtype),
        scratch_types=[pltpu.VMEM((TOP_K, SC_CHUNK), I32), pltpu.VMEM((SC_CHUNK, width), src.dtype),
                       pltpu.VMEM((SC_CHUNK, width), src.dtype), pltpu.SemaphoreType.DMA, pltpu.SemaphoreType.DMA],
        name="sc_gather_rows",
    )
    def k(src_hbm, dest_hbm, out_hbm, idx_v, rows_a, rows_b, sem_a, sem_b):
        wid = _sc_worker()
        base = wid * (nchunks * SC_CHUNK)
        bufs = ((rows_a, sem_a), (rows_b, sem_b))

        @pl.loop(0, nchunks)
        def _(j):
            pltpu.sync_copy(dest_hbm.at[wid, j], idx_v)
            pending = pltpu.async_copy(src_hbm.at[idx_v.at[0]], rows_a, sem_a)
            for kk in range(TOP_K):
                buf, _ = bufs[kk % 2]
                pending.wait()
                if kk + 1 < TOP_K:
                    nbuf, nsem = bufs[(kk + 1) % 2]
                    pending = pltpu.async_copy(src_hbm.at[idx_v.at[kk + 1]], nbuf, nsem)
                pltpu.sync_copy(buf, out_hbm.at[kk, pl.ds(base + j * SC_CHUNK, SC_CHUNK)])

    return k(src, dest4)


def _expert_kernel(be_ref, nu_ref, xs_ref, wgu_ref, bgu_ref, wd_ref, bd_ref, ys_ref, wgu_bf, wd_bf, *, d_ff, ff_chunk):
    i = pl.program_id(0)

    @pl.when((i == 0) | (be_ref[i] != be_ref[jnp.maximum(i - 1, 0)]))
    def _():
        wgu_bf[...] = wgu_ref[0].astype(BF16)
        wd_bf[...] = wd_ref[0].astype(BF16)

    @pl.when(i < nu_ref[0])
    def _():
        lo, hi = _unpack_bf16_pair(xs_ref[...])
        xb = jnp.concatenate([lo.astype(BF16), hi.astype(BF16)], axis=1)
        y = bd_ref[0]
        for c0 in range(0, d_ff, ff_chunk):
            glu = jnp.dot(xb, wgu_bf[:, c0:c0 + ff_chunk], preferred_element_type=F32) + bgu_ref[0, :, c0:c0 + ff_chunk]
            lin = (jnp.dot(xb, wgu_bf[:, d_ff + c0:d_ff + c0 + ff_chunk], preferred_element_type=F32)
                   + bgu_ref[0, :, d_ff + c0:d_ff + c0 + ff_chunk])
            glu = jnp.minimum(glu, SWIGLU_LIMIT)
            lin = jnp.clip(lin, -SWIGLU_LIMIT, SWIGLU_LIMIT)
            act = (glu * jax.nn.sigmoid(SWIGLU_ALPHA * glu)) * (lin + 1.0)
            y = y + jnp.dot(act.astype(BF16), wd_bf[c0:c0 + ff_chunk, :], preferred_element_type=F32)
        yb = y.astype(BF16).astype(F32)
        half = y.shape[1] // 2
        ys_ref[...] = _pack_bf16_pair(yb[:, :half], yb[:, half:])


def _experts(xs, block_e, n_used, wgu, bgu, wd, bd, *, tm, n_blocks):
    half = xs.shape[1]
    p = n_blocks * tm
    e, d_model, two_ff = wgu.shape
    d_ff = two_ff // 2
    kern = functools.partial(_expert_kernel, d_ff=d_ff, ff_chunk=min(512, d_ff))
    blk = lambda i, be, nu: (jnp.minimum(i, nu[0] - 1), 0)
    grid_spec = pltpu.PrefetchScalarGridSpec(
        num_scalar_prefetch=2,
        grid=(n_blocks,),
        in_specs=[
            pl.BlockSpec((tm, half), blk),
            pl.BlockSpec((1, d_model, two_ff), lambda i, be, nu: (be[i], 0, 0)),
            pl.BlockSpec((1, 1, two_ff), lambda i, be, nu: (be[i], 0, 0)),
            pl.BlockSpec((1, d_ff, d_model), lambda i, be, nu: (be[i], 0, 0)),
            pl.BlockSpec((1, 1, d_model), lambda i, be, nu: (be[i], 0, 0)),
        ],
        out_specs=pl.BlockSpec((tm, half), blk),
        scratch_shapes=[pltpu.VMEM((d_model, two_ff), BF16), pltpu.VMEM((d_ff, d_model), BF16)],
    )
    return pl.pallas_call(
        kern,
        grid_spec=grid_spec,
        out_shape=jax.ShapeDtypeStruct((p, half), U32),
        compiler_params=_params(("arbitrary",)),
        name="experts",
    )(block_e, n_used, xs, wgu, bgu.reshape(e, 1, two_ff), wd, bd.reshape(e, 1, d_model))


def _final_kernel(yk_ref, gate_ref, x1_ref, g_ref, out_ref):
    tm, d_model = x1_ref.shape
    half = d_model // 2
    gates = gate_ref[...]
    acc_lo = jnp.zeros((tm, half), F32)
    acc_hi = jnp.zeros((tm, half), F32)
    for kk in range(TOP_K):
        lo, hi = _unpack_bf16_pair(yk_ref[kk])
        gk = gates[:, kk:kk + 1]
        acc_lo = acc_lo + gk * lo
        acc_hi = acc_hi + gk * hi
    x2 = x1_ref[...] + jnp.concatenate([acc_lo, acc_hi], axis=1)
    ms = jnp.mean(x2 * x2, axis=-1, keepdims=True)
    out_ref[...] = (x2 * lax.rsqrt(ms + RMS_EPS)) * g_ref[...]


def _final(yk, gates, x1, g_final, *, tm, tile0):
    n, d_model = x1.shape
    rows = lambda w: pl.BlockSpec((tm, w), lambda i: (i + tile0, 0))
    return pl.pallas_call(
        _final_kernel,
        grid=(yk.shape[1] // tm,),
        in_specs=[pl.BlockSpec((TOP_K, tm, d_model // 2), lambda i: (0, i, 0)), rows(LANES), rows(d_model),
                  pl.BlockSpec((1, d_model), lambda i: (0, 0))],
        out_specs=rows(d_model),
        out_shape=jax.ShapeDtypeStruct((n, d_model), F32),
        input_output_aliases={2: 0},
        compiler_params=_params(("arbitrary",)),
        name="final",
    )(yk, gates, x1, g_final)


def _tiles(seq):
    tm = min(512, seq)
    return dict(tm_proj=tm, t_attn=min(256, seq), tm_mix=tm, tm_expert=512, tm_final=tm, moe_groups=2)


def _forward(x, g_mix, w_in, conv_w, b_f, w_conv_o, w_attn_o, w_out, g_ffn, w_router, b_router,
             w_gate_up, b_gate_up, w_down, b_down, g_final, tiles):
    batch, seq, d_model = x.shape
    n = batch * seq
    conv_dim = conv_w.shape[1]
    attn_dim = w_attn_o.shape[0]
    heads = b_f.shape[0]
    head_dim = attn_dim // heads
    n_experts = w_router.shape[1]
    x2d = x.reshape(n, d_model)

    c0 = 3 * conv_dim
    a0 = c0 + 3 * attn_dim
    wa = w_in[:, :c0].astype(BF16)
    scale = LOG2E / (head_dim ** 0.5)
    wqk = jnp.concatenate([w_in[:, c0:c0 + attn_dim] * scale, w_in[:, c0 + attn_dim:c0 + 2 * attn_dim]],
                          axis=1).astype(BF16)
    wvt = w_in[:, c0 + 2 * attn_dim:a0].T.astype(BF16)
    wf = jnp.pad(w_in[:, a0:a0 + heads], ((0, 0), (0, LANES - heads))).astype(BF16)
    wg = w_in[:, a0 + heads:].astype(BF16)
    bf_pad = jnp.pad(b_f, (0, LANES - heads)).reshape(1, LANES)
    wr_hi = w_router.astype(BF16)
    wr_lo = (w_router - wr_hi.astype(F32)).astype(BF16)
    wr = jnp.concatenate([wr_hi, wr_lo], axis=1).T
    br = b_router.reshape(n_experts, 1)

    cb, ucx, qa, ka, vt, sgc, sga = _inproj(
        x2d, g_mix.reshape(1, d_model), wa, wqk, wvt, wg, wf, bf_pad,
        batch=batch, seq=seq, heads=heads, tm=tiles["tm_proj"], tk=tiles["t_attn"])
    o = _attention(qa, ka, vt, batch=batch, seq=seq, heads=heads, tq=tiles["t_attn"])
    groups = tiles["moe_groups"]
    ng = n // groups
    x1, h2p, topi, gates, rank, cnt = _mix(
        x2d, cb, ucx, o, sgc, sga, conv_w, w_conv_o.astype(BF16), w_attn_o.astype(BF16), w_out.astype(BF16),
        g_ffn.reshape(1, d_model), wr, br, seq=seq, n_experts=n_experts, tm=tiles["tm_mix"],
        tile0=0, nt=n // tiles["tm_mix"], tiles_per_group=ng // tiles["tm_mix"])

    tme = tiles["tm_expert"]
    n_blocks = (ng * TOP_K) // tme + n_experts
    p_rows = n_blocks * tme
    nchunks = ng // (SC_WORKERS * SC_CHUNK)
    jj = jnp.arange(tme, dtype=I32)[None, :]
    spare = p_rows + jnp.arange(n_experts, dtype=I32)[:, None] * tme + jj
    expert_ids = jnp.arange(n_experts, dtype=I32)[None, None, :]
    ys_groups, dest_groups = [], []
    for g in range(groups):
        counts = cnt[(g + 1) * (ng // tiles["tm_mix"]) - 1, :, 0].astype(I32)
        padded = ((counts + tme - 1) // tme) * tme
        pend = jnp.cumsum(padded)
        pstart = pend - padded
        n_used = (pend[-1] // tme).astype(I32).reshape(1)
        blk_start = jnp.minimum(jnp.arange(n_blocks, dtype=I32), n_used[0] - 1) * tme
        block_e = jnp.minimum(jnp.sum((pend[None, :] <= blk_start[:, None]).astype(I32), axis=1), n_experts - 1)
        tok = slice(g * ng, (g + 1) * ng)
        onehot = topi[:TOP_K, tok, None] == expert_ids
        dest = jnp.sum(jnp.where(onehot, pstart[None, None, :], 0), axis=2) + rank[:TOP_K, tok]
        dest4 = dest.reshape(TOP_K, SC_WORKERS, nchunks, SC_CHUNK).transpose(1, 2, 0, 3)
        pad_idx = jnp.where(jj < (padded - counts)[:, None], (pstart + counts)[:, None] + jj, spare)
        pad_idx = pad_idx.reshape(SC_WORKERS, (n_experts * tme) // (SC_WORKERS * SC_CHUNK), SC_CHUNK)
        xs = _sc_scatter_rows(h2p, dest4, pad_idx, p_rows + n_experts * tme, g * ng)
        ys_groups.append(_experts(xs, block_e, n_used, w_gate_up, b_gate_up, w_down, b_down,
                                  tm=tme, n_blocks=n_blocks))
        dest_groups.append(dest4)

    out = x1
    for g in range(groups):
        yk = _sc_gather_rows(ys_groups[g], dest_groups[g], ng)
        out = _final(yk, gates, out, g_final.reshape(1, d_model), tm=tiles["tm_final"],
                     tile0=g * (ng // tiles["tm_final"]))
    return out.reshape(batch, seq, d_model)


def kernel(x, g_mix, w_in, conv_w, b_f, w_conv_o, w_attn_o, w_out, g_ffn, w_router, b_router, w_gate_up,
           b_gate_up, w_down, b_down, g_final):
    return _forward(x, g_mix, w_in, conv_w, b_f, w_conv_o, w_attn_o, w_out, g_ffn, w_router, b_router,
                    w_gate_up, b_gate_up, w_down, b_down, g_final, _tiles(x.shape[1]))
```

```python
import functools

import jax
import jax.numpy as jnp
from jax import lax
from jax.experimental import pallas as pl
from jax.experimental.pallas import tpu as pltpu
from jax.experimental.pallas import tpu_sc as plsc

TOP_K = 4
RMS_EPS = 1e-5
SWIGLU_ALPHA = 1.702
SWIGLU_LIMIT = 7.0
LOG2E = 1.4426950408889634

LANES = 128
BF16_SUBLANES = 16
V_ONES_ROWS = BF16_SUBLANES
VMEM_LIMIT_BYTES = 56 * 1024 * 1024

F32 = jnp.float32
BF16 = jnp.bfloat16
U32 = jnp.uint32
I32 = jnp.int32
HI_MASK = 0xFFFF0000


def _params(sem):
    return pltpu.CompilerParams(dimension_semantics=sem, vmem_limit_bytes=VMEM_LIMIT_BYTES)


def _as_words(x_bf16):
    return pltpu.bitcast(x_bf16, U32)


def _as_bf16(w_u32):
    return pltpu.bitcast(w_u32, BF16)


def _pack_bf16_pair(lo_f32, hi_f32):
    lo = lax.bitcast_convert_type(lo_f32, U32)
    hi = lax.bitcast_convert_type(hi_f32, U32)
    return (lo >> 16) | (hi & U32(HI_MASK))


def _unpack_bf16_pair(w):
    lo = lax.bitcast_convert_type(w << 16, F32)
    hi = lax.bitcast_convert_type(w & U32(HI_MASK), F32)
    return lo, hi


def _inproj_kernel(x_ref, g_ref, wa_ref, wqk_ref, wvt_ref, wg_ref, wf_ref, bf_ref,
                   cb_ref, ucx_ref, qa_ref, ka_ref, vt_ref, sgc_ref, sga_ref,
                   carry_ref, *, tiles_per_batch, conv_dim, attn_dim, d_model, heads, tk):
    tm = x_ref.shape[0]
    t = lax.rem(pl.program_id(0), tiles_per_batch)
    x = x_ref[...]
    ms = jnp.mean(x * x, axis=-1, keepdims=True)
    h = ((x * lax.rsqrt(ms + RMS_EPS)) * g_ref[...]).astype(BF16)

    def mm(w_ref, c0, n):
        return jnp.dot(h, w_ref[:, c0:c0 + n], preferred_element_type=F32)

    z = mm(wf_ref, 0, LANES) + bf_ref[...]
    qf = mm(wqk_ref, 0, attn_dim)
    kf = mm(wqk_ref, attn_dim, attn_dim)
    cb = mm(wa_ref, 0, conv_dim)
    ucx = mm(wa_ref, conv_dim, conv_dim) * mm(wa_ref, 2 * conv_dim, conv_dim)

    lane = lax.broadcasted_iota(I32, z.shape, 1)
    logf = jnp.minimum(z, 0.0) - jnp.log1p(jnp.exp(-jnp.abs(z)))
    logf = jnp.where(lane < heads, logf, 0.0)
    p1 = logf.astype(BF16).astype(F32)
    r1 = logf - p1
    p2 = r1.astype(BF16).astype(F32)
    p3 = (r1 - p2).astype(BF16).astype(F32)
    packed = (p1 + pltpu.roll(p2, heads, 1) + pltpu.roll(p3, 2 * heads, 1)).astype(BF16)
    row = lax.broadcasted_iota(I32, (tm, tm), 0)
    col = lax.broadcasted_iota(I32, (tm, tm), 1)
    tri = jnp.where(col <= row, 1.0, 0.0).astype(BF16)
    r = jnp.dot(tri, packed, preferred_element_type=F32)
    local = r + pltpu.roll(r, LANES - heads, 1) + pltpu.roll(r, LANES - 2 * heads, 1)

    @pl.when(t == 0)
    def _():
        carry_ref[...] = jnp.zeros_like(carry_ref)

    c = jnp.where(lane < heads, local + carry_ref[0:1, :], 0.0)
    carry_ref[0:1, :] = c[tm - 1:tm, :]

    c2 = c * LOG2E
    c_hi = c2.astype(BF16).astype(F32)
    c_r = c2 - c_hi
    c_mid = c_r.astype(BF16).astype(F32)
    c_lo = (c_r - c_mid).astype(BF16).astype(F32)
    head_dim = attn_dim // heads
    ext = LANES - head_dim
    lane_e = lax.broadcasted_iota(I32, (tm, ext), 1)
    ones3 = jnp.where(lane_e < 3, 1.0, 0.0)
    for hh in range(heads):
        kext = jnp.where(lane_e == 0, -c_hi[:, hh:hh + 1],
                         jnp.where(lane_e == 1, -c_mid[:, hh:hh + 1],
                                   jnp.where(lane_e == 2, -c_lo[:, hh:hh + 1], 0.0)))
        sl = slice(hh * head_dim, (hh + 1) * head_dim)
        qa_ref[:, hh * LANES:(hh + 1) * LANES] = _as_words(jnp.concatenate([qf[:, sl], ones3], axis=1).astype(BF16))
        ka_ref[:, hh * LANES:(hh + 1) * LANES] = _as_words(jnp.concatenate([kf[:, sl], kext], axis=1).astype(BF16))

    cb_ref[...] = _as_words(cb.astype(BF16))
    ucx_ref[...] = _as_words(ucx.astype(BF16))
    half = d_model // 2
    for c in range(2):
        sgc_ref[:, c * half:(c + 1) * half] = _as_words(jax.nn.sigmoid(mm(wg_ref, c * half, half)).astype(BF16))
        sga_ref[:, c * half:(c + 1) * half] = _as_words(
            jax.nn.sigmoid(mm(wg_ref, d_model + c * half, half)).astype(BF16))

    vt = lax.dot_general(wvt_ref[...], h, (((1,), (1,)), ((), ())), preferred_element_type=F32).astype(BF16)
    ones_rows = jnp.ones((V_ONES_ROWS, tm), BF16)
    vt = jnp.concatenate([piece for hh in range(heads)
                          for piece in (vt[hh * head_dim:(hh + 1) * head_dim], ones_rows)], axis=0)
    for u in range(tm // tk):
        vt_ref[u] = _as_words(vt[:, u * tk:(u + 1) * tk])


def _inproj(x2d, g_mix, wa, wqk, wvt, wg, wf, bf_pad, *, batch, seq, heads, tm, tk):
    n, d_model = x2d.shape
    conv_dim = wa.shape[1] // 3
    attn_dim = wvt.shape[0]
    tiles_per_batch = seq // tm
    grid = (n // tm,)
    full = lambda shape: pl.BlockSpec(shape, lambda i: (0,) * len(shape))
    rows = lambda w: pl.BlockSpec((tm, w), lambda i: (i, 0))
    kern = functools.partial(_inproj_kernel, tiles_per_batch=tiles_per_batch, conv_dim=conv_dim,
                             attn_dim=attn_dim, d_model=d_model, heads=heads, tk=tk)
    vt_rows = attn_dim + heads * V_ONES_ROWS
    words = lambda w: jax.ShapeDtypeStruct((n // 2, w), U32)
    wrows = lambda w: pl.BlockSpec((tm // 2, w), lambda i: (i, 0))
    out_shape = (
        words(conv_dim),
        words(conv_dim),
        words(heads * LANES),
        words(heads * LANES),
        jax.ShapeDtypeStruct((n // tk, vt_rows // 2, tk), U32),
        words(d_model),
        words(d_model),
    )
    out_specs = (
        wrows(conv_dim), wrows(conv_dim), wrows(heads * LANES), wrows(heads * LANES),
        pl.BlockSpec((tm // tk, vt_rows // 2, tk), lambda i: (i, 0, 0)),
        wrows(d_model), wrows(d_model),
    )
    return pl.pallas_call(
        kern,
        grid=grid,
        in_specs=[rows(d_model), full((1, d_model)), full(wa.shape), full(wqk.shape), full(wvt.shape),
                  full(wg.shape), full(wf.shape), full((1, LANES))],
        out_specs=out_specs,
        out_shape=out_shape,
        scratch_shapes=[pltpu.VMEM((8, LANES), F32)],
        compiler_params=_params(("arbitrary",)),
        name="inproj",
    )(x2d, g_mix, wa, wqk, wvt, wg, wf, bf_pad)


def _attn_kernel(q_ref, k_ref, v_ref, o_ref, s_ref, *, tq, tk, head_dim, heads):
    qi = pl.program_id(1)
    row = lax.broadcasted_iota(I32, (tk, tq), 0)
    col = lax.broadcasted_iota(I32, (tk, tq), 1)

    def score_tile(j, slot):
        off = pl.multiple_of(j * (tk // 2), tk // 2)
        for hh in range(heads):
            qa = _as_bf16(q_ref[:, hh * LANES:(hh + 1) * LANES])
            ka = _as_bf16(k_ref[pl.ds(off, tk // 2), hh * LANES:(hh + 1) * LANES])
            s_ref[slot, hh] = lax.dot_general(ka, qa, (((1,), (1,)), ((), ())), preferred_element_type=F32)

    def block(j, slot, carry, masked):
        if not masked:
            score_tile(j + 1, 1 - slot)
        stats = []
        for hh in range(heads):
            m, _ = carry[hh]
            s = s_ref[slot, hh]
            if masked:
                s = jnp.where(row <= col, s, -jnp.inf)
            m_new = jnp.maximum(m, jnp.max(s, axis=0, keepdims=True))
            stats.append((m_new, jnp.exp2(m - m_new), jnp.exp2(s - m_new).astype(BF16)))
        out = []
        for hh in range(heads):
            m_new, alpha, p = stats[hh]
            vth = _as_bf16(v_ref[j, hh * (vrows // 2):(hh + 1) * (vrows // 2), :])
            out.append((m_new, alpha * carry[hh][1] + jnp.dot(vth, p, preferred_element_type=F32)))
        return tuple(out)

    vrows = head_dim + V_ONES_ROWS
    init = tuple((jnp.full((1, tq), -jnp.inf, F32), jnp.zeros((vrows, tq), F32)) for _ in range(heads))
    score_tile(0, 0)

    def pair(p, c):
        return block(2 * p + 1, 1, block(2 * p, 0, c, False), False)

    carry = lax.fori_loop(0, qi // 2, pair, init)
    final = lax.cond(
        lax.rem(qi, 2) == 0,
        lambda c: block(qi, 0, c, True),
        lambda c: block(qi, 1, block(qi - 1, 0, c, False), True),
        carry)
    per_group = LANES // head_dim
    for g in range(heads // per_group):
        accs = [final[g * per_group + u][1] for u in range(per_group)]
        ot = jnp.concatenate([a[:head_dim] / a[head_dim:head_dim + 1] for a in accs],
                             axis=0)
        o_ref[:, g * LANES:(g + 1) * LANES] = _as_words(ot.T.astype(BF16))


def _attention(qa, ka, vt, *, batch, seq, heads, tq):
    n = 2 * qa.shape[0]
    vt_rows, tk = 2 * vt.shape[1], vt.shape[2]
    head_dim = vt_rows // heads - V_ONES_ROWS
    attn_dim = heads * head_dim
    assert LANES % head_dim == 0 and tq == tk
    nq = seq // tq
    nk = seq // tk
    kern = functools.partial(_attn_kernel, tq=tq, tk=tk, head_dim=head_dim, heads=heads)
    return pl.pallas_call(
        kern,
        grid=(batch, nq),
        in_specs=[
            pl.BlockSpec((tq // 2, heads * LANES), lambda b, qi: (b * nq + qi, 0)),
            pl.BlockSpec((seq // 2, heads * LANES), lambda b, qi: (b, 0)),
            pl.BlockSpec((nk, vt_rows // 2, tk), lambda b, qi: (b, 0, 0)),
        ],
        out_specs=pl.BlockSpec((tq // 2, attn_dim), lambda b, qi: (b * nq + qi, 0)),
        out_shape=jax.ShapeDtypeStruct((n // 2, attn_dim), U32),
        scratch_shapes=[pltpu.VMEM((2, heads, tk, tq), F32)],
        compiler_params=_params(("arbitrary", "arbitrary")),
        name="attn",
    )(qa, ka, vt)


def _mix_kernel(x_ref, cb_ref, u_ref, uh_ref, o_ref, sgc_ref, sga_ref, cw_ref, wco_ref, wao_ref, wout_ref,
                gffn_ref, wr_ref, br_ref,
                x1_ref, h2p_ref, topi_ref, gate_ref, rank_ref, cnt_ref, carry_ref,
                *, tiles_per_batch, tiles_per_group, n_experts):
    tm, d_model = x_ref.shape
    i = pl.program_id(0)
    first = lax.rem(i, tiles_per_batch) == 0

    u = _as_bf16(u_ref[...]).astype(F32)
    halo = jnp.where(first, 0.0, _as_bf16(uh_ref[...]).astype(F32))
    ext = jnp.concatenate([halo, u], axis=0)
    hs = halo.shape[0]
    u1 = pltpu.roll(ext, 1, 0)[hs:]
    u2 = pltpu.roll(ext, 2, 0)[hs:]
    cw = cw_ref[...]
    conv = cw[0:1, :] * u2 + cw[1:2, :] * u1 + cw[2:3, :] * u
    yc = (_as_bf16(cb_ref[...]).astype(F32) * conv).astype(BF16)
    y_conv = jnp.dot(yc, wco_ref[...], preferred_element_type=F32)
    y_attn = jnp.dot(_as_bf16(o_ref[...]), wao_ref[...], preferred_element_type=F32)
    mixed = (_as_bf16(sgc_ref[...]).astype(F32) * y_conv
             + _as_bf16(sga_ref[...]).astype(F32) * y_attn).astype(BF16)
    x1 = x_ref[...] + jnp.dot(mixed, wout_ref[...], preferred_element_type=F32)
    x1_ref[...] = x1

    ms = jnp.mean(x1 * x1, axis=-1, keepdims=True)
    h2 = (x1 * lax.rsqrt(ms + RMS_EPS)) * gffn_ref[...]
    hb = h2.astype(BF16)
    hbf = hb.astype(F32)
    half = d_model // 2
    h2p_ref[...] = _pack_bf16_pair(hbf[:, :half], hbf[:, half:])

    hlo = (h2 - hbf).astype(BF16)
    nt_dims = (((1,), (1,)), ((), ()))
    rt = (lax.dot_general(wr_ref[...], hb, nt_dims, preferred_element_type=F32)
          + lax.dot_general(wr_ref[...], hlo, nt_dims, preferred_element_type=F32))
    lg = rt[:n_experts] + rt[n_experts:] + br_ref[...]
    sub = lax.broadcasted_iota(I32, (n_experts, tm), 0)

    vals, idxs = [], []
    for _ in range(TOP_K):
        m = jnp.max(lg, axis=0, keepdims=True)
        idx = jnp.min(jnp.where(lg == m, sub, n_experts), axis=0, keepdims=True)
        vals.append(m)
        idxs.append(idx)
        lg = jnp.where(sub == idx, -jnp.inf, lg)
    es = [jnp.exp(vk - vals[0]) for vk in vals]
    denom = es[0] + es[1] + es[2] + es[3]

    sub8 = lax.broadcasted_iota(I32, (8, tm), 0)
    subl = lax.broadcasted_iota(I32, (LANES, tm), 0)
    chosen = jnp.zeros((n_experts, tm), F32)
    topi = jnp.zeros((8, tm), I32)
    gates_t = jnp.zeros((LANES, tm), F32)
    for kk in range(TOP_K):
        chosen = jnp.where(sub == idxs[kk], 1.0, chosen)
        topi = jnp.where(sub8 == kk, idxs[kk], topi)
        gates_t = jnp.where(subl == kk, es[kk] / denom, gates_t)
    topi_ref[...] = topi
    gate_ref[...] = gates_t.T

    @pl.when(lax.rem(i, tiles_per_group) == 0)
    def _():
        carry_ref[...] = jnp.zeros_like(carry_ref)

    row = lax.broadcasted_iota(I32, (tm, tm), 0)
    col = lax.broadcasted_iota(I32, (tm, tm), 1)
    tri = jnp.where(row < col, 1.0, 0.0).astype(BF16)
    before = jnp.dot(chosen.astype(BF16), tri, preferred_element_type=F32) + carry_ref[:, 0:1]
    rank = jnp.zeros((8, tm), F32)
    for kk in range(TOP_K):
        rk = jnp.sum(jnp.where(sub == idxs[kk], before, 0.0), axis=0, keepdims=True)
        rank = jnp.where(sub8 == kk, rk, rank)
    rank_ref[...] = rank.astype(I32)
    total = carry_ref[...] + jnp.sum(chosen, axis=1, keepdims=True)
    carry_ref[...] = total
    cnt_ref[0] = total


def _mix(x2d, cb, ucx, o, sgc, sga, conv_w, wco, wao, wout, g_ffn, wr, br, *, seq, n_experts, tm, tile0, nt,
         tiles_per_group):
    d_model = x2d.shape[1]
    n = nt * tm
    conv_dim = cb.shape[1]
    attn_dim = o.shape[1]
    tiles_per_batch = seq // tm
    assert tile0 % tiles_per_batch == 0
    hb = tm // BF16_SUBLANES
    full = lambda shape: pl.BlockSpec(shape, lambda i: (0,) * len(shape))
    rows_in = lambda w: pl.BlockSpec((tm, w), lambda i: (i + tile0, 0))
    words_in = lambda w: pl.BlockSpec((tm // 2, w), lambda i: (i + tile0, 0))
    rows = lambda w: pl.BlockSpec((tm, w), lambda i: (i, 0))
    kern = functools.partial(_mix_kernel, tiles_per_batch=tiles_per_batch, tiles_per_group=tiles_per_group,
                             n_experts=n_experts)
    out_shape = (
        jax.ShapeDtypeStruct((n, d_model), F32),
        jax.ShapeDtypeStruct((n, d_model // 2), U32),
        jax.ShapeDtypeStruct((8, n), I32),
        jax.ShapeDtypeStruct((n, LANES), F32),
        jax.ShapeDtypeStruct((8, n), I32),
        jax.ShapeDtypeStruct((nt, n_experts, LANES), F32),
    )
    cols = pl.BlockSpec((8, tm), lambda i: (0, i))
    out_specs = (rows(d_model), rows(d_model // 2), cols, rows(LANES), cols,
                 pl.BlockSpec((1, n_experts, LANES), lambda i: (i, 0, 0)))
    return pl.pallas_call(
        kern,
        grid=(nt,),
        in_specs=[rows_in(d_model), words_in(conv_dim), words_in(conv_dim),
                  pl.BlockSpec((BF16_SUBLANES // 2, conv_dim), lambda i: (jnp.maximum((i + tile0) * hb - 1, 0), 0)),
                  words_in(attn_dim), words_in(d_model), words_in(d_model),
                  full(conv_w.shape), full(wco.shape), full(wao.shape), full(wout.shape),
                  full((1, d_model)), full(wr.shape), full((n_experts, 1))],
        out_specs=out_specs,
        out_shape=out_shape,
        scratch_shapes=[pltpu.VMEM((n_experts, LANES), F32)],
        compiler_params=_params(("arbitrary",)),
        name="mix",
    )(x2d, cb, ucx, ucx, o, sgc, sga, conv_w, wco, wao, wout, g_ffn, wr, br)


SC_CORES = 2
SC_SUBCORES = 16
SC_WORKERS = SC_CORES * SC_SUBCORES
SC_CHUNK = 64


def _sc_mesh():
    return plsc.VectorSubcoreMesh(core_axis_name="c", subcore_axis_name="s",
                                  num_cores=SC_CORES, num_subcores=SC_SUBCORES)


def _sc_worker():
    return lax.axis_index("s") * SC_CORES + lax.axis_index("c")


def _sc_scatter_rows(src, dest4, pad_idx, n_out, row0):
    width = src.shape[1]
    nchunks = dest4.shape[1]
    npad = pad_idx.shape[1]

    @functools.partial(
        pl.kernel, mesh=_sc_mesh(),
        out_type=jax.ShapeDtypeStruct((n_out, width), src.dtype),
        scratch_types=[pltpu.VMEM((TOP_K, SC_CHUNK), I32), pltpu.VMEM((SC_CHUNK, width), src.dtype),
                       pltpu.VMEM((npad, SC_CHUNK), I32), pltpu.SemaphoreType.DMA],
        name="sc_scatter_rows",
    )
    def k(src_hbm, dest_hbm, pad_hbm, out_hbm, idx_v, rows_v, pad_v, sem):
        wid = _sc_worker()
        base = row0 + wid * (nchunks * SC_CHUNK)

        @pl.loop(0, nchunks)
        def _(j):
            pltpu.sync_copy(src_hbm.at[pl.ds(base + j * SC_CHUNK, SC_CHUNK)], rows_v)
            pltpu.sync_copy(dest_hbm.at[wid, j], idx_v)
            copies = [pltpu.async_copy(rows_v, out_hbm.at[idx_v.at[kk]], sem) for kk in range(TOP_K)]
            for cp in copies:
                cp.wait()

        pltpu.sync_copy(pad_hbm.at[wid], pad_v)
        fills = [pltpu.async_copy(rows_v, out_hbm.at[pad_v.at[p]], sem) for p in range(npad)]
        for cp in fills:
            cp.wait()

    return k(src, dest4, pad_idx)


def _sc_gather_rows(src, dest4, n):
    width = src.shape[1]
    nchunks = dest4.shape[1]

    @functools.partial(
        pl.kernel, mesh=_sc_mesh(),
        out_type=jax.ShapeDtypeStruct((TOP_K, n, width), src.dtype),
        scratch_types=[pltpu.VMEM((TOP_K, SC_CHUNK), I32), pltpu.VMEM((SC_CHUNK, width), src.dtype),
                       pltpu.VMEM((SC_CHUNK, width), src.dtype), pltpu.SemaphoreType.DMA, pltpu.SemaphoreType.DMA],
        name="sc_gather_rows",
    )
    def k(src_hbm, dest_hbm, out_hbm, idx_v, rows_a, rows_b, sem_a, sem_b):
        wid = _sc_worker()
        base = wid * (nchunks * SC_CHUNK)
        bufs = ((rows_a, sem_a), (rows_b, sem_b))

        @pl.loop(0, nchunks)
        def _(j):
            pltpu.sync_copy(dest_hbm.at[wid, j], idx_v)
            pending = pltpu.async_copy(src_hbm.at[idx_v.at[0]], rows_a, sem_a)
            for kk in range(TOP_K):
                buf, _ = bufs[kk % 2]
                pending.wait()
                if kk + 1 < TOP_K:
                    nbuf, nsem = bufs[(kk + 1) % 2]
                    pending = pltpu.async_copy(src_hbm.at[idx_v.at[kk + 1]], nbuf, nsem)
                pltpu.sync_copy(buf, out_hbm.at[kk, pl.ds(base + j * SC_CHUNK, SC_CHUNK)])

    return k(src, dest4)


def _expert_kernel(be_ref, nu_ref, xs_ref, wgu_ref, bgu_ref, wd_ref, bd_ref, ys_ref, wgu_bf, wd_bf, *, d_ff, ff_chunk):
    i = pl.program_id(0)

    @pl.when((i == 0) | (be_ref[i] != be_ref[jnp.maximum(i - 1, 0)]))
    def _():
        wgu_bf[...] = wgu_ref[0].astype(BF16)
        wd_bf[...] = wd_ref[0].astype(BF16)

    @pl.when(i < nu_ref[0])
    def _():
        lo, hi = _unpack_bf16_pair(xs_ref[...])
        xb = jnp.concatenate([lo.astype(BF16), hi.astype(BF16)], axis=1)
        y = bd_ref[0]
        for c0 in range(0, d_ff, ff_chunk):
            glu = jnp.dot(xb, wgu_bf[:, c0:c0 + ff_chunk], preferred_element_type=F32) + bgu_ref[0, :, c0:c0 + ff_chunk]
            lin = (jnp.dot(xb, wgu_bf[:, d_ff + c0:d_ff + c0 + ff_chunk], preferred_element_type=F32)
                   + bgu_ref[0, :, d_ff + c0:d_ff + c0 + ff_chunk])
            glu = jnp.minimum(glu, SWIGLU_LIMIT)
            lin = jnp.clip(lin, -SWIGLU_LIMIT, SWIGLU_LIMIT)
            act = (glu * jax.nn.sigmoid(SWIGLU_ALPHA * glu)) * (lin + 1.0)
            y = y + jnp.dot(act.astype(BF16), wd_bf[c0:c0 + ff_chunk, :], preferred_element_type=F32)
        yb = y.astype(BF16).astype(F32)
        half = y.shape[1] // 2
        ys_ref[...] = _pack_bf16_pair(yb[:, :half], yb[:, half:])


def _experts(xs, block_e, n_used, wgu, bgu, wd, bd, *, tm, n_blocks):
    half = xs.shape[1]
    p = n_blocks * tm
    e, d_model, two_ff = wgu.shape
    d_ff = two_ff // 2
    kern = functools.partial(_expert_kernel, d_ff=d_ff, ff_chunk=min(512, d_ff))
    blk = lambda i, be, nu: (jnp.minimum(i, nu[0] - 1), 0)
    grid_spec = pltpu.PrefetchScalarGridSpec(
        num_scalar_prefetch=2,
        grid=(n_blocks,),
        in_specs=[
            pl.BlockSpec((tm, half), blk),
            pl.BlockSpec((1, d_model, two_ff), lambda i, be, nu: (be[i], 0, 0)),
            pl.BlockSpec((1, 1, two_ff), lambda i, be, nu: (be[i], 0, 0)),
            pl.BlockSpec((1, d_ff, d_model), lambda i, be, nu: (be[i], 0, 0)),
            pl.BlockSpec((1, 1, d_model), lambda i, be, nu: (be[i], 0, 0)),
        ],
        out_specs=pl.BlockSpec((tm, half), blk),
        scratch_shapes=[pltpu.VMEM((d_model, two_ff), BF16), pltpu.VMEM((d_ff, d_model), BF16)],
    )
    return pl.pallas_call(
        kern,
        grid_spec=grid_spec,
        out_shape=jax.ShapeDtypeStruct((p, half), U32),
        compiler_params=_params(("arbitrary",)),
        name="experts",
    )(block_e, n_used, xs, wgu, bgu.reshape(e, 1, two_ff), wd, bd.reshape(e, 1, d_model))


def _final_kernel(yk_ref, gate_ref, x1_ref, g_ref, out_ref):
    tm, d_model = x1_ref.shape
    half = d_model // 2
    gates = gate_ref[...]
    acc_lo = jnp.zeros((tm, half), F32)
    acc_hi = jnp.zeros((tm, half), F32)
    for kk in range(TOP_K):
        lo, hi = _unpack_bf16_pair(yk_ref[kk])
        gk = gates[:, kk:kk + 1]
        acc_lo = acc_lo + gk * lo
        acc_hi = acc_hi + gk * hi
    x2 = x1_ref[...] + jnp.concatenate([acc_lo, acc_hi], axis=1)
    ms = jnp.mean(x2 * x2, axis=-1, keepdims=True)
    out_ref[...] = (x2 * lax.rsqrt(ms + RMS_EPS)) * g_ref[...]


def _final(yk, gates, x1, g_final, *, tm, tile0):
    n, d_model = x1.shape
    rows = lambda w: pl.BlockSpec((tm, w), lambda i: (i + tile0, 0))
    return pl.pallas_call(
        _final_kernel,
        grid=(yk.shape[1] // tm,),
        in_specs=[pl.BlockSpec((TOP_K, tm, d_model // 2), lambda i: (0, i, 0)), rows(LANES), rows(d_model),
                  pl.BlockSpec((1, d_model), lambda i: (0, 0))],
        out_specs=rows(d_model),
        out_shape=jax.ShapeDtypeStruct((n, d_model), F32),
        input_output_aliases={2: 0},
        compiler_params=_params(("arbitrary",)),
        name="final",
    )(yk, gates, x1, g_final)


def _tiles(seq):
    tm = min(512, seq)
    return dict(tm_proj=tm, t_attn=min(256, seq), tm_mix=tm, tm_expert=512, tm_final=tm, moe_groups=2)


def _forward(x, g_mix, w_in, conv_w, b_f, w_conv_o, w_attn_o, w_out, g_ffn, w_router, b_router,
             w_gate_up, b_gate_up, w_down, b_down, g_final, tiles):
    batch, seq, d_model = x.shape
    n = batch * seq
    conv_dim = conv_w.shape[1]
    attn_dim = w_attn_o.shape[0]
    heads = b_f.shape[0]
    head_dim = attn_dim // heads
    n_experts = w_router.shape[1]
    x2d = x.reshape(n, d_model)

    c0 = 3 * conv_dim
    a0 = c0 + 3 * attn_dim
    wa = w_in[:, :c0].astype(BF16)
    scale = LOG2E / (head_dim ** 0.5)
    wqk = jnp.concatenate([w_in[:, c0:c0 + attn_dim] * scale, w_in[:, c0 + attn_dim:c0 + 2 * attn_dim]],
                          axis=1).astype(BF16)
    wvt = w_in[:, c0 + 2 * attn_dim:a0].T.astype(BF16)
    wf = jnp.pad(w_in[:, a0:a0 + heads], ((0, 0), (0, LANES - heads))).astype(BF16)
    wg = w_in[:, a0 + heads:].astype(BF16)
    bf_pad = jnp.pad(b_f, (0, LANES - heads)).reshape(1, LANES)
    wr_hi = w_router.astype(BF16)
    wr_lo = (w_router - wr_hi.astype(F32)).astype(BF16)
    wr = jnp.concatenate([wr_hi, wr_lo], axis=1).T
    br = b_router.reshape(n_experts, 1)

    cb, ucx, qa, ka, vt, sgc, sga = _inproj(
        x2d, g_mix.reshape(1, d_model), wa, wqk, wvt, wg, wf, bf_pad,
        batch=batch, seq=seq, heads=heads, tm=tiles["tm_proj"], tk=tiles["t_attn"])
    o = _attention(qa, ka, vt, batch=batch, seq=seq, heads=heads, tq=tiles["t_attn"])
    groups = tiles["moe_groups"]
    ng = n // groups
    x1, h2p, topi, gates, rank, cnt = _mix(
        x2d, cb, ucx, o, sgc, sga, conv_w, w_conv_o.astype(BF16), w_attn_o.astype(BF16), w_out.astype(BF16),
        g_ffn.reshape(1, d_model), wr, br, seq=seq, n_experts=n_experts, tm=tiles["tm_mix"],
        tile0=0, nt=n // tiles["tm_mix"], tiles_per_group=ng // tiles["tm_mix"])

    tme = tiles["tm_expert"]
    n_blocks = (ng * TOP_K) // tme + n_experts
    p_rows = n_blocks * tme
    nchunks = ng // (SC_WORKERS * SC_CHUNK)
    jj = jnp.arange(tme, dtype=I32)[None, :]
    spare = p_rows + jnp.arange(n_experts, dtype=I32)[:, None] * tme + jj
    expert_ids = jnp.arange(n_experts, dtype=I32)[None, None, :]
    ys_groups, dest_groups = [], []
    for g in range(groups):
        counts = cnt[(g + 1) * (ng // tiles["tm_mix"]) - 1, :, 0].astype(I32)
        padded = ((counts + tme - 1) // tme) * tme
        pend = jnp.cumsum(padded)
        pstart = pend - padded
        n_used = (pend[-1] // tme).astype(I32).reshape(1)
        blk_start = jnp.minimum(jnp.arange(n_blocks, dtype=I32), n_used[0] - 1) * tme
        block_e = jnp.minimum(jnp.sum((pend[None, :] <= blk_start[:, None]).astype(I32), axis=1), n_experts - 1)
        tok = slice(g * ng, (g + 1) * ng)
        onehot = topi[:TOP_K, tok, None] == expert_ids
        dest = jnp.sum(jnp.where(onehot, pstart[None, None, :], 0), axis=2) + rank[:TOP_K, tok]
        dest4 = dest.reshape(TOP_K, SC_WORKERS, nchunks, SC_CHUNK).transpose(1, 2, 0, 3)
        pad_idx = jnp.where(jj < (padded - counts)[:, None], (pstart + counts)[:, None] + jj, spare)
        pad_idx = pad_idx.reshape(SC_WORKERS, (n_experts * tme) // (SC_WORKERS * SC_CHUNK), SC_CHUNK)
        xs = _sc_scatter_rows(h2p, dest4, pad_idx, p_rows + n_experts * tme, g * ng)
        ys_groups.append(_experts(xs, block_e, n_used, w_gate_up, b_gate_up, w_down, b_down,
                                  tm=tme, n_blocks=n_blocks))
        dest_groups.append(dest4)

    out = x1
    for g in range(groups):
        yk = _sc_gather_rows(ys_groups[g], dest_groups[g], ng)
        out = _final(yk, gates, out, g_final.reshape(1, d_model), tm=tiles["tm_final"],
                     tile0=g * (ng // tiles["tm_final"]))
    return out.reshape(batch, seq, d_model)


def kernel(x, g_mix, w_in, conv_w, b_f, w_conv_o, w_attn_o, w_out, g_ffn, w_router, b_router, w_gate_up,
           b_gate_up, w_down, b_down, g_final):
    return _forward(x, g_mix, w_in, conv_w, b_f, w_conv_o, w_attn_o, w_out, g_ffn, w_router, b_router,
                    w_gate_up, b_gate_up, w_down, b_down, g_final, _tiles(x.shape[1]))
```

```python
import functools

import jax
import jax.numpy as jnp
from jax import lax
from jax.experimental import pallas as pl
from jax.experimental.pallas import tpu as pltpu
from jax.experimental.pallas import tpu_sc as plsc

TOP_K = 4
RMS_EPS = 1e-5
SWIGLU_ALPHA = 1.702
SWIGLU_LIMIT = 7.0
LOG2E = 1.4426950408889634

LANES = 128
BF16_SUBLANES = 16
V_ONES_ROWS = BF16_SUBLANES
VMEM_LIMIT_BYTES = 56 * 1024 * 1024

F32 = jnp.float32
BF16 = jnp.bfloat16
U32 = jnp.uint32
I32 = jnp.int32
HI_MASK = 0xFFFF0000


def _params(sem):
    return pltpu.CompilerParams(dimension_semantics=sem, vmem_limit_bytes=VMEM_LIMIT_BYTES)


def _as_words(x_bf16):
    return pltpu.bitcast(x_bf16, U32)


def _as_bf16(w_u32):
    return pltpu.bitcast(w_u32, BF16)


def _pack_bf16_pair(lo_f32, hi_f32):
    lo = lax.bitcast_convert_type(lo_f32, U32)
    hi = lax.bitcast_convert_type(hi_f32, U32)
    return (lo >> 16) | (hi & U32(HI_MASK))


def _unpack_bf16_pair(w):
    lo = lax.bitcast_convert_type(w << 16, F32)
    hi = lax.bitcast_convert_type(w & U32(HI_MASK), F32)
    return lo, hi


def _inproj_kernel(x_ref, g_ref, wa_ref, wqk_ref, wvt_ref, wg_ref, wf_ref, bf_ref,
                   cb_ref, ucx_ref, qa_ref, ka_ref, vt_ref, sgc_ref, sga_ref,
                   carry_ref, *, tiles_per_batch, conv_dim, attn_dim, d_model, heads, tk):
    tm = x_ref.shape[0]
    t = lax.rem(pl.program_id(0), tiles_per_batch)
    x = x_ref[...]
    ms = jnp.mean(x * x, axis=-1, keepdims=True)
    h = ((x * lax.rsqrt(ms + RMS_EPS)) * g_ref[...]).astype(BF16)

    def mm(w_ref, c0, n):
        return jnp.dot(h, w_ref[:, c0:c0 + n], preferred_element_type=F32)

    z = mm(wf_ref, 0, LANES) + bf_ref[...]
    qf = mm(wqk_ref, 0, attn_dim)
    kf = mm(wqk_ref, attn_dim, attn_dim)
    cb = mm(wa_ref, 0, conv_dim)
    ucx = mm(wa_ref, conv_dim, conv_dim) * mm(wa_ref, 2 * conv_dim, conv_dim)

    lane = lax.broadcasted_iota(I32, z.shape, 1)
    logf = jnp.minimum(z, 0.0) - jnp.log1p(jnp.exp(-jnp.abs(z)))
    logf = jnp.where(lane < heads, logf, 0.0)
    p1 = logf.astype(BF16).astype(F32)
    r1 = logf - p1
    p2 = r1.astype(BF16).astype(F32)
    p3 = (r1 - p2).astype(BF16).astype(F32)
    packed = (p1 + pltpu.roll(p2, heads, 1) + pltpu.roll(p3, 2 * heads, 1)).astype(BF16)
    row = lax.broadcasted_iota(I32, (tm, tm), 0)
    col = lax.broadcasted_iota(I32, (tm, tm), 1)
    tri = jnp.where(col <= row, 1.0, 0.0).astype(BF16)
    r = jnp.dot(tri, packed, preferred_element_type=F32)
    local = r + pltpu.roll(r, LANES - heads, 1) + pltpu.roll(r, LANES - 2 * heads, 1)

    @pl.when(t == 0)
    def _():
        carry_ref[...] = jnp.zeros_like(carry_ref)

    c = jnp.where(lane < heads, local + carry_ref[0:1, :], 0.0)
    carry_ref[0:1, :] = c[tm - 1:tm, :]

    c2 = c * LOG2E
    c_hi = c2.astype(BF16).astype(F32)
    c_r = c2 - c_hi
    c_mid = c_r.astype(BF16).astype(F32)
    c_lo = (c_r - c_mid).astype(BF16).astype(F32)
    head_dim = attn_dim // heads
    ext = LANES - head_dim
    lane_e = lax.broadcasted_iota(I32, (tm, ext), 1)
    ones3 = jnp.where(lane_e < 3, 1.0, 0.0)
    for hh in range(heads):
        kext = jnp.where(lane_e == 0, -c_hi[:, hh:hh + 1],
                         jnp.where(lane_e == 1, -c_mid[:, hh:hh + 1],
                                   jnp.where(lane_e == 2, -c_lo[:, hh:hh + 1], 0.0)))
        sl = slice(hh * head_dim, (hh + 1) * head_dim)
        qa_ref[:, hh * LANES:(hh + 1) * LANES] = _as_words(jnp.concatenate([qf[:, sl], ones3], axis=1).astype(BF16))
        ka_ref[:, hh * LANES:(hh + 1) * LANES] = _as_words(jnp.concatenate([kf[:, sl], kext], axis=1).astype(BF16))

    cb_ref[...] = _as_words(cb.astype(BF16))
    ucx_ref[...] = _as_words(ucx.astype(BF16))
    half = d_model // 2
    for c in range(2):
        sgc_ref[:, c * half:(c + 1) * half] = _as_words(jax.nn.sigmoid(mm(wg_ref, c * half, half)).astype(BF16))
        sga_ref[:, c * half:(c + 1) * half] = _as_words(
            jax.nn.sigmoid(mm(wg_ref, d_model + c * half, half)).astype(BF16))

    vt = lax.dot_general(wvt_ref[...], h, (((1,), (1,)), ((), ())), preferred_element_type=F32).astype(BF16)
    ones_rows = jnp.ones((V_ONES_ROWS, tm), BF16)
    vt = jnp.concatenate([piece for hh in range(heads)
                          for piece in (vt[hh * head_dim:(hh + 1) * head_dim], ones_rows)], axis=0)
    for u in range(tm // tk):
        vt_ref[u] = _as_words(vt[:, u * tk:(u + 1) * tk])


def _inproj(x2d, g_mix, wa, wqk, wvt, wg, wf, bf_pad, *, batch, seq, heads, tm, tk):
    n, d_model = x2d.shape
    conv_dim = wa.shape[1] // 3
    attn_dim = wvt.shape[0]
    tiles_per_batch = seq // tm
    grid = (n // tm,)
    full = lambda shape: pl.BlockSpec(shape, lambda i: (0,) * len(shape))
    rows = lambda w: pl.BlockSpec((tm, w), lambda i: (i, 0))
    kern = functools.partial(_inproj_kernel, tiles_per_batch=tiles_per_batch, conv_dim=conv_dim,
                             attn_dim=attn_dim, d_model=d_model, heads=heads, tk=tk)
    vt_rows = attn_dim + heads * V_ONES_ROWS
    words = lambda w: jax.ShapeDtypeStruct((n // 2, w), U32)
    wrows = lambda w: pl.BlockSpec((tm // 2, w), lambda i: (i, 0))
    out_shape = (
        words(conv_dim),
        words(conv_dim),
        words(heads * LANES),
        words(heads * LANES),
        jax.ShapeDtypeStruct((n // tk, vt_rows // 2, tk), U32),
        words(d_model),
        words(d_model),
    )
    out_specs = (
        wrows(conv_dim), wrows(conv_dim), wrows(heads * LANES), wrows(heads * LANES),
        pl.BlockSpec((tm // tk, vt_rows // 2, tk), lambda i: (i, 0, 0)),
        wrows(d_model), wrows(d_model),
    )
    return pl.pallas_call(
        kern,
        grid=grid,
        in_specs=[rows(d_model), full((1, d_model)), full(wa.shape), full(wqk.shape), full(wvt.shape),
                  full(wg.shape), full(wf.shape), full((1, LANES))],
        out_specs=out_specs,
        out_shape=out_shape,
        scratch_shapes=[pltpu.VMEM((8, LANES), F32)],
        compiler_params=_params(("arbitrary",)),
        name="inproj",
    )(x2d, g_mix, wa, wqk, wvt, wg, wf, bf_pad)


def _attn_kernel(q_ref, k_ref, v_ref, o_ref, s_ref, *, tq, tk, head_dim, heads):
    qi = pl.program_id(1)
    row = lax.broadcasted_iota(I32, (tk, tq), 0)
    col = lax.broadcasted_iota(I32, (tk, tq), 1)

    def score_tile(j, slot):
        off = pl.multiple_of(j * (tk // 2), tk // 2)
        for hh in range(heads):
            qa = _as_bf16(q_ref[:, hh * LANES:(hh + 1) * LANES])
            ka = _as_bf16(k_ref[pl.ds(off, tk // 2), hh * LANES:(hh + 1) * LANES])
            s_ref[slot, hh] = lax.dot_general(ka, qa, (((1,), (1,)), ((), ())), preferred_element_type=F32)

    def block(j, slot, carry, masked):
        if not masked:
            score_tile(j + 1, 1 - slot)
        stats = []
        for hh in range(heads):
            m, _ = carry[hh]
            s = s_ref[slot, hh]
            if masked:
                s = jnp.where(row <= col, s, -jnp.inf)
            m_new = jnp.maximum(m, jnp.max(s, axis=0, keepdims=True))
            stats.append((m_new, jnp.exp2(m - m_new), jnp.exp2(s - m_new).astype(BF16)))
        out = []
        for hh in range(heads):
            m_new, alpha, p = stats[hh]
            vth = _as_bf16(v_ref[j, hh * (vrows // 2):(hh + 1) * (vrows // 2), :])
            out.append((m_new, alpha * carry[hh][1] + jnp.dot(vth, p, preferred_element_type=F32)))
        return tuple(out)

    vrows = head_dim + V_ONES_ROWS
    init = tuple((jnp.full((1, tq), -jnp.inf, F32), jnp.zeros((vrows, tq), F32)) for _ in range(heads))
    score_tile(0, 0)

    def pair(p, c):
        return block(2 * p + 1, 1, block(2 * p, 0, c, False), False)

    carry = lax.fori_loop(0, qi // 2, pair, init)
    final = lax.cond(
        lax.rem(qi, 2) == 0,
        lambda c: block(qi, 0, c, True),
        lambda c: block(qi, 1, block(qi - 1, 0, c, False), True),
        carry)
    per_group = LANES // head_dim
    for g in range(heads // per_group):
        accs = [final[g * per_group + u][1] for u in range(per_group)]
        ot = jnp.concatenate([a[:head_dim] / a[head_dim:head_dim + 1] for a in accs],
                             axis=0)
        o_ref[:, g * LANES:(g + 1) * LANES] = _as_words(ot.T.astype(BF16))


def _attention(qa, ka, vt, *, batch, seq, heads, tq):
    n = 2 * qa.shape[0]
    vt_rows, tk = 2 * vt.shape[1], vt.shape[2]
    head_dim = vt_rows // heads - V_ONES_ROWS
    attn_dim = heads * head_dim
    assert LANES % head_dim == 0 and tq == tk
    nq = seq // tq
    nk = seq // tk
    kern = functools.partial(_attn_kernel, tq=tq, tk=tk, head_dim=head_dim, heads=heads)
    return pl.pallas_call(
        kern,
        grid=(batch, nq),
        in_specs=[
            pl.BlockSpec((tq // 2, heads * LANES), lambda b, qi: (b * nq + qi, 0)),
            pl.BlockSpec((seq // 2, heads * LANES), lambda b, qi: (b, 0)),
            pl.BlockSpec((nk, vt_rows // 2, tk), lambda b, qi: (b, 0, 0)),
        ],
        out_specs=pl.BlockSpec((tq // 2, attn_dim), lambda b, qi: (b * nq + qi, 0)),
        out_shape=jax.ShapeDtypeStruct((n // 2, attn_dim), U32),
        scratch_shapes=[pltpu.VMEM((2, heads, tk, tq), F32)],
        compiler_params=_params(("arbitrary", "arbitrary")),
        name="attn",
    )(qa, ka, vt)


def _mix_kernel(x_ref, cb_ref, u_ref, uh_ref, o_ref, sgc_ref, sga_ref, cw_ref, wco_ref, wao_ref, wout_ref,
                gffn_ref, wr_ref, br_ref,
                x1_ref, h2p_ref, topi_ref, gate_ref, rank_ref, cnt_ref, carry_ref, lg_scr,
                *, tiles_per_batch, tiles_per_group, n_experts, nt):
    tm, d_model = x_ref.shape
    i = pl.program_id(0)

    @pl.when(i == 0)
    def _():
        lg_scr[...] = jnp.zeros_like(lg_scr)

    @pl.when((i == 0) | (lax.rem(i - 1, tiles_per_group) == 0))
    def _():
        carry_ref[...] = jnp.zeros_like(carry_ref)

    lg = lg_scr[...]
    sub = lax.broadcasted_iota(I32, (n_experts, tm), 0)

    vals, idxs = [], []
    for _ in range(TOP_K):
        m = jnp.max(lg, axis=0, keepdims=True)
        idx = jnp.min(jnp.where(lg == m, sub, n_experts), axis=0, keepdims=True)
        vals.append(m)
        idxs.append(idx)
        lg = jnp.where(sub == idx, -jnp.inf, lg)
    es = [jnp.exp(vk - vals[0]) for vk in vals]
    denom = es[0] + es[1] + es[2] + es[3]

    sub8 = lax.broadcasted_iota(I32, (8, tm), 0)
    subl = lax.broadcasted_iota(I32, (LANES, tm), 0)
    chosen = jnp.zeros((n_experts, tm), F32)
    topi = jnp.zeros((8, tm), I32)
    gates_t = jnp.zeros((LANES, tm), F32)
    for kk in range(TOP_K):
        chosen = jnp.where(sub == idxs[kk], 1.0, chosen)
        topi = jnp.where(sub8 == kk, idxs[kk], topi)
        gates_t = jnp.where(subl == kk, es[kk] / denom, gates_t)
    topi_ref[...] = topi
    gate_ref[...] = gates_t.T

    first = lax.rem(jnp.minimum(i, nt - 1), tiles_per_batch) == 0

    u = _as_bf16(u_ref[...]).astype(F32)
    halo = jnp.where(first, 0.0, _as_bf16(uh_ref[...]).astype(F32))
    ext = jnp.concatenate([halo, u], axis=0)
    hs = halo.shape[0]
    u1 = pltpu.roll(ext, 1, 0)[hs:]
    u2 = pltpu.roll(ext, 2, 0)[hs:]
    cw = cw_ref[...]
    conv = cw[0:1, :] * u2 + cw[1:2, :] * u1 + cw[2:3, :] * u
    yc = (_as_bf16(cb_ref[...]).astype(F32) * conv).astype(BF16)
    y_conv = jnp.dot(yc, wco_ref[...], preferred_element_type=F32)
    y_attn = jnp.dot(_as_bf16(o_ref[...]), wao_ref[...], preferred_element_type=F32)
    mixed = (_as_bf16(sgc_ref[...]).astype(F32) * y_conv
             + _as_bf16(sga_ref[...]).astype(F32) * y_attn).astype(BF16)
    x1 = x_ref[...] + jnp.dot(mixed, wout_ref[...], preferred_element_type=F32)
    x1_ref[...] = x1

    ms = jnp.mean(x1 * x1, axis=-1, keepdims=True)
    h2 = (x1 * lax.rsqrt(ms + RMS_EPS)) * gffn_ref[...]
    hb = h2.astype(BF16)
    hbf = hb.astype(F32)
    half = d_model // 2
    h2p_ref[...] = _pack_bf16_pair(hbf[:, :half], hbf[:, half:])

    hlo = (h2 - hbf).astype(BF16)
    nt_dims = (((1,), (1,)), ((), ()))
    rt = (lax.dot_general(wr_ref[...], hb, nt_dims, preferred_element_type=F32)
          + lax.dot_general(wr_ref[...], hlo, nt_dims, preferred_element_type=F32))
    lg_scr[...] = rt[:n_experts] + rt[n_experts:] + br_ref[...]

    row = lax.broadcasted_iota(I32, (tm, tm), 0)
    col = lax.broadcasted_iota(I32, (tm, tm), 1)
    tri = jnp.where(row < col, 1.0, 0.0).astype(BF16)
    before = jnp.dot(chosen.astype(BF16), tri, preferred_element_type=F32) + carry_ref[:, 0:1]
    rank = jnp.zeros((8, tm), F32)
    for kk in range(TOP_K):
        rk = jnp.sum(jnp.where(sub == idxs[kk], before, 0.0), axis=0, keepdims=True)
        rank = jnp.where(sub8 == kk, rk, rank)
    rank_ref[...] = rank.astype(I32)
    total = carry_ref[...] + jnp.sum(chosen, axis=1, keepdims=True)
    carry_ref[...] = total
    cnt_ref[0] = total


def _mix(x2d, cb, ucx, o, sgc, sga, conv_w, wco, wao, wout, g_ffn, wr, br, *, seq, n_experts, tm, tile0, nt,
         tiles_per_group):
    d_model = x2d.shape[1]
    n = nt * tm
    conv_dim = cb.shape[1]
    attn_dim = o.shape[1]
    tiles_per_batch = seq // tm
    assert tile0 % tiles_per_batch == 0
    hb = tm // BF16_SUBLANES
    full = lambda shape: pl.BlockSpec(shape, lambda i: (0,) * len(shape))
    cur = lambda i: jnp.minimum(i, nt - 1)
    prev = lambda i: jnp.maximum(i - 1, 0)
    rows_in = lambda w: pl.BlockSpec((tm, w), lambda i: (cur(i) + tile0, 0))
    words_in = lambda w: pl.BlockSpec((tm // 2, w), lambda i: (cur(i) + tile0, 0))
    rows = lambda w: pl.BlockSpec((tm, w), lambda i: (cur(i), 0))
    kern = functools.partial(_mix_kernel, tiles_per_batch=tiles_per_batch, tiles_per_group=tiles_per_group,
                             n_experts=n_experts, nt=nt)
    out_shape = (
        jax.ShapeDtypeStruct((n, d_model), F32),
        jax.ShapeDtypeStruct((n, d_model // 2), U32),
        jax.ShapeDtypeStruct((8, n), I32),
        jax.ShapeDtypeStruct((n, LANES), F32),
        jax.ShapeDtypeStruct((8, n), I32),
        jax.ShapeDtypeStruct((nt, n_experts, LANES), F32),
    )
    cols = pl.BlockSpec((8, tm), lambda i: (0, prev(i)))
    out_specs = (rows(d_model), rows(d_model // 2), cols, pl.BlockSpec((tm, LANES), lambda i: (prev(i), 0)), cols,
                 pl.BlockSpec((1, n_experts, LANES), lambda i: (prev(i), 0, 0)))
    return pl.pallas_call(
        kern,
        grid=(nt + 1,),
        in_specs=[rows_in(d_model), words_in(conv_dim), words_in(conv_dim),
                  pl.BlockSpec((BF16_SUBLANES // 2, conv_dim),
                               lambda i: (jnp.maximum((cur(i) + tile0) * hb - 1, 0), 0)),
                  words_in(attn_dim), words_in(d_model), words_in(d_model),
                  full(conv_w.shape), full(wco.shape), full(wao.shape), full(wout.shape),
                  full((1, d_model)), full(wr.shape), full((n_experts, 1))],
        out_specs=out_specs,
        out_shape=out_shape,
        scratch_shapes=[pltpu.VMEM((n_experts, LANES), F32), pltpu.VMEM((n_experts, tm), F32)],
        compiler_params=_params(("arbitrary",)),
        name="mix",
    )(x2d, cb, ucx, ucx, o, sgc, sga, conv_w, wco, wao, wout, g_ffn, wr, br)


SC_CORES = 2
SC_SUBCORES = 16
SC_WORKERS = SC_CORES * SC_SUBCORES
SC_CHUNK = 64


def _sc_mesh():
    return plsc.VectorSubcoreMesh(core_axis_name="c", subcore_axis_name="s",
                                  num_cores=SC_CORES, num_subcores=SC_SUBCORES)


def _sc_worker():
    return lax.axis_index("s") * SC_CORES + lax.axis_index("c")


def _sc_scatter_rows(src, dest4, pad_idx, n_out, row0):
    width = src.shape[1]
    nchunks = dest4.shape[1]
    npad = pad_idx.shape[1]

    @functools.partial(
        pl.kernel, mesh=_sc_mesh(),
        out_type=jax.ShapeDtypeStruct((n_out, width), src.dtype),
        scratch_types=[pltpu.VMEM((TOP_K, SC_CHUNK), I32), pltpu.VMEM((SC_CHUNK, width), src.dtype),
                       pltpu.VMEM((npad, SC_CHUNK), I32), pltpu.SemaphoreType.DMA],
        name="sc_scatter_rows",
    )
    def k(src_hbm, dest_hbm, pad_hbm, out_hbm, idx_v, rows_v, pad_v, sem):
        wid = _sc_worker()
        base = row0 + wid * (nchunks * SC_CHUNK)

        @pl.loop(0, nchunks)
        def _(j):
            pltpu.sync_copy(src_hbm.at[pl.ds(base + j * SC_CHUNK, SC_CHUNK)], rows_v)
            pltpu.sync_copy(dest_hbm.at[wid, j], idx_v)
            copies = [pltpu.async_copy(rows_v, out_hbm.at[idx_v.at[kk]], sem) for kk in range(TOP_K)]
            for cp in copies:
                cp.wait()

        pltpu.sync_copy(pad_hbm.at[wid], pad_v)
        fills = [pltpu.async_copy(rows_v, out_hbm.at[pad_v.at[p]], sem) for p in range(npad)]
        for cp in fills:
            cp.wait()

    return k(src, dest4, pad_idx)


def _sc_gather_rows(src, dest4, n):
    width = src.shape[1]
    nchunks = dest4.shape[1]

    @functools.partial(
        pl.kernel, mesh=_sc_mesh(),
        out_type=jax.ShapeDtypeStruct((TOP_K, n, width), src.dtype),
        scratch_types=[pltpu.VMEM((TOP_K, SC_CHUNK), I32), pltpu.VMEM((SC_CHUNK, width), src.dtype),
                       pltpu.VMEM((SC_CHUNK, width), src.dtype), pltpu.SemaphoreType.DMA, pltpu.SemaphoreType.DMA],
        name="sc_gather_rows",
    )
    def k(src_hbm, dest_hbm, out_hbm, idx_v, rows_a, rows_b, sem_a, sem_b):
        wid = _sc_worker()
        base = wid * (nchunks * SC_CHUNK)
        bufs = ((rows_a, sem_a), (rows_b, sem_b))

        @pl.loop(0, nchunks)
        def _(j):
            pltpu.sync_copy(dest_hbm.at[wid, j], idx_v)
            pending = pltpu.async_copy(src_hbm.at[idx_v.at[0]], rows_a, sem_a)
            for kk in range(TOP_K):
                buf, _ = bufs[kk % 2]
                pending.wait()
                if kk + 1 < TOP_K:
                    nbuf, nsem = bufs[(kk + 1) % 2]
                    pending = pltpu.async_copy(src_hbm.at[idx_v.at[kk + 1]], nbuf, nsem)
                pltpu.sync_copy(buf, out_hbm.at[kk, pl.ds(base + j * SC_CHUNK, SC_CHUNK)])

    return k(src, dest4)


def _expert_kernel(be_ref, nu_ref, xs_ref, wgu_ref, bgu_ref, wd_ref, bd_ref, ys_ref, wgu_bf, wd_bf, *, d_ff, ff_chunk):
    i = pl.program_id(0)

    @pl.when((i == 0) | (be_ref[i] != be_ref[jnp.maximum(i - 1, 0)]))
    def _():
        wgu_bf[...] = wgu_ref[0].astype(BF16)
        wd_bf[...] = wd_ref[0].astype(BF16)

    @pl.when(i < nu_ref[0])
    def _():
        lo, hi = _unpack_bf16_pair(xs_ref[...])
        xb = jnp.concatenate([lo.astype(BF16), hi.astype(BF16)], axis=1)
        chunks = range(0, d_ff, ff_chunk)
        gus = []
        for c0 in chunks:
            glu = jnp.dot(xb, wgu_bf[:, c0:c0 + ff_chunk], preferred_element_type=F32) + bgu_ref[0, :, c0:c0 + ff_chunk]
            lin = (jnp.dot(xb, wgu_bf[:, d_ff + c0:d_ff + c0 + ff_chunk], preferred_element_type=F32)
                   + bgu_ref[0, :, d_ff + c0:d_ff + c0 + ff_chunk])
            gus.append((glu, lin))
        y = bd_ref[0]
        for c0, (glu, lin) in zip(chunks, gus):
            glu = jnp.minimum(glu, SWIGLU_LIMIT)
            lin = jnp.clip(lin, -SWIGLU_LIMIT, SWIGLU_LIMIT)
            act = (glu * jax.nn.sigmoid(SWIGLU_ALPHA * glu)) * (lin + 1.0)
            y = y + jnp.dot(act.astype(BF16), wd_bf[c0:c0 + ff_chunk, :], preferred_element_type=F32)
        yb = y.astype(BF16).astype(F32)
        half = y.shape[1] // 2
        ys_ref[...] = _pack_bf16_pair(yb[:, :half], yb[:, half:])


def _experts(xs, block_e, n_used, wgu, bgu, wd, bd, *, tm, n_blocks):
    half = xs.shape[1]
    p = n_blocks * tm
    e, d_model, two_ff = wgu.shape
    d_ff = two_ff // 2
    kern = functools.partial(_expert_kernel, d_ff=d_ff, ff_chunk=min(256, d_ff))
    blk = lambda i, be, nu: (jnp.minimum(i, nu[0] - 1), 0)
    grid_spec = pltpu.PrefetchScalarGridSpec(
        num_scalar_prefetch=2,
        grid=(n_blocks,),
        in_specs=[
            pl.BlockSpec((tm, half), blk),
            pl.BlockSpec((1, d_model, two_ff), lambda i, be, nu: (be[i], 0, 0)),
            pl.BlockSpec((1, 1, two_ff), lambda i, be, nu: (be[i], 0, 0)),
            pl.BlockSpec((1, d_ff, d_model), lambda i, be, nu: (be[i], 0, 0)),
            pl.BlockSpec((1, 1, d_model), lambda i, be, nu: (be[i], 0, 0)),
        ],
        out_specs=pl.BlockSpec((tm, half), blk),
        scratch_shapes=[pltpu.VMEM((d_model, two_ff), BF16), pltpu.VMEM((d_ff, d_model), BF16)],
    )
    return pl.pallas_call(
        kern,
        grid_spec=grid_spec,
        out_shape=jax.ShapeDtypeStruct((p, half), U32),
        compiler_params=_params(("arbitrary",)),
        name="experts",
    )(block_e, n_used, xs, wgu, bgu.reshape(e, 1, two_ff), wd, bd.reshape(e, 1, d_model))


def _final_kernel(yk_ref, gate_ref, x1_ref, g_ref, out_ref):
    tm, d_model = x1_ref.shape
    half = d_model // 2
    gates = gate_ref[...]
    acc_lo = jnp.zeros((tm, half), F32)
    acc_hi = jnp.zeros((tm, half), F32)
    for kk in range(TOP_K):
        lo, hi = _unpack_bf16_pair(yk_ref[kk])
        gk = gates[:, kk:kk + 1]
        acc_lo = acc_lo + gk * lo
        acc_hi = acc_hi + gk * hi
    x2 = x1_ref[...] + jnp.concatenate([acc_lo, acc_hi], axis=1)
    ms = jnp.mean(x2 * x2, axis=-1, keepdims=True)
    out_ref[...] = (x2 * lax.rsqrt(ms + RMS_EPS)) * g_ref[...]


def _final(yk, gates, x1, g_final, *, tm, tile0):
    n, d_model = x1.shape
    rows = lambda w: pl.BlockSpec((tm, w), lambda i: (i + tile0, 0))
    return pl.pallas_call(
        _final_kernel,
        grid=(yk.shape[1] // tm,),
        in_specs=[pl.BlockSpec((TOP_K, tm, d_model // 2), lambda i: (0, i, 0)), rows(LANES), rows(d_model),
                  pl.BlockSpec((1, d_model), lambda i: (0, 0))],
        out_specs=rows(d_model),
        out_shape=jax.ShapeDtypeStruct((n, d_model), F32),
        input_output_aliases={2: 0},
        compiler_params=_params(("arbitrary",)),
        name="final",
    )(yk, gates, x1, g_final)


def _tiles(seq):
    tm = min(512, seq)
    return dict(tm_proj=tm, t_attn=min(256, seq), tm_mix=tm, tm_expert=512, tm_final=tm, moe_groups=2)


def _forward(x, g_mix, w_in, conv_w, b_f, w_conv_o, w_attn_o, w_out, g_ffn, w_router, b_router,
             w_gate_up, b_gate_up, w_down, b_down, g_final, tiles):
    batch, seq, d_model = x.shape
    n = batch * seq
    conv_dim = conv_w.shape[1]
    attn_dim = w_attn_o.shape[0]
    heads = b_f.shape[0]
    head_dim = attn_dim // heads
    n_experts = w_router.shape[1]
    x2d = x.reshape(n, d_model)

    c0 = 3 * conv_dim
    a0 = c0 + 3 * attn_dim
    wa = w_in[:, :c0].astype(BF16)
    scale = LOG2E / (head_dim ** 0.5)
    wqk = jnp.concatenate([w_in[:, c0:c0 + attn_dim] * scale, w_in[:, c0 + attn_dim:c0 + 2 * attn_dim]],
                          axis=1).astype(BF16)
    wvt = w_in[:, c0 + 2 * attn_dim:a0].T.astype(BF16)
    wf = jnp.pad(w_in[:, a0:a0 + heads], ((0, 0), (0, LANES - heads))).astype(BF16)
    wg = w_in[:, a0 + heads:].astype(BF16)
    bf_pad = jnp.pad(b_f, (0, LANES - heads)).reshape(1, LANES)
    wr_hi = w_router.astype(BF16)
    wr_lo = (w_router - wr_hi.astype(F32)).astype(BF16)
    wr = jnp.concatenate([wr_hi, wr_lo], axis=1).T
    br = b_router.reshape(n_experts, 1)

    cb, ucx, qa, ka, vt, sgc, sga = _inproj(
        x2d, g_mix.reshape(1, d_model), wa, wqk, wvt, wg, wf, bf_pad,
        batch=batch, seq=seq, heads=heads, tm=tiles["tm_proj"], tk=tiles["t_attn"])
    o = _attention(qa, ka, vt, batch=batch, seq=seq, heads=heads, tq=tiles["t_attn"])
    groups = tiles["moe_groups"]
    ng = n // groups
    x1, h2p, topi, gates, rank, cnt = _mix(
        x2d, cb, ucx, o, sgc, sga, conv_w, w_conv_o.astype(BF16), w_attn_o.astype(BF16), w_out.astype(BF16),
        g_ffn.reshape(1, d_model), wr, br, seq=seq, n_experts=n_experts, tm=tiles["tm_mix"],
        tile0=0, nt=n // tiles["tm_mix"], tiles_per_group=ng // tiles["tm_mix"])

    tme = tiles["tm_expert"]
    n_blocks = (ng * TOP_K) // tme + n_experts
    p_rows = n_blocks * tme
    nchunks = ng // (SC_WORKERS * SC_CHUNK)
    jj = jnp.arange(tme, dtype=I32)[None, :]
    spare = p_rows + jnp.arange(n_experts, dtype=I32)[:, None] * tme + jj
    expert_ids = jnp.arange(n_experts, dtype=I32)[None, None, :]
    ys_groups, dest_groups = [], []
    for g in range(groups):
        counts = cnt[(g + 1) * (ng // tiles["tm_mix"]) - 1, :, 0].astype(I32)
        padded = ((counts + tme - 1) // tme) * tme
        pend = jnp.cumsum(padded)
        pstart = pend - padded
        n_used = (pend[-1] // tme).astype(I32).reshape(1)
        blk_start = jnp.minimum(jnp.arange(n_blocks, dtype=I32), n_used[0] - 1) * tme
        block_e = jnp.minimum(jnp.sum((pend[None, :] <= blk_start[:, None]).astype(I32), axis=1), n_experts - 1)
        tok = slice(g * ng, (g + 1) * ng)
        onehot = topi[:TOP_K, tok, None] == expert_ids
        dest = jnp.sum(jnp.where(onehot, pstart[None, None, :], 0), axis=2) + rank[:TOP_K, tok]
        dest4 = dest.reshape(TOP_K, SC_WORKERS, nchunks, SC_CHUNK).transpose(1, 2, 0, 3)
        pad_idx = jnp.where(jj < (padded - counts)[:, None], (pstart + counts)[:, None] + jj, spare)
        pad_idx = pad_idx.reshape(SC_WORKERS, (n_experts * tme) // (SC_WORKERS * SC_CHUNK), SC_CHUNK)
        xs = _sc_scatter_rows(h2p, dest4, pad_idx, p_rows + n_experts * tme, g * ng)
        ys_groups.append(_experts(xs, block_e, n_used, w_gate_up, b_gate_up, w_down, b_down,
                                  tm=tme, n_blocks=n_blocks))
        dest_groups.append(dest4)

    out = x1
    for g in range(groups):
        yk = _sc_gather_rows(ys_groups[g], dest_groups[g], ng)
        out = _final(yk, gates, out, g_final.reshape(1, d_model), tm=tiles["tm_final"],
                     tile0=g * (ng // tiles["tm_final"]))
    return out.reshape(batch, seq, d_model)


def kernel(x, g_mix, w_in, conv_w, b_f, w_conv_o, w_attn_o, w_out, g_ffn, w_router, b_router, w_gate_up,
           b_gate_up, w_down, b_down, g_final):
    return _forward(x, g_mix, w_in, conv_w, b_f, w_conv_o, w_attn_o, w_out, g_ffn, w_router, b_router,
                    w_gate_up, b_gate_up, w_down, b_down, g_final, _tiles(x.shape[1]))
```

```python
import functools

import jax
import jax.numpy as jnp
from jax import lax
from jax.experimental import pallas as pl
from jax.experimental.pallas import tpu as pltpu
from jax.experimental.pallas import tpu_sc as plsc

TOP_K = 4
RMS_EPS = 1e-5
SWIGLU_ALPHA = 1.702
SWIGLU_LIMIT = 7.0
LOG2E = 1.4426950408889634

LANES = 128
BF16_SUBLANES = 16
V_ONES_ROWS = BF16_SUBLANES
VMEM_LIMIT_BYTES = 56 * 1024 * 1024

F32 = jnp.float32
BF16 = jnp.bfloat16
U32 = jnp.uint32
I32 = jnp.int32
HI_MASK = 0xFFFF0000


def _params(sem):
    return pltpu.CompilerParams(dimension_semantics=sem, vmem_limit_bytes=VMEM_LIMIT_BYTES)


def _as_words(x_bf16):
    return pltpu.bitcast(x_bf16, U32)


def _as_bf16(w_u32):
    return pltpu.bitcast(w_u32, BF16)


def _pack_bf16_pair(lo_f32, hi_f32):
    lo = lax.bitcast_convert_type(lo_f32, U32)
    hi = lax.bitcast_convert_type(hi_f32, U32)
    return (lo >> 16) | (hi & U32(HI_MASK))


def _unpack_bf16_pair(w):
    lo = lax.bitcast_convert_type(w << 16, F32)
    hi = lax.bitcast_convert_type(w & U32(HI_MASK), F32)
    return lo, hi


def _inproj_kernel(x_ref, g_ref, wa_ref, wqk_ref, wvt_ref, wg_ref, wf_ref, bf_ref,
                   cb_ref, ucx_ref, qa_ref, ka_ref, vt_ref, sgc_ref, sga_ref,
                   carry_ref, *, tiles_per_batch, conv_dim, attn_dim, d_model, heads, tk):
    tm = x_ref.shape[0]
    t = lax.rem(pl.program_id(0), tiles_per_batch)
    x = x_ref[...]
    ms = jnp.mean(x * x, axis=-1, keepdims=True)
    h = ((x * lax.rsqrt(ms + RMS_EPS)) * g_ref[...]).astype(BF16)

    def mm(w_ref, c0, n):
        return jnp.dot(h, w_ref[:, c0:c0 + n], preferred_element_type=F32)

    z = mm(wf_ref, 0, LANES) + bf_ref[...]
    qf = mm(wqk_ref, 0, attn_dim)
    kf = mm(wqk_ref, attn_dim, attn_dim)
    cb = mm(wa_ref, 0, conv_dim)
    ucx = mm(wa_ref, conv_dim, conv_dim) * mm(wa_ref, 2 * conv_dim, conv_dim)

    lane = lax.broadcasted_iota(I32, z.shape, 1)
    logf = jnp.minimum(z, 0.0) - jnp.log1p(jnp.exp(-jnp.abs(z)))
    logf = jnp.where(lane < heads, logf, 0.0)
    p1 = logf.astype(BF16).astype(F32)
    r1 = logf - p1
    p2 = r1.astype(BF16).astype(F32)
    p3 = (r1 - p2).astype(BF16).astype(F32)
    packed = (p1 + pltpu.roll(p2, heads, 1) + pltpu.roll(p3, 2 * heads, 1)).astype(BF16)
    row = lax.broadcasted_iota(I32, (tm, tm), 0)
    col = lax.broadcasted_iota(I32, (tm, tm), 1)
    tri = jnp.where(col <= row, 1.0, 0.0).astype(BF16)
    r = jnp.dot(tri, packed, preferred_element_type=F32)
    local = r + pltpu.roll(r, LANES - heads, 1) + pltpu.roll(r, LANES - 2 * heads, 1)

    @pl.when(t == 0)
    def _():
        carry_ref[...] = jnp.zeros_like(carry_ref)

    c = jnp.where(lane < heads, local + carry_ref[0:1, :], 0.0)
    carry_ref[0:1, :] = c[tm - 1:tm, :]

    c2 = c * LOG2E
    c_hi = c2.astype(BF16).astype(F32)
    c_r = c2 - c_hi
    c_mid = c_r.astype(BF16).astype(F32)
    c_lo = (c_r - c_mid).astype(BF16).astype(F32)
    head_dim = attn_dim // heads
    ext = LANES - head_dim
    lane_e = lax.broadcasted_iota(I32, (tm, ext), 1)
    ones3 = jnp.where(lane_e < 3, 1.0, 0.0)
    for hh in range(heads):
        kext = jnp.where(lane_e == 0, -c_hi[:, hh:hh + 1],
                         jnp.where(lane_e == 1, -c_mid[:, hh:hh + 1],
                                   jnp.where(lane_e == 2, -c_lo[:, hh:hh + 1], 0.0)))
        sl = slice(hh * head_dim, (hh + 1) * head_dim)
        qa_ref[:, hh * LANES:(hh + 1) * LANES] = _as_words(jnp.concatenate([qf[:, sl], ones3], axis=1).astype(BF16))
        ka_ref[:, hh * LANES:(hh + 1) * LANES] = _as_words(jnp.concatenate([kf[:, sl], kext], axis=1).astype(BF16))

    cb_ref[...] = _as_words(cb.astype(BF16))
    ucx_ref[...] = _as_words(ucx.astype(BF16))
    half = d_model // 2
    for c in range(2):
        sgc_ref[:, c * half:(c + 1) * half] = _as_words(jax.nn.sigmoid(mm(wg_ref, c * half, half)).astype(BF16))
        sga_ref[:, c * half:(c + 1) * half] = _as_words(
            jax.nn.sigmoid(mm(wg_ref, d_model + c * half, half)).astype(BF16))

    vt = lax.dot_general(wvt_ref[...], h, (((1,), (1,)), ((), ())), preferred_element_type=F32).astype(BF16)
    ones_rows = jnp.ones((V_ONES_ROWS, tm), BF16)
    vt = jnp.concatenate([piece for hh in range(heads)
                          for piece in (vt[hh * head_dim:(hh + 1) * head_dim], ones_rows)], axis=0)
    for u in range(tm // tk):
        vt_ref[u] = _as_words(vt[:, u * tk:(u + 1) * tk])


def _inproj(x2d, g_mix, wa, wqk, wvt, wg, wf, bf_pad, *, batch, seq, heads, tm, tk):
    n, d_model = x2d.shape
    conv_dim = wa.shape[1] // 3
    attn_dim = wvt.shape[0]
    tiles_per_batch = seq // tm
    grid = (n // tm,)
    full = lambda shape: pl.BlockSpec(shape, lambda i: (0,) * len(shape))
    rows = lambda w: pl.BlockSpec((tm, w), lambda i: (i, 0))
    kern = functools.partial(_inproj_kernel, tiles_per_batch=tiles_per_batch, conv_dim=conv_dim,
                             attn_dim=attn_dim, d_model=d_model, heads=heads, tk=tk)
    vt_rows = attn_dim + heads * V_ONES_ROWS
    words = lambda w: jax.ShapeDtypeStruct((n // 2, w), U32)
    wrows = lambda w: pl.BlockSpec((tm // 2, w), lambda i: (i, 0))
    out_shape = (
        words(conv_dim),
        words(conv_dim),
        words(heads * LANES),
        words(heads * LANES),
        jax.ShapeDtypeStruct((n // tk, vt_rows // 2, tk), U32),
        words(d_model),
        words(d_model),
    )
    out_specs = (
        wrows(conv_dim), wrows(conv_dim), wrows(heads * LANES), wrows(heads * LANES),
        pl.BlockSpec((tm // tk, vt_rows // 2, tk), lambda i: (i, 0, 0)),
        wrows(d_model), wrows(d_model),
    )
    return pl.pallas_call(
        kern,
        grid=grid,
        in_specs=[rows(d_model), full((1, d_model)), full(wa.shape), full(wqk.shape), full(wvt.shape),
                  full(wg.shape), full(wf.shape), full((1, LANES))],
        out_specs=out_specs,
        out_shape=out_shape,
        scratch_shapes=[pltpu.VMEM((8, LANES), F32)],
        compiler_params=_params(("arbitrary",)),
        name="inproj",
    )(x2d, g_mix, wa, wqk, wvt, wg, wf, bf_pad)


def _attn_kernel(q_ref, k_ref, v_ref, o_ref, s_ref, *, tq, tk, head_dim, heads):
    qi = pl.program_id(1)
    row = lax.broadcasted_iota(I32, (tk, tq), 0)
    col = lax.broadcasted_iota(I32, (tk, tq), 1)

    def score_tile(j, slot):
        off = pl.multiple_of(j * (tk // 2), tk // 2)
        for hh in range(heads):
            qa = _as_bf16(q_ref[:, hh * LANES:(hh + 1) * LANES])
            ka = _as_bf16(k_ref[pl.ds(off, tk // 2), hh * LANES:(hh + 1) * LANES])
            s_ref[slot, hh] = lax.dot_general(ka, qa, (((1,), (1,)), ((), ())), preferred_element_type=F32)

    def block(j, slot, carry, masked):
        if not masked:
            score_tile(j + 1, 1 - slot)
        stats = []
        for hh in range(heads):
            m, _ = carry[hh]
            s = s_ref[slot, hh]
            if masked:
                s = jnp.where(row <= col, s, -jnp.inf)
            m_new = jnp.maximum(m, jnp.max(s, axis=0, keepdims=True))
            stats.append((m_new, jnp.exp2(m - m_new), jnp.exp2(s - m_new).astype(BF16)))
        out = []
        for hh in range(heads):
            m_new, alpha, p = stats[hh]
            vth = _as_bf16(v_ref[j, hh * (vrows // 2):(hh + 1) * (vrows // 2), :])
            out.append((m_new, alpha * carry[hh][1] + jnp.dot(vth, p, preferred_element_type=F32)))
        return tuple(out)

    vrows = head_dim + V_ONES_ROWS
    init = tuple((jnp.full((1, tq), -jnp.inf, F32), jnp.zeros((vrows, tq), F32)) for _ in range(heads))
    score_tile(0, 0)

    def pair(p, c):
        return block(2 * p + 1, 1, block(2 * p, 0, c, False), False)

    carry = lax.fori_loop(0, qi // 2, pair, init)
    final = lax.cond(
        lax.rem(qi, 2) == 0,
        lambda c: block(qi, 0, c, True),
        lambda c: block(qi, 1, block(qi - 1, 0, c, False), True),
        carry)
    per_group = LANES // head_dim
    for g in range(heads // per_group):
        accs = [final[g * per_group + u][1] for u in range(per_group)]
        ot = jnp.concatenate([a[:head_dim] / a[head_dim:head_dim + 1] for a in accs],
                             axis=0)
        o_ref[:, g * LANES:(g + 1) * LANES] = _as_words(ot.T.astype(BF16))


def _attention(qa, ka, vt, *, batch, seq, heads, tq):
    n = 2 * qa.shape[0]
    vt_rows, tk = 2 * vt.shape[1], vt.shape[2]
    head_dim = vt_rows // heads - V_ONES_ROWS
    attn_dim = heads * head_dim
    assert LANES % head_dim == 0 and tq == tk
    nq = seq // tq
    nk = seq // tk
    kern = functools.partial(_attn_kernel, tq=tq, tk=tk, head_dim=head_dim, heads=heads)
    return pl.pallas_call(
        kern,
        grid=(batch, nq),
        in_specs=[
            pl.BlockSpec((tq // 2, heads * LANES), lambda b, qi: (b * nq + qi, 0)),
            pl.BlockSpec((seq // 2, heads * LANES), lambda b, qi: (b, 0)),
            pl.BlockSpec((nk, vt_rows // 2, tk), lambda b, qi: (b, 0, 0)),
        ],
        out_specs=pl.BlockSpec((tq // 2, attn_dim), lambda b, qi: (b * nq + qi, 0)),
        out_shape=jax.ShapeDtypeStruct((n // 2, attn_dim), U32),
        scratch_shapes=[pltpu.VMEM((2, heads, tk, tq), F32)],
        compiler_params=_params(("arbitrary", "arbitrary")),
        name="attn",
    )(qa, ka, vt)


def _mix_kernel(x_ref, cb_ref, u_ref, uh_ref, o_ref, sgc_ref, sga_ref, cw_ref, wco_ref, wao_ref, wout_ref,
                gffn_ref, wr_ref, br_ref,
                x1_ref, h2p_ref, topi_ref, gate_ref, rank_ref, cnt_ref, carry_ref, lg_scr,
                *, tiles_per_batch, tiles_per_group, n_experts, nt):
    tm, d_model = x_ref.shape
    i = pl.program_id(0)

    @pl.when(i == 0)
    def _():
        lg_scr[...] = jnp.zeros_like(lg_scr)

    @pl.when((i == 0) | (lax.rem(i - 1, tiles_per_group) == 0))
    def _():
        carry_ref[...] = jnp.zeros_like(carry_ref)

    lg = lg_scr[...]
    sub = lax.broadcasted_iota(I32, (n_experts, tm), 0)

    vals, idxs = [], []
    for _ in range(TOP_K):
        m = jnp.max(lg, axis=0, keepdims=True)
        idx = jnp.min(jnp.where(lg == m, sub, n_experts), axis=0, keepdims=True)
        vals.append(m)
        idxs.append(idx)
        lg = jnp.where(sub == idx, -jnp.inf, lg)
    es = [jnp.exp(vk - vals[0]) for vk in vals]
    denom = es[0] + es[1] + es[2] + es[3]

    sub8 = lax.broadcasted_iota(I32, (8, tm), 0)
    subl = lax.broadcasted_iota(I32, (LANES, tm), 0)
    chosen = jnp.zeros((n_experts, tm), F32)
    topi = jnp.zeros((8, tm), I32)
    gates_t = jnp.zeros((LANES, tm), F32)
    for kk in range(TOP_K):
        chosen = jnp.where(sub == idxs[kk], 1.0, chosen)
        topi = jnp.where(sub8 == kk, idxs[kk], topi)
        gates_t = jnp.where(subl == kk, es[kk] / denom, gates_t)
    topi_ref[...] = topi
    gate_ref[...] = gates_t.T

    first = lax.rem(jnp.minimum(i, nt - 1), tiles_per_batch) == 0

    u = _as_bf16(u_ref[...]).astype(F32)
    halo = jnp.where(first, 0.0, _as_bf16(uh_ref[...]).astype(F32))
    ext = jnp.concatenate([halo, u], axis=0)
    hs = halo.shape[0]
    u1 = pltpu.roll(ext, 1, 0)[hs:]
    u2 = pltpu.roll(ext, 2, 0)[hs:]
    cw = cw_ref[...]
    conv = cw[0:1, :] * u2 + cw[1:2, :] * u1 + cw[2:3, :] * u
    yc = (_as_bf16(cb_ref[...]).astype(F32) * conv).astype(BF16)
    y_conv = jnp.dot(yc, wco_ref[...], preferred_element_type=F32)
    y_attn = jnp.dot(_as_bf16(o_ref[...]), wao_ref[...], preferred_element_type=F32)
    mixed = (_as_bf16(sgc_ref[...]).astype(F32) * y_conv
             + _as_bf16(sga_ref[...]).astype(F32) * y_attn).astype(BF16)
    x1 = x_ref[...] + jnp.dot(mixed, wout_ref[...], preferred_element_type=F32)
    x1_ref[...] = x1

    ms = jnp.mean(x1 * x1, axis=-1, keepdims=True)
    h2 = (x1 * lax.rsqrt(ms + RMS_EPS)) * gffn_ref[...]
    hb = h2.astype(BF16)
    hbf = hb.astype(F32)
    half = d_model // 2
    h2p_ref[...] = _pack_bf16_pair(hbf[:, :half], hbf[:, half:])

    hlo = (h2 - hbf).astype(BF16)
    nt_dims = (((1,), (1,)), ((), ()))
    rt = (lax.dot_general(wr_ref[...], hb, nt_dims, preferred_element_type=F32)
          + lax.dot_general(wr_ref[...], hlo, nt_dims, preferred_element_type=F32))
    lg_scr[...] = rt[:n_experts] + rt[n_experts:] + br_ref[...]

    row = lax.broadcasted_iota(I32, (tm, tm), 0)
    col = lax.broadcasted_iota(I32, (tm, tm), 1)
    tri = jnp.where(row < col, 1.0, 0.0).astype(BF16)
    before = jnp.dot(chosen.astype(BF16), tri, preferred_element_type=F32) + carry_ref[:, 0:1]
    rank = jnp.zeros((8, tm), F32)
    for kk in range(TOP_K):
        rk = jnp.sum(jnp.where(sub == idxs[kk], before, 0.0), axis=0, keepdims=True)
        rank = jnp.where(sub8 == kk, rk, rank)
    rank_ref[...] = rank.astype(I32)
    total = carry_ref[...] + jnp.sum(chosen, axis=1, keepdims=True)
    carry_ref[...] = total
    cnt_ref[0] = total


def _mix(x2d, cb, ucx, o, sgc, sga, conv_w, wco, wao, wout, g_ffn, wr, br, *, seq, n_experts, tm, tile0, nt,
         tiles_per_group):
    d_model = x2d.shape[1]
    n = nt * tm
    conv_dim = cb.shape[1]
    attn_dim = o.shape[1]
    tiles_per_batch = seq // tm
    assert tile0 % tiles_per_batch == 0
    hb = tm // BF16_SUBLANES
    full = lambda shape: pl.BlockSpec(shape, lambda i: (0,) * len(shape))
    cur = lambda i: jnp.minimum(i, nt - 1)
    prev = lambda i: jnp.maximum(i - 1, 0)
    rows_in = lambda w: pl.BlockSpec((tm, w), lambda i: (cur(i) + tile0, 0))
    words_in = lambda w: pl.BlockSpec((tm // 2, w), lambda i: (cur(i) + tile0, 0))
    rows = lambda w: pl.BlockSpec((tm, w), lambda i: (cur(i), 0))
    kern = functools.partial(_mix_kernel, tiles_per_batch=tiles_per_batch, tiles_per_group=tiles_per_group,
                             n_experts=n_experts, nt=nt)
    out_shape = (
        jax.ShapeDtypeStruct((n, d_model), F32),
        jax.ShapeDtypeStruct((n, d_model // 2), U32),
        jax.ShapeDtypeStruct((8, n), I32),
        jax.ShapeDtypeStruct((n, LANES), F32),
        jax.ShapeDtypeStruct((8, n), I32),
        jax.ShapeDtypeStruct((nt, n_experts, LANES), F32),
    )
    cols = pl.BlockSpec((8, tm), lambda i: (0, prev(i)))
    out_specs = (rows(d_model), rows(d_model // 2), cols, pl.BlockSpec((tm, LANES), lambda i: (prev(i), 0)), cols,
                 pl.BlockSpec((1, n_experts, LANES), lambda i: (prev(i), 0, 0)))
    return pl.pallas_call(
        kern,
        grid=(nt + 1,),
        in_specs=[rows_in(d_model), words_in(conv_dim), words_in(conv_dim),
                  pl.BlockSpec((BF16_SUBLANES // 2, conv_dim),
                               lambda i: (jnp.maximum((cur(i) + tile0) * hb - 1, 0), 0)),
                  words_in(attn_dim), words_in(d_model), words_in(d_model),
                  full(conv_w.shape), full(wco.shape), full(wao.shape), full(wout.shape),
                  full((1, d_model)), full(wr.shape), full((n_experts, 1))],
        out_specs=out_specs,
        out_shape=out_shape,
        scratch_shapes=[pltpu.VMEM((n_experts, LANES), F32), pltpu.VMEM((n_experts, tm), F32)],
        compiler_params=_params(("arbitrary",)),
        name="mix",
    )(x2d, cb, ucx, ucx, o, sgc, sga, conv_w, wco, wao, wout, g_ffn, wr, br)


SC_CORES = 2
SC_SUBCORES = 16
SC_WORKERS = SC_CORES * SC_SUBCORES
SC_CHUNK = 64


def _sc_mesh():
    return plsc.VectorSubcoreMesh(core_axis_name="c", subcore_axis_name="s",
                                  num_cores=SC_CORES, num_subcores=SC_SUBCORES)


def _sc_worker():
    return lax.axis_index("s") * SC_CORES + lax.axis_index("c")


def _sc_scatter_rows(src, dest4, pad_idx, n_out, row0):
    width = src.shape[1]
    nchunks = dest4.shape[1]
    npad = pad_idx.shape[1]

    @functools.partial(
        pl.kernel, mesh=_sc_mesh(),
        out_type=jax.ShapeDtypeStruct((n_out, width), src.dtype),
        scratch_types=[pltpu.VMEM((TOP_K, SC_CHUNK), I32), pltpu.VMEM((SC_CHUNK, width), src.dtype),
                       pltpu.VMEM((npad, SC_CHUNK), I32), pltpu.SemaphoreType.DMA],
        name="sc_scatter_rows",
    )
    def k(src_hbm, dest_hbm, pad_hbm, out_hbm, idx_v, rows_v, pad_v, sem):
        wid = _sc_worker()
        base = row0 + wid * (nchunks * SC_CHUNK)

        @pl.loop(0, nchunks)
        def _(j):
            pltpu.sync_copy(src_hbm.at[pl.ds(base + j * SC_CHUNK, SC_CHUNK)], rows_v)
            pltpu.sync_copy(dest_hbm.at[wid, j], idx_v)
            copies = [pltpu.async_copy(rows_v, out_hbm.at[idx_v.at[kk]], sem) for kk in range(TOP_K)]
            for cp in copies:
                cp.wait()

        pltpu.sync_copy(pad_hbm.at[wid], pad_v)
        fills = [pltpu.async_copy(rows_v, out_hbm.at[pad_v.at[p]], sem) for p in range(npad)]
        for cp in fills:
            cp.wait()

    return k(src, dest4, pad_idx)


def _sc_gather_rows(src, dest4, n):
    width = src.shape[1]
    nchunks = dest4.shape[1]

    @functools.partial(
        pl.kernel, mesh=_sc_mesh(),
        out_type=jax.ShapeDtypeStruct((TOP_K, n, width), src.dtype),
        scratch_types=[pltpu.VMEM((TOP_K, SC_CHUNK), I32), pltpu.VMEM((SC_CHUNK, width), src.dtype),
                       pltpu.VMEM((SC_CHUNK, width), src.dtype), pltpu.SemaphoreType.DMA, pltpu.SemaphoreType.DMA],
        name="sc_gather_rows",
    )
    def k(src_hbm, dest_hbm, out_hbm, idx_v, rows_a, rows_b, sem_a, sem_b):
        wid = _sc_worker()
        base = wid * (nchunks * SC_CHUNK)
        bufs = ((rows_a, sem_a), (rows_b, sem_b))

        @pl.loop(0, nchunks)
        def _(j):
            pltpu.sync_copy(dest_hbm.at[wid, j], idx_v)
            pending = pltpu.async_copy(src_hbm.at[idx_v.at[0]], rows_a, sem_a)
            for kk in range(TOP_K):
                buf, _ = bufs[kk % 2]
                pending.wait()
                if kk + 1 < TOP_K:
                    nbuf, nsem = bufs[(kk + 1) % 2]
                    pending = pltpu.async_copy(src_hbm.at[idx_v.at[kk + 1]], nbuf, nsem)
                pltpu.sync_copy(buf, out_hbm.at[kk, pl.ds(base + j * SC_CHUNK, SC_CHUNK)])

    return k(src, dest4)


def _expert_kernel(be_ref, nu_ref, slot_ref, nxt_ref, xs_ref, wgu_hbm, bgu_ref, wd_hbm, bd_ref, ys_ref,
                   wgu_f32, wd_f32, wgu_bf, wd_bf, sem, *, d_ff, ff_chunk):
    i = pl.program_id(0)

    def weight_copies(e, s):
        return (pltpu.make_async_copy(wgu_hbm.at[e], wgu_f32.at[s], sem.at[s]),
                pltpu.make_async_copy(wd_hbm.at[e], wd_f32.at[s], sem.at[s]))

    @pl.when(i == 0)
    def _():
        for cp in weight_copies(be_ref[0], 0):
            cp.start()

    @pl.when((i < nu_ref[0]) & ((i == 0) | (be_ref[i] != be_ref[jnp.maximum(i - 1, 0)])))
    def _():
        s = slot_ref[i]
        for cp in weight_copies(be_ref[i], s):
            cp.wait()
        wgu_bf[...] = wgu_f32[s].astype(BF16)
        wd_bf[...] = wd_f32[s].astype(BF16)
        nxt = nxt_ref[i]

        @pl.when(nxt >= 0)
        def _():
            for cp in weight_copies(nxt, 1 - s):
                cp.start()

    @pl.when(i < nu_ref[0])
    def _():
        lo, hi = _unpack_bf16_pair(xs_ref[...])
        xb = jnp.concatenate([lo.astype(BF16), hi.astype(BF16)], axis=1)
        chunks = range(0, d_ff, ff_chunk)
        gus = []
        for c0 in chunks:
            glu = jnp.dot(xb, wgu_bf[:, c0:c0 + ff_chunk], preferred_element_type=F32) + bgu_ref[0, :, c0:c0 + ff_chunk]
            lin = (jnp.dot(xb, wgu_bf[:, d_ff + c0:d_ff + c0 + ff_chunk], preferred_element_type=F32)
                   + bgu_ref[0, :, d_ff + c0:d_ff + c0 + ff_chunk])
            gus.append((glu, lin))
        y = bd_ref[0]
        for c0, (glu, lin) in zip(chunks, gus):
            glu = jnp.minimum(glu, SWIGLU_LIMIT)
            lin = jnp.clip(lin, -SWIGLU_LIMIT, SWIGLU_LIMIT)
            act = (glu * jax.nn.sigmoid(SWIGLU_ALPHA * glu)) * (lin + 1.0)
            y = y + jnp.dot(act.astype(BF16), wd_bf[c0:c0 + ff_chunk, :], preferred_element_type=F32)
        yb = y.astype(BF16).astype(F32)
        half = y.shape[1] // 2
        ys_ref[...] = _pack_bf16_pair(yb[:, :half], yb[:, half:])


def _experts(xs, block_e, n_used, slot, next_e, wgu, bgu, wd, bd, *, tm, n_blocks):
    half = xs.shape[1]
    p = n_blocks * tm
    e, d_model, two_ff = wgu.shape
    d_ff = two_ff // 2
    kern = functools.partial(_expert_kernel, d_ff=d_ff, ff_chunk=min(256, d_ff))
    blk = lambda i, be, nu, sl, nx: (jnp.minimum(i, nu[0] - 1), 0)
    bias = lambda i, be, nu, sl, nx: (be[i], 0, 0)
    grid_spec = pltpu.PrefetchScalarGridSpec(
        num_scalar_prefetch=4,
        grid=(n_blocks,),
        in_specs=[
            pl.BlockSpec((tm, half), blk),
            pl.BlockSpec(memory_space=pl.ANY),
            pl.BlockSpec((1, 1, two_ff), bias),
            pl.BlockSpec(memory_space=pl.ANY),
            pl.BlockSpec((1, 1, d_model), bias),
        ],
        out_specs=pl.BlockSpec((tm, half), blk),
        scratch_shapes=[pltpu.VMEM((2, d_model, two_ff), F32), pltpu.VMEM((2, d_ff, d_model), F32),
                        pltpu.VMEM((d_model, two_ff), BF16), pltpu.VMEM((d_ff, d_model), BF16),
                        pltpu.SemaphoreType.DMA((2,))],
    )
    return pl.pallas_call(
        kern,
        grid_spec=grid_spec,
        out_shape=jax.ShapeDtypeStruct((p, half), U32),
        compiler_params=_params(("arbitrary",)),
        name="experts",
    )(block_e, n_used, slot, next_e, xs, wgu, bgu.reshape(e, 1, two_ff), wd, bd.reshape(e, 1, d_model))


def _final_kernel(yk_ref, gate_ref, x1_ref, g_ref, out_ref):
    tm, d_model = x1_ref.shape
    half = d_model // 2
    gates = gate_ref[...]
    acc_lo = jnp.zeros((tm, half), F32)
    acc_hi = jnp.zeros((tm, half), F32)
    for kk in range(TOP_K):
        lo, hi = _unpack_bf16_pair(yk_ref[kk])
        gk = gates[:, kk:kk + 1]
        acc_lo = acc_lo + gk * lo
        acc_hi = acc_hi + gk * hi
    x2 = x1_ref[...] + jnp.concatenate([acc_lo, acc_hi], axis=1)
    ms = jnp.mean(x2 * x2, axis=-1, keepdims=True)
    out_ref[...] = (x2 * lax.rsqrt(ms + RMS_EPS)) * g_ref[...]


def _final(yk, gates, x1, g_final, *, tm, tile0):
    n, d_model = x1.shape
    rows = lambda w: pl.BlockSpec((tm, w), lambda i: (i + tile0, 0))
    return pl.pallas_call(
        _final_kernel,
        grid=(yk.shape[1] // tm,),
        in_specs=[pl.BlockSpec((TOP_K, tm, d_model // 2), lambda i: (0, i, 0)), rows(LANES), rows(d_model),
                  pl.BlockSpec((1, d_model), lambda i: (0, 0))],
        out_specs=rows(d_model),
        out_shape=jax.ShapeDtypeStruct((n, d_model), F32),
        input_output_aliases={2: 0},
        compiler_params=_params(("arbitrary",)),
        name="final",
    )(yk, gates, x1, g_final)


def _tiles(seq):
    tm = min(512, seq)
    return dict(tm_proj=tm, t_attn=min(256, seq), tm_mix=tm, tm_expert=512, tm_final=tm, moe_groups=2)


def _forward(x, g_mix, w_in, conv_w, b_f, w_conv_o, w_attn_o, w_out, g_ffn, w_router, b_router,
             w_gate_up, b_gate_up, w_down, b_down, g_final, tiles):
    batch, seq, d_model = x.shape
    n = batch * seq
    conv_dim = conv_w.shape[1]
    attn_dim = w_attn_o.shape[0]
    heads = b_f.shape[0]
    head_dim = attn_dim // heads
    n_experts = w_router.shape[1]
    x2d = x.reshape(n, d_model)

    c0 = 3 * conv_dim
    a0 = c0 + 3 * attn_dim
    wa = w_in[:, :c0].astype(BF16)
    scale = LOG2E / (head_dim ** 0.5)
    wqk = jnp.concatenate([w_in[:, c0:c0 + attn_dim] * scale, w_in[:, c0 + attn_dim:c0 + 2 * attn_dim]],
                          axis=1).astype(BF16)
    wvt = w_in[:, c0 + 2 * attn_dim:a0].T.astype(BF16)
    wf = jnp.pad(w_in[:, a0:a0 + heads], ((0, 0), (0, LANES - heads))).astype(BF16)
    wg = w_in[:, a0 + heads:].astype(BF16)
    bf_pad = jnp.pad(b_f, (0, LANES - heads)).reshape(1, LANES)
    wr_hi = w_router.astype(BF16)
    wr_lo = (w_router - wr_hi.astype(F32)).astype(BF16)
    wr = jnp.concatenate([wr_hi, wr_lo], axis=1).T
    br = b_router.reshape(n_experts, 1)

    cb, ucx, qa, ka, vt, sgc, sga = _inproj(
        x2d, g_mix.reshape(1, d_model), wa, wqk, wvt, wg, wf, bf_pad,
        batch=batch, seq=seq, heads=heads, tm=tiles["tm_proj"], tk=tiles["t_attn"])
    o = _attention(qa, ka, vt, batch=batch, seq=seq, heads=heads, tq=tiles["t_attn"])
    groups = tiles["moe_groups"]
    ng = n // groups
    x1, h2p, topi, gates, rank, cnt = _mix(
        x2d, cb, ucx, o, sgc, sga, conv_w, w_conv_o.astype(BF16), w_attn_o.astype(BF16), w_out.astype(BF16),
        g_ffn.reshape(1, d_model), wr, br, seq=seq, n_experts=n_experts, tm=tiles["tm_mix"],
        tile0=0, nt=n // tiles["tm_mix"], tiles_per_group=ng // tiles["tm_mix"])

    tme = tiles["tm_expert"]
    n_blocks = (ng * TOP_K) // tme + n_experts
    p_rows = n_blocks * tme
    nchunks = ng // (SC_WORKERS * SC_CHUNK)
    jj = jnp.arange(tme, dtype=I32)[None, :]
    spare = p_rows + jnp.arange(n_experts, dtype=I32)[:, None] * tme + jj
    expert_ids = jnp.arange(n_experts, dtype=I32)[None, None, :]
    ys_groups, dest_groups = [], []
    for g in range(groups):
        counts = cnt[(g + 1) * (ng // tiles["tm_mix"]) - 1, :, 0].astype(I32)
        padded = ((counts + tme - 1) // tme) * tme
        pend = jnp.cumsum(padded)
        pstart = pend - padded
        n_used = (pend[-1] // tme).astype(I32).reshape(1)
        blk_start = jnp.minimum(jnp.arange(n_blocks, dtype=I32), n_used[0] - 1) * tme
        block_e = jnp.minimum(jnp.sum((pend[None, :] <= blk_start[:, None]).astype(I32), axis=1), n_experts - 1)
        tok = slice(g * ng, (g + 1) * ng)
        onehot = topi[:TOP_K, tok, None] == expert_ids
        dest = jnp.sum(jnp.where(onehot, pstart[None, None, :], 0), axis=2) + rank[:TOP_K, tok]
        dest4 = dest.reshape(TOP_K, SC_WORKERS, nchunks, SC_CHUNK).transpose(1, 2, 0, 3)
        pad_idx = jnp.where(jj < (padded - counts)[:, None], (pstart + counts)[:, None] + jj, spare)
        pad_idx = pad_idx.reshape(SC_WORKERS, (n_experts * tme) // (SC_WORKERS * SC_CHUNK), SC_CHUNK)
        xs = _sc_scatter_rows(h2p, dest4, pad_idx, p_rows + n_experts * tme, g * ng)
        change = jnp.concatenate([jnp.ones((1,), I32), (block_e[1:] != block_e[:-1]).astype(I32)])
        slot = lax.rem(jnp.cumsum(change) - 1, 2).astype(I32)
        eid = jnp.arange(n_experts, dtype=I32)
        later = (eid[None, :] > eid[:, None]) & (padded[None, :] > 0)
        next_tbl = jnp.min(jnp.where(later, eid[None, :], n_experts), axis=1)
        next_tbl = jnp.where(next_tbl < n_experts, next_tbl, -1).astype(I32)
        next_e = jnp.take(next_tbl, block_e)
        ys_groups.append(_experts(xs, block_e, n_used, slot, next_e, w_gate_up, b_gate_up, w_down, b_down,
                                  tm=tme, n_blocks=n_blocks))
        dest_groups.append(dest4)

    out = x1
    for g in range(groups):
        yk = _sc_gather_rows(ys_groups[g], dest_groups[g], ng)
        out = _final(yk, gates, out, g_final.reshape(1, d_model), tm=tiles["tm_final"],
                     tile0=g * (ng // tiles["tm_final"]))
    return out.reshape(batch, seq, d_model)


def kernel(x, g_mix, w_in, conv_w, b_f, w_conv_o, w_attn_o, w_out, g_ffn, w_router, b_router, w_gate_up,
           b_gate_up, w_down, b_down, g_final):
    return _forward(x, g_mix, w_in, conv_w, b_f, w_conv_o, w_attn_o, w_out, g_ffn, w_router, b_router,
                    w_gate_up, b_gate_up, w_down, b_down, g_final, _tiles(x.shape[1]))
```

```python
import functools

import jax
import jax.numpy as jnp
from jax import lax
from jax.experimental import pallas as pl
from jax.experimental.pallas import tpu as pltpu
from jax.experimental.pallas import tpu_sc as plsc

TOP_K = 4
RMS_EPS = 1e-5
SWIGLU_ALPHA = 1.702
SWIGLU_LIMIT = 7.0
LOG2E = 1.4426950408889634

LANES = 128
BF16_SUBLANES = 16
V_ONES_ROWS = BF16_SUBLANES
VMEM_LIMIT_BYTES = 56 * 1024 * 1024

F32 = jnp.float32
BF16 = jnp.bfloat16
U32 = jnp.uint32
I32 = jnp.int32
HI_MASK = 0xFFFF0000


def _params(sem):
    return pltpu.CompilerParams(dimension_semantics=sem, vmem_limit_bytes=VMEM_LIMIT_BYTES)


def _as_words(x_bf16):
    return pltpu.bitcast(x_bf16, U32)


def _as_bf16(w_u32):
    return pltpu.bitcast(w_u32, BF16)


def _pack_bf16_pair(lo_f32, hi_f32):
    lo = lax.bitcast_convert_type(lo_f32, U32)
    hi = lax.bitcast_convert_type(hi_f32, U32)
    return (lo >> 16) | (hi & U32(HI_MASK))


def _unpack_bf16_pair(w):
    lo = lax.bitcast_convert_type(w << 16, F32)
    hi = lax.bitcast_convert_type(w & U32(HI_MASK), F32)
    return lo, hi


def _inproj_kernel(x_ref, g_ref, wa_ref, wqk_ref, wvt_ref, wg_ref, wf_ref, bf_ref,
                   cb_ref, ucx_ref, qa_ref, ka_ref, vt_ref, sgc_ref, sga_ref,
                   carry_ref, *, tiles_per_batch, conv_dim, attn_dim, d_model, heads, tk):
    tm = x_ref.shape[0]
    t = lax.rem(pl.program_id(0), tiles_per_batch)
    x = x_ref[...]
    ms = jnp.mean(x * x, axis=-1, keepdims=True)
    h = ((x * lax.rsqrt(ms + RMS_EPS)) * g_ref[...]).astype(BF16)

    def mm(w_ref, c0, n):
        return jnp.dot(h, w_ref[:, c0:c0 + n], preferred_element_type=F32)

    z = mm(wf_ref, 0, LANES) + bf_ref[...]
    qf = mm(wqk_ref, 0, attn_dim)
    kf = mm(wqk_ref, attn_dim, attn_dim)
    cb = mm(wa_ref, 0, conv_dim)
    ucx = mm(wa_ref, conv_dim, conv_dim) * mm(wa_ref, 2 * conv_dim, conv_dim)

    lane = lax.broadcasted_iota(I32, z.shape, 1)
    logf = jnp.minimum(z, 0.0) - jnp.log1p(jnp.exp(-jnp.abs(z)))
    logf = jnp.where(lane < heads, logf, 0.0)
    p1 = logf.astype(BF16).astype(F32)
    r1 = logf - p1
    p2 = r1.astype(BF16).astype(F32)
    p3 = (r1 - p2).astype(BF16).astype(F32)
    packed = (p1 + pltpu.roll(p2, heads, 1) + pltpu.roll(p3, 2 * heads, 1)).astype(BF16)
    row = lax.broadcasted_iota(I32, (tm, tm), 0)
    col = lax.broadcasted_iota(I32, (tm, tm), 1)
    tri = jnp.where(col <= row, 1.0, 0.0).astype(BF16)
    r = jnp.dot(tri, packed, preferred_element_type=F32)
    local = r + pltpu.roll(r, LANES - heads, 1) + pltpu.roll(r, LANES - 2 * heads, 1)

    @pl.when(t == 0)
    def _():
        carry_ref[...] = jnp.zeros_like(carry_ref)

    c = jnp.where(lane < heads, local + carry_ref[0:1, :], 0.0)
    carry_ref[0:1, :] = c[tm - 1:tm, :]

    c2 = c * LOG2E
    c_hi = c2.astype(BF16).astype(F32)
    c_r = c2 - c_hi
    c_mid = c_r.astype(BF16).astype(F32)
    c_lo = (c_r - c_mid).astype(BF16).astype(F32)
    head_dim = attn_dim // heads
    ext = LANES - head_dim
    lane_e = lax.broadcasted_iota(I32, (tm, ext), 1)
    ones3 = jnp.where(lane_e < 3, 1.0, 0.0)
    for hh in range(heads):
        kext = jnp.where(lane_e == 0, -c_hi[:, hh:hh + 1],
                         jnp.where(lane_e == 1, -c_mid[:, hh:hh + 1],
                                   jnp.where(lane_e == 2, -c_lo[:, hh:hh + 1], 0.0)))
        sl = slice(hh * head_dim, (hh + 1) * head_dim)
        qa_ref[:, hh * LANES:(hh + 1) * LANES] = _as_words(jnp.concatenate([qf[:, sl], ones3], axis=1).astype(BF16))
        ka_ref[:, hh * LANES:(hh + 1) * LANES] = _as_words(jnp.concatenate([kf[:, sl], kext], axis=1).astype(BF16))

    cb_ref[...] = _as_words(cb.astype(BF16))
    ucx_ref[...] = _as_words(ucx.astype(BF16))
    half = d_model // 2
    for c in range(2):
        sgc_ref[:, c * half:(c + 1) * half] = _as_words(jax.nn.sigmoid(mm(wg_ref, c * half, half)).astype(BF16))
        sga_ref[:, c * half:(c + 1) * half] = _as_words(
            jax.nn.sigmoid(mm(wg_ref, d_model + c * half, half)).astype(BF16))

    vt = lax.dot_general(wvt_ref[...], h, (((1,), (1,)), ((), ())), preferred_element_type=F32).astype(BF16)
    ones_rows = jnp.ones((V_ONES_ROWS, tm), BF16)
    vt = jnp.concatenate([piece for hh in range(heads)
                          for piece in (vt[hh * head_dim:(hh + 1) * head_dim], ones_rows)], axis=0)
    for u in range(tm // tk):
        vt_ref[u] = _as_words(vt[:, u * tk:(u + 1) * tk])


def _inproj(x2d, g_mix, wa, wqk, wvt, wg, wf, bf_pad, *, batch, seq, heads, tm, tk):
    n, d_model = x2d.shape
    conv_dim = wa.shape[1] // 3
    attn_dim = wvt.shape[0]
    tiles_per_batch = seq // tm
    grid = (n // tm,)
    full = lambda shape: pl.BlockSpec(shape, lambda i: (0,) * len(shape))
    rows = lambda w: pl.BlockSpec((tm, w), lambda i: (i, 0))
    kern = functools.partial(_inproj_kernel, tiles_per_batch=tiles_per_batch, conv_dim=conv_dim,
                             attn_dim=attn_dim, d_model=d_model, heads=heads, tk=tk)
    vt_rows = attn_dim + heads * V_ONES_ROWS
    words = lambda w: jax.ShapeDtypeStruct((n // 2, w), U32)
    wrows = lambda w: pl.BlockSpec((tm // 2, w), lambda i: (i, 0))
    out_shape = (
        words(conv_dim),
        words(conv_dim),
        words(heads * LANES),
        words(heads * LANES),
        jax.ShapeDtypeStruct((n // tk, vt_rows // 2, tk), U32),
        words(d_model),
        words(d_model),
    )
    out_specs = (
        wrows(conv_dim), wrows(conv_dim), wrows(heads * LANES), wrows(heads * LANES),
        pl.BlockSpec((tm // tk, vt_rows // 2, tk), lambda i: (i, 0, 0)),
        wrows(d_model), wrows(d_model),
    )
    return pl.pallas_call(
        kern,
        grid=grid,
        in_specs=[rows(d_model), full((1, d_model)), full(wa.shape), full(wqk.shape), full(wvt.shape),
                  full(wg.shape), full(wf.shape), full((1, LANES))],
        out_specs=out_specs,
        out_shape=out_shape,
        scratch_shapes=[pltpu.VMEM((8, LANES), F32)],
        compiler_params=_params(("arbitrary",)),
        name="inproj",
    )(x2d, g_mix, wa, wqk, wvt, wg, wf, bf_pad)


def _attn_kernel(q_ref, k_ref, v_ref, o_ref, s_ref, *, tq, tk, head_dim, heads):
    qi = pl.program_id(1)
    row = lax.broadcasted_iota(I32, (tk, tq), 0)
    col = lax.broadcasted_iota(I32, (tk, tq), 1)

    def score_tile(j, slot):
        off = pl.multiple_of(j * (tk // 2), tk // 2)
        for hh in range(heads):
            qa = _as_bf16(q_ref[:, hh * LANES:(hh + 1) * LANES])
            ka = _as_bf16(k_ref[pl.ds(off, tk // 2), hh * LANES:(hh + 1) * LANES])
            s_ref[slot, hh] = lax.dot_general(ka, qa, (((1,), (1,)), ((), ())), preferred_element_type=F32)

    def block(j, slot, carry, masked):
        if not masked:
            score_tile(j + 1, 1 - slot)
        stats = []
        for hh in range(heads):
            m, _ = carry[hh]
            s = s_ref[slot, hh]
            if masked:
                s = jnp.where(row <= col, s, -jnp.inf)
            m_new = jnp.maximum(m, jnp.max(s, axis=0, keepdims=True))
            stats.append((m_new, jnp.exp2(m - m_new), jnp.exp2(s - m_new).astype(BF16)))
        out = []
        for hh in range(heads):
            m_new, alpha, p = stats[hh]
            vth = _as_bf16(v_ref[j, hh * (vrows // 2):(hh + 1) * (vrows // 2), :])
            out.append((m_new, alpha * carry[hh][1] + jnp.dot(vth, p, preferred_element_type=F32)))
        return tuple(out)

    vrows = head_dim + V_ONES_ROWS
    init = tuple((jnp.full((1, tq), -jnp.inf, F32), jnp.zeros((vrows, tq), F32)) for _ in range(heads))
    score_tile(0, 0)

    def pair(p, c):
        return block(2 * p + 1, 1, block(2 * p, 0, c, False), False)

    carry = lax.fori_loop(0, qi // 2, pair, init)
    final = lax.cond(
        lax.rem(qi, 2) == 0,
        lambda c: block(qi, 0, c, True),
        lambda c: block(qi, 1, block(qi - 1, 0, c, False), True),
        carry)
    per_group = LANES // head_dim
    for g in range(heads // per_group):
        accs = [final[g * per_group + u][1] for u in range(per_group)]
        ot = jnp.concatenate([a[:head_dim] / a[head_dim:head_dim + 1] for a in accs],
                             axis=0)
        o_ref[:, g * LANES:(g + 1) * LANES] = _as_words(ot.T.astype(BF16))


def _attention(qa, ka, vt, *, batch, seq, heads, tq):
    n = 2 * qa.shape[0]
    vt_rows, tk = 2 * vt.shape[1], vt.shape[2]
    head_dim = vt_rows // heads - V_ONES_ROWS
    attn_dim = heads * head_dim
    assert LANES % head_dim == 0 and tq == tk
    nq = seq // tq
    nk = seq // tk
    kern = functools.partial(_attn_kernel, tq=tq, tk=tk, head_dim=head_dim, heads=heads)
    return pl.pallas_call(
        kern,
        grid=(batch, nq),
        in_specs=[
            pl.BlockSpec((tq // 2, heads * LANES), lambda b, qi: (b * nq + qi, 0)),
            pl.BlockSpec((seq // 2, heads * LANES), lambda b, qi: (b, 0)),
            pl.BlockSpec((nk, vt_rows // 2, tk), lambda b, qi: (b, 0, 0)),
        ],
        out_specs=pl.BlockSpec((tq // 2, attn_dim), lambda b, qi: (b * nq + qi, 0)),
        out_shape=jax.ShapeDtypeStruct((n // 2, attn_dim), U32),
        scratch_shapes=[pltpu.VMEM((2, heads, tk, tq), F32)],
        compiler_params=_params(("arbitrary", "arbitrary")),
        name="attn",
    )(qa, ka, vt)


def _mix_kernel(x_ref, cb_ref, u_ref, uh_ref, o_ref, sgc_ref, sga_ref, cw_ref, wco_ref, wao_ref, wout_ref,
                gffn_ref, wr_ref, br_ref,
                x1_ref, h2p_ref, topi_ref, gate_ref, rank_ref, cnt_ref, carry_ref, lg_scr,
                *, tiles_per_batch, tiles_per_group, n_experts, nt):
    tm, d_model = x_ref.shape
    i = pl.program_id(0)

    @pl.when(i == 0)
    def _():
        lg_scr[...] = jnp.zeros_like(lg_scr)

    @pl.when((i == 0) | (lax.rem(i - 1, tiles_per_group) == 0))
    def _():
        carry_ref[...] = jnp.zeros_like(carry_ref)

    lg = lg_scr[...]
    sub = lax.broadcasted_iota(I32, (n_experts, tm), 0)

    vals, idxs = [], []
    for _ in range(TOP_K):
        m = jnp.max(lg, axis=0, keepdims=True)
        idx = jnp.min(jnp.where(lg == m, sub, n_experts), axis=0, keepdims=True)
        vals.append(m)
        idxs.append(idx)
        lg = jnp.where(sub == idx, -jnp.inf, lg)
    es = [jnp.exp(vk - vals[0]) for vk in vals]
    denom = es[0] + es[1] + es[2] + es[3]

    sub8 = lax.broadcasted_iota(I32, (8, tm), 0)
    subl = lax.broadcasted_iota(I32, (LANES, tm), 0)
    chosen = jnp.zeros((n_experts, tm), F32)
    topi = jnp.zeros((8, tm), I32)
    gates_t = jnp.zeros((LANES, tm), F32)
    for kk in range(TOP_K):
        chosen = jnp.where(sub == idxs[kk], 1.0, chosen)
        topi = jnp.where(sub8 == kk, idxs[kk], topi)
        gates_t = jnp.where(subl == kk, es[kk] / denom, gates_t)
    topi_ref[...] = topi
    gate_ref[...] = gates_t.T

    first = lax.rem(jnp.minimum(i, nt - 1), tiles_per_batch) == 0

    u = _as_bf16(u_ref[...]).astype(F32)
    halo = jnp.where(first, 0.0, _as_bf16(uh_ref[...]).astype(F32))
    ext = jnp.concatenate([halo, u], axis=0)
    hs = halo.shape[0]
    u1 = pltpu.roll(ext, 1, 0)[hs:]
    u2 = pltpu.roll(ext, 2, 0)[hs:]
    cw = cw_ref[...]
    conv = cw[0:1, :] * u2 + cw[1:2, :] * u1 + cw[2:3, :] * u
    yc = (_as_bf16(cb_ref[...]).astype(F32) * conv).astype(BF16)
    y_conv = jnp.dot(yc, wco_ref[...], preferred_element_type=F32)
    y_attn = jnp.dot(_as_bf16(o_ref[...]), wao_ref[...], preferred_element_type=F32)
    mixed = (_as_bf16(sgc_ref[...]).astype(F32) * y_conv
             + _as_bf16(sga_ref[...]).astype(F32) * y_attn).astype(BF16)
    x1 = x_ref[...] + jnp.dot(mixed, wout_ref[...], preferred_element_type=F32)
    x1_ref[...] = x1

    ms = jnp.mean(x1 * x1, axis=-1, keepdims=True)
    h2 = (x1 * lax.rsqrt(ms + RMS_EPS)) * gffn_ref[...]
    hb = h2.astype(BF16)
    hbf = hb.astype(F32)
    half = d_model // 2
    h2p_ref[...] = _pack_bf16_pair(hbf[:, :half], hbf[:, half:])

    hlo = (h2 - hbf).astype(BF16)
    nt_dims = (((1,), (1,)), ((), ()))
    rt = (lax.dot_general(wr_ref[...], hb, nt_dims, preferred_element_type=F32)
          + lax.dot_general(wr_ref[...], hlo, nt_dims, preferred_element_type=F32))
    lg_scr[...] = rt[:n_experts] + rt[n_experts:] + br_ref[...]

    row = lax.broadcasted_iota(I32, (tm, tm), 0)
    col = lax.broadcasted_iota(I32, (tm, tm), 1)
    tri = jnp.where(row < col, 1.0, 0.0).astype(BF16)
    before = jnp.dot(chosen.astype(BF16), tri, preferred_element_type=F32) + carry_ref[:, 0:1]
    rank = jnp.zeros((8, tm), F32)
    for kk in range(TOP_K):
        rk = jnp.sum(jnp.where(sub == idxs[kk], before, 0.0), axis=0, keepdims=True)
        rank = jnp.where(sub8 == kk, rk, rank)
    rank_ref[...] = rank.astype(I32)
    total = carry_ref[...] + jnp.sum(chosen, axis=1, keepdims=True)
    carry_ref[...] = total
    cnt_ref[0] = total


def _mix(x2d, cb, ucx, o, sgc, sga, conv_w, wco, wao, wout, g_ffn, wr, br, *, seq, n_experts, tm, tile0, nt,
         tiles_per_group):
    d_model = x2d.shape[1]
    n = nt * tm
    conv_dim = cb.shape[1]
    attn_dim = o.shape[1]
    tiles_per_batch = seq // tm
    assert tile0 % tiles_per_batch == 0
    hb = tm // BF16_SUBLANES
    full = lambda shape: pl.BlockSpec(shape, lambda i: (0,) * len(shape))
    cur = lambda i: jnp.minimum(i, nt - 1)
    prev = lambda i: jnp.maximum(i - 1, 0)
    rows_in = lambda w: pl.BlockSpec((tm, w), lambda i: (cur(i) + tile0, 0))
    words_in = lambda w: pl.BlockSpec((tm // 2, w), lambda i: (cur(i) + tile0, 0))
    rows = lambda w: pl.BlockSpec((tm, w), lambda i: (cur(i), 0))
    kern = functools.partial(_mix_kernel, tiles_per_batch=tiles_per_batch, tiles_per_group=tiles_per_group,
                             n_experts=n_experts, nt=nt)
    out_shape = (
        jax.ShapeDtypeStruct((n, d_model), F32),
        jax.ShapeDtypeStruct((n, d_model // 2), U32),
        jax.ShapeDtypeStruct((8, n), I32),
        jax.ShapeDtypeStruct((n, LANES), F32),
        jax.ShapeDtypeStruct((8, n), I32),
        jax.ShapeDtypeStruct((nt, n_experts, LANES), F32),
    )
    cols = pl.BlockSpec((8, tm), lambda i: (0, prev(i)))
    out_specs = (rows(d_model), rows(d_model // 2), cols, pl.BlockSpec((tm, LANES), lambda i: (prev(i), 0)), cols,
                 pl.BlockSpec((1, n_experts, LANES), lambda i: (prev(i), 0, 0)))
    return pl.pallas_call(
        kern,
        grid=(nt + 1,),
        in_specs=[rows_in(d_model), words_in(conv_dim), words_in(conv_dim),
                  pl.BlockSpec((BF16_SUBLANES // 2, conv_dim),
                               lambda i: (jnp.maximum((cur(i) + tile0) * hb - 1, 0), 0)),
                  words_in(attn_dim), words_in(d_model), words_in(d_model),
                  full(conv_w.shape), full(wco.shape), full(wao.shape), full(wout.shape),
                  full((1, d_model)), full(wr.shape), full((n_experts, 1))],
        out_specs=out_specs,
        out_shape=out_shape,
        scratch_shapes=[pltpu.VMEM((n_experts, LANES), F32), pltpu.VMEM((n_experts, tm), F32)],
        compiler_params=_params(("arbitrary",)),
        name="mix",
    )(x2d, cb, ucx, ucx, o, sgc, sga, conv_w, wco, wao, wout, g_ffn, wr, br)


SC_CORES = 2
SC_SUBCORES = 16
SC_WORKERS = SC_CORES * SC_SUBCORES
SC_CHUNK = 64


def _sc_mesh():
    return plsc.VectorSubcoreMesh(core_axis_name="c", subcore_axis_name="s",
                                  num_cores=SC_CORES, num_subcores=SC_SUBCORES)


def _sc_worker():
    return lax.axis_index("s") * SC_CORES + lax.axis_index("c")


def _sc_scatter_rows(src, dest4, pad_idx, n_out, row0):
    width = src.shape[1]
    nchunks = dest4.shape[1]
    npad = pad_idx.shape[1]

    @functools.partial(
        pl.kernel, mesh=_sc_mesh(),
        out_type=jax.ShapeDtypeStruct((n_out, width), src.dtype),
        scratch_types=[pltpu.VMEM((TOP_K, SC_CHUNK), I32), pltpu.VMEM((SC_CHUNK, width), src.dtype),
                       pltpu.VMEM((npad, SC_CHUNK), I32), pltpu.SemaphoreType.DMA],
        name="sc_scatter_rows",
    )
    def k(src_hbm, dest_hbm, pad_hbm, out_hbm, idx_v, rows_v, pad_v, sem):
        wid = _sc_worker()
        base = row0 + wid * (nchunks * SC_CHUNK)

        @pl.loop(0, nchunks)
        def _(j):
            pltpu.sync_copy(src_hbm.at[pl.ds(base + j * SC_CHUNK, SC_CHUNK)], rows_v)
            pltpu.sync_copy(dest_hbm.at[wid, j], idx_v)
            copies = [pltpu.async_copy(rows_v, out_hbm.at[idx_v.at[kk]], sem) for kk in range(TOP_K)]
            for cp in copies:
                cp.wait()

        pltpu.sync_copy(pad_hbm.at[wid], pad_v)
        fills = [pltpu.async_copy(rows_v, out_hbm.at[pad_v.at[p]], sem) for p in range(npad)]
        for cp in fills:
            cp.wait()

    return k(src, dest4, pad_idx)


def _sc_gather_rows(src, dest4, n):
    width = src.shape[1]
    nchunks = dest4.shape[1]

    @functools.partial(
        pl.kernel, mesh=_sc_mesh(),
        out_type=jax.ShapeDtypeStruct((TOP_K, n, width), src.dtype),
        scratch_types=[pltpu.VMEM((TOP_K, SC_CHUNK), I32), pltpu.VMEM((SC_CHUNK, width), src.dtype),
                       pltpu.VMEM((SC_CHUNK, width), src.dtype), pltpu.SemaphoreType.DMA, pltpu.SemaphoreType.DMA],
        name="sc_gather_rows",
    )
    def k(src_hbm, dest_hbm, out_hbm, idx_v, rows_a, rows_b, sem_a, sem_b):
        wid = _sc_worker()
        base = wid * (nchunks * SC_CHUNK)
        bufs = ((rows_a, sem_a), (rows_b, sem_b))

        @pl.loop(0, nchunks)
        def _(j):
            pltpu.sync_copy(dest_hbm.at[wid, j], idx_v)
            pending = pltpu.async_copy(src_hbm.at[idx_v.at[0]], rows_a, sem_a)
            for kk in range(TOP_K):
                buf, _ = bufs[kk % 2]
                pending.wait()
                if kk + 1 < TOP_K:
                    nbuf, nsem = bufs[(kk + 1) % 2]
                    pending = pltpu.async_copy(src_hbm.at[idx_v.at[kk + 1]], nbuf, nsem)
                pltpu.sync_copy(buf, out_hbm.at[kk, pl.ds(base + j * SC_CHUNK, SC_CHUNK)])

    return k(src, dest4)


def _expert_kernel(be_ref, nu_ref, slot_ref, nxt_ref, xs_ref, wgu_hbm, bgu_ref, wd_hbm, bd_ref, ys_ref,
                   wgu_f32, wd_f32, wgu_bf, wd_bf, sem, *, d_ff, ff_chunk):
    i = pl.program_id(0)

    def weight_copies(e, s):
        return (pltpu.make_async_copy(wgu_hbm.at[e], wgu_f32.at[s], sem.at[s]),
                pltpu.make_async_copy(wd_hbm.at[e], wd_f32.at[s], sem.at[s]))

    @pl.when(i == 0)
    def _():
        for cp in weight_copies(be_ref[0], 0):
            cp.start()

    @pl.when((i < nu_ref[0]) & ((i == 0) | (be_ref[i] != be_ref[jnp.maximum(i - 1, 0)])))
    def _():
        s = slot_ref[i]
        for cp in weight_copies(be_ref[i], s):
            cp.wait()
        wgu_bf[...] = wgu_f32[s].astype(BF16)
        wd_bf[...] = wd_f32[s].astype(BF16)
        nxt = nxt_ref[i]

        @pl.when(nxt >= 0)
        def _():
            for cp in weight_copies(nxt, 1 - s):
                cp.start()

    @pl.when(i < nu_ref[0])
    def _():
        lo, hi = _unpack_bf16_pair(xs_ref[...])
        xb = jnp.concatenate([lo.astype(BF16), hi.astype(BF16)], axis=1)
        chunks = range(0, d_ff, ff_chunk)
        gus = []
        for c0 in chunks:
            glu = jnp.dot(xb, wgu_bf[:, c0:c0 + ff_chunk], preferred_element_type=F32) + bgu_ref[0, :, c0:c0 + ff_chunk]
            lin = (jnp.dot(xb, wgu_bf[:, d_ff + c0:d_ff + c0 + ff_chunk], preferred_element_type=F32)
                   + bgu_ref[0, :, d_ff + c0:d_ff + c0 + ff_chunk])
            gus.append((glu, lin))
        y = bd_ref[0]
        for c0, (glu, lin) in zip(chunks, gus):
            glu = jnp.minimum(glu, SWIGLU_LIMIT)
            lin = jnp.clip(lin, -SWIGLU_LIMIT, SWIGLU_LIMIT)
            act = (glu * jax.nn.sigmoid(SWIGLU_ALPHA * glu)) * (lin + 1.0)
            y = y + jnp.dot(act.astype(BF16), wd_bf[c0:c0 + ff_chunk, :], preferred_element_type=F32)
        yb = y.astype(BF16).astype(F32)
        half = y.shape[1] // 2
        ys_ref[...] = _pack_bf16_pair(yb[:, :half], yb[:, half:])


def _experts(xs, block_e, n_used, slot, next_e, wgu, bgu, wd, bd, *, tm, n_blocks):
    half = xs.shape[1]
    p = n_blocks * tm
    e, d_model, two_ff = wgu.shape
    d_ff = two_ff // 2
    kern = functools.partial(_expert_kernel, d_ff=d_ff, ff_chunk=min(256, d_ff))
    blk = lambda i, be, nu, sl, nx: (jnp.minimum(i, nu[0] - 1), 0)
    bias = lambda i, be, nu, sl, nx: (be[i], 0, 0)
    grid_spec = pltpu.PrefetchScalarGridSpec(
        num_scalar_prefetch=4,
        grid=(n_blocks,),
        in_specs=[
            pl.BlockSpec((tm, half), blk),
            pl.BlockSpec(memory_space=pl.ANY),
            pl.BlockSpec((1, 1, two_ff), bias),
            pl.BlockSpec(memory_space=pl.ANY),
            pl.BlockSpec((1, 1, d_model), bias),
        ],
        out_specs=pl.BlockSpec((tm, half), blk),
        scratch_shapes=[pltpu.VMEM((2, d_model, two_ff), F32), pltpu.VMEM((2, d_ff, d_model), F32),
                        pltpu.VMEM((d_model, two_ff), BF16), pltpu.VMEM((d_ff, d_model), BF16),
                        pltpu.SemaphoreType.DMA((2,))],
    )
    return pl.pallas_call(
        kern,
        grid_spec=grid_spec,
        out_shape=jax.ShapeDtypeStruct((p, half), U32),
        compiler_params=_params(("arbitrary",)),
        name="experts",
    )(block_e, n_used, slot, next_e, xs, wgu, bgu.reshape(e, 1, two_ff), wd, bd.reshape(e, 1, d_model))


def _final_kernel(yk_ref, gate_ref, x1_ref, g_ref, out_ref):
    tm, d_model = x1_ref.shape
    half = d_model // 2
    gates = gate_ref[...]
    acc_lo = jnp.zeros((tm, half), F32)
    acc_hi = jnp.zeros((tm, half), F32)
    for kk in range(TOP_K):
        lo, hi = _unpack_bf16_pair(yk_ref[kk])
        gk = gates[:, kk:kk + 1]
        acc_lo = acc_lo + gk * lo
        acc_hi = acc_hi + gk * hi
    x2 = x1_ref[...] + jnp.concatenate([acc_lo, acc_hi], axis=1)
    ms = jnp.mean(x2 * x2, axis=-1, keepdims=True)
    out_ref[...] = (x2 * lax.rsqrt(ms + RMS_EPS)) * g_ref[...]


def _final(yk, gates, x1, g_final, *, tm, tile0):
    n, d_model = x1.shape
    rows = lambda w: pl.BlockSpec((tm, w), lambda i: (i + tile0, 0))
    return pl.pallas_call(
        _final_kernel,
        grid=(yk.shape[1] // tm,),
        in_specs=[pl.BlockSpec((TOP_K, tm, d_model // 2), lambda i: (0, i, 0)), rows(LANES), rows(d_model),
                  pl.BlockSpec((1, d_model), lambda i: (0, 0))],
        out_specs=rows(d_model),
        out_shape=jax.ShapeDtypeStruct((n, d_model), F32),
        input_output_aliases={2: 0},
        compiler_params=_params(("arbitrary",)),
        name="final",
    )(yk, gates, x1, g_final)


def _tiles(seq):
    tm = min(512, seq)
    return dict(tm_proj=tm, t_attn=min(256, seq), tm_mix=tm, tm_expert=512, tm_final=tm, moe_groups=2)


def _forward(x, g_mix, w_in, conv_w, b_f, w_conv_o, w_attn_o, w_out, g_ffn, w_router, b_router,
             w_gate_up, b_gate_up, w_down, b_down, g_final, tiles):
    batch, seq, d_model = x.shape
    n = batch * seq
    conv_dim = conv_w.shape[1]
    attn_dim = w_attn_o.shape[0]
    heads = b_f.shape[0]
    head_dim = attn_dim // heads
    n_experts = w_router.shape[1]
    x2d = x.reshape(n, d_model)

    c0 = 3 * conv_dim
    a0 = c0 + 3 * attn_dim
    wa = w_in[:, :c0].astype(BF16)
    scale = LOG2E / (head_dim ** 0.5)
    wqk = jnp.concatenate([w_in[:, c0:c0 + attn_dim] * scale, w_in[:, c0 + attn_dim:c0 + 2 * attn_dim]],
                          axis=1).astype(BF16)
    wvt = w_in[:, c0 + 2 * attn_dim:a0].T.astype(BF16)
    wf = jnp.pad(w_in[:, a0:a0 + heads], ((0, 0), (0, LANES - heads))).astype(BF16)
    wg = w_in[:, a0 + heads:].astype(BF16)
    bf_pad = jnp.pad(b_f, (0, LANES - heads)).reshape(1, LANES)
    wr_hi = w_router.astype(BF16)
    wr_lo = (w_router - wr_hi.astype(F32)).astype(BF16)
    wr = jnp.concatenate([wr_hi, wr_lo], axis=1).T
    br = b_router.reshape(n_experts, 1)

    cb, ucx, qa, ka, vt, sgc, sga = _inproj(
        x2d, g_mix.reshape(1, d_model), wa, wqk, wvt, wg, wf, bf_pad,
        batch=batch, seq=seq, heads=heads, tm=tiles["tm_proj"], tk=tiles["t_attn"])
    o = _attention(qa, ka, vt, batch=batch, seq=seq, heads=heads, tq=tiles["t_attn"])
    groups = tiles["moe_groups"]
    ng = n // groups
    x1, h2p, topi, gates, rank, cnt = _mix(
        x2d, cb, ucx, o, sgc, sga, conv_w, w_conv_o.astype(BF16), w_attn_o.astype(BF16), w_out.astype(BF16),
        g_ffn.reshape(1, d_model), wr, br, seq=seq, n_experts=n_experts, tm=tiles["tm_mix"],
        tile0=0, nt=n // tiles["tm_mix"], tiles_per_group=ng // tiles["tm_mix"])

    tme = tiles["tm_expert"]
    n_blocks = (ng * TOP_K) // tme + n_experts
    p_rows = n_blocks * tme
    nchunks = ng // (SC_WORKERS * SC_CHUNK)
    jj = jnp.arange(tme, dtype=I32)[None, :]
    spare = p_rows + jnp.arange(n_experts, dtype=I32)[:, None] * tme + jj
    expert_ids = jnp.arange(n_experts, dtype=I32)[None, None, :]
    ys_groups, dest_groups = [], []
    for g in range(groups):
        counts = cnt[(g + 1) * (ng // tiles["tm_mix"]) - 1, :, 0].astype(I32)
        padded = ((counts + tme - 1) // tme) * tme
        pend = jnp.cumsum(padded)
        pstart = pend - padded
        n_used = (pend[-1] // tme).astype(I32).reshape(1)
        blk_start = jnp.minimum(jnp.arange(n_blocks, dtype=I32), n_used[0] - 1) * tme
        block_e = jnp.minimum(jnp.sum((pend[None, :] <= blk_start[:, None]).astype(I32), axis=1), n_experts - 1)
        tok = slice(g * ng, (g + 1) * ng)
        onehot = topi[:TOP_K, tok, None] == expert_ids
        dest = jnp.sum(jnp.where(onehot, pstart[None, None, :], 0), axis=2) + rank[:TOP_K, tok]
        dest4 = dest.reshape(TOP_K, SC_WORKERS, nchunks, SC_CHUNK).transpose(1, 2, 0, 3)
        pad_idx = jnp.where(jj < (padded - counts)[:, None], (pstart + counts)[:, None] + jj, spare)
        pad_idx = pad_idx.reshape(SC_WORKERS, (n_experts * tme) // (SC_WORKERS * SC_CHUNK), SC_CHUNK)
        xs = _sc_scatter_rows(h2p, dest4, pad_idx, p_rows + n_experts * tme, g * ng)
        eid = jnp.arange(n_experts, dtype=I32)
        active = padded > 0
        run_tbl = jnp.sum((eid[None, :] < eid[:, None]) & active[None, :], axis=1)
        later = (eid[None, :] > eid[:, None]) & active[None, :]
        next_tbl = jnp.min(jnp.where(later, eid[None, :], n_experts), axis=1)
        next_tbl = jnp.where(next_tbl < n_experts, next_tbl, -1)
        is_e = block_e[:, None] == eid[None, :]
        slot = lax.rem(jnp.sum(jnp.where(is_e, run_tbl[None, :], 0), axis=1), 2).astype(I32)
        next_e = jnp.sum(jnp.where(is_e, next_tbl[None, :], 0), axis=1).astype(I32)
        ys_groups.append(_experts(xs, block_e, n_used, slot, next_e, w_gate_up, b_gate_up, w_down, b_down,
                                  tm=tme, n_blocks=n_blocks))
        dest_groups.append(dest4)

    out = x1
    for g in range(groups):
        yk = _sc_gather_rows(ys_groups[g], dest_groups[g], ng)
        out = _final(yk, gates, out, g_final.reshape(1, d_model), tm=tiles["tm_final"],
                     tile0=g * (ng // tiles["tm_final"]))
    return out.reshape(batch, seq, d_model)


def kernel(x, g_mix, w_in, conv_w, b_f, w_conv_o, w_attn_o, w_out, g_ffn, w_router, b_router, w_gate_up,
           b_gate_up, w_down, b_down, g_final):
    return _forward(x, g_mix, w_in, conv_w, b_f, w_conv_o, w_attn_o, w_out, g_ffn, w_router, b_router,
                    w_gate_up, b_gate_up, w_down, b_down, g_final, _tiles(x.shape[1]))
```

```python
import functools

import jax
import jax.numpy as jnp
from jax import lax
from jax.experimental import pallas as pl
from jax.experimental.pallas import tpu as pltpu
from jax.experimental.pallas import tpu_sc as plsc

TOP_K = 4
RMS_EPS = 1e-5
SWIGLU_ALPHA = 1.702
SWIGLU_LIMIT = 7.0
LOG2E = 1.4426950408889634

LANES = 128
BF16_SUBLANES = 16
V_ONES_ROWS = BF16_SUBLANES
FORGET_ROWS = BF16_SUBLANES
VMEM_LIMIT_BYTES = 56 * 1024 * 1024

F32 = jnp.float32
BF16 = jnp.bfloat16
U32 = jnp.uint32
I32 = jnp.int32
HI_MASK = 0xFFFF0000


def _params(sem):
    return pltpu.CompilerParams(dimension_semantics=sem, vmem_limit_bytes=VMEM_LIMIT_BYTES)


def _as_words(x_bf16):
    return pltpu.bitcast(x_bf16, U32)


def _as_bf16(w_u32):
    return pltpu.bitcast(w_u32, BF16)


def _pack_bf16_pair(lo_f32, hi_f32):
    lo = lax.bitcast_convert_type(lo_f32, U32)
    hi = lax.bitcast_convert_type(hi_f32, U32)
    return (lo >> 16) | (hi & U32(HI_MASK))


def _unpack_bf16_pair(w):
    lo = lax.bitcast_convert_type(w << 16, F32)
    hi = lax.bitcast_convert_type(w & U32(HI_MASK), F32)
    return lo, hi


def _inproj_kernel(x_ref, g_ref, wa_ref, wqk_ref, wvt_ref, wg_ref, bf_ref,
                   cb_ref, ucx_ref, qa_ref, ka_ref, vt_ref, sgc_ref, sga_ref,
                   carry_ref, *, tiles_per_batch, conv_dim, attn_dim, d_model, heads, tk):
    tm = x_ref.shape[0]
    t = lax.rem(pl.program_id(0), tiles_per_batch)
    x = x_ref[...]
    ms = jnp.mean(x * x, axis=-1, keepdims=True)
    h = ((x * lax.rsqrt(ms + RMS_EPS)) * g_ref[...]).astype(BF16)

    def mm(w_ref, c0, n):
        return jnp.dot(h, w_ref[:, c0:c0 + n], preferred_element_type=F32)

    vtf = lax.dot_general(wvt_ref[...], h, (((1,), (1,)), ((), ())), preferred_element_type=F32)
    qf = mm(wqk_ref, 0, attn_dim)
    kf = mm(wqk_ref, attn_dim, attn_dim)
    cb = mm(wa_ref, 0, conv_dim)
    ucx = mm(wa_ref, conv_dim, conv_dim) * mm(wa_ref, 2 * conv_dim, conv_dim)

    z = vtf[attn_dim:attn_dim + FORGET_ROWS] + bf_ref[...]
    sub = lax.broadcasted_iota(I32, z.shape, 0)
    logf = jnp.where(sub < heads, jnp.minimum(z, 0.0) - jnp.log1p(jnp.exp(-jnp.abs(z))), 0.0)
    p1 = logf.astype(BF16).astype(F32)
    r1 = logf - p1
    p2 = r1.astype(BF16).astype(F32)
    p3 = (r1 - p2).astype(BF16).astype(F32)
    packed = jnp.concatenate([p1, p2, p3], axis=0).astype(BF16)
    row = lax.broadcasted_iota(I32, (tm, tm), 0)
    col = lax.broadcasted_iota(I32, (tm, tm), 1)
    tri = jnp.where(row <= col, 1.0, 0.0).astype(BF16)
    r = jnp.dot(packed, tri, preferred_element_type=F32)
    local = r[:FORGET_ROWS] + r[FORGET_ROWS:2 * FORGET_ROWS] + r[2 * FORGET_ROWS:]

    @pl.when(t == 0)
    def _():
        carry_ref[...] = jnp.zeros_like(carry_ref)

    c = local + carry_ref[:, 0:1]
    carry_ref[...] = jnp.broadcast_to(c[:, tm - 1:tm], carry_ref.shape)

    c2 = c * LOG2E
    c_hi_t = c2.astype(BF16).astype(F32)
    c_r = c2 - c_hi_t
    c_mid_t = c_r.astype(BF16).astype(F32)
    c_lo_t = (c_r - c_mid_t).astype(BF16).astype(F32)
    parts = jnp.concatenate([c_hi_t, c_mid_t, c_lo_t, jnp.zeros((LANES - 3 * FORGET_ROWS, tm), F32)], axis=0).T
    c_hi = parts[:, 0:FORGET_ROWS]
    c_mid = parts[:, FORGET_ROWS:2 * FORGET_ROWS]
    c_lo = parts[:, 2 * FORGET_ROWS:3 * FORGET_ROWS]
    head_dim = attn_dim // heads
    ext = LANES - head_dim
    lane_e = lax.broadcasted_iota(I32, (tm, ext), 1)
    ones3 = jnp.where(lane_e < 3, 1.0, 0.0)
    for hh in range(heads):
        kext = jnp.where(lane_e == 0, -c_hi[:, hh:hh + 1],
                         jnp.where(lane_e == 1, -c_mid[:, hh:hh + 1],
                                   jnp.where(lane_e == 2, -c_lo[:, hh:hh + 1], 0.0)))
        sl = slice(hh * head_dim, (hh + 1) * head_dim)
        qa_ref[:, hh * LANES:(hh + 1) * LANES] = _as_words(jnp.concatenate([qf[:, sl], ones3], axis=1).astype(BF16))
        ka_ref[:, hh * LANES:(hh + 1) * LANES] = _as_words(jnp.concatenate([kf[:, sl], kext], axis=1).astype(BF16))

    cb_ref[...] = _as_words(cb.astype(BF16))
    ucx_ref[...] = _as_words(ucx.astype(BF16))
    half = d_model // 2
    for c in range(2):
        sgc_ref[:, c * half:(c + 1) * half] = _as_words(jax.nn.sigmoid(mm(wg_ref, c * half, half)).astype(BF16))
        sga_ref[:, c * half:(c + 1) * half] = _as_words(
            jax.nn.sigmoid(mm(wg_ref, d_model + c * half, half)).astype(BF16))

    vt = vtf[:attn_dim].astype(BF16)
    ones_rows = jnp.ones((V_ONES_ROWS, tm), BF16)
    vt = jnp.concatenate([piece for hh in range(heads)
                          for piece in (vt[hh * head_dim:(hh + 1) * head_dim], ones_rows)], axis=0)
    for u in range(tm // tk):
        vt_ref[u] = _as_words(vt[:, u * tk:(u + 1) * tk])


def _inproj(x2d, g_mix, wa, wqk, wvt, wg, bf_col, *, batch, seq, heads, tm, tk):
    n, d_model = x2d.shape
    conv_dim = wa.shape[1] // 3
    attn_dim = wvt.shape[0] - FORGET_ROWS
    assert heads <= FORGET_ROWS
    tiles_per_batch = seq // tm
    grid = (n // tm,)
    full = lambda shape: pl.BlockSpec(shape, lambda i: (0,) * len(shape))
    rows = lambda w: pl.BlockSpec((tm, w), lambda i: (i, 0))
    kern = functools.partial(_inproj_kernel, tiles_per_batch=tiles_per_batch, conv_dim=conv_dim,
                             attn_dim=attn_dim, d_model=d_model, heads=heads, tk=tk)
    vt_rows = attn_dim + heads * V_ONES_ROWS
    words = lambda w: jax.ShapeDtypeStruct((n // 2, w), U32)
    wrows = lambda w: pl.BlockSpec((tm // 2, w), lambda i: (i, 0))
    out_shape = (
        words(conv_dim),
        words(conv_dim),
        words(heads * LANES),
        words(heads * LANES),
        jax.ShapeDtypeStruct((n // tk, vt_rows // 2, tk), U32),
        words(d_model),
        words(d_model),
    )
    out_specs = (
        wrows(conv_dim), wrows(conv_dim), wrows(heads * LANES), wrows(heads * LANES),
        pl.BlockSpec((tm // tk, vt_rows // 2, tk), lambda i: (i, 0, 0)),
        wrows(d_model), wrows(d_model),
    )
    return pl.pallas_call(
        kern,
        grid=grid,
        in_specs=[rows(d_model), full((1, d_model)), full(wa.shape), full(wqk.shape), full(wvt.shape),
                  full(wg.shape), full((FORGET_ROWS, 1))],
        out_specs=out_specs,
        out_shape=out_shape,
        scratch_shapes=[pltpu.VMEM((FORGET_ROWS, LANES), F32)],
        compiler_params=_params(("arbitrary",)),
        name="inproj",
    )(x2d, g_mix, wa, wqk, wvt, wg, bf_col)


def _attn_kernel(q_ref, k_ref, v_ref, o_ref, s_ref, *, tq, tk, head_dim, heads):
    qi = pl.program_id(1)
    row = lax.broadcasted_iota(I32, (tk, tq), 0)
    col = lax.broadcasted_iota(I32, (tk, tq), 1)

    def score_tile(j, slot):
        off = pl.multiple_of(j * (tk // 2), tk // 2)
        for hh in range(heads):
            qa = _as_bf16(q_ref[:, hh * LANES:(hh + 1) * LANES])
            ka = _as_bf16(k_ref[pl.ds(off, tk // 2), hh * LANES:(hh + 1) * LANES])
            s_ref[slot, hh] = lax.dot_general(ka, qa, (((1,), (1,)), ((), ())), preferred_element_type=F32)

    def block(j, slot, carry, masked):
        if not masked:
            score_tile(j + 1, 1 - slot)
        stats = []
        for hh in range(heads):
            m, _ = carry[hh]
            s = s_ref[slot, hh]
            if masked:
                s = jnp.where(row <= col, s, -jnp.inf)
            m_new = jnp.maximum(m, jnp.max(s, axis=0, keepdims=True))
            stats.append((m_new, jnp.exp2(m - m_new), jnp.exp2(s - m_new).astype(BF16)))
        out = []
        for hh in range(heads):
            m_new, alpha, p = stats[hh]
            vth = _as_bf16(v_ref[j, hh * (vrows // 2):(hh + 1) * (vrows // 2), :])
            out.append((m_new, alpha * carry[hh][1] + jnp.dot(vth, p, preferred_element_type=F32)))
        return tuple(out)

    vrows = head_dim + V_ONES_ROWS
    init = tuple((jnp.full((1, tq), -jnp.inf, F32), jnp.zeros((vrows, tq), F32)) for _ in range(heads))
    score_tile(0, 0)

    def pair(p, c):
        return block(2 * p + 1, 1, block(2 * p, 0, c, False), False)

    carry = lax.fori_loop(0, qi // 2, pair, init)
    final = lax.cond(
        lax.rem(qi, 2) == 0,
        lambda c: block(qi, 0, c, True),
        lambda c: block(qi, 1, block(qi - 1, 0, c, False), True),
        carry)
    per_group = LANES // head_dim
    for g in range(heads // per_group):
        accs = [final[g * per_group + u][1] for u in range(per_group)]
        ot = jnp.concatenate([a[:head_dim] / a[head_dim:head_dim + 1] for a in accs],
                             axis=0)
        o_ref[:, g * LANES:(g + 1) * LANES] = _as_words(ot.T.astype(BF16))


def _attention(qa, ka, vt, *, batch, seq, heads, tq):
    n = 2 * qa.shape[0]
    vt_rows, tk = 2 * vt.shape[1], vt.shape[2]
    head_dim = vt_rows // heads - V_ONES_ROWS
    attn_dim = heads * head_dim
    assert LANES % head_dim == 0 and tq == tk
    nq = seq // tq
    nk = seq // tk
    kern = functools.partial(_attn_kernel, tq=tq, tk=tk, head_dim=head_dim, heads=heads)
    return pl.pallas_call(
        kern,
        grid=(batch, nq),
        in_specs=[
            pl.BlockSpec((tq // 2, heads * LANES), lambda b, qi: (b * nq + qi, 0)),
            pl.BlockSpec((seq // 2, heads * LANES), lambda b, qi: (b, 0)),
            pl.BlockSpec((nk, vt_rows // 2, tk), lambda b, qi: (b, 0, 0)),
        ],
        out_specs=pl.BlockSpec((tq // 2, attn_dim), lambda b, qi: (b * nq + qi, 0)),
        out_shape=jax.ShapeDtypeStruct((n // 2, attn_dim), U32),
        scratch_shapes=[pltpu.VMEM((2, heads, tk, tq), F32)],
        compiler_params=_params(("arbitrary", "arbitrary")),
        name="attn",
    )(qa, ka, vt)


def _mix_kernel(x_ref, cb_ref, u_ref, uh_ref, o_ref, sgc_ref, sga_ref, cw_ref, wco_ref, wao_ref, wout_ref,
                gffn_ref, wr_ref, br_ref,
                x1_ref, h2p_ref, topi_ref, gate_ref, rank_ref, cnt_ref, carry_ref, lg_scr,
                *, tiles_per_batch, tiles_per_group, n_experts, nt):
    tm, d_model = x_ref.shape
    i = pl.program_id(0)

    @pl.when(i == 0)
    def _():
        lg_scr[...] = jnp.zeros_like(lg_scr)

    @pl.when((i == 0) | (lax.rem(i - 1, tiles_per_group) == 0))
    def _():
        carry_ref[...] = jnp.zeros_like(carry_ref)

    lg = lg_scr[...]
    sub = lax.broadcasted_iota(I32, (n_experts, tm), 0)

    vals, idxs = [], []
    for _ in range(TOP_K):
        m = jnp.max(lg, axis=0, keepdims=True)
        idx = jnp.min(jnp.where(lg == m, sub, n_experts), axis=0, keepdims=True)
        vals.append(m)
        idxs.append(idx)
        lg = jnp.where(sub == idx, -jnp.inf, lg)
    es = [jnp.exp(vk - vals[0]) for vk in vals]
    denom = es[0] + es[1] + es[2] + es[3]

    sub8 = lax.broadcasted_iota(I32, (8, tm), 0)
    subl = lax.broadcasted_iota(I32, (LANES, tm), 0)
    chosen = jnp.zeros((n_experts, tm), F32)
    topi = jnp.zeros((8, tm), I32)
    gates_t = jnp.zeros((LANES, tm), F32)
    for kk in range(TOP_K):
        chosen = jnp.where(sub == idxs[kk], 1.0, chosen)
        topi = jnp.where(sub8 == kk, idxs[kk], topi)
        gates_t = jnp.where(subl == kk, es[kk] / denom, gates_t)
    topi_ref[...] = topi
    gate_ref[...] = gates_t.T

    first = lax.rem(jnp.minimum(i, nt - 1), tiles_per_batch) == 0

    u = _as_bf16(u_ref[...]).astype(F32)
    halo = jnp.where(first, 0.0, _as_bf16(uh_ref[...]).astype(F32))
    ext = jnp.concatenate([halo, u], axis=0)
    hs = halo.shape[0]
    u1 = pltpu.roll(ext, 1, 0)[hs:]
    u2 = pltpu.roll(ext, 2, 0)[hs:]
    cw = cw_ref[...]
    conv = cw[0:1, :] * u2 + cw[1:2, :] * u1 + cw[2:3, :] * u
    yc = (_as_bf16(cb_ref[...]).astype(F32) * conv).astype(BF16)
    y_conv = jnp.dot(yc, wco_ref[...], preferred_element_type=F32)
    y_attn = jnp.dot(_as_bf16(o_ref[...]), wao_ref[...], preferred_element_type=F32)
    mixed = (_as_bf16(sgc_ref[...]).astype(F32) * y_conv
             + _as_bf16(sga_ref[...]).astype(F32) * y_attn).astype(BF16)
    x1 = x_ref[...] + jnp.dot(mixed, wout_ref[...], preferred_element_type=F32)
    x1_ref[...] = x1

    ms = jnp.mean(x1 * x1, axis=-1, keepdims=True)
    h2 = (x1 * lax.rsqrt(ms + RMS_EPS)) * gffn_ref[...]
    hb = h2.astype(BF16)
    hbf = hb.astype(F32)
    half = d_model // 2
    h2p_ref[...] = _pack_bf16_pair(hbf[:, :half], hbf[:, half:])

    hlo = (h2 - hbf).astype(BF16)
    nt_dims = (((1,), (1,)), ((), ()))
    rt = (lax.dot_general(wr_ref[...], hb, nt_dims, preferred_element_type=F32)
          + lax.dot_general(wr_ref[...], hlo, nt_dims, preferred_element_type=F32))
    lg_scr[...] = rt[:n_experts] + rt[n_experts:] + br_ref[...]

    row = lax.broadcasted_iota(I32, (tm, tm), 0)
    col = lax.broadcasted_iota(I32, (tm, tm), 1)
    tri = jnp.where(row < col, 1.0, 0.0).astype(BF16)
    before = jnp.dot(chosen.astype(BF16), tri, preferred_element_type=F32) + carry_ref[:, 0:1]
    rank = jnp.zeros((8, tm), F32)
    for kk in range(TOP_K):
        rk = jnp.sum(jnp.where(sub == idxs[kk], before, 0.0), axis=0, keepdims=True)
        rank = jnp.where(sub8 == kk, rk, rank)
    rank_ref[...] = rank.astype(I32)
    total = carry_ref[...] + jnp.sum(chosen, axis=1, keepdims=True)
    carry_ref[...] = total
    cnt_ref[0] = total


def _mix(x2d, cb, ucx, o, sgc, sga, conv_w, wco, wao, wout, g_ffn, wr, br, *, seq, n_experts, tm, tile0, nt,
         tiles_per_group):
    d_model = x2d.shape[1]
    n = nt * tm
    conv_dim = cb.shape[1]
    attn_dim = o.shape[1]
    tiles_per_batch = seq // tm
    assert tile0 % tiles_per_batch == 0
    hb = tm // BF16_SUBLANES
    full = lambda shape: pl.BlockSpec(shape, lambda i: (0,) * len(shape))
    cur = lambda i: jnp.minimum(i, nt - 1)
    prev = lambda i: jnp.maximum(i - 1, 0)
    rows_in = lambda w: pl.BlockSpec((tm, w), lambda i: (cur(i) + tile0, 0))
    words_in = lambda w: pl.BlockSpec((tm // 2, w), lambda i: (cur(i) + tile0, 0))
    rows = lambda w: pl.BlockSpec((tm, w), lambda i: (cur(i), 0))
    kern = functools.partial(_mix_kernel, tiles_per_batch=tiles_per_batch, tiles_per_group=tiles_per_group,
                             n_experts=n_experts, nt=nt)
    out_shape = (
        jax.ShapeDtypeStruct((n, d_model), F32),
        jax.ShapeDtypeStruct((n, d_model // 2), U32),
        jax.ShapeDtypeStruct((8, n), I32),
        jax.ShapeDtypeStruct((n, LANES), F32),
        jax.ShapeDtypeStruct((8, n), I32),
        jax.ShapeDtypeStruct((nt, n_experts, LANES), F32),
    )
    cols = pl.BlockSpec((8, tm), lambda i: (0, prev(i)))
    out_specs = (rows(d_model), rows(d_model // 2), cols, pl.BlockSpec((tm, LANES), lambda i: (prev(i), 0)), cols,
                 pl.BlockSpec((1, n_experts, LANES), lambda i: (prev(i), 0, 0)))
    return pl.pallas_call(
        kern,
        grid=(nt + 1,),
        in_specs=[rows_in(d_model), words_in(conv_dim), words_in(conv_dim),
                  pl.BlockSpec((BF16_SUBLANES // 2, conv_dim),
                               lambda i: (jnp.maximum((cur(i) + tile0) * hb - 1, 0), 0)),
                  words_in(attn_dim), words_in(d_model), words_in(d_model),
                  full(conv_w.shape), full(wco.shape), full(wao.shape), full(wout.shape),
                  full((1, d_model)), full(wr.shape), full((n_experts, 1))],
        out_specs=out_specs,
        out_shape=out_shape,
        scratch_shapes=[pltpu.VMEM((n_experts, LANES), F32), pltpu.VMEM((n_experts, tm), F32)],
        compiler_params=_params(("arbitrary",)),
        name="mix",
    )(x2d, cb, ucx, ucx, o, sgc, sga, conv_w, wco, wao, wout, g_ffn, wr, br)


SC_CORES = 2
SC_SUBCORES = 16
SC_WORKERS = SC_CORES * SC_SUBCORES
SC_CHUNK = 64


def _sc_mesh():
    return plsc.VectorSubcoreMesh(core_axis_name="c", subcore_axis_name="s",
                                  num_cores=SC_CORES, num_subcores=SC_SUBCORES)


def _sc_worker():
    return lax.axis_index("s") * SC_CORES + lax.axis_index("c")


def _sc_scatter_rows(src, dest4, pad_idx, n_out, row0):
    width = src.shape[1]
    nchunks = dest4.shape[1]
    npad = pad_idx.shape[1]

    @functools.partial(
        pl.kernel, mesh=_sc_mesh(),
        out_type=jax.ShapeDtypeStruct((n_out, width), src.dtype),
        scratch_types=[pltpu.VMEM((TOP_K, SC_CHUNK), I32), pltpu.VMEM((SC_CHUNK, width), src.dtype),
                       pltpu.VMEM((npad, SC_CHUNK), I32), pltpu.SemaphoreType.DMA],
        name="sc_scatter_rows",
    )
    def k(src_hbm, dest_hbm, pad_hbm, out_hbm, idx_v, rows_v, pad_v, sem):
        wid = _sc_worker()
        base = row0 + wid * (nchunks * SC_CHUNK)

        @pl.loop(0, nchunks)
        def _(j):
            pltpu.sync_copy(src_hbm.at[pl.ds(base + j * SC_CHUNK, SC_CHUNK)], rows_v)
            pltpu.sync_copy(dest_hbm.at[wid, j], idx_v)
            copies = [pltpu.async_copy(rows_v, out_hbm.at[idx_v.at[kk]], sem) for kk in range(TOP_K)]
            for cp in copies:
                cp.wait()

        pltpu.sync_copy(pad_hbm.at[wid], pad_v)
        fills = [pltpu.async_copy(rows_v, out_hbm.at[pad_v.at[p]], sem) for p in range(npad)]
        for cp in fills:
            cp.wait()

    return k(src, dest4, pad_idx)


def _sc_gather_rows(src, dest4, n):
    width = src.shape[1]
    nchunks = dest4.shape[1]

    @functools.partial(
        pl.kernel, mesh=_sc_mesh(),
        out_type=jax.ShapeDtypeStruct((TOP_K, n, width), src.dtype),
        scratch_types=[pltpu.VMEM((TOP_K, SC_CHUNK), I32), pltpu.VMEM((SC_CHUNK, width), src.dtype),
                       pltpu.VMEM((SC_CHUNK, width), src.dtype), pltpu.SemaphoreType.DMA, pltpu.SemaphoreType.DMA],
        name="sc_gather_rows",
    )
    def k(src_hbm, dest_hbm, out_hbm, idx_v, rows_a, rows_b, sem_a, sem_b):
        wid = _sc_worker()
        base = wid * (nchunks * SC_CHUNK)
        bufs = ((rows_a, sem_a), (rows_b, sem_b))

        @pl.loop(0, nchunks)
        def _(j):
            pltpu.sync_copy(dest_hbm.at[wid, j], idx_v)
            pending = pltpu.async_copy(src_hbm.at[idx_v.at[0]], rows_a, sem_a)
            for kk in range(TOP_K):
                buf, _ = bufs[kk % 2]
                pending.wait()
                if kk + 1 < TOP_K:
                    nbuf, nsem = bufs[(kk + 1) % 2]
                    pending = pltpu.async_copy(src_hbm.at[idx_v.at[kk + 1]], nbuf, nsem)
                pltpu.sync_copy(buf, out_hbm.at[kk, pl.ds(base + j * SC_CHUNK, SC_CHUNK)])

    return k(src, dest4)


def _expert_kernel(be_ref, nu_ref, slot_ref, nxt_ref, xs_ref, wgu_hbm, bgu_ref, wd_hbm, bd_ref, ys_ref,
                   wgu_f32, wd_f32, wgu_bf, wd_bf, sem, *, d_ff, ff_chunk):
    i = pl.program_id(0)

    def weight_copies(e, s):
        return (pltpu.make_async_copy(wgu_hbm.at[e], wgu_f32.at[s], sem.at[s]),
                pltpu.make_async_copy(wd_hbm.at[e], wd_f32.at[s], sem.at[s]))

    @pl.when(i == 0)
    def _():
        for cp in weight_copies(be_ref[0], 0):
            cp.start()

    @pl.when((i < nu_ref[0]) & ((i == 0) | (be_ref[i] != be_ref[jnp.maximum(i - 1, 0)])))
    def _():
        s = slot_ref[i]
        for cp in weight_copies(be_ref[i], s):
            cp.wait()
        wgu_bf[...] = wgu_f32[s].astype(BF16)
        wd_bf[...] = wd_f32[s].astype(BF16)
        nxt = nxt_ref[i]

        @pl.when(nxt >= 0)
        def _():
            for cp in weight_copies(nxt, 1 - s):
                cp.start()

    @pl.when(i < nu_ref[0])
    def _():
        lo, hi = _unpack_bf16_pair(xs_ref[...])
        xb = jnp.concatenate([lo.astype(BF16), hi.astype(BF16)], axis=1)
        chunks = range(0, d_ff, ff_chunk)
        gus = []
        for c0 in chunks:
            glu = jnp.dot(xb, wgu_bf[:, c0:c0 + ff_chunk], preferred_element_type=F32) + bgu_ref[0, :, c0:c0 + ff_chunk]
            lin = (jnp.dot(xb, wgu_bf[:, d_ff + c0:d_ff + c0 + ff_chunk], preferred_element_type=F32)
                   + bgu_ref[0, :, d_ff + c0:d_ff + c0 + ff_chunk])
            gus.append((glu, lin))
        y = bd_ref[0]
        for c0, (glu, lin) in zip(chunks, gus):
            glu = jnp.minimum(glu, SWIGLU_LIMIT)
            lin = jnp.clip(lin, -SWIGLU_LIMIT, SWIGLU_LIMIT)
            act = (glu * jax.nn.sigmoid(SWIGLU_ALPHA * glu)) * (lin + 1.0)
            y = y + jnp.dot(act.astype(BF16), wd_bf[c0:c0 + ff_chunk, :], preferred_element_type=F32)
        yb = y.astype(BF16).astype(F32)
        half = y.shape[1] // 2
        ys_ref[...] = _pack_bf16_pair(yb[:, :half], yb[:, half:])


def _experts(xs, block_e, n_used, slot, next_e, wgu, bgu, wd, bd, *, tm, n_blocks):
    half = xs.shape[1]
    p = n_blocks * tm
    e, d_model, two_ff = wgu.shape
    d_ff = two_ff // 2
    kern = functools.partial(_expert_kernel, d_ff=d_ff, ff_chunk=min(256, d_ff))
    blk = lambda i, be, nu, sl, nx: (jnp.minimum(i, nu[0] - 1), 0)
    bias = lambda i, be, nu, sl, nx: (be[i], 0, 0)
    grid_spec = pltpu.PrefetchScalarGridSpec(
        num_scalar_prefetch=4,
        grid=(n_blocks,),
        in_specs=[
            pl.BlockSpec((tm, half), blk),
            pl.BlockSpec(memory_space=pl.ANY),
            pl.BlockSpec((1, 1, two_ff), bias),
            pl.BlockSpec(memory_space=pl.ANY),
            pl.BlockSpec((1, 1, d_model), bias),
        ],
        out_specs=pl.BlockSpec((tm, half), blk),
        scratch_shapes=[pltpu.VMEM((2, d_model, two_ff), F32), pltpu.VMEM((2, d_ff, d_model), F32),
                        pltpu.VMEM((d_model, two_ff), BF16), pltpu.VMEM((d_ff, d_model), BF16),
                        pltpu.SemaphoreType.DMA((2,))],
    )
    return pl.pallas_call(
        kern,
        grid_spec=grid_spec,
        out_shape=jax.ShapeDtypeStruct((p, half), U32),
        compiler_params=_params(("arbitrary",)),
        name="experts",
    )(block_e, n_used, slot, next_e, xs, wgu, bgu.reshape(e, 1, two_ff), wd, bd.reshape(e, 1, d_model))


def _final_kernel(yk_ref, gate_ref, x1_ref, g_ref, out_ref):
    tm, d_model = x1_ref.shape
    half = d_model // 2
    gates = gate_ref[...]
    acc_lo = jnp.zeros((tm, half), F32)
    acc_hi = jnp.zeros((tm, half), F32)
    for kk in range(TOP_K):
        lo, hi = _unpack_bf16_pair(yk_ref[kk])
        gk = gates[:, kk:kk + 1]
        acc_lo = acc_lo + gk * lo
        acc_hi = acc_hi + gk * hi
    x2 = x1_ref[...] + jnp.concatenate([acc_lo, acc_hi], axis=1)
    ms = jnp.mean(x2 * x2, axis=-1, keepdims=True)
    out_ref[...] = (x2 * lax.rsqrt(ms + RMS_EPS)) * g_ref[...]


def _final(yk, gates, x1, g_final, *, tm, tile0):
    n, d_model = x1.shape
    rows = lambda w: pl.BlockSpec((tm, w), lambda i: (i + tile0, 0))
    return pl.pallas_call(
        _final_kernel,
        grid=(yk.shape[1] // tm,),
        in_specs=[pl.BlockSpec((TOP_K, tm, d_model // 2), lambda i: (0, i, 0)), rows(LANES), rows(d_model),
                  pl.BlockSpec((1, d_model), lambda i: (0, 0))],
        out_specs=rows(d_model),
        out_shape=jax.ShapeDtypeStruct((n, d_model), F32),
        input_output_aliases={2: 0},
        compiler_params=_params(("arbitrary",)),
        name="final",
    )(yk, gates, x1, g_final)


def _tiles(seq):
    tm = min(512, seq)
    return dict(tm_proj=tm, t_attn=min(256, seq), tm_mix=tm, tm_expert=512, tm_final=tm, moe_groups=2)


def _forward(x, g_mix, w_in, conv_w, b_f, w_conv_o, w_attn_o, w_out, g_ffn, w_router, b_router,
             w_gate_up, b_gate_up, w_down, b_down, g_final, tiles):
    batch, seq, d_model = x.shape
    n = batch * seq
    conv_dim = conv_w.shape[1]
    attn_dim = w_attn_o.shape[0]
    heads = b_f.shape[0]
    head_dim = attn_dim // heads
    n_experts = w_router.shape[1]
    x2d = x.reshape(n, d_model)

    c0 = 3 * conv_dim
    a0 = c0 + 3 * attn_dim
    wa = w_in[:, :c0].astype(BF16)
    scale = LOG2E / (head_dim ** 0.5)
    wqk = jnp.concatenate([w_in[:, c0:c0 + attn_dim] * scale, w_in[:, c0 + attn_dim:c0 + 2 * attn_dim]],
                          axis=1).astype(BF16)
    wvt = jnp.pad(w_in[:, c0 + 2 * attn_dim:a0 + heads].T, ((0, FORGET_ROWS - heads), (0, 0))).astype(BF16)
    wg = w_in[:, a0 + heads:].astype(BF16)
    bf_col = jnp.pad(b_f, (0, FORGET_ROWS - heads)).reshape(FORGET_ROWS, 1)
    wr_hi = w_router.astype(BF16)
    wr_lo = (w_router - wr_hi.astype(F32)).astype(BF16)
    wr = jnp.concatenate([wr_hi, wr_lo], axis=1).T
    br = b_router.reshape(n_experts, 1)

    cb, ucx, qa, ka, vt, sgc, sga = _inproj(
        x2d, g_mix.reshape(1, d_model), wa, wqk, wvt, wg, bf_col,
        batch=batch, seq=seq, heads=heads, tm=tiles["tm_proj"], tk=tiles["t_attn"])
    o = _attention(qa, ka, vt, batch=batch, seq=seq, heads=heads, tq=tiles["t_attn"])
    groups = tiles["moe_groups"]
    ng = n // groups
    x1, h2p, topi, gates, rank, cnt = _mix(
        x2d, cb, ucx, o, sgc, sga, conv_w, w_conv_o.astype(BF16), w_attn_o.astype(BF16), w_out.astype(BF16),
        g_ffn.reshape(1, d_model), wr, br, seq=seq, n_experts=n_experts, tm=tiles["tm_mix"],
        tile0=0, nt=n // tiles["tm_mix"], tiles_per_group=ng // tiles["tm_mix"])

    tme = tiles["tm_expert"]
    n_blocks = (ng * TOP_K) // tme + n_experts
    p_rows = n_blocks * tme
    nchunks = ng // (SC_WORKERS * SC_CHUNK)
    jj = jnp.arange(tme, dtype=I32)[None, :]
    spare = p_rows + jnp.arange(n_experts, dtype=I32)[:, None] * tme + jj
    expert_ids = jnp.arange(n_experts, dtype=I32)[None, None, :]
    ys_groups, dest_groups = [], []
    for g in range(groups):
        counts = cnt[(g + 1) * (ng // tiles["tm_mix"]) - 1, :, 0].astype(I32)
        padded = ((counts + tme - 1) // tme) * tme
        pend = jnp.cumsum(padded)
        pstart = pend - padded
        n_used = (pend[-1] // tme).astype(I32).reshape(1)
        blk_start = jnp.minimum(jnp.arange(n_blocks, dtype=I32), n_used[0] - 1) * tme
        block_e = jnp.minimum(jnp.sum((pend[None, :] <= blk_start[:, None]).astype(I32), axis=1), n_experts - 1)
        tok = slice(g * ng, (g + 1) * ng)
        onehot = topi[:TOP_K, tok, None] == expert_ids
        dest = jnp.sum(jnp.where(onehot, pstart[None, None, :], 0), axis=2) + rank[:TOP_K, tok]
        dest4 = dest.reshape(TOP_K, SC_WORKERS, nchunks, SC_CHUNK).transpose(1, 2, 0, 3)
        pad_idx = jnp.where(jj < (padded - counts)[:, None], (pstart + counts)[:, None] + jj, spare)
        pad_idx = pad_idx.reshape(SC_WORKERS, (n_experts * tme) // (SC_WORKERS * SC_CHUNK), SC_CHUNK)
        xs = _sc_scatter_rows(h2p, dest4, pad_idx, p_rows + n_experts * tme, g * ng)
        eid = jnp.arange(n_experts, dtype=I32)
        active = padded > 0
        run_tbl = jnp.sum((eid[None, :] < eid[:, None]) & active[None, :], axis=1)
        later = (eid[None, :] > eid[:, None]) & active[None, :]
        next_tbl = jnp.min(jnp.where(later, eid[None, :], n_experts), axis=1)
        next_tbl = jnp.where(next_tbl < n_experts, next_tbl, -1)
        is_e = block_e[:, None] == eid[None, :]
        slot = lax.rem(jnp.sum(jnp.where(is_e, run_tbl[None, :], 0), axis=1), 2).astype(I32)
        next_e = jnp.sum(jnp.where(is_e, next_tbl[None, :], 0), axis=1).astype(I32)
        ys_groups.append(_experts(xs, block_e, n_used, slot, next_e, w_gate_up, b_gate_up, w_down, b_down,
                                  tm=tme, n_blocks=n_blocks))
        dest_groups.append(dest4)

    out = x1
    for g in range(groups):
        yk = _sc_gather_rows(ys_groups[g], dest_groups[g], ng)
        out = _final(yk, gates, out, g_final.reshape(1, d_model), tm=tiles["tm_final"],
                     tile0=g * (ng // tiles["tm_final"]))
    return out.reshape(batch, seq, d_model)


def kernel(x, g_mix, w_in, conv_w, b_f, w_conv_o, w_attn_o, w_out, g_ffn, w_router, b_router, w_gate_up,
           b_gate_up, w_down, b_down, g_final):
    return _forward(x, g_mix, w_in, conv_w, b_f, w_conv_o, w_attn_o, w_out, g_ffn, w_router, b_router,
                    w_gate_up, b_gate_up, w_down, b_down, g_final, _tiles(x.shape[1]))
```

```python
import functools

import jax
import jax.numpy as jnp
from jax import lax
from jax.experimental import pallas as pl
from jax.experimental.pallas import tpu as pltpu
from jax.experimental.pallas import tpu_sc as plsc

TOP_K = 4
RMS_EPS = 1e-5
SWIGLU_ALPHA = 1.702
SWIGLU_LIMIT = 7.0
LOG2E = 1.4426950408889634

LANES = 128
SUBLANES = 8
BF16_SUBLANES = 16
V_ONES_ROWS = BF16_SUBLANES
FORGET_ROWS = BF16_SUBLANES
VMEM_LIMIT_BYTES = 56 * 1024 * 1024

F32 = jnp.float32
BF16 = jnp.bfloat16
U32 = jnp.uint32
I32 = jnp.int32
HI_MASK = 0xFFFF0000


def _params(sem):
    return pltpu.CompilerParams(dimension_semantics=sem, vmem_limit_bytes=VMEM_LIMIT_BYTES)


def _as_words(x_bf16):
    return pltpu.bitcast(x_bf16, U32)


def _as_bf16(w_u32):
    return pltpu.bitcast(w_u32, BF16)


def _pack_bf16_pair(lo_f32, hi_f32):
    lo = lax.bitcast_convert_type(lo_f32, U32)
    hi = lax.bitcast_convert_type(hi_f32, U32)
    return (lo >> 16) | (hi & U32(HI_MASK))


def _unpack_bf16_pair(w):
    lo = lax.bitcast_convert_type(w << 16, F32)
    hi = lax.bitcast_convert_type(w & U32(HI_MASK), F32)
    return lo, hi


def _inproj_kernel(x_ref, g_ref, wa_ref, wqk_ref, wvt_ref, wg_ref, bf_ref,
                   cb_ref, ucx_ref, qa_ref, ka_ref, vt_ref, sgc_ref, sga_ref,
                   carry_ref, *, tiles_per_batch, conv_dim, attn_dim, d_model, heads, tk):
    tm = x_ref.shape[0]
    t = lax.rem(pl.program_id(0), tiles_per_batch)
    x = x_ref[...]
    ms = jnp.mean(x * x, axis=-1, keepdims=True)
    h = ((x * lax.rsqrt(ms + RMS_EPS)) * g_ref[...]).astype(BF16)

    def mm(w_ref, c0, n):
        return jnp.dot(h, w_ref[:, c0:c0 + n], preferred_element_type=F32)

    vtf =lax.dot_general(wvt_ref[...], h, (((1,), (1,)), ((), ())), preferred_element_type=F32)
    qf = mm(wqk_ref, 0, attn_dim)
    kf = mm(wqk_ref, attn_dim, attn_dim)
    cb = mm(wa_ref, 0, conv_dim)
    ucx = mm(wa_ref, conv_dim, conv_dim) * mm(wa_ref, 2 * conv_dim, conv_dim)

    z = vtf[attn_dim:attn_dim + FORGET_ROWS] + bf_ref[...]
    sub = lax.broadcasted_iota(I32, z.shape, 0)
    logf = jnp.where(sub < heads, jnp.minimum(z, 0.0) - jnp.log1p(jnp.exp(-jnp.abs(z))), 0.0)
    p1 = logf.astype(BF16).astype(F32)
    r1 = logf - p1
    p2 = r1.astype(BF16).astype(F32)
    p3 = (r1 - p2).astype(BF16).astype(F32)
    packed = jnp.concatenate([p1, p2, p3], axis=0).astype(BF16)
    row = lax.broadcasted_iota(I32, (tm, tm), 0)
    col = lax.broadcasted_iota(I32, (tm, tm), 1)
    tri = jnp.where(row <= col, 1.0, 0.0).astype(BF16)
    r = jnp.dot(packed, tri, preferred_element_type=F32)
    local = r[:FORGET_ROWS] + r[FORGET_ROWS:2 * FORGET_ROWS] + r[2 * FORGET_ROWS:]

    @pl.when(t == 0)
    def _():
        carry_ref[...] = jnp.zeros_like(carry_ref)

    c = local + carry_ref[:, 0:1]
    carry_ref[...] = jnp.broadcast_to(c[:, tm - 1:tm], carry_ref.shape)

    c2 = c * LOG2E
    c_hi_t = c2.astype(BF16).astype(F32)
    c_r = c2 - c_hi_t
    c_mid_t = c_r.astype(BF16).astype(F32)
    c_lo_t = (c_r - c_mid_t).astype(BF16).astype(F32)
    split = (c_hi_t, c_mid_t, c_lo_t)
    pad = jnp.zeros((LANES - len(split) * FORGET_ROWS, tm), F32)
    parts = jnp.concatenate(split + (pad,), axis=0).T
    head_dim = attn_dim // heads
    ext = LANES - head_dim
    lane_e = lax.broadcasted_iota(I32, (tm, ext), 1)
    ones = jnp.where(lane_e < len(split), 1.0, 0.0)
    for hh in range(heads):
        kext = jnp.zeros((tm, ext), F32)
        for p in range(len(split)):
            col_p = parts[:, p * FORGET_ROWS + hh:p * FORGET_ROWS + hh + 1]
            kext = jnp.where(lane_e == p, -col_p, kext)
        sl = slice(hh * head_dim, (hh + 1) * head_dim)
        qa_ref[:, hh * LANES:(hh + 1) * LANES] = _as_words(jnp.concatenate([qf[:, sl], ones], axis=1).astype(BF16))
        ka_ref[:, hh * LANES:(hh + 1) * LANES] = _as_words(jnp.concatenate([kf[:, sl], kext], axis=1).astype(BF16))

    cb_ref[...] = _as_words(cb.astype(BF16))
    ucx_ref[...] = _as_words(ucx.astype(BF16))
    half = d_model // 2
    for c in range(2):
        sgc_ref[:, c * half:(c + 1) * half] = _as_words(jax.nn.sigmoid(mm(wg_ref, c * half, half)).astype(BF16))
        sga_ref[:, c * half:(c + 1) * half] = _as_words(
            jax.nn.sigmoid(mm(wg_ref, d_model + c * half, half)).astype(BF16))

    vt = vtf[:attn_dim].astype(BF16)
    ones_rows = jnp.ones((V_ONES_ROWS, tm), BF16)
    vt = jnp.concatenate([piece for hh in range(heads)
                          for piece in (vt[hh * head_dim:(hh + 1) * head_dim], ones_rows)], axis=0)
    for u in range(tm // tk):
        vt_ref[u] = _as_words(vt[:, u * tk:(u + 1) * tk])


def _inproj(x2d, g_mix, wa, wqk, wvt, wg, bf_col, *, batch, seq, heads, tm, tk):
    n, d_model = x2d.shape
    conv_dim = wa.shape[1] // 3
    attn_dim = wvt.shape[0] - FORGET_ROWS
    assert heads <= FORGET_ROWS
    tiles_per_batch = seq // tm
    grid = (n // tm,)
    full = lambda shape: pl.BlockSpec(shape, lambda i: (0,) * len(shape))
    rows = lambda w: pl.BlockSpec((tm, w), lambda i: (i, 0))
    kern = functools.partial(_inproj_kernel, tiles_per_batch=tiles_per_batch, conv_dim=conv_dim,
                             attn_dim=attn_dim, d_model=d_model, heads=heads, tk=tk)
    vt_rows = attn_dim + heads * V_ONES_ROWS
    words = lambda w: jax.ShapeDtypeStruct((n // 2, w), U32)
    wrows = lambda w: pl.BlockSpec((tm // 2, w), lambda i: (i, 0))
    out_shape = (
        words(conv_dim),
        words(conv_dim),
        words(heads * LANES),
        words(heads * LANES),
        jax.ShapeDtypeStruct((n // tk, vt_rows // 2, tk), U32),
        words(d_model),
        words(d_model),
    )
    out_specs = (
        wrows(conv_dim), wrows(conv_dim), wrows(heads * LANES), wrows(heads * LANES),
        pl.BlockSpec((tm // tk, vt_rows // 2, tk), lambda i: (i, 0, 0)),
        wrows(d_model), wrows(d_model),
    )
    return pl.pallas_call(
        kern,
        grid=grid,
        in_specs=[rows(d_model), full((1, d_model)), full(wa.shape), full(wqk.shape), full(wvt.shape),
                  full(wg.shape), full((FORGET_ROWS, 1))],
        out_specs=out_specs,
        out_shape=out_shape,
        scratch_shapes=[pltpu.VMEM((FORGET_ROWS, LANES), F32)],
        compiler_params=_params(("arbitrary",)),
        name="inproj",
    )(x2d, g_mix, wa, wqk, wvt, wg, bf_col)


def _attn_kernel(q_ref, k_ref, v_ref, o_ref, s_ref, *, tq, tk, head_dim, heads):
    qi = pl.program_id(1)
    row = lax.broadcasted_iota(I32, (tk, tq), 0)
    col = lax.broadcasted_iota(I32, (tk, tq), 1)

    def score_tile(j, slot):
        off = pl.multiple_of(j * (tk // 2), tk // 2)
        for hh in range(heads):
            qa = _as_bf16(q_ref[:, hh * LANES:(hh + 1) * LANES])
            ka = _as_bf16(k_ref[pl.ds(off, tk // 2), hh * LANES:(hh + 1) * LANES])
            s_ref[slot, hh] = lax.dot_general(ka, qa, (((1,), (1,)), ((), ())), preferred_element_type=F32)

    def block(j, slot, carry, masked):
        if not masked:
            score_tile(j + 1, 1 - slot)
        stats = []
        for hh in range(heads):
            m, _ = carry[hh]
            s = s_ref[slot, hh]
            if masked:
                s = jnp.where(row <= col, s, -jnp.inf)
            m_new = jnp.maximum(m, jnp.max(s, axis=0, keepdims=True))
            stats.append((m_new, jnp.exp2(m - m_new), jnp.exp2(s - m_new).astype(BF16)))
        out = []
        for hh in range(heads):
            m_new, alpha, p = stats[hh]
            vth = _as_bf16(v_ref[j, hh * (vrows // 2):(hh + 1) * (vrows // 2), :])
            out.append((m_new, alpha * carry[hh][1] + jnp.dot(vth, p, preferred_element_type=F32)))
        return tuple(out)

    vrows = head_dim + V_ONES_ROWS
    init = tuple((jnp.full((1, tq), -jnp.inf, F32), jnp.zeros((vrows, tq), F32)) for _ in range(heads))
    score_tile(0, 0)

    def pair(p, c):
        return block(2 * p + 1, 1, block(2 * p, 0, c, False), False)

    carry = lax.fori_loop(0, qi // 2, pair, init)
    final = lax.cond(
        lax.rem(qi, 2) == 0,
        lambda c: block(qi, 0, c, True),
        lambda c: block(qi, 1, block(qi - 1, 0, c, False), True),
        carry)
    per_group = LANES // head_dim
    for g in range(heads // per_group):
        accs = [final[g * per_group + u][1] for u in range(per_group)]
        ot = jnp.concatenate([a[:head_dim] / a[head_dim:head_dim + 1] for a in accs],
                             axis=0)
        o_ref[:, g * LANES:(g + 1) * LANES] = _as_words(ot.T.astype(BF16))


def _attention(qa, ka, vt, *, batch, seq, heads, tq):
    n = 2 * qa.shape[0]
    vt_rows, tk = 2 * vt.shape[1], vt.shape[2]
    head_dim = vt_rows // heads - V_ONES_ROWS
    attn_dim = heads * head_dim
    assert LANES % head_dim == 0 and tq == tk
    nq = seq // tq
    nk = seq // tk
    kern = functools.partial(_attn_kernel, tq=tq, tk=tk, head_dim=head_dim, heads=heads)
    return pl.pallas_call(
        kern,
        grid=(batch, nq),
        in_specs=[
            pl.BlockSpec((tq // 2, heads * LANES), lambda b, qi: (b * nq + qi, 0)),
            pl.BlockSpec((seq // 2, heads * LANES), lambda b, qi: (b, 0)),
            pl.BlockSpec((nk, vt_rows // 2, tk), lambda b, qi: (b, 0, 0)),
        ],
        out_specs=pl.BlockSpec((tq // 2, attn_dim), lambda b, qi: (b * nq + qi, 0)),
        out_shape=jax.ShapeDtypeStruct((n // 2, attn_dim), U32),
        scratch_shapes=[pltpu.VMEM((2, heads, tk, tq), F32)],
        compiler_params=_params(("arbitrary", "arbitrary")),
        name="attn",
    )(qa, ka, vt)


def _mix_kernel(x_ref, cb_ref, u_ref, uh_ref, o_ref, sgc_ref, sga_ref, cw_ref, wco_ref, wao_ref, wout_ref,
                gffn_ref, wr_ref, br_ref,
                x1_ref, h2p_ref, topi_ref, gate_ref, rank_ref, cnt_ref, carry_ref, lg_scr,
                *, tiles_per_batch, tiles_per_group, n_experts, nt):
    tm, d_model = x_ref.shape
    i = pl.program_id(0)

    @pl.when(i == 0)
    def _():
        lg_scr[...] = jnp.zeros_like(lg_scr)

    @pl.when((i == 0) | (lax.rem(i - 1, tiles_per_group) == 0))
    def _():
        carry_ref[...] = jnp.zeros_like(carry_ref)

    lg = lg_scr[...]
    sub = lax.broadcasted_iota(I32, (n_experts, tm), 0)

    vals, idxs = [], []
    for _ in range(TOP_K):
        m = jnp.max(lg, axis=0, keepdims=True)
        idx = jnp.min(jnp.where(lg == m, sub, n_experts), axis=0, keepdims=True)
        vals.append(m)
        idxs.append(idx)
        lg = jnp.where(sub == idx, -jnp.inf, lg)
    es = [jnp.exp(vk - vals[0]) for vk in vals]
    denom = es[0] + es[1] + es[2] + es[3]

    sub8 = lax.broadcasted_iota(I32, (SUBLANES, tm), 0)
    subl = lax.broadcasted_iota(I32, (LANES, tm), 0)
    chosen = jnp.zeros((n_experts, tm), F32)
    topi = jnp.zeros((SUBLANES, tm), I32)
    gates_t = jnp.zeros((LANES, tm), F32)
    for kk in range(TOP_K):
        chosen = jnp.where(sub == idxs[kk], 1.0, chosen)
        topi = jnp.where(sub8 == kk, idxs[kk], topi)
        gates_t = jnp.where(subl == kk, es[kk] / denom, gates_t)
    topi_ref[...] = topi
    gate_ref[...] = gates_t.T

    first = lax.rem(jnp.minimum(i, nt - 1), tiles_per_batch) == 0

    u = _as_bf16(u_ref[...]).astype(F32)
    halo = jnp.where(first, 0.0, _as_bf16(uh_ref[...]).astype(F32))
    ext = jnp.concatenate([halo, u], axis=0)
    hs = halo.shape[0]
    u1 = pltpu.roll(ext, 1, 0)[hs:]
    u2 = pltpu.roll(ext, 2, 0)[hs:]
    cw = cw_ref[...]
    conv = cw[0:1, :] * u2 + cw[1:2, :] * u1 + cw[2:3, :] * u
    yc = (_as_bf16(cb_ref[...]).astype(F32) * conv).astype(BF16)
    y_conv = jnp.dot(yc, wco_ref[...], preferred_element_type=F32)
    y_attn = jnp.dot(_as_bf16(o_ref[...]), wao_ref[...], preferred_element_type=F32)
    mixed = (_as_bf16(sgc_ref[...]).astype(F32) * y_conv
             + _as_bf16(sga_ref[...]).astype(F32) * y_attn).astype(BF16)
    x1 = x_ref[...] + jnp.dot(mixed, wout_ref[...], preferred_element_type=F32)
    x1_ref[...] = x1

    ms = jnp.mean(x1 * x1, axis=-1, keepdims=True)
    h2 = (x1 * lax.rsqrt(ms + RMS_EPS)) * gffn_ref[...]
    hb = h2.astype(BF16)
    hbf = hb.astype(F32)
    half = d_model // 2
    h2p_ref[...] = _pack_bf16_pair(hbf[:, :half], hbf[:, half:])

    hlo = (h2 - hbf).astype(BF16)
    nt_dims = (((1,), (1,)), ((), ()))
    rt = (lax.dot_general(wr_ref[...], hb, nt_dims, preferred_element_type=F32)
          + lax.dot_general(wr_ref[...], hlo, nt_dims, preferred_element_type=F32))
    lg_scr[...] = rt[:n_experts] + rt[n_experts:] + br_ref[...]

    row = lax.broadcasted_iota(I32, (tm, tm), 0)
    col = lax.broadcasted_iota(I32, (tm, tm), 1)
    tri = jnp.where(row < col, 1.0, 0.0).astype(BF16)
    before = jnp.dot(chosen.astype(BF16), tri, preferred_element_type=F32) + carry_ref[:, 0:1]
    rank = jnp.zeros((SUBLANES, tm), F32)
    for kk in range(TOP_K):
        rk = jnp.sum(jnp.where(sub == idxs[kk], before, 0.0), axis=0, keepdims=True)
        rank = jnp.where(sub8 == kk, rk, rank)
    rank_ref[...] = rank.astype(I32)
    total = carry_ref[...] + jnp.sum(chosen, axis=1, keepdims=True)
    carry_ref[...] = total
    cnt_ref[0] = total


def _mix(x2d, cb, ucx, o, sgc, sga, conv_w, wco, wao, wout, g_ffn, wr, br, *, seq, n_experts, tm, tile0, nt,
         tiles_per_group):
    d_model = x2d.shape[1]
    n = nt * tm
    conv_dim = cb.shape[1]
    attn_dim = o.shape[1]
    tiles_per_batch = seq // tm
    assert tile0 % tiles_per_batch == 0
    hb = tm // BF16_SUBLANES
    full = lambda shape: pl.BlockSpec(shape, lambda i: (0,) * len(shape))
    cur = lambda i: jnp.minimum(i, nt - 1)
    prev = lambda i: jnp.maximum(i - 1, 0)
    rows_in = lambda w: pl.BlockSpec((tm, w), lambda i: (cur(i) + tile0, 0))
    words_in = lambda w: pl.BlockSpec((tm // 2, w), lambda i: (cur(i) + tile0, 0))
    rows = lambda w: pl.BlockSpec((tm, w), lambda i: (cur(i), 0))
    kern = functools.partial(_mix_kernel, tiles_per_batch=tiles_per_batch, tiles_per_group=tiles_per_group,
                             n_experts=n_experts, nt=nt)
    out_shape = (
        jax.ShapeDtypeStruct((n, d_model), F32),
        jax.ShapeDtypeStruct((n, d_model // 2), U32),
        jax.ShapeDtypeStruct((SUBLANES, n), I32),
        jax.ShapeDtypeStruct((n, LANES), F32),
        jax.ShapeDtypeStruct((SUBLANES, n), I32),
        jax.ShapeDtypeStruct((nt, n_experts, LANES), F32),
    )
    cols = pl.BlockSpec((SUBLANES, tm), lambda i: (0, prev(i)))
    out_specs = (rows(d_model), rows(d_model // 2), cols, pl.BlockSpec((tm, LANES), lambda i: (prev(i), 0)), cols,
                 pl.BlockSpec((1, n_experts, LANES), lambda i: (prev(i), 0, 0)))
    return pl.pallas_call(
        kern,
        grid=(nt + 1,),
        in_specs=[rows_in(d_model), words_in(conv_dim), words_in(conv_dim),
                  pl.BlockSpec((BF16_SUBLANES // 2, conv_dim),
                               lambda i: (jnp.maximum((cur(i) + tile0) * hb - 1, 0), 0)),
                  words_in(attn_dim), words_in(d_model), words_in(d_model),
                  full(conv_w.shape), full(wco.shape), full(wao.shape), full(wout.shape),
                  full((1, d_model)), full(wr.shape), full((n_experts, 1))],
        out_specs=out_specs,
        out_shape=out_shape,
        scratch_shapes=[pltpu.VMEM((n_experts, LANES), F32), pltpu.VMEM((n_experts, tm), F32)],
        compiler_params=_params(("arbitrary",)),
        name="mix",
    )(x2d, cb, ucx, ucx, o, sgc, sga, conv_w, wco, wao, wout, g_ffn, wr, br)


SC_CORES = 2
SC_SUBCORES = 16
SC_WORKERS = SC_CORES * SC_SUBCORES
SC_CHUNK = 64


def _sc_mesh():
    return plsc.VectorSubcoreMesh(core_axis_name="c", subcore_axis_name="s",
                                  num_cores=SC_CORES, num_subcores=SC_SUBCORES)


def _sc_worker():
    return lax.axis_index("s") * SC_CORES + lax.axis_index("c")


def _sc_scatter_rows(src, dest4, pad_idx, n_out, row0):
    width = src.shape[1]
    nchunks = dest4.shape[1]
    npad = pad_idx.shape[1]

    @functools.partial(
        pl.kernel, mesh=_sc_mesh(),
        out_type=jax.ShapeDtypeStruct((n_out, width), src.dtype),
        scratch_types=[pltpu.VMEM((TOP_K, SC_CHUNK), I32), pltpu.VMEM((SC_CHUNK, width), src.dtype),
                       pltpu.VMEM((npad, SC_CHUNK), I32), pltpu.SemaphoreType.DMA],
        name="sc_scatter_rows",
    )
    def k(src_hbm, dest_hbm, pad_hbm, out_hbm, idx_v, rows_v, pad_v, sem):
        wid = _sc_worker()
        base = row0 + wid * (nchunks * SC_CHUNK)

        @pl.loop(0, nchunks)
        def _(j):
            pltpu.sync_copy(src_hbm.at[pl.ds(base + j * SC_CHUNK, SC_CHUNK)], rows_v)
            pltpu.sync_copy(dest_hbm.at[wid, j], idx_v)
            copies = [pltpu.async_copy(rows_v, out_hbm.at[idx_v.at[kk]], sem) for kk in range(TOP_K)]
            for cp in copies:
                cp.wait()

        pltpu.sync_copy(pad_hbm.at[wid], pad_v)
        fills = [pltpu.async_copy(rows_v, out_hbm.at[pad_v.at[p]], sem) for p in range(npad)]
        for cp in fills:
            cp.wait()

    return k(src, dest4, pad_idx)


def _sc_gather_rows(src, dest4, n):
    width = src.shape[1]
    nchunks = dest4.shape[1]

    @functools.partial(
        pl.kernel, mesh=_sc_mesh(),
        out_type=jax.ShapeDtypeStruct((TOP_K, n, width), src.dtype),
        scratch_types=[pltpu.VMEM((TOP_K, SC_CHUNK), I32), pltpu.VMEM((SC_CHUNK, width), src.dtype),
                       pltpu.VMEM((SC_CHUNK, width), src.dtype), pltpu.SemaphoreType.DMA, pltpu.SemaphoreType.DMA],
        name="sc_gather_rows",
    )
    def k(src_hbm, dest_hbm, out_hbm, idx_v, rows_a, rows_b, sem_a, sem_b):
        wid = _sc_worker()
        base = wid * (nchunks * SC_CHUNK)
        bufs = ((rows_a, sem_a), (rows_b, sem_b))

        @pl.loop(0, nchunks)
        def _(j):
            pltpu.sync_copy(dest_hbm.at[wid, j], idx_v)
            pending = pltpu.async_copy(src_hbm.at[idx_v.at[0]], rows_a, sem_a)
            for kk in range(TOP_K):
                buf, _ = bufs[kk % 2]
                pending.wait()
                if kk + 1 < TOP_K:
                    nbuf, nsem = bufs[(kk + 1) % 2]
                    pending = pltpu.async_copy(src_hbm.at[idx_v.at[kk + 1]], nbuf, nsem)
                pltpu.sync_copy(buf, out_hbm.at[kk, pl.ds(base + j * SC_CHUNK, SC_CHUNK)])

    return k(src, dest4)


def _expert_kernel(be_ref, nu_ref, slot_ref, nxt_ref, xs_ref, wgu_hbm, bgu_ref, wd_hbm, bd_ref, ys_ref,
                   wgu_f32, wd_f32, wgu_bf, wd_bf, sem, *, d_ff, ff_chunk):
    i = pl.program_id(0)

    def weight_copies(e, s):
        return (pltpu.make_async_copy(wgu_hbm.at[e], wgu_f32.at[s], sem.at[s]),
                pltpu.make_async_copy(wd_hbm.at[e], wd_f32.at[s], sem.at[s]))

    @pl.when(i == 0)
    def _():
        for cp in weight_copies(be_ref[0], 0):
            cp.start()

    @pl.when((i < nu_ref[0]) & ((i == 0) | (be_ref[i] != be_ref[jnp.maximum(i - 1, 0)])))
    def _():
        s = slot_ref[i]
        for cp in weight_copies(be_ref[i], s):
            cp.wait()
        wgu_bf[...] = wgu_f32[s].astype(BF16)
        wd_bf[...] = wd_f32[s].astype(BF16)
        nxt = nxt_ref[i]

        @pl.when(nxt >= 0)
        def _():
            for cp in weight_copies(nxt, 1 - s):
                cp.start()

    @pl.when(i < nu_ref[0])
    def _():
        lo, hi = _unpack_bf16_pair(xs_ref[...])
        xb = jnp.concatenate([lo.astype(BF16), hi.astype(BF16)], axis=1)
        chunks = range(0, d_ff, ff_chunk)
        gus = []
        for c0 in chunks:
            glu = jnp.dot(xb, wgu_bf[:, c0:c0 + ff_chunk], preferred_element_type=F32) + bgu_ref[0, :, c0:c0 + ff_chunk]
            lin = (jnp.dot(xb, wgu_bf[:, d_ff + c0:d_ff + c0 + ff_chunk], preferred_element_type=F32)
                   + bgu_ref[0, :, d_ff + c0:d_ff + c0 + ff_chunk])
            gus.append((glu, lin))
        y = bd_ref[0]
        for c0, (glu, lin) in zip(chunks, gus):
            glu = jnp.minimum(glu, SWIGLU_LIMIT)
            lin = jnp.clip(lin, -SWIGLU_LIMIT, SWIGLU_LIMIT)
            act = (glu * jax.nn.sigmoid(SWIGLU_ALPHA * glu)) * (lin + 1.0)
            y = y + jnp.dot(act.astype(BF16), wd_bf[c0:c0 + ff_chunk, :], preferred_element_type=F32)
        yb = y.astype(BF16).astype(F32)
        half = y.shape[1] // 2
        ys_ref[...] = _pack_bf16_pair(yb[:, :half], yb[:, half:])


def _experts(xs, block_e, n_used, slot, next_e, wgu, bgu, wd, bd, *, tm, n_blocks):
    half = xs.shape[1]
    p = n_blocks * tm
    e, d_model, two_ff = wgu.shape
    d_ff = two_ff // 2
    kern = functools.partial(_expert_kernel, d_ff=d_ff, ff_chunk=min(256, d_ff))
    blk = lambda i, be, nu, sl, nx: (jnp.minimum(i, nu[0] - 1), 0)
    bias = lambda i, be, nu, sl, nx: (be[i], 0, 0)
    grid_spec = pltpu.PrefetchScalarGridSpec(
        num_scalar_prefetch=4,
        grid=(n_blocks,),
        in_specs=[
            pl.BlockSpec((tm, half), blk),
            pl.BlockSpec(memory_space=pl.ANY),
            pl.BlockSpec((1, 1, two_ff), bias),
            pl.BlockSpec(memory_space=pl.ANY),
            pl.BlockSpec((1, 1, d_model), bias),
        ],
        out_specs=pl.BlockSpec((tm, half), blk),
        scratch_shapes=[pltpu.VMEM((2, d_model, two_ff), F32), pltpu.VMEM((2, d_ff, d_model), F32),
                        pltpu.VMEM((d_model, two_ff), BF16), pltpu.VMEM((d_ff, d_model), BF16),
                        pltpu.SemaphoreType.DMA((2,))],
    )
    return pl.pallas_call(
        kern,
        grid_spec=grid_spec,
        out_shape=jax.ShapeDtypeStruct((p, half), U32),
        compiler_params=_params(("arbitrary",)),
        name="experts",
    )(block_e, n_used, slot, next_e, xs, wgu, bgu.reshape(e, 1, two_ff), wd, bd.reshape(e, 1, d_model))


def _final_kernel(yk_ref, gate_ref, x1_ref, g_ref, out_ref):
    tm, d_model = x1_ref.shape
    half = d_model // 2
    gates = gate_ref[...]
    acc_lo = jnp.zeros((tm, half), F32)
    acc_hi = jnp.zeros((tm, half), F32)
    for kk in range(TOP_K):
        lo, hi = _unpack_bf16_pair(yk_ref[kk])
        gk = gates[:, kk:kk + 1]
        acc_lo = acc_lo + gk * lo
        acc_hi = acc_hi + gk * hi
    x2 = x1_ref[...] + jnp.concatenate([acc_lo, acc_hi], axis=1)
    ms = jnp.mean(x2 * x2, axis=-1, keepdims=True)
    out_ref[...] = (x2 * lax.rsqrt(ms + RMS_EPS)) * g_ref[...]


def _final(yk, gates, x1, g_final, *, tm, tile0):
    n, d_model = x1.shape
    rows = lambda w: pl.BlockSpec((tm, w), lambda i: (i + tile0, 0))
    return pl.pallas_call(
        _final_kernel,
        grid=(yk.shape[1] // tm,),
        in_specs=[pl.BlockSpec((TOP_K, tm, d_model // 2), lambda i: (0, i, 0)), rows(LANES), rows(d_model),
                  pl.BlockSpec((1, d_model), lambda i: (0, 0))],
        out_specs=rows(d_model),
        out_shape=jax.ShapeDtypeStruct((n, d_model), F32),
        input_output_aliases={2: 0},
        compiler_params=_params(("arbitrary",)),
        name="final",
    )(yk, gates, x1, g_final)


def _tiles(seq):
    tm = min(512, seq)
    return dict(tm_proj=tm, t_attn=min(256, seq), tm_mix=tm, tm_expert=512, tm_final=tm, moe_groups=2)


def _forward(x, g_mix, w_in, conv_w, b_f, w_conv_o, w_attn_o, w_out, g_ffn, w_router, b_router,
             w_gate_up, b_gate_up, w_down, b_down, g_final, tiles):
    batch, seq, d_model = x.shape
    n = batch * seq
    conv_dim = conv_w.shape[1]
    attn_dim = w_attn_o.shape[0]
    heads = b_f.shape[0]
    head_dim = attn_dim // heads
    n_experts = w_router.shape[1]
    x2d = x.reshape(n, d_model)

    c0 = 3 * conv_dim
    a0 = c0 + 3 * attn_dim
    wa = w_in[:, :c0].astype(BF16)
    scale = LOG2E / (head_dim ** 0.5)
    wqk = jnp.concatenate([w_in[:, c0:c0 + attn_dim] * scale, w_in[:, c0 + attn_dim:c0 + 2 * attn_dim]],
                          axis=1).astype(BF16)
    wvt = jnp.pad(w_in[:, c0 + 2 * attn_dim:a0 + heads].T, ((0, FORGET_ROWS - heads), (0, 0))).astype(BF16)
    wg = w_in[:, a0 + heads:].astype(BF16)
    bf_col = jnp.pad(b_f, (0, FORGET_ROWS - heads)).reshape(FORGET_ROWS, 1)
    wr_hi = w_router.astype(BF16)
    wr_lo = (w_router - wr_hi.astype(F32)).astype(BF16)
    wr = jnp.concatenate([wr_hi, wr_lo], axis=1).T
    br = b_router.reshape(n_experts, 1)

    cb, ucx, qa, ka, vt, sgc, sga = _inproj(
        x2d, g_mix.reshape(1, d_model), wa, wqk, wvt, wg, bf_col,
        batch=batch, seq=seq, heads=heads, tm=tiles["tm_proj"], tk=tiles["t_attn"])
    o = _attention(qa, ka, vt, batch=batch, seq=seq, heads=heads, tq=tiles["t_attn"])
    groups = tiles["moe_groups"]
    ng = n // groups
    x1, h2p, topi, gates, rank, cnt = _mix(
        x2d, cb, ucx, o, sgc, sga, conv_w, w_conv_o.astype(BF16), w_attn_o.astype(BF16), w_out.astype(BF16),
        g_ffn.reshape(1, d_model), wr, br, seq=seq, n_experts=n_experts, tm=tiles["tm_mix"],
        tile0=0, nt=n // tiles["tm_mix"], tiles_per_group=ng // tiles["tm_mix"])

    tme = tiles["tm_expert"]
    n_blocks = (ng * TOP_K) // tme + n_experts
    p_rows = n_blocks * tme
    nchunks = ng // (SC_WORKERS * SC_CHUNK)
    jj = jnp.arange(tme, dtype=I32)[None, :]
    spare = p_rows + jnp.arange(n_experts, dtype=I32)[:, None] * tme + jj
    expert_ids = jnp.arange(n_experts, dtype=I32)[None, None, :]
    ys_groups, dest_groups = [], []
    for g in range(groups):
        counts = cnt[(g + 1) * (ng // tiles["tm_mix"]) - 1, :, 0].astype(I32)
        padded = ((counts + tme - 1) // tme) * tme
        pend = jnp.cumsum(padded)
        pstart = pend - padded
        n_used = (pend[-1] // tme).astype(I32).reshape(1)
        blk_start = jnp.minimum(jnp.arange(n_blocks, dtype=I32), n_used[0] - 1) * tme
        block_e = jnp.minimum(jnp.sum((pend[None, :] <= blk_start[:, None]).astype(I32), axis=1), n_experts - 1)
        tok = slice(g * ng, (g + 1) * ng)
        onehot = topi[:TOP_K, tok, None] == expert_ids
        dest = jnp.sum(jnp.where(onehot, pstart[None, None, :], 0), axis=2) + rank[:TOP_K, tok]
        dest4 = dest.reshape(TOP_K, SC_WORKERS, nchunks, SC_CHUNK).transpose(1, 2, 0, 3)
        pad_idx = jnp.where(jj < (padded - counts)[:, None], (pstart + counts)[:, None] + jj, spare)
        pad_idx = pad_idx.reshape(SC_WORKERS, (n_experts * tme) // (SC_WORKERS * SC_CHUNK), SC_CHUNK)
        xs = _sc_scatter_rows(h2p, dest4, pad_idx, p_rows + n_experts * tme, g * ng)
        eid = jnp.arange(n_experts, dtype=I32)
        active = padded > 0
        run_tbl = jnp.sum((eid[None, :] < eid[:, None]) & active[None, :], axis=1)
        later = (eid[None, :] > eid[:, None]) & active[None, :]
        next_tbl = jnp.min(jnp.where(later, eid[None, :], n_experts), axis=1)
        next_tbl = jnp.where(next_tbl < n_experts, next_tbl, -1)
        is_e = block_e[:, None] == eid[None, :]
        slot = lax.rem(jnp.sum(jnp.where(is_e, run_tbl[None, :], 0), axis=1), 2).astype(I32)
        next_e = jnp.sum(jnp.where(is_e, next_tbl[None, :], 0), axis=1).astype(I32)
        ys_groups.append(_experts(xs, block_e, n_used, slot, next_e, w_gate_up, b_gate_up, w_down, b_down,
                                  tm=tme, n_blocks=n_blocks))
        dest_groups.append(dest4)

    out = x1
    for g in range(groups):
        yk = _sc_gather_rows(ys_groups[g], dest_groups[g], ng)
        out = _final(yk, gates, out, g_final.reshape(1, d_model), tm=tiles["tm_final"],
                     tile0=g * (ng // tiles["tm_final"]))
    return out.reshape(batch, seq, d_model)


def kernel(x, g_mix, w_in, conv_w, b_f, w_conv_o, w_attn_o, w_out, g_ffn, w_router, b_router, w_gate_up,
           b_gate_up, w_down, b_down, g_final):
    return _forward(x, g_mix, w_in, conv_w, b_f, w_conv_o, w_attn_o, w_out, g_ffn, w_router, b_router,
                    w_gate_up, b_gate_up, w_down, b_down, g_final, _tiles(x.shape[1]))
```

```python
import functools

import jax
import jax.numpy as jnp
from jax import lax
from jax.experimental import pallas as pl
from jax.experimental.pallas import tpu as pltpu
from jax.experimental.pallas import tpu_sc as plsc

TOP_K = 4
RMS_EPS = 1e-5
SWIGLU_ALPHA = 1.702
SWIGLU_LIMIT = 7.0
LOG2E = 1.4426950408889634

LANES = 128
SUBLANES = 8
BF16_SUBLANES = 16
V_ONES_ROWS = BF16_SUBLANES
FORGET_ROWS = BF16_SUBLANES
VMEM_LIMIT_BYTES = 56 * 1024 * 1024

F32 = jnp.float32
BF16 = jnp.bfloat16
U32 = jnp.uint32
I32 = jnp.int32
HI_MASK = 0xFFFF0000


def _params(sem):
    return pltpu.CompilerParams(dimension_semantics=sem, vmem_limit_bytes=VMEM_LIMIT_BYTES)


def _as_words(x_bf16):
    return pltpu.bitcast(x_bf16, U32)


def _as_bf16(w_u32):
    return pltpu.bitcast(w_u32, BF16)


def _pack_bf16_pair(lo_f32, hi_f32):
    lo = lax.bitcast_convert_type(lo_f32, U32)
    hi = lax.bitcast_convert_type(hi_f32, U32)
    return (lo >> 16) | (hi & U32(HI_MASK))


def _unpack_bf16_pair(w):
    lo = lax.bitcast_convert_type(w << 16, F32)
    hi = lax.bitcast_convert_type(w & U32(HI_MASK), F32)
    return lo, hi


def _inproj_kernel(x_ref, g_ref, wa_ref, wqk_ref, wvt_ref, wg_ref, bf_ref,
                   cb_ref, ucx_ref, qa_ref, ka_ref, vt_ref, sgc_ref, sga_ref,
                   carry_ref, *, tiles_per_batch, conv_dim, attn_dim, d_model, heads, tk):
    tm = x_ref.shape[0]
    t = lax.rem(pl.program_id(0), tiles_per_batch)
    x = x_ref[...]
    ms = jnp.mean(x * x, axis=-1, keepdims=True)
    h = ((x * lax.rsqrt(ms + RMS_EPS)) * g_ref[...]).astype(BF16)

    def mm(w_ref, c0, n):
        return jnp.dot(h, w_ref[:, c0:c0 + n], preferred_element_type=F32)

    vtf =lax.dot_general(wvt_ref[...], h, (((1,), (1,)), ((), ())), preferred_element_type=F32)
    qf = mm(wqk_ref, 0, attn_dim)
    kf = mm(wqk_ref, attn_dim, attn_dim)
    cb = mm(wa_ref, 0, conv_dim)
    ucx = mm(wa_ref, conv_dim, conv_dim) * mm(wa_ref, 2 * conv_dim, conv_dim)

    z = vtf[attn_dim:attn_dim + FORGET_ROWS] + bf_ref[...]
    sub = lax.broadcasted_iota(I32, z.shape, 0)
    logf = jnp.where(sub < heads, jnp.minimum(z, 0.0) - jnp.log1p(jnp.exp(-jnp.abs(z))), 0.0)
    p1 = logf.astype(BF16).astype(F32)
    r1 = logf - p1
    p2 = r1.astype(BF16).astype(F32)
    p3 = (r1 - p2).astype(BF16).astype(F32)
    packed = jnp.concatenate([p1, p2, p3], axis=0).astype(BF16)
    row = lax.broadcasted_iota(I32, (tm, tm), 0)
    col = lax.broadcasted_iota(I32, (tm, tm), 1)
    tri = jnp.where(row <= col, 1.0, 0.0).astype(BF16)
    r = jnp.dot(packed, tri, preferred_element_type=F32)
    local = r[:FORGET_ROWS] + r[FORGET_ROWS:2 * FORGET_ROWS] + r[2 * FORGET_ROWS:]

    @pl.when(t == 0)
    def _():
        carry_ref[...] = jnp.zeros_like(carry_ref)

    c = local + carry_ref[:, 0:1]
    carry_ref[...] = jnp.broadcast_to(c[:, tm - 1:tm], carry_ref.shape)

    c2 = c * LOG2E
    c_hi_t = c2.astype(BF16).astype(F32)
    c_r = c2 - c_hi_t
    c_mid_t = c_r.astype(BF16).astype(F32)
    c_lo_t = (c_r - c_mid_t).astype(BF16).astype(F32)
    split = (c_hi_t, c_mid_t, c_lo_t)
    pad = jnp.zeros((LANES - len(split) * FORGET_ROWS, tm), F32)
    parts = jnp.concatenate(split + (pad,), axis=0).T
    head_dim = attn_dim // heads
    ext = LANES - head_dim
    lane_e = lax.broadcasted_iota(I32, (tm, ext), 1)
    ones = jnp.where(lane_e < len(split), 1.0, 0.0)
    for hh in range(heads):
        kext = jnp.zeros((tm, ext), F32)
        for p in range(len(split)):
            col_p = parts[:, p * FORGET_ROWS + hh:p * FORGET_ROWS + hh + 1]
            kext = jnp.where(lane_e == p, -col_p, kext)
        sl = slice(hh * head_dim, (hh + 1) * head_dim)
        qa_ref[:, hh * LANES:(hh + 1) * LANES] = _as_words(jnp.concatenate([qf[:, sl], ones], axis=1).astype(BF16))
        ka_ref[:, hh * LANES:(hh + 1) * LANES] = _as_words(jnp.concatenate([kf[:, sl], kext], axis=1).astype(BF16))

    cb_ref[...] = _as_words(cb.astype(BF16))
    ucx_ref[...] = _as_words(ucx.astype(BF16))
    half = d_model // 2
    for c in range(2):
        sgc_ref[:, c * half:(c + 1) * half] = _as_words(jax.nn.sigmoid(mm(wg_ref, c * half, half)).astype(BF16))
        sga_ref[:, c * half:(c + 1) * half] = _as_words(
            jax.nn.sigmoid(mm(wg_ref, d_model + c * half, half)).astype(BF16))

    vt = vtf[:attn_dim].astype(BF16)
    ones_rows = jnp.ones((V_ONES_ROWS, tm), BF16)
    vt = jnp.concatenate([piece for hh in range(heads)
                          for piece in (vt[hh * head_dim:(hh + 1) * head_dim], ones_rows)], axis=0)
    for u in range(tm // tk):
        vt_ref[u] = _as_words(vt[:, u * tk:(u + 1) * tk])


def _inproj(x2d, g_mix, wa, wqk, wvt, wg, bf_col, *, batch, seq, heads, tm, tk):
    n, d_model = x2d.shape
    conv_dim = wa.shape[1] // 3
    attn_dim = wvt.shape[0] - FORGET_ROWS
    assert heads <= FORGET_ROWS
    tiles_per_batch = seq // tm
    grid = (n // tm,)
    full = lambda shape: pl.BlockSpec(shape, lambda i: (0,) * len(shape))
    rows = lambda w: pl.BlockSpec((tm, w), lambda i: (i, 0))
    kern = functools.partial(_inproj_kernel, tiles_per_batch=tiles_per_batch, conv_dim=conv_dim,
                             attn_dim=attn_dim, d_model=d_model, heads=heads, tk=tk)
    vt_rows = attn_dim + heads * V_ONES_ROWS
    words = lambda w: jax.ShapeDtypeStruct((n // 2, w), U32)
    wrows = lambda w: pl.BlockSpec((tm // 2, w), lambda i: (i, 0))
    out_shape = (
        words(conv_dim),
        words(conv_dim),
        words(heads * LANES),
        words(heads * LANES),
        jax.ShapeDtypeStruct((n // tk, vt_rows // 2, tk), U32),
        words(d_model),
        words(d_model),
    )
    out_specs = (
        wrows(conv_dim), wrows(conv_dim), wrows(heads * LANES), wrows(heads * LANES),
        pl.BlockSpec((tm // tk, vt_rows // 2, tk), lambda i: (i, 0, 0)),
        wrows(d_model), wrows(d_model),
    )
    return pl.pallas_call(
        kern,
        grid=grid,
        in_specs=[rows(d_model), full((1, d_model)), full(wa.shape), full(wqk.shape), full(wvt.shape),
                  full(wg.shape), full((FORGET_ROWS, 1))],
        out_specs=out_specs,
        out_shape=out_shape,
        scratch_shapes=[pltpu.VMEM((FORGET_ROWS, LANES), F32)],
        compiler_params=_params(("arbitrary",)),
        name="inproj",
    )(x2d, g_mix, wa, wqk, wvt, wg, bf_col)


def _attn_kernel(q_ref, k_ref, v_ref, o_ref, s_ref, *, tq, tk, head_dim, heads):
    qi = pl.program_id(1)
    row = lax.broadcasted_iota(I32, (tk, tq), 0)
    col = lax.broadcasted_iota(I32, (tk, tq), 1)

    def score_tile(j, slot):
        off = pl.multiple_of(j * (tk // 2), tk // 2)
        for hh in range(heads):
            qa = _as_bf16(q_ref[:, hh * LANES:(hh + 1) * LANES])
            ka = _as_bf16(k_ref[pl.ds(off, tk // 2), hh * LANES:(hh + 1) * LANES])
            s_ref[slot, hh] = lax.dot_general(ka, qa, (((1,), (1,)), ((), ())), preferred_element_type=F32)

    def block(j, slot, carry, masked):
        if not masked:
            score_tile(j + 1, 1 - slot)
        stats = []
        for hh in range(heads):
            m, _ = carry[hh]
            s = s_ref[slot, hh]
            if masked:
                s = jnp.where(row <= col, s, -jnp.inf)
            m_new = jnp.maximum(m, jnp.max(s, axis=0, keepdims=True))
            stats.append((m_new, jnp.exp2(m - m_new), jnp.exp2(s - m_new).astype(BF16)))
        out = []
        for hh in range(heads):
            m_new, alpha, p = stats[hh]
            vth = _as_bf16(v_ref[j, hh * (vrows // 2):(hh + 1) * (vrows // 2), :])
            out.append((m_new, alpha * carry[hh][1] + jnp.dot(vth, p, preferred_element_type=F32)))
        return tuple(out)

    vrows = head_dim + V_ONES_ROWS
    init = tuple((jnp.full((1, tq), -jnp.inf, F32), jnp.zeros((vrows, tq), F32)) for _ in range(heads))
    score_tile(0, 0)

    def pair(p, c):
        return block(2 * p + 1, 1, block(2 * p, 0, c, False), False)

    carry = lax.fori_loop(0, qi // 2, pair, init)
    final = lax.cond(
        lax.rem(qi, 2) == 0,
        lambda c: block(qi, 0, c, True),
        lambda c: block(qi, 1, block(qi - 1, 0, c, False), True),
        carry)
    per_group = LANES // head_dim
    for g in range(heads // per_group):
        accs = [final[g * per_group + u][1] for u in range(per_group)]
        ot = jnp.concatenate([a[:head_dim] / a[head_dim:head_dim + 1] for a in accs],
                             axis=0)
        o_ref[:, g * LANES:(g + 1) * LANES] = _as_words(ot.T.astype(BF16))


def _attention(qa, ka, vt, *, batch, seq, heads, tq):
    n = 2 * qa.shape[0]
    vt_rows, tk = 2 * vt.shape[1], vt.shape[2]
    head_dim = vt_rows // heads - V_ONES_ROWS
    attn_dim = heads * head_dim
    assert LANES % head_dim == 0 and tq == tk
    nq = seq // tq
    nk = seq // tk
    kern = functools.partial(_attn_kernel, tq=tq, tk=tk, head_dim=head_dim, heads=heads)
    return pl.pallas_call(
        kern,
        grid=(batch, nq),
        in_specs=[
            pl.BlockSpec((tq // 2, heads * LANES), lambda b, qi: (b * nq + qi, 0)),
            pl.BlockSpec((seq // 2, heads * LANES), lambda b, qi: (b, 0)),
            pl.BlockSpec((nk, vt_rows // 2, tk), lambda b, qi: (b, 0, 0)),
        ],
        out_specs=pl.BlockSpec((tq // 2, attn_dim), lambda b, qi: (b * nq + qi, 0)),
        out_shape=jax.ShapeDtypeStruct((n // 2, attn_dim), U32),
        scratch_shapes=[pltpu.VMEM((2, heads, tk, tq), F32)],
        compiler_params=_params(("arbitrary", "arbitrary")),
        name="attn",
    )(qa, ka, vt)


def _mix_kernel(x_ref, cb_ref, u_ref, uh_ref, o_ref, sgc_ref, sga_ref, cw_ref, wco_ref, wao_ref, wout_ref,
                gffn_ref, wr_ref, br_ref,
                x1_ref, h2p_ref, topi_ref, gate_ref, rank_ref, cnt_ref, carry_ref, lg_scr,
                *, tiles_per_batch, tiles_per_group, n_experts, nt):
    tm, d_model = x_ref.shape
    i = pl.program_id(0)

    @pl.when(i == 0)
    def _():
        lg_scr[...] = jnp.zeros_like(lg_scr)

    @pl.when((i == 0) | (lax.rem(i - 1, tiles_per_group) == 0))
    def _():
        carry_ref[...] = jnp.zeros_like(carry_ref)

    lg = lg_scr[...]
    sub = lax.broadcasted_iota(I32, (n_experts, tm), 0)

    vals, idxs = [], []
    for _ in range(TOP_K):
        m = jnp.max(lg, axis=0, keepdims=True)
        idx = jnp.min(jnp.where(lg == m, sub, n_experts), axis=0, keepdims=True)
        vals.append(m)
        idxs.append(idx)
        lg = jnp.where(sub == idx, -jnp.inf, lg)
    es = [jnp.exp(vk - vals[0]) for vk in vals]
    denom = es[0] + es[1] + es[2] + es[3]

    sub8 = lax.broadcasted_iota(I32, (SUBLANES, tm), 0)
    subl = lax.broadcasted_iota(I32, (LANES, tm), 0)
    chosen = jnp.zeros((n_experts, tm), F32)
    topi = jnp.zeros((SUBLANES, tm), I32)
    gates_t = jnp.zeros((LANES, tm), F32)
    for kk in range(TOP_K):
        chosen = jnp.where(sub == idxs[kk], 1.0, chosen)
        topi = jnp.where(sub8 == kk, idxs[kk], topi)
        gates_t = jnp.where(subl == kk, es[kk] / denom, gates_t)
    topi_ref[...] = topi
    gate_ref[...] = gates_t.T

    first = lax.rem(jnp.minimum(i, nt - 1), tiles_per_batch) == 0

    u = _as_bf16(u_ref[...]).astype(F32)
    halo = jnp.where(first, 0.0, _as_bf16(uh_ref[...]).astype(F32))
    ext = jnp.concatenate([halo, u], axis=0)
    hs = halo.shape[0]
    u1 = pltpu.roll(ext, 1, 0)[hs:]
    u2 = pltpu.roll(ext, 2, 0)[hs:]
    cw = cw_ref[...]
    conv = cw[0:1, :] * u2 + cw[1:2, :] * u1 + cw[2:3, :] * u
    yc = (_as_bf16(cb_ref[...]).astype(F32) * conv).astype(BF16)
    y_conv = jnp.dot(yc, wco_ref[...], preferred_element_type=F32)
    y_attn = jnp.dot(_as_bf16(o_ref[...]), wao_ref[...], preferred_element_type=F32)
    mixed = (_as_bf16(sgc_ref[...]).astype(F32) * y_conv
             + _as_bf16(sga_ref[...]).astype(F32) * y_attn).astype(BF16)
    x1 = x_ref[...] + jnp.dot(mixed, wout_ref[...], preferred_element_type=F32)
    x1_ref[...] = x1

    ms = jnp.mean(x1 * x1, axis=-1, keepdims=True)
    h2 = (x1 * lax.rsqrt(ms + RMS_EPS)) * gffn_ref[...]
    hb = h2.astype(BF16)
    hbf = hb.astype(F32)
    half = d_model // 2
    h2p_ref[...] = _pack_bf16_pair(hbf[:, :half], hbf[:, half:])

    hlo = (h2 - hbf).astype(BF16)
    nt_dims = (((1,), (1,)), ((), ()))
    rt = (lax.dot_general(wr_ref[...], hb, nt_dims, preferred_element_type=F32)
          + lax.dot_general(wr_ref[...], hlo, nt_dims, preferred_element_type=F32))
    lg_scr[...] = rt[:n_experts] + rt[n_experts:] + br_ref[...]

    row = lax.broadcasted_iota(I32, (tm, tm), 0)
    col = lax.broadcasted_iota(I32, (tm, tm), 1)
    tri = jnp.where(row < col, 1.0, 0.0).astype(BF16)
    before = jnp.dot(chosen.astype(BF16), tri, preferred_element_type=F32) + carry_ref[:, 0:1]
    rank = jnp.zeros((SUBLANES, tm), F32)
    for kk in range(TOP_K):
        rk = jnp.sum(jnp.where(sub == idxs[kk], before, 0.0), axis=0, keepdims=True)
        rank = jnp.where(sub8 == kk, rk, rank)
    rank_ref[...] = rank.astype(I32)
    total = carry_ref[...] + jnp.sum(chosen, axis=1, keepdims=True)
    carry_ref[...] = total
    cnt_ref[0] = total


def _mix(x2d, cb, ucx, o, sgc, sga, conv_w, wco, wao, wout, g_ffn, wr, br, *, seq, n_experts, tm, tile0, nt,
         tiles_per_group):
    d_model = x2d.shape[1]
    n = nt * tm
    conv_dim = cb.shape[1]
    attn_dim = o.shape[1]
    tiles_per_batch = seq // tm
    assert tile0 % tiles_per_batch == 0
    hb = tm // BF16_SUBLANES
    full = lambda shape: pl.BlockSpec(shape, lambda i: (0,) * len(shape))
    cur = lambda i: jnp.minimum(i, nt - 1)
    prev = lambda i: jnp.maximum(i - 1, 0)
    rows_in = lambda w: pl.BlockSpec((tm, w), lambda i: (cur(i) + tile0, 0))
    words_in = lambda w: pl.BlockSpec((tm // 2, w), lambda i: (cur(i) + tile0, 0))
    rows = lambda w: pl.BlockSpec((tm, w), lambda i: (cur(i), 0))
    kern = functools.partial(_mix_kernel, tiles_per_batch=tiles_per_batch, tiles_per_group=tiles_per_group,
                             n_experts=n_experts, nt=nt)
    out_shape = (
        jax.ShapeDtypeStruct((n, d_model), F32),
        jax.ShapeDtypeStruct((n, d_model // 2), U32),
        jax.ShapeDtypeStruct((SUBLANES, n), I32),
        jax.ShapeDtypeStruct((n, LANES), F32),
        jax.ShapeDtypeStruct((SUBLANES, n), I32),
        jax.ShapeDtypeStruct((nt, n_experts, LANES), F32),
    )
    cols = pl.BlockSpec((SUBLANES, tm), lambda i: (0, prev(i)))
    out_specs = (rows(d_model), rows(d_model // 2), cols, pl.BlockSpec((tm, LANES), lambda i: (prev(i), 0)), cols,
                 pl.BlockSpec((1, n_experts, LANES), lambda i: (prev(i), 0, 0)))
    return pl.pallas_call(
        kern,
        grid=(nt + 1,),
        in_specs=[rows_in(d_model), words_in(conv_dim), words_in(conv_dim),
                  pl.BlockSpec((BF16_SUBLANES // 2, conv_dim),
                               lambda i: (jnp.maximum((cur(i) + tile0) * hb - 1, 0), 0)),
                  words_in(attn_dim), words_in(d_model), words_in(d_model),
                  full(conv_w.shape), full(wco.shape), full(wao.shape), full(wout.shape),
                  full((1, d_model)), full(wr.shape), full((n_experts, 1))],
        out_specs=out_specs,
        out_shape=out_shape,
        scratch_shapes=[pltpu.VMEM((n_experts, LANES), F32), pltpu.VMEM((n_experts, tm), F32)],
        compiler_params=_params(("arbitrary",)),
        name="mix",
    )(x2d, cb, ucx, ucx, o, sgc, sga, conv_w, wco, wao, wout, g_ffn, wr, br)


SC_CORES = 2
SC_SUBCORES = 16
SC_WORKERS = SC_CORES * SC_SUBCORES
SC_CHUNK = 64


def _sc_mesh():
    return plsc.VectorSubcoreMesh(core_axis_name="c", subcore_axis_name="s",
                                  num_cores=SC_CORES, num_subcores=SC_SUBCORES)


def _sc_worker():
    return lax.axis_index("s") * SC_CORES + lax.axis_index("c")


def _sc_scatter_rows(src, dest4, pad_idx, n_out, row0):
    width = src.shape[1]
    nchunks = dest4.shape[1]
    npad = pad_idx.shape[1]

    @functools.partial(
        pl.kernel, mesh=_sc_mesh(),
        out_type=jax.ShapeDtypeStruct((n_out, width), src.dtype),
        scratch_types=[pltpu.VMEM((TOP_K, SC_CHUNK), I32), pltpu.VMEM((SC_CHUNK, width), src.dtype),
                       pltpu.VMEM((npad, SC_CHUNK), I32), pltpu.SemaphoreType.DMA],
        name="sc_scatter_rows",
    )
    def k(src_hbm, dest_hbm, pad_hbm, out_hbm, idx_v, rows_v, pad_v, sem):
        wid = _sc_worker()
        base = row0 + wid * (nchunks * SC_CHUNK)

        @pl.loop(0, nchunks)
        def _(j):
            pltpu.sync_copy(src_hbm.at[pl.ds(base + j * SC_CHUNK, SC_CHUNK)], rows_v)
            pltpu.sync_copy(dest_hbm.at[wid, j], idx_v)
            copies = [pltpu.async_copy(rows_v, out_hbm.at[idx_v.at[kk]], sem) for kk in range(TOP_K)]
            for cp in copies:
                cp.wait()

        pltpu.sync_copy(pad_hbm.at[wid], pad_v)
        fills = [pltpu.async_copy(rows_v, out_hbm.at[pad_v.at[p]], sem) for p in range(npad)]
        for cp in fills:
            cp.wait()

    return k(src, dest4, pad_idx)


def _sc_gather_rows(src, dest4, n):
    width = src.shape[1]
    nchunks = dest4.shape[1]

    @functools.partial(
        pl.kernel, mesh=_sc_mesh(),
        out_type=jax.ShapeDtypeStruct((TOP_K, n, width), src.dtype),
        scratch_types=[pltpu.VMEM((TOP_K, SC_CHUNK), I32), pltpu.VMEM((SC_CHUNK, width), src.dtype),
                       pltpu.VMEM((SC_CHUNK, width), src.dtype), pltpu.SemaphoreType.DMA, pltpu.SemaphoreType.DMA],
        name="sc_gather_rows",
    )
    def k(src_hbm, dest_hbm, out_hbm, idx_v, rows_a, rows_b, sem_a, sem_b):
        wid = _sc_worker()
        base = wid * (nchunks * SC_CHUNK)
        bufs = ((rows_a, sem_a), (rows_b, sem_b))

        @pl.loop(0, nchunks)
        def _(j):
            pltpu.sync_copy(dest_hbm.at[wid, j], idx_v)
            pending = pltpu.async_copy(src_hbm.at[idx_v.at[0]], rows_a, sem_a)
            for kk in range(TOP_K):
                buf, _ = bufs[kk % 2]
                pending.wait()
                if kk + 1 < TOP_K:
                    nbuf, nsem = bufs[(kk + 1) % 2]
                    pending = pltpu.async_copy(src_hbm.at[idx_v.at[kk + 1]], nbuf, nsem)
                pltpu.sync_copy(buf, out_hbm.at[kk, pl.ds(base + j * SC_CHUNK, SC_CHUNK)])

    return k(src, dest4)


def _expert_kernel(be_ref, nu_ref, slot_ref, nxt_ref, nv_ref, xs_ref, wgu_hbm, bgu_ref, wd_hbm, bd_ref, ys_ref,
                   wgu_f32, wd_f32, wgu_bf, wd_bf, sem, *, d_ff, ff_chunk):
    i = pl.program_id(0)

    def weight_copies(e, s):
        return (pltpu.make_async_copy(wgu_hbm.at[e], wgu_f32.at[s], sem.at[s]),
                pltpu.make_async_copy(wd_hbm.at[e], wd_f32.at[s], sem.at[s]))

    @pl.when(i == 0)
    def _():
        for cp in weight_copies(be_ref[0], 0):
            cp.start()

    @pl.when((i < nu_ref[0]) & ((i == 0) | (be_ref[i] != be_ref[jnp.maximum(i - 1, 0)])))
    def _():
        s = slot_ref[i]
        for cp in weight_copies(be_ref[i], s):
            cp.wait()
        wgu_bf[...] = wgu_f32[s].astype(BF16)
        wd_bf[...] = wd_f32[s].astype(BF16)
        nxt = nxt_ref[i]

        @pl.when(nxt >= 0)
        def _():
            for cp in weight_copies(nxt, 1 - s):
                cp.start()

    def mlp_rows(rows):
        lo, hi = _unpack_bf16_pair(xs_ref[0:rows, :])
        xb = jnp.concatenate([lo.astype(BF16), hi.astype(BF16)], axis=1)
        chunks = range(0, d_ff, ff_chunk)
        gus = []
        for c0 in chunks:
            glu = jnp.dot(xb, wgu_bf[:, c0:c0 + ff_chunk], preferred_element_type=F32) + bgu_ref[0, :, c0:c0 + ff_chunk]
            lin = (jnp.dot(xb, wgu_bf[:, d_ff + c0:d_ff + c0 + ff_chunk], preferred_element_type=F32)
                   + bgu_ref[0, :, d_ff + c0:d_ff + c0 + ff_chunk])
            gus.append((glu, lin))
        y = bd_ref[0]
        for c0, (glu, lin) in zip(chunks, gus):
            glu = jnp.minimum(glu, SWIGLU_LIMIT)
            lin = jnp.clip(lin, -SWIGLU_LIMIT, SWIGLU_LIMIT)
            act = (glu * jax.nn.sigmoid(SWIGLU_ALPHA * glu)) * (lin + 1.0)
            y = y + jnp.dot(act.astype(BF16), wd_bf[c0:c0 + ff_chunk, :], preferred_element_type=F32)
        yb = y.astype(BF16).astype(F32)
        half = y.shape[1] // 2
        ys_ref[0:rows, :] = _pack_bf16_pair(yb[:, :half], yb[:, half:])

    tm = xs_ref.shape[0]
    used = i < nu_ref[0]
    half_rows = used & (nv_ref[i] <= tm // 2)

    @pl.when(half_rows)
    def _():
        mlp_rows(tm // 2)
        ys_ref[tm // 2:, :] = jnp.zeros((tm - tm // 2, ys_ref.shape[1]), U32)

    @pl.when(used & jnp.logical_not(half_rows))
    def _():
        mlp_rows(tm)


def _experts(xs, block_e, n_used, slot, next_e, n_valid, wgu, bgu, wd, bd, *, tm, n_blocks):
    half = xs.shape[1]
    p = n_blocks * tm
    e, d_model, two_ff = wgu.shape
    d_ff = two_ff // 2
    kern = functools.partial(_expert_kernel, d_ff=d_ff, ff_chunk=min(256, d_ff))
    blk = lambda i, be, nu, sl, nx, nv: (jnp.minimum(i, nu[0] - 1), 0)
    bias = lambda i, be, nu, sl, nx, nv: (be[i], 0, 0)
    grid_spec = pltpu.PrefetchScalarGridSpec(
        num_scalar_prefetch=5,
        grid=(n_blocks,),
        in_specs=[
            pl.BlockSpec((tm, half), blk),
            pl.BlockSpec(memory_space=pl.ANY),
            pl.BlockSpec((1, 1, two_ff), bias),
            pl.BlockSpec(memory_space=pl.ANY),
            pl.BlockSpec((1, 1, d_model), bias),
        ],
        out_specs=pl.BlockSpec((tm, half), blk),
        scratch_shapes=[pltpu.VMEM((2, d_model, two_ff), F32), pltpu.VMEM((2, d_ff, d_model), F32),
                        pltpu.VMEM((d_model, two_ff), BF16), pltpu.VMEM((d_ff, d_model), BF16),
                        pltpu.SemaphoreType.DMA((2,))],
    )
    return pl.pallas_call(
        kern,
        grid_spec=grid_spec,
        out_shape=jax.ShapeDtypeStruct((p, half), U32),
        compiler_params=_params(("arbitrary",)),
        name="experts",
    )(block_e, n_used, slot, next_e, n_valid, xs, wgu, bgu.reshape(e, 1, two_ff), wd, bd.reshape(e, 1, d_model))


def _final_kernel(yk_ref, gate_ref, x1_ref, g_ref, out_ref):
    tm, d_model = x1_ref.shape
    half = d_model // 2
    gates = gate_ref[...]
    acc_lo = jnp.zeros((tm, half), F32)
    acc_hi = jnp.zeros((tm, half), F32)
    for kk in range(TOP_K):
        lo, hi = _unpack_bf16_pair(yk_ref[kk])
        gk = gates[:, kk:kk + 1]
        acc_lo = acc_lo + gk * lo
        acc_hi = acc_hi + gk * hi
    x2 = x1_ref[...] + jnp.concatenate([acc_lo, acc_hi], axis=1)
    ms = jnp.mean(x2 * x2, axis=-1, keepdims=True)
    out_ref[...] = (x2 * lax.rsqrt(ms + RMS_EPS)) * g_ref[...]


def _final(yk, gates, x1, g_final, *, tm, tile0):
    n, d_model = x1.shape
    rows = lambda w: pl.BlockSpec((tm, w), lambda i: (i + tile0, 0))
    return pl.pallas_call(
        _final_kernel,
        grid=(yk.shape[1] // tm,),
        in_specs=[pl.BlockSpec((TOP_K, tm, d_model // 2), lambda i: (0, i, 0)), rows(LANES), rows(d_model),
                  pl.BlockSpec((1, d_model), lambda i: (0, 0))],
        out_specs=rows(d_model),
        out_shape=jax.ShapeDtypeStruct((n, d_model), F32),
        input_output_aliases={2: 0},
        compiler_params=_params(("arbitrary",)),
        name="final",
    )(yk, gates, x1, g_final)


def _tiles(seq):
    tm = min(512, seq)
    return dict(tm_proj=tm, t_attn=min(256, seq), tm_mix=tm, tm_expert=512, tm_final=tm, moe_groups=2)


def _forward(x, g_mix, w_in, conv_w, b_f, w_conv_o, w_attn_o, w_out, g_ffn, w_router, b_router,
             w_gate_up, b_gate_up, w_down, b_down, g_final, tiles):
    batch, seq, d_model = x.shape
    n = batch * seq
    conv_dim = conv_w.shape[1]
    attn_dim = w_attn_o.shape[0]
    heads = b_f.shape[0]
    head_dim = attn_dim // heads
    n_experts = w_router.shape[1]
    x2d = x.reshape(n, d_model)

    c0 = 3 * conv_dim
    a0 = c0 + 3 * attn_dim
    wa = w_in[:, :c0].astype(BF16)
    scale = LOG2E / (head_dim ** 0.5)
    wqk = jnp.concatenate([w_in[:, c0:c0 + attn_dim] * scale, w_in[:, c0 + attn_dim:c0 + 2 * attn_dim]],
                          axis=1).astype(BF16)
    wvt = jnp.pad(w_in[:, c0 + 2 * attn_dim:a0 + heads].T, ((0, FORGET_ROWS - heads), (0, 0))).astype(BF16)
    wg = w_in[:, a0 + heads:].astype(BF16)
    bf_col = jnp.pad(b_f, (0, FORGET_ROWS - heads)).reshape(FORGET_ROWS, 1)
    wr_hi = w_router.astype(BF16)
    wr_lo = (w_router - wr_hi.astype(F32)).astype(BF16)
    wr = jnp.concatenate([wr_hi, wr_lo], axis=1).T
    br = b_router.reshape(n_experts, 1)

    cb, ucx, qa, ka, vt, sgc, sga = _inproj(
        x2d, g_mix.reshape(1, d_model), wa, wqk, wvt, wg, bf_col,
        batch=batch, seq=seq, heads=heads, tm=tiles["tm_proj"], tk=tiles["t_attn"])
    o = _attention(qa, ka, vt, batch=batch, seq=seq, heads=heads, tq=tiles["t_attn"])
    groups = tiles["moe_groups"]
    ng = n // groups
    x1, h2p, topi, gates, rank, cnt = _mix(
        x2d, cb, ucx, o, sgc, sga, conv_w, w_conv_o.astype(BF16), w_attn_o.astype(BF16), w_out.astype(BF16),
        g_ffn.reshape(1, d_model), wr, br, seq=seq, n_experts=n_experts, tm=tiles["tm_mix"],
        tile0=0, nt=n // tiles["tm_mix"], tiles_per_group=ng // tiles["tm_mix"])

    tme = tiles["tm_expert"]
    n_blocks = (ng * TOP_K) // tme + n_experts
    p_rows = n_blocks * tme
    nchunks = ng // (SC_WORKERS * SC_CHUNK)
    jj = jnp.arange(tme, dtype=I32)[None, :]
    spare = p_rows + jnp.arange(n_experts, dtype=I32)[:, None] * tme + jj
    expert_ids = jnp.arange(n_experts, dtype=I32)[None, None, :]
    ys_groups, dest_groups = [], []
    for g in range(groups):
        counts = cnt[(g + 1) * (ng // tiles["tm_mix"]) - 1, :, 0].astype(I32)
        padded = ((counts + tme - 1) // tme) * tme
        pend = jnp.cumsum(padded)
        pstart = pend - padded
        n_used = (pend[-1] // tme).astype(I32).reshape(1)
        blk_start = jnp.minimum(jnp.arange(n_blocks, dtype=I32), n_used[0] - 1) * tme
        block_e = jnp.minimum(jnp.sum((pend[None, :] <= blk_start[:, None]).astype(I32), axis=1), n_experts - 1)
        tok = slice(g * ng, (g + 1) * ng)
        onehot = topi[:TOP_K, tok, None] == expert_ids
        dest = jnp.sum(jnp.where(onehot, pstart[None, None, :], 0), axis=2) + rank[:TOP_K, tok]
        dest4 = dest.reshape(TOP_K, SC_WORKERS, nchunks, SC_CHUNK).transpose(1, 2, 0, 3)
        pad_idx = jnp.where(jj < (padded - counts)[:, None], (pstart + counts)[:, None] + jj, spare)
        pad_idx = pad_idx.reshape(SC_WORKERS, (n_experts * tme) // (SC_WORKERS * SC_CHUNK), SC_CHUNK)
        xs = _sc_scatter_rows(h2p, dest4, pad_idx, p_rows + n_experts * tme, g * ng)
        eid = jnp.arange(n_experts, dtype=I32)
        active = padded > 0
        run_tbl = jnp.sum((eid[None, :] < eid[:, None]) & active[None, :], axis=1)
        later = (eid[None, :] > eid[:, None]) & active[None, :]
        next_tbl = jnp.min(jnp.where(later, eid[None, :], n_experts), axis=1)
        next_tbl = jnp.where(next_tbl < n_experts, next_tbl, -1)
        is_e = block_e[:, None] == eid[None, :]
        slot = lax.rem(jnp.sum(jnp.where(is_e, run_tbl[None, :], 0), axis=1), 2).astype(I32)
        next_e = jnp.sum(jnp.where(is_e, next_tbl[None, :], 0), axis=1).astype(I32)
        row_in_e = jnp.arange(n_blocks, dtype=I32) * tme - jnp.sum(jnp.where(is_e, pstart[None, :], 0), axis=1)
        n_valid = jnp.clip(jnp.sum(jnp.where(is_e, counts[None, :], 0), axis=1) - row_in_e, 0, tme).astype(I32)
        ys_groups.append(_experts(xs, block_e, n_used, slot, next_e, n_valid, w_gate_up, b_gate_up, w_down,
                                  b_down, tm=tme, n_blocks=n_blocks))
        dest_groups.append(dest4)

    out = x1
    for g in range(groups):
        yk = _sc_gather_rows(ys_groups[g], dest_groups[g], ng)
        out = _final(yk, gates, out, g_final.reshape(1, d_model), tm=tiles["tm_final"],
                     tile0=g * (ng // tiles["tm_final"]))
    return out.reshape(batch, seq, d_model)


def kernel(x, g_mix, w_in, conv_w, b_f, w_conv_o, w_attn_o, w_out, g_ffn, w_router, b_router, w_gate_up,
           b_gate_up, w_down, b_down, g_final):
    return _forward(x, g_mix, w_in, conv_w, b_f, w_conv_o, w_attn_o, w_out, g_ffn, w_router, b_router,
                    w_gate_up, b_gate_up, w_down, b_down, g_final, _tiles(x.shape[1]))
```

```python
import functools

import jax
import jax.numpy as jnp
from jax import lax
from jax.experimental import pallas as pl
from jax.experimental.pallas import tpu as pltpu
from jax.experimental.pallas import tpu_sc as plsc

TOP_K = 4
RMS_EPS = 1e-5
SWIGLU_ALPHA = 1.702
SWIGLU_LIMIT = 7.0
LOG2E = 1.4426950408889634

LANES = 128
SUBLANES = 8
BF16_SUBLANES = 16
V_ONES_ROWS = BF16_SUBLANES
EXPERT_ROW_PARTS = 4
FORGET_ROWS = BF16_SUBLANES
VMEM_LIMIT_BYTES = 56 * 1024 * 1024

F32 = jnp.float32
BF16 = jnp.bfloat16
U32 = jnp.uint32
I32 = jnp.int32
HI_MASK = 0xFFFF0000


def _params(sem):
    return pltpu.CompilerParams(dimension_semantics=sem, vmem_limit_bytes=VMEM_LIMIT_BYTES)


def _as_words(x_bf16):
    return pltpu.bitcast(x_bf16, U32)


def _as_bf16(w_u32):
    return pltpu.bitcast(w_u32, BF16)


def _pack_bf16_pair(lo_f32, hi_f32):
    lo = lax.bitcast_convert_type(lo_f32, U32)
    hi = lax.bitcast_convert_type(hi_f32, U32)
    return (lo >> 16) | (hi & U32(HI_MASK))


def _unpack_bf16_pair(w):
    lo = lax.bitcast_convert_type(w << 16, F32)
    hi = lax.bitcast_convert_type(w & U32(HI_MASK), F32)
    return lo, hi


def _inproj_kernel(x_ref, g_ref, wa_ref, wqk_ref, wvt_ref, wg_ref, bf_ref,
                   cb_ref, ucx_ref, qa_ref, ka_ref, vt_ref, sgc_ref, sga_ref,
                   carry_ref, *, tiles_per_batch, conv_dim, attn_dim, d_model, heads, tk):
    tm = x_ref.shape[0]
    t = lax.rem(pl.program_id(0), tiles_per_batch)
    x = x_ref[...]
    ms = jnp.mean(x * x, axis=-1, keepdims=True)
    h = ((x * lax.rsqrt(ms + RMS_EPS)) * g_ref[...]).astype(BF16)

    def mm(w_ref, c0, n):
        return jnp.dot(h, w_ref[:, c0:c0 + n], preferred_element_type=F32)

    vtf =lax.dot_general(wvt_ref[...], h, (((1,), (1,)), ((), ())), preferred_element_type=F32)
    qf = mm(wqk_ref, 0, attn_dim)
    kf = mm(wqk_ref, attn_dim, attn_dim)
    cb = mm(wa_ref, 0, conv_dim)
    ucx = mm(wa_ref, conv_dim, conv_dim) * mm(wa_ref, 2 * conv_dim, conv_dim)

    z = vtf[attn_dim:attn_dim + FORGET_ROWS] + bf_ref[...]
    sub = lax.broadcasted_iota(I32, z.shape, 0)
    logf = jnp.where(sub < heads, jnp.minimum(z, 0.0) - jnp.log1p(jnp.exp(-jnp.abs(z))), 0.0)
    p1 = logf.astype(BF16).astype(F32)
    r1 = logf - p1
    p2 = r1.astype(BF16).astype(F32)
    p3 = (r1 - p2).astype(BF16).astype(F32)
    packed = jnp.concatenate([p1, p2, p3], axis=0).astype(BF16)
    row = lax.broadcasted_iota(I32, (tm, tm), 0)
    col = lax.broadcasted_iota(I32, (tm, tm), 1)
    tri = jnp.where(row <= col, 1.0, 0.0).astype(BF16)
    r = jnp.dot(packed, tri, preferred_element_type=F32)
    local = r[:FORGET_ROWS] + r[FORGET_ROWS:2 * FORGET_ROWS] + r[2 * FORGET_ROWS:]

    @pl.when(t == 0)
    def _():
        carry_ref[...] = jnp.zeros_like(carry_ref)

    c = local + carry_ref[:, 0:1]
    carry_ref[...] = jnp.broadcast_to(c[:, tm - 1:tm], carry_ref.shape)

    c2 = c * LOG2E
    c_hi_t = c2.astype(BF16).astype(F32)
    c_r = c2 - c_hi_t
    c_mid_t = c_r.astype(BF16).astype(F32)
    c_lo_t = (c_r - c_mid_t).astype(BF16).astype(F32)
    split = (c_hi_t, c_mid_t, c_lo_t)
    pad = jnp.zeros((LANES - len(split) * FORGET_ROWS, tm), F32)
    parts = jnp.concatenate(split + (pad,), axis=0).T
    head_dim = attn_dim // heads
    ext = LANES - head_dim
    lane_e = lax.broadcasted_iota(I32, (tm, ext), 1)
    ones = jnp.where(lane_e < len(split), 1.0, 0.0)
    for hh in range(heads):
        kext = jnp.zeros((tm, ext), F32)
        for p in range(len(split)):
            col_p = parts[:, p * FORGET_ROWS + hh:p * FORGET_ROWS + hh + 1]
            kext = jnp.where(lane_e == p, -col_p, kext)
        sl = slice(hh * head_dim, (hh + 1) * head_dim)
        qa_ref[:, hh * LANES:(hh + 1) * LANES] = _as_words(jnp.concatenate([qf[:, sl], ones], axis=1).astype(BF16))
        ka_ref[:, hh * LANES:(hh + 1) * LANES] = _as_words(jnp.concatenate([kf[:, sl], kext], axis=1).astype(BF16))

    cb_ref[...] = _as_words(cb.astype(BF16))
    ucx_ref[...] = _as_words(ucx.astype(BF16))
    half = d_model // 2
    for c in range(2):
        sgc_ref[:, c * half:(c + 1) * half] = _as_words(jax.nn.sigmoid(mm(wg_ref, c * half, half)).astype(BF16))
        sga_ref[:, c * half:(c + 1) * half] = _as_words(
            jax.nn.sigmoid(mm(wg_ref, d_model + c * half, half)).astype(BF16))

    vt = vtf[:attn_dim].astype(BF16)
    ones_rows = jnp.ones((V_ONES_ROWS, tm), BF16)
    vt = jnp.concatenate([piece for hh in range(heads)
                          for piece in (vt[hh * head_dim:(hh + 1) * head_dim], ones_rows)], axis=0)
    for u in range(tm // tk):
        vt_ref[u] = _as_words(vt[:, u * tk:(u + 1) * tk])


def _inproj(x2d, g_mix, wa, wqk, wvt, wg, bf_col, *, batch, seq, heads, tm, tk):
    n, d_model = x2d.shape
    conv_dim = wa.shape[1] // 3
    attn_dim = wvt.shape[0] - FORGET_ROWS
    assert heads <= FORGET_ROWS
    tiles_per_batch = seq // tm
    grid = (n // tm,)
    full = lambda shape: pl.BlockSpec(shape, lambda i: (0,) * len(shape))
    rows = lambda w: pl.BlockSpec((tm, w), lambda i: (i, 0))
    kern = functools.partial(_inproj_kernel, tiles_per_batch=tiles_per_batch, conv_dim=conv_dim,
                             attn_dim=attn_dim, d_model=d_model, heads=heads, tk=tk)
    vt_rows = attn_dim + heads * V_ONES_ROWS
    words = lambda w: jax.ShapeDtypeStruct((n // 2, w), U32)
    wrows = lambda w: pl.BlockSpec((tm // 2, w), lambda i: (i, 0))
    out_shape = (
        words(conv_dim),
        words(conv_dim),
        words(heads * LANES),
        words(heads * LANES),
        jax.ShapeDtypeStruct((n // tk, vt_rows // 2, tk), U32),
        words(d_model),
        words(d_model),
    )
    out_specs = (
        wrows(conv_dim), wrows(conv_dim), wrows(heads * LANES), wrows(heads * LANES),
        pl.BlockSpec((tm // tk, vt_rows // 2, tk), lambda i: (i, 0, 0)),
        wrows(d_model), wrows(d_model),
    )
    return pl.pallas_call(
        kern,
        grid=grid,
        in_specs=[rows(d_model), full((1, d_model)), full(wa.shape), full(wqk.shape), full(wvt.shape),
                  full(wg.shape), full((FORGET_ROWS, 1))],
        out_specs=out_specs,
        out_shape=out_shape,
        scratch_shapes=[pltpu.VMEM((FORGET_ROWS, LANES), F32)],
        compiler_params=_params(("arbitrary",)),
        name="inproj",
    )(x2d, g_mix, wa, wqk, wvt, wg, bf_col)


def _attn_kernel(q_ref, k_ref, v_ref, o_ref, s_ref, *, tq, tk, head_dim, heads):
    qi = pl.program_id(1)
    row = lax.broadcasted_iota(I32, (tk, tq), 0)
    col = lax.broadcasted_iota(I32, (tk, tq), 1)

    def score_tile(j, slot):
        off = pl.multiple_of(j * (tk // 2), tk // 2)
        for hh in range(heads):
            qa = _as_bf16(q_ref[:, hh * LANES:(hh + 1) * LANES])
            ka = _as_bf16(k_ref[pl.ds(off, tk // 2), hh * LANES:(hh + 1) * LANES])
            s_ref[slot, hh] = lax.dot_general(ka, qa, (((1,), (1,)), ((), ())), preferred_element_type=F32)

    def block(j, slot, carry, masked):
        if not masked:
            score_tile(j + 1, 1 - slot)
        stats = []
        for hh in range(heads):
            m, _ = carry[hh]
            s = s_ref[slot, hh]
            if masked:
                s = jnp.where(row <= col, s, -jnp.inf)
            m_new = jnp.maximum(m, jnp.max(s, axis=0, keepdims=True))
            stats.append((m_new, jnp.exp2(m - m_new), jnp.exp2(s - m_new).astype(BF16)))
        out = []
        for hh in range(heads):
            m_new, alpha, p = stats[hh]
            vth = _as_bf16(v_ref[j, hh * (vrows // 2):(hh + 1) * (vrows // 2), :])
            out.append((m_new, alpha * carry[hh][1] + jnp.dot(vth, p, preferred_element_type=F32)))
        return tuple(out)

    vrows = head_dim + V_ONES_ROWS
    init = tuple((jnp.full((1, tq), -jnp.inf, F32), jnp.zeros((vrows, tq), F32)) for _ in range(heads))
    score_tile(0, 0)

    def pair(p, c):
        return block(2 * p + 1, 1, block(2 * p, 0, c, False), False)

    carry = lax.fori_loop(0, qi // 2, pair, init)
    final = lax.cond(
        lax.rem(qi, 2) == 0,
        lambda c: block(qi, 0, c, True),
        lambda c: block(qi, 1, block(qi - 1, 0, c, False), True),
        carry)
    per_group = LANES // head_dim
    for g in range(heads // per_group):
        accs = [final[g * per_group + u][1] for u in range(per_group)]
        ot = jnp.concatenate([a[:head_dim] / a[head_dim:head_dim + 1] for a in accs],
                             axis=0)
        o_ref[:, g * LANES:(g + 1) * LANES] = _as_words(ot.T.astype(BF16))


def _attention(qa, ka, vt, *, batch, seq, heads, tq):
    n = 2 * qa.shape[0]
    vt_rows, tk = 2 * vt.shape[1], vt.shape[2]
    head_dim = vt_rows // heads - V_ONES_ROWS
    attn_dim = heads * head_dim
    assert LANES % head_dim == 0 and tq == tk
    nq = seq // tq
    nk = seq // tk
    kern = functools.partial(_attn_kernel, tq=tq, tk=tk, head_dim=head_dim, heads=heads)
    return pl.pallas_call(
        kern,
        grid=(batch, nq),
        in_specs=[
            pl.BlockSpec((tq // 2, heads * LANES), lambda b, qi: (b * nq + qi, 0)),
            pl.BlockSpec((seq // 2, heads * LANES), lambda b, qi: (b, 0)),
            pl.BlockSpec((nk, vt_rows // 2, tk), lambda b, qi: (b, 0, 0)),
        ],
        out_specs=pl.BlockSpec((tq // 2, attn_dim), lambda b, qi: (b * nq + qi, 0)),
        out_shape=jax.ShapeDtypeStruct((n // 2, attn_dim), U32),
        scratch_shapes=[pltpu.VMEM((2, heads, tk, tq), F32)],
        compiler_params=_params(("arbitrary", "arbitrary")),
        name="attn",
    )(qa, ka, vt)


def _mix_kernel(x_ref, cb_ref, u_ref, uh_ref, o_ref, sgc_ref, sga_ref, cw_ref, wco_ref, wao_ref, wout_ref,
                gffn_ref, wr_ref, br_ref,
                x1_ref, h2p_ref, topi_ref, gate_ref, rank_ref, cnt_ref, carry_ref, lg_scr,
                *, tiles_per_batch, tiles_per_group, n_experts, nt):
    tm, d_model = x_ref.shape
    i = pl.program_id(0)

    @pl.when(i == 0)
    def _():
        lg_scr[...] = jnp.zeros_like(lg_scr)

    @pl.when((i == 0) | (lax.rem(i - 1, tiles_per_group) == 0))
    def _():
        carry_ref[...] = jnp.zeros_like(carry_ref)

    lg = lg_scr[...]
    sub = lax.broadcasted_iota(I32, (n_experts, tm), 0)

    vals, idxs = [], []
    for _ in range(TOP_K):
        m = jnp.max(lg, axis=0, keepdims=True)
        idx = jnp.min(jnp.where(lg == m, sub, n_experts), axis=0, keepdims=True)
        vals.append(m)
        idxs.append(idx)
        lg = jnp.where(sub == idx, -jnp.inf, lg)
    es = [jnp.exp(vk - vals[0]) for vk in vals]
    denom = es[0] + es[1] + es[2] + es[3]

    sub8 = lax.broadcasted_iota(I32, (SUBLANES, tm), 0)
    subl = lax.broadcasted_iota(I32, (LANES, tm), 0)
    chosen = jnp.zeros((n_experts, tm), F32)
    topi = jnp.zeros((SUBLANES, tm), I32)
    gates_t = jnp.zeros((LANES, tm), F32)
    for kk in range(TOP_K):
        chosen = jnp.where(sub == idxs[kk], 1.0, chosen)
        topi = jnp.where(sub8 == kk, idxs[kk], topi)
        gates_t = jnp.where(subl == kk, es[kk] / denom, gates_t)
    topi_ref[...] = topi
    gate_ref[...] = gates_t.T

    first = lax.rem(jnp.minimum(i, nt - 1), tiles_per_batch) == 0

    u = _as_bf16(u_ref[...]).astype(F32)
    halo = jnp.where(first, 0.0, _as_bf16(uh_ref[...]).astype(F32))
    ext = jnp.concatenate([halo, u], axis=0)
    hs = halo.shape[0]
    u1 = pltpu.roll(ext, 1, 0)[hs:]
    u2 = pltpu.roll(ext, 2, 0)[hs:]
    cw = cw_ref[...]
    conv = cw[0:1, :] * u2 + cw[1:2, :] * u1 + cw[2:3, :] * u
    yc = (_as_bf16(cb_ref[...]).astype(F32) * conv).astype(BF16)
    y_conv = jnp.dot(yc, wco_ref[...], preferred_element_type=F32)
    y_attn = jnp.dot(_as_bf16(o_ref[...]), wao_ref[...], preferred_element_type=F32)
    mixed = (_as_bf16(sgc_ref[...]).astype(F32) * y_conv
             + _as_bf16(sga_ref[...]).astype(F32) * y_attn).astype(BF16)
    x1 = x_ref[...] + jnp.dot(mixed, wout_ref[...], preferred_element_type=F32)
    x1_ref[...] = x1

    ms = jnp.mean(x1 * x1, axis=-1, keepdims=True)
    h2 = (x1 * lax.rsqrt(ms + RMS_EPS)) * gffn_ref[...]
    hb = h2.astype(BF16)
    hbf = hb.astype(F32)
    half = d_model // 2
    h2p_ref[...] = _pack_bf16_pair(hbf[:, :half], hbf[:, half:])

    hlo = (h2 - hbf).astype(BF16)
    nt_dims = (((1,), (1,)), ((), ()))
    rt = (lax.dot_general(wr_ref[...], hb, nt_dims, preferred_element_type=F32)
          + lax.dot_general(wr_ref[...], hlo, nt_dims, preferred_element_type=F32))
    lg_scr[...] = rt[:n_experts] + rt[n_experts:] + br_ref[...]

    row = lax.broadcasted_iota(I32, (tm, tm), 0)
    col = lax.broadcasted_iota(I32, (tm, tm), 1)
    tri = jnp.where(row < col, 1.0, 0.0).astype(BF16)
    before = jnp.dot(chosen.astype(BF16), tri, preferred_element_type=F32) + carry_ref[:, 0:1]
    rank = jnp.zeros((SUBLANES, tm), F32)
    for kk in range(TOP_K):
        rk = jnp.sum(jnp.where(sub == idxs[kk], before, 0.0), axis=0, keepdims=True)
        rank = jnp.where(sub8 == kk, rk, rank)
    rank_ref[...] = rank.astype(I32)
    total = carry_ref[...] + jnp.sum(chosen, axis=1, keepdims=True)
    carry_ref[...] = total
    cnt_ref[0] = total


def _mix(x2d, cb, ucx, o, sgc, sga, conv_w, wco, wao, wout, g_ffn, wr, br, *, seq, n_experts, tm, tile0, nt,
         tiles_per_group):
    d_model = x2d.shape[1]
    n = nt * tm
    conv_dim = cb.shape[1]
    attn_dim = o.shape[1]
    tiles_per_batch = seq // tm
    assert tile0 % tiles_per_batch == 0
    hb = tm // BF16_SUBLANES
    full = lambda shape: pl.BlockSpec(shape, lambda i: (0,) * len(shape))
    cur = lambda i: jnp.minimum(i, nt - 1)
    prev = lambda i: jnp.maximum(i - 1, 0)
    rows_in = lambda w: pl.BlockSpec((tm, w), lambda i: (cur(i) + tile0, 0))
    words_in = lambda w: pl.BlockSpec((tm // 2, w), lambda i: (cur(i) + tile0, 0))
    rows = lambda w: pl.BlockSpec((tm, w), lambda i: (cur(i), 0))
    kern = functools.partial(_mix_kernel, tiles_per_batch=tiles_per_batch, tiles_per_group=tiles_per_group,
                             n_experts=n_experts, nt=nt)
    out_shape = (
        jax.ShapeDtypeStruct((n, d_model), F32),
        jax.ShapeDtypeStruct((n, d_model // 2), U32),
        jax.ShapeDtypeStruct((SUBLANES, n), I32),
        jax.ShapeDtypeStruct((n, LANES), F32),
        jax.ShapeDtypeStruct((SUBLANES, n), I32),
        jax.ShapeDtypeStruct((nt, n_experts, LANES), F32),
    )
    cols = pl.BlockSpec((SUBLANES, tm), lambda i: (0, prev(i)))
    out_specs = (rows(d_model), rows(d_model // 2), cols, pl.BlockSpec((tm, LANES), lambda i: (prev(i), 0)), cols,
                 pl.BlockSpec((1, n_experts, LANES), lambda i: (prev(i), 0, 0)))
    return pl.pallas_call(
        kern,
        grid=(nt + 1,),
        in_specs=[rows_in(d_model), words_in(conv_dim), words_in(conv_dim),
                  pl.BlockSpec((BF16_SUBLANES // 2, conv_dim),
                               lambda i: (jnp.maximum((cur(i) + tile0) * hb - 1, 0), 0)),
                  words_in(attn_dim), words_in(d_model), words_in(d_model),
                  full(conv_w.shape), full(wco.shape), full(wao.shape), full(wout.shape),
                  full((1, d_model)), full(wr.shape), full((n_experts, 1))],
        out_specs=out_specs,
        out_shape=out_shape,
        scratch_shapes=[pltpu.VMEM((n_experts, LANES), F32), pltpu.VMEM((n_experts, tm), F32)],
        compiler_params=_params(("arbitrary",)),
        name="mix",
    )(x2d, cb, ucx, ucx, o, sgc, sga, conv_w, wco, wao, wout, g_ffn, wr, br)


SC_CORES = 2
SC_SUBCORES = 16
SC_WORKERS = SC_CORES * SC_SUBCORES
SC_CHUNK = 64


def _sc_mesh():
    return plsc.VectorSubcoreMesh(core_axis_name="c", subcore_axis_name="s",
                                  num_cores=SC_CORES, num_subcores=SC_SUBCORES)


def _sc_worker():
    return lax.axis_index("s") * SC_CORES + lax.axis_index("c")


def _sc_scatter_rows(src, dest4, pad_idx, n_out, row0):
    width = src.shape[1]
    nchunks = dest4.shape[1]
    npad = pad_idx.shape[1]

    @functools.partial(
        pl.kernel, mesh=_sc_mesh(),
        out_type=jax.ShapeDtypeStruct((n_out, width), src.dtype),
        scratch_types=[pltpu.VMEM((TOP_K, SC_CHUNK), I32), pltpu.VMEM((SC_CHUNK, width), src.dtype),
                       pltpu.VMEM((npad, SC_CHUNK), I32), pltpu.SemaphoreType.DMA],
        name="sc_scatter_rows",
    )
    def k(src_hbm, dest_hbm, pad_hbm, out_hbm, idx_v, rows_v, pad_v, sem):
        wid = _sc_worker()
        base = row0 + wid * (nchunks * SC_CHUNK)

        @pl.loop(0, nchunks)
        def _(j):
            pltpu.sync_copy(src_hbm.at[pl.ds(base + j * SC_CHUNK, SC_CHUNK)], rows_v)
            pltpu.sync_copy(dest_hbm.at[wid, j], idx_v)
            copies = [pltpu.async_copy(rows_v, out_hbm.at[idx_v.at[kk]], sem) for kk in range(TOP_K)]
            for cp in copies:
                cp.wait()

        pltpu.sync_copy(pad_hbm.at[wid], pad_v)
        fills = [pltpu.async_copy(rows_v, out_hbm.at[pad_v.at[p]], sem) for p in range(npad)]
        for cp in fills:
            cp.wait()

    return k(src, dest4, pad_idx)


def _sc_gather_rows(src, dest4, n):
    width = src.shape[1]
    nchunks = dest4.shape[1]

    @functools.partial(
        pl.kernel, mesh=_sc_mesh(),
        out_type=jax.ShapeDtypeStruct((TOP_K, n, width), src.dtype),
        scratch_types=[pltpu.VMEM((TOP_K, SC_CHUNK), I32), pltpu.VMEM((SC_CHUNK, width), src.dtype),
                       pltpu.VMEM((SC_CHUNK, width), src.dtype), pltpu.SemaphoreType.DMA, pltpu.SemaphoreType.DMA],
        name="sc_gather_rows",
    )
    def k(src_hbm, dest_hbm, out_hbm, idx_v, rows_a, rows_b, sem_a, sem_b):
        wid = _sc_worker()
        base = wid * (nchunks * SC_CHUNK)
        bufs = ((rows_a, sem_a), (rows_b, sem_b))

        @pl.loop(0, nchunks)
        def _(j):
            pltpu.sync_copy(dest_hbm.at[wid, j], idx_v)
            pending = pltpu.async_copy(src_hbm.at[idx_v.at[0]], rows_a, sem_a)
            for kk in range(TOP_K):
                buf, _ = bufs[kk % 2]
                pending.wait()
                if kk + 1 < TOP_K:
                    nbuf, nsem = bufs[(kk + 1) % 2]
                    pending = pltpu.async_copy(src_hbm.at[idx_v.at[kk + 1]], nbuf, nsem)
                pltpu.sync_copy(buf, out_hbm.at[kk, pl.ds(base + j * SC_CHUNK, SC_CHUNK)])

    return k(src, dest4)


def _expert_kernel(be_ref, nu_ref, slot_ref, nxt_ref, nv_ref, xs_ref, wgu_hbm, bgu_ref, wd_hbm, bd_ref, ys_ref,
                   wgu_f32, wd_f32, wgu_bf, wd_bf, sem, *, d_ff, ff_chunk):
    i = pl.program_id(0)

    def weight_copies(e, s):
        return (pltpu.make_async_copy(wgu_hbm.at[e], wgu_f32.at[s], sem.at[s]),
                pltpu.make_async_copy(wd_hbm.at[e], wd_f32.at[s], sem.at[s]))

    @pl.when(i == 0)
    def _():
        for cp in weight_copies(be_ref[0], 0):
            cp.start()

    @pl.when((i < nu_ref[0]) & ((i == 0) | (be_ref[i] != be_ref[jnp.maximum(i - 1, 0)])))
    def _():
        s = slot_ref[i]
        for cp in weight_copies(be_ref[i], s):
            cp.wait()
        wgu_bf[...] = wgu_f32[s].astype(BF16)
        wd_bf[...] = wd_f32[s].astype(BF16)
        nxt = nxt_ref[i]

        @pl.when(nxt >= 0)
        def _():
            for cp in weight_copies(nxt, 1 - s):
                cp.start()

    def mlp_rows(rows):
        lo, hi = _unpack_bf16_pair(xs_ref[0:rows, :])
        xb = jnp.concatenate([lo.astype(BF16), hi.astype(BF16)], axis=1)
        chunks = range(0, d_ff, ff_chunk)
        gus = []
        for c0 in chunks:
            glu = jnp.dot(xb, wgu_bf[:, c0:c0 + ff_chunk], preferred_element_type=F32) + bgu_ref[0, :, c0:c0 + ff_chunk]
            lin = (jnp.dot(xb, wgu_bf[:, d_ff + c0:d_ff + c0 + ff_chunk], preferred_element_type=F32)
                   + bgu_ref[0, :, d_ff + c0:d_ff + c0 + ff_chunk])
            gus.append((glu, lin))
        y = bd_ref[0]
        for c0, (glu, lin) in zip(chunks, gus):
            glu = jnp.minimum(glu, SWIGLU_LIMIT)
            lin = jnp.clip(lin, -SWIGLU_LIMIT, SWIGLU_LIMIT)
            act = (glu * jax.nn.sigmoid(SWIGLU_ALPHA * glu)) * (lin + 1.0)
            y = y + jnp.dot(act.astype(BF16), wd_bf[c0:c0 + ff_chunk, :], preferred_element_type=F32)
        yb = y.astype(BF16).astype(F32)
        half = y.shape[1] // 2
        ys_ref[0:rows, :] = _pack_bf16_pair(yb[:, :half], yb[:, half:])

    tm = xs_ref.shape[0]
    used = i < nu_ref[0]
    part = tm // EXPERT_ROW_PARTS
    parts_needed = jnp.maximum((nv_ref[i] + part - 1) // part, 1)
    for k in range(1, EXPERT_ROW_PARTS + 1):
        @pl.when(used & (parts_needed == k))
        def _(k=k):
            mlp_rows(k * part)
            if k < EXPERT_ROW_PARTS:
                ys_ref[k * part:, :] = jnp.zeros((tm - k * part, ys_ref.shape[1]), U32)


def _experts(xs, block_e, n_used, slot, next_e, n_valid, wgu, bgu, wd, bd, *, tm, n_blocks):
    half = xs.shape[1]
    p = n_blocks * tm
    e, d_model, two_ff = wgu.shape
    d_ff = two_ff // 2
    kern = functools.partial(_expert_kernel, d_ff=d_ff, ff_chunk=min(256, d_ff))
    blk = lambda i, be, nu, sl, nx, nv: (jnp.minimum(i, nu[0] - 1), 0)
    bias = lambda i, be, nu, sl, nx, nv: (be[i], 0, 0)
    grid_spec = pltpu.PrefetchScalarGridSpec(
        num_scalar_prefetch=5,
        grid=(n_blocks,),
        in_specs=[
            pl.BlockSpec((tm, half), blk),
            pl.BlockSpec(memory_space=pl.ANY),
            pl.BlockSpec((1, 1, two_ff), bias),
            pl.BlockSpec(memory_space=pl.ANY),
            pl.BlockSpec((1, 1, d_model), bias),
        ],
        out_specs=pl.BlockSpec((tm, half), blk),
        scratch_shapes=[pltpu.VMEM((2, d_model, two_ff), F32), pltpu.VMEM((2, d_ff, d_model), F32),
                        pltpu.VMEM((d_model, two_ff), BF16), pltpu.VMEM((d_ff, d_model), BF16),
                        pltpu.SemaphoreType.DMA((2,))],
    )
    return pl.pallas_call(
        kern,
        grid_spec=grid_spec,
        out_shape=jax.ShapeDtypeStruct((p, half), U32),
        compiler_params=_params(("arbitrary",)),
        name="experts",
    )(block_e, n_used, slot, next_e, n_valid, xs, wgu, bgu.reshape(e, 1, two_ff), wd, bd.reshape(e, 1, d_model))


def _final_kernel(yk_ref, gate_ref, x1_ref, g_ref, out_ref):
    tm, d_model = x1_ref.shape
    half = d_model // 2
    gates = gate_ref[...]
    acc_lo = jnp.zeros((tm, half), F32)
    acc_hi = jnp.zeros((tm, half), F32)
    for kk in range(TOP_K):
        lo, hi = _unpack_bf16_pair(yk_ref[kk])
        gk = gates[:, kk:kk + 1]
        acc_lo = acc_lo + gk * lo
        acc_hi = acc_hi + gk * hi
    x2 = x1_ref[...] + jnp.concatenate([acc_lo, acc_hi], axis=1)
    ms = jnp.mean(x2 * x2, axis=-1, keepdims=True)
    out_ref[...] = (x2 * lax.rsqrt(ms + RMS_EPS)) * g_ref[...]


def _final(yk, gates, x1, g_final, *, tm, tile0):
    n, d_model = x1.shape
    rows = lambda w: pl.BlockSpec((tm, w), lambda i: (i + tile0, 0))
    return pl.pallas_call(
        _final_kernel,
        grid=(yk.shape[1] // tm,),
        in_specs=[pl.BlockSpec((TOP_K, tm, d_model // 2), lambda i: (0, i, 0)), rows(LANES), rows(d_model),
                  pl.BlockSpec((1, d_model), lambda i: (0, 0))],
        out_specs=rows(d_model),
        out_shape=jax.ShapeDtypeStruct((n, d_model), F32),
        input_output_aliases={2: 0},
        compiler_params=_params(("arbitrary",)),
        name="final",
    )(yk, gates, x1, g_final)


def _tiles(seq):
    tm = min(512, seq)
    return dict(tm_proj=tm, t_attn=min(256, seq), tm_mix=tm, tm_expert=512, tm_final=tm, moe_groups=2)


def _forward(x, g_mix, w_in, conv_w, b_f, w_conv_o, w_attn_o, w_out, g_ffn, w_router, b_router,
             w_gate_up, b_gate_up, w_down, b_down, g_final, tiles):
    batch, seq, d_model = x.shape
    n = batch * seq
    conv_dim = conv_w.shape[1]
    attn_dim = w_attn_o.shape[0]
    heads = b_f.shape[0]
    head_dim = attn_dim // heads
    n_experts = w_router.shape[1]
    x2d = x.reshape(n, d_model)

    c0 = 3 * conv_dim
    a0 = c0 + 3 * attn_dim
    wa = w_in[:, :c0].astype(BF16)
    scale = LOG2E / (head_dim ** 0.5)
    wqk = jnp.concatenate([w_in[:, c0:c0 + attn_dim] * scale, w_in[:, c0 + attn_dim:c0 + 2 * attn_dim]],
                          axis=1).astype(BF16)
    wvt = jnp.pad(w_in[:, c0 + 2 * attn_dim:a0 + heads].T, ((0, FORGET_ROWS - heads), (0, 0))).astype(BF16)
    wg = w_in[:, a0 + heads:].astype(BF16)
    bf_col = jnp.pad(b_f, (0, FORGET_ROWS - heads)).reshape(FORGET_ROWS, 1)
    wr_hi = w_router.astype(BF16)
    wr_lo = (w_router - wr_hi.astype(F32)).astype(BF16)
    wr = jnp.concatenate([wr_hi, wr_lo], axis=1).T
    br = b_router.reshape(n_experts, 1)

    cb, ucx, qa, ka, vt, sgc, sga = _inproj(
        x2d, g_mix.reshape(1, d_model), wa, wqk, wvt, wg, bf_col,
        batch=batch, seq=seq, heads=heads, tm=tiles["tm_proj"], tk=tiles["t_attn"])
    o = _attention(qa, ka, vt, batch=batch, seq=seq, heads=heads, tq=tiles["t_attn"])
    groups = tiles["moe_groups"]
    ng = n // groups
    x1, h2p, topi, gates, rank, cnt = _mix(
        x2d, cb, ucx, o, sgc, sga, conv_w, w_conv_o.astype(BF16), w_attn_o.astype(BF16), w_out.astype(BF16),
        g_ffn.reshape(1, d_model), wr, br, seq=seq, n_experts=n_experts, tm=tiles["tm_mix"],
        tile0=0, nt=n // tiles["tm_mix"], tiles_per_group=ng // tiles["tm_mix"])

    tme = tiles["tm_expert"]
    n_blocks = (ng * TOP_K) // tme + n_experts
    p_rows = n_blocks * tme
    nchunks = ng // (SC_WORKERS * SC_CHUNK)
    jj = jnp.arange(tme, dtype=I32)[None, :]
    spare = p_rows + jnp.arange(n_experts, dtype=I32)[:, None] * tme + jj
    expert_ids = jnp.arange(n_experts, dtype=I32)[None, None, :]
    ys_groups, dest_groups = [], []
    for g in range(groups):
        counts = cnt[(g + 1) * (ng // tiles["tm_mix"]) - 1, :, 0].astype(I32)
        padded = ((counts + tme - 1) // tme) * tme
        pend = jnp.cumsum(padded)
        pstart = pend - padded
        n_used = (pend[-1] // tme).astype(I32).reshape(1)
        blk_start = jnp.minimum(jnp.arange(n_blocks, dtype=I32), n_used[0] - 1) * tme
        block_e = jnp.minimum(jnp.sum((pend[None, :] <= blk_start[:, None]).astype(I32), axis=1), n_experts - 1)
        tok = slice(g * ng, (g + 1) * ng)
        onehot = topi[:TOP_K, tok, None] == expert_ids
        dest = jnp.sum(jnp.where(onehot, pstart[None, None, :], 0), axis=2) + rank[:TOP_K, tok]
        dest4 = dest.reshape(TOP_K, SC_WORKERS, nchunks, SC_CHUNK).transpose(1, 2, 0, 3)
        pad_idx = jnp.where(jj < (padded - counts)[:, None], (pstart + counts)[:, None] + jj, spare)
        pad_idx = pad_idx.reshape(SC_WORKERS, (n_experts * tme) // (SC_WORKERS * SC_CHUNK), SC_CHUNK)
        xs = _sc_scatter_rows(h2p, dest4, pad_idx, p_rows + n_experts * tme, g * ng)
        eid = jnp.arange(n_experts, dtype=I32)
        active = padded > 0
        run_tbl = jnp.sum((eid[None, :] < eid[:, None]) & active[None, :], axis=1)
        later = (eid[None, :] > eid[:, None]) & active[None, :]
        next_tbl = jnp.min(jnp.where(later, eid[None, :], n_experts), axis=1)
        next_tbl = jnp.where(next_tbl < n_experts, next_tbl, -1)
        is_e = block_e[:, None] == eid[None, :]
        slot = lax.rem(jnp.sum(jnp.where(is_e, run_tbl[None, :], 0), axis=1), 2).astype(I32)
        next_e = jnp.sum(jnp.where(is_e, next_tbl[None, :], 0), axis=1).astype(I32)
        row_in_e = jnp.arange(n_blocks, dtype=I32) * tme - jnp.sum(jnp.where(is_e, pstart[None, :], 0), axis=1)
        n_valid = jnp.clip(jnp.sum(jnp.where(is_e, counts[None, :], 0), axis=1) - row_in_e, 0, tme).astype(I32)
        ys_groups.append(_experts(xs, block_e, n_used, slot, next_e, n_valid, w_gate_up, b_gate_up, w_down,
                                  b_down, tm=tme, n_blocks=n_blocks))
        dest_groups.append(dest4)

    out = x1
    for g in range(groups):
        yk = _sc_gather_rows(ys_groups[g], dest_groups[g], ng)
        out = _final(yk, gates, out, g_final.reshape(1, d_model), tm=tiles["tm_final"],
                     tile0=g * (ng // tiles["tm_final"]))
    return out.reshape(batch, seq, d_model)


def kernel(x, g_mix, w_in, conv_w, b_f, w_conv_o, w_attn_o, w_out, g_ffn, w_router, b_router, w_gate_up,
           b_gate_up, w_down, b_down, g_final):
    return _forward(x, g_mix, w_in, conv_w, b_f, w_conv_o, w_attn_o, w_out, g_ffn, w_router, b_router,
                    w_gate_up, b_gate_up, w_down, b_down, g_final, _tiles(x.shape[1]))
```

```python
import functools

import jax
import jax.numpy as jnp
from jax import lax
from jax.experimental import pallas as pl
from jax.experimental.pallas import tpu as pltpu
from jax.experimental.pallas import tpu_sc as plsc

TOP_K = 4
RMS_EPS = 1e-5
SWIGLU_ALPHA = 1.702
SWIGLU_LIMIT = 7.0
LOG2E = 1.4426950408889634

LANES = 128
SUBLANES = 8
BF16_SUBLANES = 16
V_ONES_ROWS = BF16_SUBLANES
EXPERT_ROW_PARTS = 4
FORGET_ROWS = BF16_SUBLANES
VMEM_LIMIT_BYTES = 56 * 1024 * 1024

F32 = jnp.float32
BF16 = jnp.bfloat16
U32 = jnp.uint32
I32 = jnp.int32
HI_MASK = 0xFFFF0000


def _params(sem):
    return pltpu.CompilerParams(dimension_semantics=sem, vmem_limit_bytes=VMEM_LIMIT_BYTES)


def _as_words(x_bf16):
    return pltpu.bitcast(x_bf16, U32)


def _as_bf16(w_u32):
    return pltpu.bitcast(w_u32, BF16)


def _pack_bf16_pair(lo_f32, hi_f32):
    lo = lax.bitcast_convert_type(lo_f32, U32)
    hi = lax.bitcast_convert_type(hi_f32, U32)
    return (lo >> 16) | (hi & U32(HI_MASK))


def _unpack_bf16_pair(w):
    lo = lax.bitcast_convert_type(w << 16, F32)
    hi = lax.bitcast_convert_type(w & U32(HI_MASK), F32)
    return lo, hi


def _inproj_kernel(x_ref, g_ref, wa_ref, wqk_ref, wvt_ref, wg_ref, bf_ref,
                   cb_ref, ucx_ref, qa_ref, ka_ref, vt_ref, sgc_ref, sga_ref,
                   carry_ref, *, tiles_per_batch, conv_dim, attn_dim, d_model, heads, tk):
    tm = x_ref.shape[0]
    t = lax.rem(pl.program_id(0), tiles_per_batch)
    x = x_ref[...]
    ms = jnp.mean(x * x, axis=-1, keepdims=True)
    h = ((x * lax.rsqrt(ms + RMS_EPS)) * g_ref[...]).astype(BF16)

    def mm(w_ref, c0, n):
        return jnp.dot(h, w_ref[:, c0:c0 + n], preferred_element_type=F32)

    vtf =lax.dot_general(wvt_ref[...], h, (((1,), (1,)), ((), ())), preferred_element_type=F32)
    qf = mm(wqk_ref, 0, attn_dim)
    kf = mm(wqk_ref, attn_dim, attn_dim)
    cb = mm(wa_ref, 0, conv_dim)
    ucx = mm(wa_ref, conv_dim, conv_dim) * mm(wa_ref, 2 * conv_dim, conv_dim)

    z = vtf[attn_dim:attn_dim + FORGET_ROWS] + bf_ref[...]
    sub = lax.broadcasted_iota(I32, z.shape, 0)
    logf = jnp.where(sub < heads, jnp.minimum(z, 0.0) - jnp.log1p(jnp.exp(-jnp.abs(z))), 0.0)
    p1 = logf.astype(BF16).astype(F32)
    r1 = logf - p1
    p2 = r1.astype(BF16).astype(F32)
    p3 = (r1 - p2).astype(BF16).astype(F32)
    packed = jnp.concatenate([p1, p2, p3], axis=0).astype(BF16)
    row = lax.broadcasted_iota(I32, (tm, tm), 0)
    col = lax.broadcasted_iota(I32, (tm, tm), 1)
    tri = jnp.where(row <= col, 1.0, 0.0).astype(BF16)
    r = jnp.dot(packed, tri, preferred_element_type=F32)
    local = r[:FORGET_ROWS] + r[FORGET_ROWS:2 * FORGET_ROWS] + r[2 * FORGET_ROWS:]

    @pl.when(t == 0)
    def _():
        carry_ref[...] = jnp.zeros_like(carry_ref)

    c = local + carry_ref[:, 0:1]
    carry_ref[...] = jnp.broadcast_to(c[:, tm - 1:tm], carry_ref.shape)

    c2 = c * LOG2E
    c_hi_t = c2.astype(BF16).astype(F32)
    c_r = c2 - c_hi_t
    c_mid_t = c_r.astype(BF16).astype(F32)
    c_lo_t = (c_r - c_mid_t).astype(BF16).astype(F32)
    split = (c_hi_t, c_mid_t, c_lo_t)
    pad = jnp.zeros((LANES - len(split) * FORGET_ROWS, tm), F32)
    parts = jnp.concatenate(split + (pad,), axis=0).T
    head_dim = attn_dim // heads
    ext = LANES - head_dim
    lane_e = lax.broadcasted_iota(I32, (tm, ext), 1)
    ones = jnp.where(lane_e < len(split), 1.0, 0.0)
    for hh in range(heads):
        kext = jnp.zeros((tm, ext), F32)
        for p in range(len(split)):
            col_p = parts[:, p * FORGET_ROWS + hh:p * FORGET_ROWS + hh + 1]
            kext = jnp.where(lane_e == p, -col_p, kext)
        sl = slice(hh * head_dim, (hh + 1) * head_dim)
        qa_ref[:, hh * LANES:(hh + 1) * LANES] = _as_words(jnp.concatenate([qf[:, sl], ones], axis=1).astype(BF16))
        ka_ref[:, hh * LANES:(hh + 1) * LANES] = _as_words(jnp.concatenate([kf[:, sl], kext], axis=1).astype(BF16))

    cb_ref[...] = _as_words(cb.astype(BF16))
    ucx_ref[...] = _as_words(ucx.astype(BF16))
    half = d_model // 2
    for c in range(2):
        sgc_ref[:, c * half:(c + 1) * half] = _as_words(jax.nn.sigmoid(mm(wg_ref, c * half, half)).astype(BF16))
        sga_ref[:, c * half:(c + 1) * half] = _as_words(
            jax.nn.sigmoid(mm(wg_ref, d_model + c * half, half)).astype(BF16))

    vt = vtf[:attn_dim].astype(BF16)
    ones_rows = jnp.ones((V_ONES_ROWS, tm), BF16)
    vt = jnp.concatenate([piece for hh in range(heads)
                          for piece in (vt[hh * head_dim:(hh + 1) * head_dim], ones_rows)], axis=0)
    for u in range(tm // tk):
        vt_ref[u] = _as_words(vt[:, u * tk:(u + 1) * tk])


def _inproj(x2d, g_mix, wa, wqk, wvt, wg, bf_col, *, batch, seq, heads, tm, tk):
    n, d_model = x2d.shape
    conv_dim = wa.shape[1] // 3
    attn_dim = wvt.shape[0] - FORGET_ROWS
    assert heads <= FORGET_ROWS
    tiles_per_batch = seq // tm
    grid = (n // tm,)
    full = lambda shape: pl.BlockSpec(shape, lambda i: (0,) * len(shape))
    rows = lambda w: pl.BlockSpec((tm, w), lambda i: (i, 0))
    kern = functools.partial(_inproj_kernel, tiles_per_batch=tiles_per_batch, conv_dim=conv_dim,
                             attn_dim=attn_dim, d_model=d_model, heads=heads, tk=tk)
    vt_rows = attn_dim + heads * V_ONES_ROWS
    words = lambda w: jax.ShapeDtypeStruct((n // 2, w), U32)
    wrows = lambda w: pl.BlockSpec((tm // 2, w), lambda i: (i, 0))
    out_shape = (
        words(conv_dim),
        words(conv_dim),
        words(heads * LANES),
        words(heads * LANES),
        jax.ShapeDtypeStruct((n // tk, vt_rows // 2, tk), U32),
        words(d_model),
        words(d_model),
    )
    out_specs = (
        wrows(conv_dim), wrows(conv_dim), wrows(heads * LANES), wrows(heads * LANES),
        pl.BlockSpec((tm // tk, vt_rows // 2, tk), lambda i: (i, 0, 0)),
        wrows(d_model), wrows(d_model),
    )
    return pl.pallas_call(
        kern,
        grid=grid,
        in_specs=[rows(d_model), full((1, d_model)), full(wa.shape), full(wqk.shape), full(wvt.shape),
                  full(wg.shape), full((FORGET_ROWS, 1))],
        out_specs=out_specs,
        out_shape=out_shape,
        scratch_shapes=[pltpu.VMEM((FORGET_ROWS, LANES), F32)],
        compiler_params=_params(("arbitrary",)),
        name="inproj",
    )(x2d, g_mix, wa, wqk, wvt, wg, bf_col)


def _attn_kernel(q_ref, k_ref, v_ref, o_ref, s_ref, *, tq, tk, head_dim, heads):
    qi = pl.program_id(1)
    row = lax.broadcasted_iota(I32, (tk, tq), 0)
    col = lax.broadcasted_iota(I32, (tk, tq), 1)

    def score_tile(j, slot):
        off = pl.multiple_of(j * (tk // 2), tk // 2)
        for hh in range(heads):
            qa = _as_bf16(q_ref[:, hh * LANES:(hh + 1) * LANES])
            ka = _as_bf16(k_ref[pl.ds(off, tk // 2), hh * LANES:(hh + 1) * LANES])
            s_ref[slot, hh] = lax.dot_general(ka, qa, (((1,), (1,)), ((), ())), preferred_element_type=F32)

    def block(j, slot, carry, masked):
        if not masked:
            score_tile(j + 1, 1 - slot)
        stats = []
        for hh in range(heads):
            m, _ = carry[hh]
            s = s_ref[slot, hh]
            if masked:
                s = jnp.where(row <= col, s, -jnp.inf)
            m_new = jnp.maximum(m, jnp.max(s, axis=0, keepdims=True))
            stats.append((m_new, jnp.exp2(m - m_new), jnp.exp2(s - m_new).astype(BF16)))
        out = []
        for hh in range(heads):
            m_new, alpha, p = stats[hh]
            vth = _as_bf16(v_ref[j, hh * (vrows // 2):(hh + 1) * (vrows // 2), :])
            out.append((m_new, alpha * carry[hh][1] + jnp.dot(vth, p, preferred_element_type=F32)))
        return tuple(out)

    vrows = head_dim + V_ONES_ROWS
    init = tuple((jnp.full((1, tq), -jnp.inf, F32), jnp.zeros((vrows, tq), F32)) for _ in range(heads))
    score_tile(0, 0)

    def pair(p, c):
        return block(2 * p + 1, 1, block(2 * p, 0, c, False), False)

    carry = lax.fori_loop(0, qi // 2, pair, init)
    final = lax.cond(
        lax.rem(qi, 2) == 0,
        lambda c: block(qi, 0, c, True),
        lambda c: block(qi, 1, block(qi - 1, 0, c, False), True),
        carry)
    per_group = LANES // head_dim
    for g in range(heads // per_group):
        accs = [final[g * per_group + u][1] for u in range(per_group)]
        ot = jnp.concatenate([a[:head_dim] / a[head_dim:head_dim + 1] for a in accs],
                             axis=0)
        o_ref[:, g * LANES:(g + 1) * LANES] = _as_words(ot.T.astype(BF16))


def _attention(qa, ka, vt, *, batch, seq, heads, tq):
    n = 2 * qa.shape[0]
    vt_rows, tk = 2 * vt.shape[1], vt.shape[2]
    head_dim = vt_rows // heads - V_ONES_ROWS
    attn_dim = heads * head_dim
    assert LANES % head_dim == 0 and tq == tk
    nq = seq // tq
    nk = seq // tk
    kern = functools.partial(_attn_kernel, tq=tq, tk=tk, head_dim=head_dim, heads=heads)
    return pl.pallas_call(
        kern,
        grid=(batch, nq),
        in_specs=[
            pl.BlockSpec((tq // 2, heads * LANES), lambda b, qi: (b * nq + qi, 0)),
            pl.BlockSpec((seq // 2, heads * LANES), lambda b, qi: (b, 0)),
            pl.BlockSpec((nk, vt_rows // 2, tk), lambda b, qi: (b, 0, 0)),
        ],
        out_specs=pl.BlockSpec((tq // 2, attn_dim), lambda b, qi: (b * nq + qi, 0)),
        out_shape=jax.ShapeDtypeStruct((n // 2, attn_dim), U32),
        scratch_shapes=[pltpu.VMEM((2, heads, tk, tq), F32)],
        compiler_params=_params(("arbitrary", "arbitrary")),
        name="attn",
    )(qa, ka, vt)


def _mix_kernel(x_ref, cb_ref, u_ref, uh_ref, o_ref, sgc_ref, sga_ref, cw_ref, wco_ref, wao_ref, wout_ref,
                gffn_ref, wr_ref, br_ref,
                x1_ref, h2p_ref, topi_ref, gate_ref, rank_ref, cnt_ref, carry_ref, lg_scr,
                *, tiles_per_batch, tiles_per_group, n_experts, nt):
    tm, d_model = x_ref.shape
    i = pl.program_id(0)

    @pl.when(i == 0)
    def _():
        lg_scr[...] = jnp.zeros_like(lg_scr)

    @pl.when((i == 0) | (lax.rem(i - 1, tiles_per_group) == 0))
    def _():
        carry_ref[...] = jnp.zeros_like(carry_ref)

    lg = lg_scr[...]
    sub = lax.broadcasted_iota(I32, (n_experts, tm), 0)

    vals, idxs = [], []
    for _ in range(TOP_K):
        m = jnp.max(lg, axis=0, keepdims=True)
        idx = jnp.min(jnp.where(lg == m, sub, n_experts), axis=0, keepdims=True)
        vals.append(m)
        idxs.append(idx)
        lg = jnp.where(sub == idx, -jnp.inf, lg)
    es = [jnp.exp(vk - vals[0]) for vk in vals]
    denom = es[0] + es[1] + es[2] + es[3]

    sub8 = lax.broadcasted_iota(I32, (SUBLANES, tm), 0)
    subl = lax.broadcasted_iota(I32, (LANES, tm), 0)
    chosen = jnp.zeros((n_experts, tm), F32)
    topi = jnp.zeros((SUBLANES, tm), I32)
    gates_t = jnp.zeros((LANES, tm), F32)
    for kk in range(TOP_K):
        chosen = jnp.where(sub == idxs[kk], 1.0, chosen)
        topi = jnp.where(sub8 == kk, idxs[kk], topi)
        gates_t = jnp.where(subl == kk, es[kk] / denom, gates_t)
    topi_ref[...] = topi
    gate_ref[...] = gates_t.T

    first = lax.rem(jnp.minimum(i, nt - 1), tiles_per_batch) == 0

    u = _as_bf16(u_ref[...]).astype(F32)
    halo = jnp.where(first, 0.0, _as_bf16(uh_ref[...]).astype(F32))
    ext = jnp.concatenate([halo, u], axis=0)
    hs = halo.shape[0]
    u1 = pltpu.roll(ext, 1, 0)[hs:]
    u2 = pltpu.roll(ext, 2, 0)[hs:]
    cw = cw_ref[...]
    conv = cw[0:1, :] * u2 + cw[1:2, :] * u1 + cw[2:3, :] * u
    yc = (_as_bf16(cb_ref[...]).astype(F32) * conv).astype(BF16)
    y_conv = jnp.dot(yc, wco_ref[...], preferred_element_type=F32)
    y_attn = jnp.dot(_as_bf16(o_ref[...]), wao_ref[...], preferred_element_type=F32)
    mixed = (_as_bf16(sgc_ref[...]).astype(F32) * y_conv
             + _as_bf16(sga_ref[...]).astype(F32) * y_attn).astype(BF16)
    x1 = x_ref[...] + jnp.dot(mixed, wout_ref[...], preferred_element_type=F32)
    x1_ref[...] = x1

    ms = jnp.mean(x1 * x1, axis=-1, keepdims=True)
    h2 = (x1 * lax.rsqrt(ms + RMS_EPS)) * gffn_ref[...]
    hb = h2.astype(BF16)
    hbf = hb.astype(F32)
    half = d_model // 2
    h2p_ref[...] = _pack_bf16_pair(hbf[:, :half], hbf[:, half:])

    hlo = (h2 - hbf).astype(BF16)
    nt_dims = (((1,), (1,)), ((), ()))
    rt = (lax.dot_general(wr_ref[...], hb, nt_dims, preferred_element_type=F32)
          + lax.dot_general(wr_ref[...], hlo, nt_dims, preferred_element_type=F32))
    lg_scr[...] = rt[:n_experts] + rt[n_experts:] + br_ref[...]

    row = lax.broadcasted_iota(I32, (tm, tm), 0)
    col = lax.broadcasted_iota(I32, (tm, tm), 1)
    tri = jnp.where(row < col, 1.0, 0.0).astype(BF16)
    before = jnp.dot(chosen.astype(BF16), tri, preferred_element_type=F32) + carry_ref[:, 0:1]
    rank = jnp.zeros((SUBLANES, tm), F32)
    for kk in range(TOP_K):
        rk = jnp.sum(jnp.where(sub == idxs[kk], before, 0.0), axis=0, keepdims=True)
        rank = jnp.where(sub8 == kk, rk, rank)
    rank_ref[...] = rank.astype(I32)
    total = carry_ref[...] + jnp.sum(chosen, axis=1, keepdims=True)
    carry_ref[...] = total
    cnt_ref[0] = total


def _mix(x2d, cb, ucx, o, sgc, sga, conv_w, wco, wao, wout, g_ffn, wr, br, *, seq, n_experts, tm, tile0, nt,
         tiles_per_group):
    d_model = x2d.shape[1]
    n = nt * tm
    conv_dim = cb.shape[1]
    attn_dim = o.shape[1]
    tiles_per_batch = seq // tm
    assert tile0 % tiles_per_batch == 0
    hb = tm // BF16_SUBLANES
    full = lambda shape: pl.BlockSpec(shape, lambda i: (0,) * len(shape))
    cur = lambda i: jnp.minimum(i, nt - 1)
    prev = lambda i: jnp.maximum(i - 1, 0)
    rows_in = lambda w: pl.BlockSpec((tm, w), lambda i: (cur(i) + tile0, 0))
    words_in = lambda w: pl.BlockSpec((tm // 2, w), lambda i: (cur(i) + tile0, 0))
    rows = lambda w: pl.BlockSpec((tm, w), lambda i: (cur(i), 0))
    kern = functools.partial(_mix_kernel, tiles_per_batch=tiles_per_batch, tiles_per_group=tiles_per_group,
                             n_experts=n_experts, nt=nt)
    out_shape = (
        jax.ShapeDtypeStruct((n, d_model), F32),
        jax.ShapeDtypeStruct((n, d_model // 2), U32),
        jax.ShapeDtypeStruct((SUBLANES, n), I32),
        jax.ShapeDtypeStruct((n, LANES), F32),
        jax.ShapeDtypeStruct((SUBLANES, n), I32),
        jax.ShapeDtypeStruct((nt, n_experts, LANES), F32),
    )
    cols = pl.BlockSpec((SUBLANES, tm), lambda i: (0, prev(i)))
    out_specs = (rows(d_model), rows(d_model // 2), cols, pl.BlockSpec((tm, LANES), lambda i: (prev(i), 0)), cols,
                 pl.BlockSpec((1, n_experts, LANES), lambda i: (prev(i), 0, 0)))
    return pl.pallas_call(
        kern,
        grid=(nt + 1,),
        in_specs=[rows_in(d_model), words_in(conv_dim), words_in(conv_dim),
                  pl.BlockSpec((BF16_SUBLANES // 2, conv_dim),
                               lambda i: (jnp.maximum((cur(i) + tile0) * hb - 1, 0), 0)),
                  words_in(attn_dim), words_in(d_model), words_in(d_model),
                  full(conv_w.shape), full(wco.shape), full(wao.shape), full(wout.shape),
                  full((1, d_model)), full(wr.shape), full((n_experts, 1))],
        out_specs=out_specs,
        out_shape=out_shape,
        scratch_shapes=[pltpu.VMEM((n_experts, LANES), F32), pltpu.VMEM((n_experts, tm), F32)],
        compiler_params=_params(("arbitrary",)),
        name="mix",
    )(x2d, cb, ucx, ucx, o, sgc, sga, conv_w, wco, wao, wout, g_ffn, wr, br)


SC_CORES = 2
SC_SUBCORES = 16
SC_WORKERS = SC_CORES * SC_SUBCORES
SC_CHUNK = 64


def _sc_mesh():
    return plsc.VectorSubcoreMesh(core_axis_name="c", subcore_axis_name="s",
                                  num_cores=SC_CORES, num_subcores=SC_SUBCORES)


def _sc_worker():
    return lax.axis_index("s") * SC_CORES + lax.axis_index("c")


def _sc_scatter_rows(src, dest4, pad_idx, n_out, row0):
    width = src.shape[1]
    nchunks = dest4.shape[1]
    npad = pad_idx.shape[1]

    @functools.partial(
        pl.kernel, mesh=_sc_mesh(),
        out_type=jax.ShapeDtypeStruct((n_out, width), src.dtype),
        scratch_types=[pltpu.VMEM((TOP_K, SC_CHUNK), I32), pltpu.VMEM((SC_CHUNK, width), src.dtype),
                       pltpu.VMEM((npad, SC_CHUNK), I32), pltpu.SemaphoreType.DMA],
        name="sc_scatter_rows",
    )
    def k(src_hbm, dest_hbm, pad_hbm, out_hbm, idx_v, rows_v, pad_v, sem):
        wid = _sc_worker()
        base = row0 + wid * (nchunks * SC_CHUNK)

        @pl.loop(0, nchunks)
        def _(j):
            pltpu.sync_copy(src_hbm.at[pl.ds(base + j * SC_CHUNK, SC_CHUNK)], rows_v)
            pltpu.sync_copy(dest_hbm.at[wid, j], idx_v)
            copies = [pltpu.async_copy(rows_v, out_hbm.at[idx_v.at[kk]], sem) for kk in range(TOP_K)]
            for cp in copies:
                cp.wait()

        pltpu.sync_copy(pad_hbm.at[wid], pad_v)
        fills = [pltpu.async_copy(rows_v, out_hbm.at[pad_v.at[p]], sem) for p in range(npad)]
        for cp in fills:
            cp.wait()

    return k(src, dest4, pad_idx)


def _sc_gather_rows(src, dest4, n):
    width = src.shape[1]
    nchunks = dest4.shape[1]

    @functools.partial(
        pl.kernel, mesh=_sc_mesh(),
        out_type=jax.ShapeDtypeStruct((TOP_K, n, width), src.dtype),
        scratch_types=[pltpu.VMEM((TOP_K, SC_CHUNK), I32), pltpu.VMEM((SC_CHUNK, width), src.dtype),
                       pltpu.VMEM((SC_CHUNK, width), src.dtype), pltpu.SemaphoreType.DMA, pltpu.SemaphoreType.DMA],
        name="sc_gather_rows",
    )
    def k(src_hbm, dest_hbm, out_hbm, idx_v, rows_a, rows_b, sem_a, sem_b):
        wid = _sc_worker()
        base = wid * (nchunks * SC_CHUNK)
        bufs = ((rows_a, sem_a), (rows_b, sem_b))

        @pl.loop(0, nchunks)
        def _(j):
            pltpu.sync_copy(dest_hbm.at[wid, j], idx_v)
            pending = pltpu.async_copy(src_hbm.at[idx_v.at[0]], rows_a, sem_a)
            for kk in range(TOP_K):
                buf, _ = bufs[kk % 2]
                pending.wait()
                if kk + 1 < TOP_K:
                    nbuf, nsem = bufs[(kk + 1) % 2]
                    pending = pltpu.async_copy(src_hbm.at[idx_v.at[kk + 1]], nbuf, nsem)
                pltpu.sync_copy(buf, out_hbm.at[kk, pl.ds(base + j * SC_CHUNK, SC_CHUNK)])

    return k(src, dest4)


def _expert_kernel(be_ref, nu_ref, slot_ref, nxt_ref, nv_ref, xs_ref, wgu_hbm, bgu_ref, wd_hbm, bd_ref, ys_ref,
                   wgu_f32, wd_f32, wgu_bf, wd_bf, sem, *, d_ff, ff_chunk):
    i = pl.program_id(0)

    def weight_copies(e, s):
        return (pltpu.make_async_copy(wgu_hbm.at[e], wgu_f32.at[s], sem.at[s]),
                pltpu.make_async_copy(wd_hbm.at[e], wd_f32.at[s], sem.at[s]))

    @pl.when(i == 0)
    def _():
        for cp in weight_copies(be_ref[0], 0):
            cp.start()

    @pl.when((i < nu_ref[0]) & ((i == 0) | (be_ref[i] != be_ref[jnp.maximum(i - 1, 0)])))
    def _():
        s = slot_ref[i]
        for cp in weight_copies(be_ref[i], s):
            cp.wait()
        wgu_bf[...] = wgu_f32[s].astype(BF16)
        wd_bf[...] = wd_f32[s].astype(BF16)
        nxt = nxt_ref[i]

        @pl.when(nxt >= 0)
        def _():
            for cp in weight_copies(nxt, 1 - s):
                cp.start()

    def mlp_rows(rows):
        lo, hi = _unpack_bf16_pair(xs_ref[0:rows, :])
        xb = jnp.concatenate([lo.astype(BF16), hi.astype(BF16)], axis=1)
        chunks = range(0, d_ff, ff_chunk)
        gus = []
        for c0 in chunks:
            glu = jnp.dot(xb, wgu_bf[:, c0:c0 + ff_chunk], preferred_element_type=F32) + bgu_ref[0, :, c0:c0 + ff_chunk]
            lin = (jnp.dot(xb, wgu_bf[:, d_ff + c0:d_ff + c0 + ff_chunk], preferred_element_type=F32)
                   + bgu_ref[0, :, d_ff + c0:d_ff + c0 + ff_chunk])
            gus.append((glu, lin))
        y = bd_ref[0]
        for c0, (glu, lin) in zip(chunks, gus):
            glu = jnp.minimum(glu, SWIGLU_LIMIT)
            lin = jnp.clip(lin, -SWIGLU_LIMIT, SWIGLU_LIMIT)
            act = (glu * jax.nn.sigmoid(SWIGLU_ALPHA * glu)) * (lin + 1.0)
            y = y + jnp.dot(act.astype(BF16), wd_bf[c0:c0 + ff_chunk, :], preferred_element_type=F32)
        yb = y.astype(BF16).astype(F32)
        half = y.shape[1] // 2
        ys_ref[0:rows, :] = _pack_bf16_pair(yb[:, :half], yb[:, half:])

    tm = xs_ref.shape[0]
    used = i < nu_ref[0]
    part = tm // EXPERT_ROW_PARTS
    parts_needed = jnp.maximum((nv_ref[i] + part - 1) // part, 1)
    for k in range(1, EXPERT_ROW_PARTS + 1):
        @pl.when(used & (parts_needed == k))
        def _(k=k):
            mlp_rows(k * part)
            if k < EXPERT_ROW_PARTS:
                ys_ref[k * part:, :] = jnp.zeros((tm - k * part, ys_ref.shape[1]), U32)


def _experts(xs, block_e, n_used, slot, next_e, n_valid, wgu, bgu, wd, bd, *, tm, n_blocks):
    half = xs.shape[1]
    p = n_blocks * tm
    e, d_model, two_ff = wgu.shape
    d_ff = two_ff // 2
    kern = functools.partial(_expert_kernel, d_ff=d_ff, ff_chunk=min(256, d_ff))
    blk = lambda i, be, nu, sl, nx, nv: (jnp.minimum(i, nu[0] - 1), 0)
    bias = lambda i, be, nu, sl, nx, nv: (be[i], 0, 0)
    grid_spec = pltpu.PrefetchScalarGridSpec(
        num_scalar_prefetch=5,
        grid=(n_blocks,),
        in_specs=[
            pl.BlockSpec((tm, half), blk),
            pl.BlockSpec(memory_space=pl.ANY),
            pl.BlockSpec((1, 1, two_ff), bias),
            pl.BlockSpec(memory_space=pl.ANY),
            pl.BlockSpec((1, 1, d_model), bias),
        ],
        out_specs=pl.BlockSpec((tm, half), blk),
        scratch_shapes=[pltpu.VMEM((2, d_model, two_ff), F32), pltpu.VMEM((2, d_ff, d_model), F32),
                        pltpu.VMEM((d_model, two_ff), BF16), pltpu.VMEM((d_ff, d_model), BF16),
                        pltpu.SemaphoreType.DMA((2,))],
    )
    return pl.pallas_call(
        kern,
        grid_spec=grid_spec,
        out_shape=jax.ShapeDtypeStruct((p, half), U32),
        compiler_params=_params(("arbitrary",)),
        name="experts",
    )(block_e, n_used, slot, next_e, n_valid, xs, wgu, bgu.reshape(e, 1, two_ff), wd, bd.reshape(e, 1, d_model))


def _final_kernel(yk_ref, gate_ref, x1_ref, g_ref, out_ref):
    tm, d_model = x1_ref.shape
    half = d_model // 2
    gates = gate_ref[...]
    acc_lo = jnp.zeros((tm, half), F32)
    acc_hi = jnp.zeros((tm, half), F32)
    for kk in range(TOP_K):
        lo, hi = _unpack_bf16_pair(yk_ref[kk])
        gk = gates[:, kk:kk + 1]
        acc_lo = acc_lo + gk * lo
        acc_hi = acc_hi + gk * hi
    x2 = x1_ref[...] + jnp.concatenate([acc_lo, acc_hi], axis=1)
    ms = jnp.mean(x2 * x2, axis=-1, keepdims=True)
    out_ref[...] = (x2 * lax.rsqrt(ms + RMS_EPS)) * g_ref[...]


def _final(yk, gates, x1, g_final, *, tm, tile0):
    n, d_model = x1.shape
    rows = lambda w: pl.BlockSpec((tm, w), lambda i: (i + tile0, 0))
    return pl.pallas_call(
        _final_kernel,
        grid=(yk.shape[1] // tm,),
        in_specs=[pl.BlockSpec((TOP_K, tm, d_model // 2), lambda i: (0, i, 0)), rows(LANES), rows(d_model),
                  pl.BlockSpec((1, d_model), lambda i: (0, 0))],
        out_specs=rows(d_model),
        out_shape=jax.ShapeDtypeStruct((n, d_model), F32),
        input_output_aliases={2: 0},
        compiler_params=_params(("arbitrary",)),
        name="final",
    )(yk, gates, x1, g_final)


def _tiles(seq):
    tm = min(512, seq)
    return dict(tm_proj=tm, t_attn=min(256, seq), tm_mix=tm, tm_expert=512, tm_final=tm, moe_groups=4)


def _forward(x, g_mix, w_in, conv_w, b_f, w_conv_o, w_attn_o, w_out, g_ffn, w_router, b_router,
             w_gate_up, b_gate_up, w_down, b_down, g_final, tiles):
    batch, seq, d_model = x.shape
    n = batch * seq
    conv_dim = conv_w.shape[1]
    attn_dim = w_attn_o.shape[0]
    heads = b_f.shape[0]
    head_dim = attn_dim // heads
    n_experts = w_router.shape[1]
    x2d = x.reshape(n, d_model)

    c0 = 3 * conv_dim
    a0 = c0 + 3 * attn_dim
    wa = w_in[:, :c0].astype(BF16)
    scale = LOG2E / (head_dim ** 0.5)
    wqk = jnp.concatenate([w_in[:, c0:c0 + attn_dim] * scale, w_in[:, c0 + attn_dim:c0 + 2 * attn_dim]],
                          axis=1).astype(BF16)
    wvt = jnp.pad(w_in[:, c0 + 2 * attn_dim:a0 + heads].T, ((0, FORGET_ROWS - heads), (0, 0))).astype(BF16)
    wg = w_in[:, a0 + heads:].astype(BF16)
    bf_col = jnp.pad(b_f, (0, FORGET_ROWS - heads)).reshape(FORGET_ROWS, 1)
    wr_hi = w_router.astype(BF16)
    wr_lo = (w_router - wr_hi.astype(F32)).astype(BF16)
    wr = jnp.concatenate([wr_hi, wr_lo], axis=1).T
    br = b_router.reshape(n_experts, 1)

    cb, ucx, qa, ka, vt, sgc, sga = _inproj(
        x2d, g_mix.reshape(1, d_model), wa, wqk, wvt, wg, bf_col,
        batch=batch, seq=seq, heads=heads, tm=tiles["tm_proj"], tk=tiles["t_attn"])
    o = _attention(qa, ka, vt, batch=batch, seq=seq, heads=heads, tq=tiles["t_attn"])
    groups = tiles["moe_groups"]
    ng = n // groups
    x1, h2p, topi, gates, rank, cnt = _mix(
        x2d, cb, ucx, o, sgc, sga, conv_w, w_conv_o.astype(BF16), w_attn_o.astype(BF16), w_out.astype(BF16),
        g_ffn.reshape(1, d_model), wr, br, seq=seq, n_experts=n_experts, tm=tiles["tm_mix"],
        tile0=0, nt=n // tiles["tm_mix"], tiles_per_group=ng // tiles["tm_mix"])

    tme = tiles["tm_expert"]
    n_blocks = (ng * TOP_K) // tme + n_experts
    p_rows = n_blocks * tme
    nchunks = ng // (SC_WORKERS * SC_CHUNK)
    jj = jnp.arange(tme, dtype=I32)[None, :]
    spare = p_rows + jnp.arange(n_experts, dtype=I32)[:, None] * tme + jj
    expert_ids = jnp.arange(n_experts, dtype=I32)[None, None, :]
    ys_groups, dest_groups = [], []
    for g in range(groups):
        counts = cnt[(g + 1) * (ng // tiles["tm_mix"]) - 1, :, 0].astype(I32)
        padded = ((counts + tme - 1) // tme) * tme
        pend = jnp.cumsum(padded)
        pstart = pend - padded
        n_used = (pend[-1] // tme).astype(I32).reshape(1)
        blk_start = jnp.minimum(jnp.arange(n_blocks, dtype=I32), n_used[0] - 1) * tme
        block_e = jnp.minimum(jnp.sum((pend[None, :] <= blk_start[:, None]).astype(I32), axis=1), n_experts - 1)
        tok = slice(g * ng, (g + 1) * ng)
        onehot = topi[:TOP_K, tok, None] == expert_ids
        dest = jnp.sum(jnp.where(onehot, pstart[None, None, :], 0), axis=2) + rank[:TOP_K, tok]
        dest4 = dest.reshape(TOP_K, SC_WORKERS, nchunks, SC_CHUNK).transpose(1, 2, 0, 3)
        pad_idx = jnp.where(jj < (padded - counts)[:, None], (pstart + counts)[:, None] + jj, spare)
        pad_idx = pad_idx.reshape(SC_WORKERS, (n_experts * tme) // (SC_WORKERS * SC_CHUNK), SC_CHUNK)
        xs = _sc_scatter_rows(h2p, dest4, pad_idx, p_rows + n_experts * tme, g * ng)
        eid = jnp.arange(n_experts, dtype=I32)
        active = padded > 0
        run_tbl = jnp.sum((eid[None, :] < eid[:, None]) & active[None, :], axis=1)
        later = (eid[None, :] > eid[:, None]) & active[None, :]
        next_tbl = jnp.min(jnp.where(later, eid[None, :], n_experts), axis=1)
        next_tbl = jnp.where(next_tbl < n_experts, next_tbl, -1)
        is_e = block_e[:, None] == eid[None, :]
        slot = lax.rem(jnp.sum(jnp.where(is_e, run_tbl[None, :], 0), axis=1), 2).astype(I32)
        next_e = jnp.sum(jnp.where(is_e, next_tbl[None, :], 0), axis=1).astype(I32)
        row_in_e = jnp.arange(n_blocks, dtype=I32) * tme - jnp.sum(jnp.where(is_e, pstart[None, :], 0), axis=1)
        n_valid = jnp.clip(jnp.sum(jnp.where(is_e, counts[None, :], 0), axis=1) - row_in_e, 0, tme).astype(I32)
        ys_groups.append(_experts(xs, block_e, n_used, slot, next_e, n_valid, w_gate_up, b_gate_up, w_down,
                                  b_down, tm=tme, n_blocks=n_blocks))
        dest_groups.append(dest4)

    out = x1
    for g in range(groups):
        yk = _sc_gather_rows(ys_groups[g], dest_groups[g], ng)
        out = _final(yk, gates, out, g_final.reshape(1, d_model), tm=tiles["tm_final"],
                     tile0=g * (ng // tiles["tm_final"]))
    return out.reshape(batch, seq, d_model)


def kernel(x, g_mix, w_in, conv_w, b_f, w_conv_o, w_attn_o, w_out, g_ffn, w_router, b_router, w_gate_up,
           b_gate_up, w_down, b_down, g_final):
    return _forward(x, g_mix, w_in, conv_w, b_f, w_conv_o, w_attn_o, w_out, g_ffn, w_router, b_router,
                    w_gate_up, b_gate_up, w_down, b_down, g_final, _tiles(x.shape[1]))
```

```python
import functools

import jax
import jax.numpy as jnp
from jax import lax
from jax.experimental import pallas as pl
from jax.experimental.pallas import tpu as pltpu
from jax.experimental.pallas import tpu_sc as plsc

TOP_K = 4
RMS_EPS = 1e-5
SWIGLU_ALPHA = 1.702
SWIGLU_LIMIT = 7.0
LOG2E = 1.4426950408889634

LANES = 128
SUBLANES = 8
BF16_SUBLANES = 16
V_ONES_ROWS = BF16_SUBLANES
EXPERT_ROW_PARTS = 4
FORGET_ROWS = BF16_SUBLANES
VMEM_LIMIT_BYTES = 56 * 1024 * 1024

F32 = jnp.float32
BF16 = jnp.bfloat16
U32 = jnp.uint32
I32 = jnp.int32
HI_MASK = 0xFFFF0000


def _params(sem):
    return pltpu.CompilerParams(dimension_semantics=sem, vmem_limit_bytes=VMEM_LIMIT_BYTES)


def _as_words(x_bf16):
    return pltpu.bitcast(x_bf16, U32)


def _as_bf16(w_u32):
    return pltpu.bitcast(w_u32, BF16)


def _pack_bf16_pair(lo_f32, hi_f32):
    lo = lax.bitcast_convert_type(lo_f32, U32)
    hi = lax.bitcast_convert_type(hi_f32, U32)
    return (lo >> 16) | (hi & U32(HI_MASK))


def _unpack_bf16_pair(w):
    lo = lax.bitcast_convert_type(w << 16, F32)
    hi = lax.bitcast_convert_type(w & U32(HI_MASK), F32)
    return lo, hi


def _inproj_kernel(x_ref, g_ref, wa_ref, wqk_ref, wvt_ref, wg_ref, bf_ref,
                   cb_ref, ucx_ref, qa_ref, ka_ref, vt_ref, sgc_ref, sga_ref,
                   carry_ref, *, tiles_per_batch, conv_dim, attn_dim, d_model, heads, tk):
    tm = x_ref.shape[0]
    t = lax.rem(pl.program_id(0), tiles_per_batch)
    x = x_ref[...]
    ms = jnp.mean(x * x, axis=-1, keepdims=True)
    h = ((x * lax.rsqrt(ms + RMS_EPS)) * g_ref[...]).astype(BF16)

    def mm(w_ref, c0, n):
        return jnp.dot(h, w_ref[:, c0:c0 + n], preferred_element_type=F32)

    vtf =lax.dot_general(wvt_ref[...], h, (((1,), (1,)), ((), ())), preferred_element_type=F32)
    qf = mm(wqk_ref, 0, attn_dim)
    kf = mm(wqk_ref, attn_dim, attn_dim)
    cb = mm(wa_ref, 0, conv_dim)
    ucx = mm(wa_ref, conv_dim, conv_dim) * mm(wa_ref, 2 * conv_dim, conv_dim)

    z = vtf[attn_dim:attn_dim + FORGET_ROWS] + bf_ref[...]
    sub = lax.broadcasted_iota(I32, z.shape, 0)
    logf = jnp.where(sub < heads, jnp.minimum(z, 0.0) - jnp.log1p(jnp.exp(-jnp.abs(z))), 0.0)
    p1 = logf.astype(BF16).astype(F32)
    r1 = logf - p1
    p2 = r1.astype(BF16).astype(F32)
    p3 = (r1 - p2).astype(BF16).astype(F32)
    packed = jnp.concatenate([p1, p2, p3], axis=0).astype(BF16)
    row = lax.broadcasted_iota(I32, (tm, tm), 0)
    col = lax.broadcasted_iota(I32, (tm, tm), 1)
    tri = jnp.where(row <= col, 1.0, 0.0).astype(BF16)
    r = jnp.dot(packed, tri, preferred_element_type=F32)
    local = r[:FORGET_ROWS] + r[FORGET_ROWS:2 * FORGET_ROWS] + r[2 * FORGET_ROWS:]

    @pl.when(t == 0)
    def _():
        carry_ref[...] = jnp.zeros_like(carry_ref)

    c = local + carry_ref[:, 0:1]
    carry_ref[...] = jnp.broadcast_to(c[:, tm - 1:tm], carry_ref.shape)

    c2 = c * LOG2E
    c_hi_t = c2.astype(BF16).astype(F32)
    c_r = c2 - c_hi_t
    c_mid_t = c_r.astype(BF16).astype(F32)
    c_lo_t = (c_r - c_mid_t).astype(BF16).astype(F32)
    split = (c_hi_t, c_mid_t, c_lo_t)
    pad = jnp.zeros((LANES - len(split) * FORGET_ROWS, tm), F32)
    parts = jnp.concatenate(split + (pad,), axis=0).T
    head_dim = attn_dim // heads
    ext = LANES - head_dim
    lane_e = lax.broadcasted_iota(I32, (tm, ext), 1)
    ones = jnp.where(lane_e < len(split), 1.0, 0.0)
    for hh in range(heads):
        kext = jnp.zeros((tm, ext), F32)
        for p in range(len(split)):
            col_p = parts[:, p * FORGET_ROWS + hh:p * FORGET_ROWS + hh + 1]
            kext = jnp.where(lane_e == p, -col_p, kext)
        sl = slice(hh * head_dim, (hh + 1) * head_dim)
        qa_ref[:, hh * LANES:(hh + 1) * LANES] = _as_words(jnp.concatenate([qf[:, sl], ones], axis=1).astype(BF16))
        ka_ref[:, hh * LANES:(hh + 1) * LANES] = _as_words(jnp.concatenate([kf[:, sl], kext], axis=1).astype(BF16))

    cb_ref[...] = _as_words(cb.astype(BF16))
    ucx_ref[...] = _as_words(ucx.astype(BF16))
    half = d_model // 2
    for c in range(2):
        sgc_ref[:, c * half:(c + 1) * half] = _as_words(jax.nn.sigmoid(mm(wg_ref, c * half, half)).astype(BF16))
        sga_ref[:, c * half:(c + 1) * half] = _as_words(
            jax.nn.sigmoid(mm(wg_ref, d_model + c * half, half)).astype(BF16))

    vt = vtf[:attn_dim].astype(BF16)
    ones_rows = jnp.ones((V_ONES_ROWS, tm), BF16)
    vt = jnp.concatenate([piece for hh in range(heads)
                          for piece in (vt[hh * head_dim:(hh + 1) * head_dim], ones_rows)], axis=0)
    for u in range(tm // tk):
        vt_ref[u] = _as_words(vt[:, u * tk:(u + 1) * tk])


def _inproj(x2d, g_mix, wa, wqk, wvt, wg, bf_col, *, batch, seq, heads, tm, tk):
    n, d_model = x2d.shape
    conv_dim = wa.shape[1] // 3
    attn_dim = wvt.shape[0] - FORGET_ROWS
    assert heads <= FORGET_ROWS
    tiles_per_batch = seq // tm
    grid = (n // tm,)
    full = lambda shape: pl.BlockSpec(shape, lambda i: (0,) * len(shape))
    rows = lambda w: pl.BlockSpec((tm, w), lambda i: (i, 0))
    kern = functools.partial(_inproj_kernel, tiles_per_batch=tiles_per_batch, conv_dim=conv_dim,
                             attn_dim=attn_dim, d_model=d_model, heads=heads, tk=tk)
    vt_rows = attn_dim + heads * V_ONES_ROWS
    words = lambda w: jax.ShapeDtypeStruct((n // 2, w), U32)
    wrows = lambda w: pl.BlockSpec((tm // 2, w), lambda i: (i, 0))
    out_shape = (
        words(conv_dim),
        words(conv_dim),
        words(heads * LANES),
        words(heads * LANES),
        jax.ShapeDtypeStruct((n // tk, vt_rows // 2, tk), U32),
        words(d_model),
        words(d_model),
    )
    out_specs = (
        wrows(conv_dim), wrows(conv_dim), wrows(heads * LANES), wrows(heads * LANES),
        pl.BlockSpec((tm // tk, vt_rows // 2, tk), lambda i: (i, 0, 0)),
        wrows(d_model), wrows(d_model),
    )
    return pl.pallas_call(
        kern,
        grid=grid,
        in_specs=[rows(d_model), full((1, d_model)), full(wa.shape), full(wqk.shape), full(wvt.shape),
                  full(wg.shape), full((FORGET_ROWS, 1))],
        out_specs=out_specs,
        out_shape=out_shape,
        scratch_shapes=[pltpu.VMEM((FORGET_ROWS, LANES), F32)],
        compiler_params=_params(("arbitrary",)),
        name="inproj",
    )(x2d, g_mix, wa, wqk, wvt, wg, bf_col)


def _attn_kernel(q_ref, k_ref, v_ref, o_ref, s_ref, *, tq, tk, head_dim, heads):
    qi = pl.program_id(1)
    row = lax.broadcasted_iota(I32, (tk, tq), 0)
    col = lax.broadcasted_iota(I32, (tk, tq), 1)

    def score_tile(j, slot):
        off = pl.multiple_of(j * (tk // 2), tk // 2)
        for hh in range(heads):
            qa = _as_bf16(q_ref[:, hh * LANES:(hh + 1) * LANES])
            ka = _as_bf16(k_ref[pl.ds(off, tk // 2), hh * LANES:(hh + 1) * LANES])
            s_ref[slot, hh] = lax.dot_general(ka, qa, (((1,), (1,)), ((), ())), preferred_element_type=F32)

    def block(j, slot, carry, masked):
        if not masked:
            score_tile(j + 1, 1 - slot)
        stats = []
        for hh in range(heads):
            m, _ = carry[hh]
            s = s_ref[slot, hh]
            if masked:
                s = jnp.where(row <= col, s, -jnp.inf)
            m_new = jnp.maximum(m, jnp.max(s, axis=0, keepdims=True))
            stats.append((m_new, jnp.exp2(m - m_new), jnp.exp2(s - m_new).astype(BF16)))
        out = []
        for hh in range(heads):
            m_new, alpha, p = stats[hh]
            vth = _as_bf16(v_ref[j, hh * (vrows // 2):(hh + 1) * (vrows // 2), :])
            out.append((m_new, alpha * carry[hh][1] + jnp.dot(vth, p, preferred_element_type=F32)))
        return tuple(out)

    vrows = head_dim + V_ONES_ROWS
    init = tuple((jnp.full((1, tq), -jnp.inf, F32), jnp.zeros((vrows, tq), F32)) for _ in range(heads))
    score_tile(0, 0)

    def pair(p, c):
        return block(2 * p + 1, 1, block(2 * p, 0, c, False), False)

    carry = lax.fori_loop(0, qi // 2, pair, init)
    final = lax.cond(
        lax.rem(qi, 2) == 0,
        lambda c: block(qi, 0, c, True),
        lambda c: block(qi, 1, block(qi - 1, 0, c, False), True),
        carry)
    per_group = LANES // head_dim
    for g in range(heads // per_group):
        accs = [final[g * per_group + u][1] for u in range(per_group)]
        ot = jnp.concatenate([a[:head_dim] / a[head_dim:head_dim + 1] for a in accs],
                             axis=0)
        o_ref[:, g * LANES:(g + 1) * LANES] = _as_words(ot.T.astype(BF16))


def _attention(qa, ka, vt, *, batch, seq, heads, tq):
    n = 2 * qa.shape[0]
    vt_rows, tk = 2 * vt.shape[1], vt.shape[2]
    head_dim = vt_rows // heads - V_ONES_ROWS
    attn_dim = heads * head_dim
    assert LANES % head_dim == 0 and tq == tk
    nq = seq // tq
    nk = seq // tk
    kern = functools.partial(_attn_kernel, tq=tq, tk=tk, head_dim=head_dim, heads=heads)
    return pl.pallas_call(
        kern,
        grid=(batch, nq),
        in_specs=[
            pl.BlockSpec((tq // 2, heads * LANES), lambda b, qi: (b * nq + qi, 0)),
            pl.BlockSpec((seq // 2, heads * LANES), lambda b, qi: (b, 0)),
            pl.BlockSpec((nk, vt_rows // 2, tk), lambda b, qi: (b, 0, 0)),
        ],
        out_specs=pl.BlockSpec((tq // 2, attn_dim), lambda b, qi: (b * nq + qi, 0)),
        out_shape=jax.ShapeDtypeStruct((n // 2, attn_dim), U32),
        scratch_shapes=[pltpu.VMEM((2, heads, tk, tq), F32)],
        compiler_params=_params(("arbitrary", "arbitrary")),
        name="attn",
    )(qa, ka, vt)


def _mix_kernel(x_ref, cb_ref, u_ref, uh_ref, o_ref, sgc_ref, sga_ref, cw_ref, wco_ref, wao_ref, wout_ref,
                gffn_ref, wr_ref, br_ref,
                x1_ref, h2p_ref, topi_ref, gate_ref, rank_ref, cnt_ref, carry_ref, lg_scr,
                *, tiles_per_batch, tiles_per_group, n_experts, nt):
    tm, d_model = x_ref.shape
    i = pl.program_id(0)

    @pl.when(i == 0)
    def _():
        lg_scr[...] = jnp.zeros_like(lg_scr)

    @pl.when((i == 0) | (lax.rem(i - 1, tiles_per_group) == 0))
    def _():
        carry_ref[...] = jnp.zeros_like(carry_ref)

    lg = lg_scr[...]
    sub = lax.broadcasted_iota(I32, (n_experts, tm), 0)

    vals, idxs = [], []
    for _ in range(TOP_K):
        m = jnp.max(lg, axis=0, keepdims=True)
        idx = jnp.min(jnp.where(lg == m, sub, n_experts), axis=0, keepdims=True)
        vals.append(m)
        idxs.append(idx)
        lg = jnp.where(sub == idx, -jnp.inf, lg)
    es = [jnp.exp(vk - vals[0]) for vk in vals]
    denom = es[0] + es[1] + es[2] + es[3]

    sub8 = lax.broadcasted_iota(I32, (SUBLANES, tm), 0)
    subl = lax.broadcasted_iota(I32, (LANES, tm), 0)
    chosen = jnp.zeros((n_experts, tm), F32)
    topi = jnp.zeros((SUBLANES, tm), I32)
    gates_t = jnp.zeros((LANES, tm), F32)
    for kk in range(TOP_K):
        chosen = jnp.where(sub == idxs[kk], 1.0, chosen)
        topi = jnp.where(sub8 == kk, idxs[kk], topi)
        gates_t = jnp.where(subl == kk, es[kk] / denom, gates_t)
    topi_ref[...] = topi
    gate_ref[...] = gates_t.T

    first = lax.rem(jnp.minimum(i, nt - 1), tiles_per_batch) == 0

    u = _as_bf16(u_ref[...]).astype(F32)
    halo = jnp.where(first, 0.0, _as_bf16(uh_ref[...]).astype(F32))
    ext = jnp.concatenate([halo, u], axis=0)
    hs = halo.shape[0]
    u1 = pltpu.roll(ext, 1, 0)[hs:]
    u2 = pltpu.roll(ext, 2, 0)[hs:]
    cw = cw_ref[...]
    conv = cw[0:1, :] * u2 + cw[1:2, :] * u1 + cw[2:3, :] * u
    yc = (_as_bf16(cb_ref[...]).astype(F32) * conv).astype(BF16)
    y_conv = jnp.dot(yc, wco_ref[...], preferred_element_type=F32)
    y_attn = jnp.dot(_as_bf16(o_ref[...]), wao_ref[...], preferred_element_type=F32)
    mixed = (_as_bf16(sgc_ref[...]).astype(F32) * y_conv
             + _as_bf16(sga_ref[...]).astype(F32) * y_attn).astype(BF16)
    x1 = x_ref[...] + jnp.dot(mixed, wout_ref[...], preferred_element_type=F32)
    x1_ref[...] = x1

    ms = jnp.mean(x1 * x1, axis=-1, keepdims=True)
    h2 = (x1 * lax.rsqrt(ms + RMS_EPS)) * gffn_ref[...]
    hb = h2.astype(BF16)
    hbf = hb.astype(F32)
    half = d_model // 2
    h2p_ref[...] = _pack_bf16_pair(hbf[:, :half], hbf[:, half:])

    hlo = (h2 - hbf).astype(BF16)
    nt_dims = (((1,), (1,)), ((), ()))
    rt = (lax.dot_general(wr_ref[...], hb, nt_dims, preferred_element_type=F32)
          + lax.dot_general(wr_ref[...], hlo, nt_dims, preferred_element_type=F32))
    lg_scr[...] = rt[:n_experts] + rt[n_experts:] + br_ref[...]

    row = lax.broadcasted_iota(I32, (tm, tm), 0)
    col = lax.broadcasted_iota(I32, (tm, tm), 1)
    tri = jnp.where(row < col, 1.0, 0.0).astype(BF16)
    before = jnp.dot(chosen.astype(BF16), tri, preferred_element_type=F32) + carry_ref[:, 0:1]
    rank = jnp.zeros((SUBLANES, tm), F32)
    for kk in range(TOP_K):
        rk = jnp.sum(jnp.where(sub == idxs[kk], before, 0.0), axis=0, keepdims=True)
        rank = jnp.where(sub8 == kk, rk, rank)
    rank_ref[...] = rank.astype(I32)
    total = carry_ref[...] + jnp.sum(chosen, axis=1, keepdims=True)
    carry_ref[...] = total
    cnt_ref[0] = total


def _mix(x2d, cb, ucx, o, sgc, sga, conv_w, wco, wao, wout, g_ffn, wr, br, *, seq, n_experts, tm, tile0, nt,
         tiles_per_group):
    d_model = x2d.shape[1]
    n = nt * tm
    conv_dim = cb.shape[1]
    attn_dim = o.shape[1]
    tiles_per_batch = seq // tm
    assert tile0 % tiles_per_batch == 0
    hb = tm // BF16_SUBLANES
    full = lambda shape: pl.BlockSpec(shape, lambda i: (0,) * len(shape))
    cur = lambda i: jnp.minimum(i, nt - 1)
    prev = lambda i: jnp.maximum(i - 1, 0)
    rows_in = lambda w: pl.BlockSpec((tm, w), lambda i: (cur(i) + tile0, 0))
    words_in = lambda w: pl.BlockSpec((tm // 2, w), lambda i: (cur(i) + tile0, 0))
    rows = lambda w: pl.BlockSpec((tm, w), lambda i: (cur(i), 0))
    kern = functools.partial(_mix_kernel, tiles_per_batch=tiles_per_batch, tiles_per_group=tiles_per_group,
                             n_experts=n_experts, nt=nt)
    out_shape = (
        jax.ShapeDtypeStruct((n, d_model), F32),
        jax.ShapeDtypeStruct((n, d_model // 2), U32),
        jax.ShapeDtypeStruct((SUBLANES, n), I32),
        jax.ShapeDtypeStruct((n, LANES), F32),
        jax.ShapeDtypeStruct((SUBLANES, n), I32),
        jax.ShapeDtypeStruct((nt, n_experts, LANES), F32),
    )
    cols = pl.BlockSpec((SUBLANES, tm), lambda i: (0, prev(i)))
    out_specs = (rows(d_model), rows(d_model // 2), cols, pl.BlockSpec((tm, LANES), lambda i: (prev(i), 0)), cols,
                 pl.BlockSpec((1, n_experts, LANES), lambda i: (prev(i), 0, 0)))
    return pl.pallas_call(
        kern,
        grid=(nt + 1,),
        in_specs=[rows_in(d_model), words_in(conv_dim), words_in(conv_dim),
                  pl.BlockSpec((BF16_SUBLANES // 2, conv_dim),
                               lambda i: (jnp.maximum((cur(i) + tile0) * hb - 1, 0), 0)),
                  words_in(attn_dim), words_in(d_model), words_in(d_model),
                  full(conv_w.shape), full(wco.shape), full(wao.shape), full(wout.shape),
                  full((1, d_model)), full(wr.shape), full((n_experts, 1))],
        out_specs=out_specs,
        out_shape=out_shape,
        scratch_shapes=[pltpu.VMEM((n_experts, LANES), F32), pltpu.VMEM((n_experts, tm), F32)],
        compiler_params=_params(("arbitrary",)),
        name="mix",
    )(x2d, cb, ucx, ucx, o, sgc, sga, conv_w, wco, wao, wout, g_ffn, wr, br)


SC_CORES = 2
SC_SUBCORES = 16
SC_WORKERS = SC_CORES * SC_SUBCORES
SC_CHUNK = 64


def _sc_mesh():
    return plsc.VectorSubcoreMesh(core_axis_name="c", subcore_axis_name="s",
                                  num_cores=SC_CORES, num_subcores=SC_SUBCORES)


def _sc_worker():
    return lax.axis_index("s") * SC_CORES + lax.axis_index("c")


def _sc_scatter_rows(src, dest4, pad_idx, n_out, row0):
    width = src.shape[1]
    nchunks = dest4.shape[1]
    npad = pad_idx.shape[1]

    @functools.partial(
        pl.kernel, mesh=_sc_mesh(),
        out_type=jax.ShapeDtypeStruct((n_out, width), src.dtype),
        scratch_types=[pltpu.VMEM((TOP_K, SC_CHUNK), I32), pltpu.VMEM((SC_CHUNK, width), src.dtype),
                       pltpu.VMEM((npad, SC_CHUNK), I32), pltpu.SemaphoreType.DMA],
        name="sc_scatter_rows",
    )
    def k(src_hbm, dest_hbm, pad_hbm, out_hbm, idx_v, rows_v, pad_v, sem):
        wid = _sc_worker()
        base = row0 + wid * (nchunks * SC_CHUNK)

        @pl.loop(0, nchunks)
        def _(j):
            pltpu.sync_copy(src_hbm.at[pl.ds(base + j * SC_CHUNK, SC_CHUNK)], rows_v)
            pltpu.sync_copy(dest_hbm.at[wid, j], idx_v)
            copies = [pltpu.async_copy(rows_v, out_hbm.at[idx_v.at[kk]], sem) for kk in range(TOP_K)]
            for cp in copies:
                cp.wait()

        pltpu.sync_copy(pad_hbm.at[wid], pad_v)
        fills = [pltpu.async_copy(rows_v, out_hbm.at[pad_v.at[p]], sem) for p in range(npad)]
        for cp in fills:
            cp.wait()

    return k(src, dest4, pad_idx)


def _sc_gather_rows(src, dest4, n):
    width = src.shape[1]
    nchunks = dest4.shape[1]

    @functools.partial(
        pl.kernel, mesh=_sc_mesh(),
        out_type=jax.ShapeDtypeStruct((TOP_K, n, width), src.dtype),
        scratch_types=[pltpu.VMEM((TOP_K, SC_CHUNK), I32), pltpu.VMEM((SC_CHUNK, width), src.dtype),
                       pltpu.VMEM((SC_CHUNK, width), src.dtype), pltpu.SemaphoreType.DMA, pltpu.SemaphoreType.DMA],
        name="sc_gather_rows",
    )
    def k(src_hbm, dest_hbm, out_hbm, idx_v, rows_a, rows_b, sem_a, sem_b):
        wid = _sc_worker()
        base = wid * (nchunks * SC_CHUNK)
        bufs = ((rows_a, sem_a), (rows_b, sem_b))

        @pl.loop(0, nchunks)
        def _(j):
            pltpu.sync_copy(dest_hbm.at[wid, j], idx_v)
            pending = pltpu.async_copy(src_hbm.at[idx_v.at[0]], rows_a, sem_a)
            for kk in range(TOP_K):
                buf, _ = bufs[kk % 2]
                pending.wait()
                if kk + 1 < TOP_K:
                    nbuf, nsem = bufs[(kk + 1) % 2]
                    pending = pltpu.async_copy(src_hbm.at[idx_v.at[kk + 1]], nbuf, nsem)
                pltpu.sync_copy(buf, out_hbm.at[kk, pl.ds(base + j * SC_CHUNK, SC_CHUNK)])

    return k(src, dest4)


def _expert_kernel(be_ref, nu_ref, slot_ref, nxt_ref, nv_ref, xs_ref, wgu_hbm, bgu_ref, wd_hbm, bd_ref, ys_ref,
                   wgu_f32, wd_f32, wgu_bf, wd_bf, sem, *, d_ff, ff_chunk):
    i = pl.program_id(0)

    def weight_copies(e, s):
        return (pltpu.make_async_copy(wgu_hbm.at[e], wgu_f32.at[s], sem.at[s]),
                pltpu.make_async_copy(wd_hbm.at[e], wd_f32.at[s], sem.at[s]))

    @pl.when(i == 0)
    def _():
        for cp in weight_copies(be_ref[0], 0):
            cp.start()

    @pl.when((i < nu_ref[0]) & ((i == 0) | (be_ref[i] != be_ref[jnp.maximum(i - 1, 0)])))
    def _():
        s = slot_ref[i]
        for cp in weight_copies(be_ref[i], s):
            cp.wait()
        wgu_bf[...] = wgu_f32[s].astype(BF16)
        wd_bf[...] = wd_f32[s].astype(BF16)
        nxt = nxt_ref[i]

        @pl.when(nxt >= 0)
        def _():
            for cp in weight_copies(nxt, 1 - s):
                cp.start()

    def mlp_rows(rows):
        lo, hi = _unpack_bf16_pair(xs_ref[0:rows, :])
        xb = jnp.concatenate([lo.astype(BF16), hi.astype(BF16)], axis=1)
        chunks = range(0, d_ff, ff_chunk)
        gus = []
        for c0 in chunks:
            glu = jnp.dot(xb, wgu_bf[:, c0:c0 + ff_chunk], preferred_element_type=F32) + bgu_ref[0, :, c0:c0 + ff_chunk]
            lin = (jnp.dot(xb, wgu_bf[:, d_ff + c0:d_ff + c0 + ff_chunk], preferred_element_type=F32)
                   + bgu_ref[0, :, d_ff + c0:d_ff + c0 + ff_chunk])
            gus.append((glu, lin))
        y = bd_ref[0]
        for c0, (glu, lin) in zip(chunks, gus):
            glu = jnp.minimum(glu, SWIGLU_LIMIT)
            lin = jnp.clip(lin, -SWIGLU_LIMIT, SWIGLU_LIMIT)
            act = (glu * jax.nn.sigmoid(SWIGLU_ALPHA * glu)) * (lin + 1.0)
            y = y + jnp.dot(act.astype(BF16), wd_bf[c0:c0 + ff_chunk, :], preferred_element_type=F32)
        yb = y.astype(BF16).astype(F32)
        half = y.shape[1] // 2
        ys_ref[0:rows, :] = _pack_bf16_pair(yb[:, :half], yb[:, half:])

    tm = xs_ref.shape[0]
    used = i < nu_ref[0]
    part = tm // EXPERT_ROW_PARTS
    parts_needed = jnp.maximum((nv_ref[i] + part - 1) // part, 1)
    for k in range(1, EXPERT_ROW_PARTS + 1):
        @pl.when(used & (parts_needed == k))
        def _(k=k):
            mlp_rows(k * part)
            if k < EXPERT_ROW_PARTS:
                ys_ref[k * part:, :] = jnp.zeros((tm - k * part, ys_ref.shape[1]), U32)


def _experts(xs, block_e, n_used, slot, next_e, n_valid, wgu, bgu, wd, bd, *, tm, n_blocks):
    half = xs.shape[1]
    p = n_blocks * tm
    e, d_model, two_ff = wgu.shape
    d_ff = two_ff // 2
    kern = functools.partial(_expert_kernel, d_ff=d_ff, ff_chunk=min(256, d_ff))
    blk = lambda i, be, nu, sl, nx, nv: (jnp.minimum(i, nu[0] - 1), 0)
    bias = lambda i, be, nu, sl, nx, nv: (be[i], 0, 0)
    grid_spec = pltpu.PrefetchScalarGridSpec(
        num_scalar_prefetch=5,
        grid=(n_blocks,),
        in_specs=[
            pl.BlockSpec((tm, half), blk),
            pl.BlockSpec(memory_space=pl.ANY),
            pl.BlockSpec((1, 1, two_ff), bias),
            pl.BlockSpec(memory_space=pl.ANY),
            pl.BlockSpec((1, 1, d_model), bias),
        ],
        out_specs=pl.BlockSpec((tm, half), blk),
        scratch_shapes=[pltpu.VMEM((2, d_model, two_ff), F32), pltpu.VMEM((2, d_ff, d_model), F32),
                        pltpu.VMEM((d_model, two_ff), BF16), pltpu.VMEM((d_ff, d_model), BF16),
                        pltpu.SemaphoreType.DMA((2,))],
    )
    return pl.pallas_call(
        kern,
        grid_spec=grid_spec,
        out_shape=jax.ShapeDtypeStruct((p, half), U32),
        compiler_params=_params(("arbitrary",)),
        name="experts",
    )(block_e, n_used, slot, next_e, n_valid, xs, wgu, bgu.reshape(e, 1, two_ff), wd, bd.reshape(e, 1, d_model))


def _final_kernel(yk_ref, gate_ref, x1_ref, g_ref, out_ref):
    tm, d_model = x1_ref.shape
    half = d_model // 2
    gates = gate_ref[...]
    acc_lo = jnp.zeros((tm, half), F32)
    acc_hi = jnp.zeros((tm, half), F32)
    for kk in range(TOP_K):
        lo, hi = _unpack_bf16_pair(yk_ref[kk])
        gk = gates[:, kk:kk + 1]
        acc_lo = acc_lo + gk * lo
        acc_hi = acc_hi + gk * hi
    x2 = x1_ref[...] + jnp.concatenate([acc_lo, acc_hi], axis=1)
    ms = jnp.mean(x2 * x2, axis=-1, keepdims=True)
    out_ref[...] = (x2 * lax.rsqrt(ms + RMS_EPS)) * g_ref[...]


def _final(yk, gates, x1, g_final, *, tm, tile0):
    n, d_model = x1.shape
    rows = lambda w: pl.BlockSpec((tm, w), lambda i: (i + tile0, 0))
    return pl.pallas_call(
        _final_kernel,
        grid=(yk.shape[1] // tm,),
        in_specs=[pl.BlockSpec((TOP_K, tm, d_model // 2), lambda i: (0, i, 0)), rows(LANES), rows(d_model),
                  pl.BlockSpec((1, d_model), lambda i: (0, 0))],
        out_specs=rows(d_model),
        out_shape=jax.ShapeDtypeStruct((n, d_model), F32),
        input_output_aliases={2: 0},
        compiler_params=_params(("arbitrary",)),
        name="final",
    )(yk, gates, x1, g_final)


def _tiles(seq):
    tm = min(512, seq)
    return dict(tm_proj=tm, t_attn=min(256, seq), tm_mix=tm, tm_expert=512, tm_final=min(2 * tm, seq), moe_groups=2)


def _forward(x, g_mix, w_in, conv_w, b_f, w_conv_o, w_attn_o, w_out, g_ffn, w_router, b_router,
             w_gate_up, b_gate_up, w_down, b_down, g_final, tiles):
    batch, seq, d_model = x.shape
    n = batch * seq
    conv_dim = conv_w.shape[1]
    attn_dim = w_attn_o.shape[0]
    heads = b_f.shape[0]
    head_dim = attn_dim // heads
    n_experts = w_router.shape[1]
    x2d = x.reshape(n, d_model)

    c0 = 3 * conv_dim
    a0 = c0 + 3 * attn_dim
    wa = w_in[:, :c0].astype(BF16)
    scale = LOG2E / (head_dim ** 0.5)
    wqk = jnp.concatenate([w_in[:, c0:c0 + attn_dim] * scale, w_in[:, c0 + attn_dim:c0 + 2 * attn_dim]],
                          axis=1).astype(BF16)
    wvt = jnp.pad(w_in[:, c0 + 2 * attn_dim:a0 + heads].T, ((0, FORGET_ROWS - heads), (0, 0))).astype(BF16)
    wg = w_in[:, a0 + heads:].astype(BF16)
    bf_col = jnp.pad(b_f, (0, FORGET_ROWS - heads)).reshape(FORGET_ROWS, 1)
    wr_hi = w_router.astype(BF16)
    wr_lo = (w_router - wr_hi.astype(F32)).astype(BF16)
    wr = jnp.concatenate([wr_hi, wr_lo], axis=1).T
    br = b_router.reshape(n_experts, 1)

    cb, ucx, qa, ka, vt, sgc, sga = _inproj(
        x2d, g_mix.reshape(1, d_model), wa, wqk, wvt, wg, bf_col,
        batch=batch, seq=seq, heads=heads, tm=tiles["tm_proj"], tk=tiles["t_attn"])
    o = _attention(qa, ka, vt, batch=batch, seq=seq, heads=heads, tq=tiles["t_attn"])
    groups = tiles["moe_groups"]
    ng = n // groups
    x1, h2p, topi, gates, rank, cnt = _mix(
        x2d, cb, ucx, o, sgc, sga, conv_w, w_conv_o.astype(BF16), w_attn_o.astype(BF16), w_out.astype(BF16),
        g_ffn.reshape(1, d_model), wr, br, seq=seq, n_experts=n_experts, tm=tiles["tm_mix"],
        tile0=0, nt=n // tiles["tm_mix"], tiles_per_group=ng // tiles["tm_mix"])

    tme = tiles["tm_expert"]
    n_blocks = (ng * TOP_K) // tme + n_experts
    p_rows = n_blocks * tme
    nchunks = ng // (SC_WORKERS * SC_CHUNK)
    jj = jnp.arange(tme, dtype=I32)[None, :]
    spare = p_rows + jnp.arange(n_experts, dtype=I32)[:, None] * tme + jj
    expert_ids = jnp.arange(n_experts, dtype=I32)[None, None, :]
    ys_groups, dest_groups = [], []
    for g in range(groups):
        counts = cnt[(g + 1) * (ng // tiles["tm_mix"]) - 1, :, 0].astype(I32)
        padded = ((counts + tme - 1) // tme) * tme
        pend = jnp.cumsum(padded)
        pstart = pend - padded
        n_used = (pend[-1] // tme).astype(I32).reshape(1)
        blk_start = jnp.minimum(jnp.arange(n_blocks, dtype=I32), n_used[0] - 1) * tme
        block_e = jnp.minimum(jnp.sum((pend[None, :] <= blk_start[:, None]).astype(I32), axis=1), n_experts - 1)
        tok = slice(g * ng, (g + 1) * ng)
        onehot = topi[:TOP_K, tok, None] == expert_ids
        dest = jnp.sum(jnp.where(onehot, pstart[None, None, :], 0), axis=2) + rank[:TOP_K, tok]
        dest4 = dest.reshape(TOP_K, SC_WORKERS, nchunks, SC_CHUNK).transpose(1, 2, 0, 3)
        pad_idx = jnp.where(jj < (padded - counts)[:, None], (pstart + counts)[:, None] + jj, spare)
        pad_idx = pad_idx.reshape(SC_WORKERS, (n_experts * tme) // (SC_WORKERS * SC_CHUNK), SC_CHUNK)
        xs = _sc_scatter_rows(h2p, dest4, pad_idx, p_rows + n_experts * tme, g * ng)
        eid = jnp.arange(n_experts, dtype=I32)
        active = padded > 0
        run_tbl = jnp.sum((eid[None, :] < eid[:, None]) & active[None, :], axis=1)
        later = (eid[None, :] > eid[:, None]) & active[None, :]
        next_tbl = jnp.min(jnp.where(later, eid[None, :], n_experts), axis=1)
        next_tbl = jnp.where(next_tbl < n_experts, next_tbl, -1)
        is_e = block_e[:, None] == eid[None, :]
        slot = lax.rem(jnp.sum(jnp.where(is_e, run_tbl[None, :], 0), axis=1), 2).astype(I32)
        next_e = jnp.sum(jnp.where(is_e, next_tbl[None, :], 0), axis=1).astype(I32)
        row_in_e = jnp.arange(n_blocks, dtype=I32) * tme - jnp.sum(jnp.where(is_e, pstart[None, :], 0), axis=1)
        n_valid = jnp.clip(jnp.sum(jnp.where(is_e, counts[None, :], 0), axis=1) - row_in_e, 0, tme).astype(I32)
        ys_groups.append(_experts(xs, block_e, n_used, slot, next_e, n_valid, w_gate_up, b_gate_up, w_down,
                                  b_down, tm=tme, n_blocks=n_blocks))
        dest_groups.append(dest4)

    out = x1
    for g in range(groups):
        yk = _sc_gather_rows(ys_groups[g], dest_groups[g], ng)
        out = _final(yk, gates, out, g_final.reshape(1, d_model), tm=tiles["tm_final"],
                     tile0=g * (ng // tiles["tm_final"]))
    return out.reshape(batch, seq, d_model)


def kernel(x, g_mix, w_in, conv_w, b_f, w_conv_o, w_attn_o, w_out, g_ffn, w_router, b_router, w_gate_up,
           b_gate_up, w_down, b_down, g_final):
    return _forward(x, g_mix, w_in, conv_w, b_f, w_conv_o, w_attn_o, w_out, g_ffn, w_router, b_router,
                    w_gate_up, b_gate_up, w_down, b_down, g_final, _tiles(x.shape[1]))
```

```python
import functools

import jax
import jax.numpy as jnp
from jax import lax
from jax.experimental import pallas as pl
from jax.experimental.pallas import tpu as pltpu
from jax.experimental.pallas import tpu_sc as plsc

TOP_K = 4
RMS_EPS = 1e-5
SWIGLU_ALPHA = 1.702
SWIGLU_LIMIT = 7.0
LOG2E = 1.4426950408889634

LANES = 128
SUBLANES = 8
BF16_SUBLANES = 16
V_ONES_ROWS = BF16_SUBLANES
EXPERT_ROW_PARTS = 4
FORGET_ROWS = BF16_SUBLANES
VMEM_LIMIT_BYTES = 56 * 1024 * 1024

F32 = jnp.float32
BF16 = jnp.bfloat16
U32 = jnp.uint32
I32 = jnp.int32
HI_MASK = 0xFFFF0000


def _params(sem):
    return pltpu.CompilerParams(dimension_semantics=sem, vmem_limit_bytes=VMEM_LIMIT_BYTES)


def _as_words(x_bf16):
    return pltpu.bitcast(x_bf16, U32)


def _as_bf16(w_u32):
    return pltpu.bitcast(w_u32, BF16)


def _pack_bf16_pair(lo_f32, hi_f32):
    lo = lax.bitcast_convert_type(lo_f32, U32)
    hi = lax.bitcast_convert_type(hi_f32, U32)
    return (lo >> 16) | (hi & U32(HI_MASK))


def _unpack_bf16_pair(w):
    lo = lax.bitcast_convert_type(w << 16, F32)
    hi = lax.bitcast_convert_type(w & U32(HI_MASK), F32)
    return lo, hi


def _inproj_kernel(x_ref, g_ref, wa_ref, wqk_ref, wvt_ref, wg_ref, bf_ref,
                   cb_ref, ucx_ref, qa_ref, ka_ref, vt_ref, sgc_ref, sga_ref,
                   carry_ref, *, tiles_per_batch, conv_dim, attn_dim, d_model, heads, tk):
    tm = x_ref.shape[0]
    t = lax.rem(pl.program_id(0), tiles_per_batch)
    x = x_ref[...]
    ms = jnp.mean(x * x, axis=-1, keepdims=True)
    h = ((x * lax.rsqrt(ms + RMS_EPS)) * g_ref[...]).astype(BF16)

    def mm(w_ref, c0, n):
        return jnp.dot(h, w_ref[:, c0:c0 + n], preferred_element_type=F32)

    vtf =lax.dot_general(wvt_ref[...], h, (((1,), (1,)), ((), ())), preferred_element_type=F32)
    qf = mm(wqk_ref, 0, attn_dim)
    kf = mm(wqk_ref, attn_dim, attn_dim)
    cb = mm(wa_ref, 0, conv_dim)
    ucx = mm(wa_ref, conv_dim, conv_dim) * mm(wa_ref, 2 * conv_dim, conv_dim)

    z = vtf[attn_dim:attn_dim + FORGET_ROWS] + bf_ref[...]
    sub = lax.broadcasted_iota(I32, z.shape, 0)
    logf = jnp.where(sub < heads, jnp.minimum(z, 0.0) - jnp.log1p(jnp.exp(-jnp.abs(z))), 0.0)
    p1 = logf.astype(BF16).astype(F32)
    r1 = logf - p1
    p2 = r1.astype(BF16).astype(F32)
    p3 = (r1 - p2).astype(BF16).astype(F32)
    packed = jnp.concatenate([p1, p2, p3], axis=0).astype(BF16)
    row = lax.broadcasted_iota(I32, (tm, tm), 0)
    col = lax.broadcasted_iota(I32, (tm, tm), 1)
    tri = jnp.where(row <= col, 1.0, 0.0).astype(BF16)
    r = jnp.dot(packed, tri, preferred_element_type=F32)
    local = r[:FORGET_ROWS] + r[FORGET_ROWS:2 * FORGET_ROWS] + r[2 * FORGET_ROWS:]

    @pl.when(t == 0)
    def _():
        carry_ref[...] = jnp.zeros_like(carry_ref)

    c = local + carry_ref[:, 0:1]
    carry_ref[...] = jnp.broadcast_to(c[:, tm - 1:tm], carry_ref.shape)

    c2 = c * LOG2E
    c_hi_t = c2.astype(BF16).astype(F32)
    c_r = c2 - c_hi_t
    c_mid_t = c_r.astype(BF16).astype(F32)
    c_lo_t = (c_r - c_mid_t).astype(BF16).astype(F32)
    split = (c_hi_t, c_mid_t, c_lo_t)
    pad = jnp.zeros((LANES - len(split) * FORGET_ROWS, tm), F32)
    parts = jnp.concatenate(split + (pad,), axis=0).T
    head_dim = attn_dim // heads
    ext = LANES - head_dim
    lane_e = lax.broadcasted_iota(I32, (tm, ext), 1)
    ones = jnp.where(lane_e < len(split), 1.0, 0.0)
    for hh in range(heads):
        kext = jnp.zeros((tm, ext), F32)
        for p in range(len(split)):
            col_p = parts[:, p * FORGET_ROWS + hh:p * FORGET_ROWS + hh + 1]
            kext = jnp.where(lane_e == p, -col_p, kext)
        sl = slice(hh * head_dim, (hh + 1) * head_dim)
        qa_ref[:, hh * LANES:(hh + 1) * LANES] = _as_words(jnp.concatenate([qf[:, sl], ones], axis=1).astype(BF16))
        ka_ref[:, hh * LANES:(hh + 1) * LANES] = _as_words(jnp.concatenate([kf[:, sl], kext], axis=1).astype(BF16))

    cb_ref[...] = _as_words(cb.astype(BF16))
    ucx_ref[...] = _as_words(ucx.astype(BF16))
    half = d_model // 2
    for c in range(2):
        sgc_ref[:, c * half:(c + 1) * half] = _as_words(jax.nn.sigmoid(mm(wg_ref, c * half, half)).astype(BF16))
        sga_ref[:, c * half:(c + 1) * half] = _as_words(
            jax.nn.sigmoid(mm(wg_ref, d_model + c * half, half)).astype(BF16))

    vt = vtf[:attn_dim].astype(BF16)
    ones_rows = jnp.ones((V_ONES_ROWS, tm), BF16)
    vt = jnp.concatenate([piece for hh in range(heads)
                          for piece in (vt[hh * head_dim:(hh + 1) * head_dim], ones_rows)], axis=0)
    for u in range(tm // tk):
        vt_ref[u] = _as_words(vt[:, u * tk:(u + 1) * tk])


def _inproj(x2d, g_mix, wa, wqk, wvt, wg, bf_col, *, batch, seq, heads, tm, tk):
    n, d_model = x2d.shape
    conv_dim = wa.shape[1] // 3
    attn_dim = wvt.shape[0] - FORGET_ROWS
    assert heads <= FORGET_ROWS
    tiles_per_batch = seq // tm
    grid = (n // tm,)
    full = lambda shape: pl.BlockSpec(shape, lambda i: (0,) * len(shape))
    rows = lambda w: pl.BlockSpec((tm, w), lambda i: (i, 0))
    kern = functools.partial(_inproj_kernel, tiles_per_batch=tiles_per_batch, conv_dim=conv_dim,
                             attn_dim=attn_dim, d_model=d_model, heads=heads, tk=tk)
    vt_rows = attn_dim + heads * V_ONES_ROWS
    words = lambda w: jax.ShapeDtypeStruct((n // 2, w), U32)
    wrows = lambda w: pl.BlockSpec((tm // 2, w), lambda i: (i, 0))
    out_shape = (
        words(conv_dim),
        words(conv_dim),
        words(heads * LANES),
        words(heads * LANES),
        jax.ShapeDtypeStruct((n // tk, vt_rows // 2, tk), U32),
        words(d_model),
        words(d_model),
    )
    out_specs = (
        wrows(conv_dim), wrows(conv_dim), wrows(heads * LANES), wrows(heads * LANES),
        pl.BlockSpec((tm // tk, vt_rows // 2, tk), lambda i: (i, 0, 0)),
        wrows(d_model), wrows(d_model),
    )
    return pl.pallas_call(
        kern,
        grid=grid,
        in_specs=[rows(d_model), full((1, d_model)), full(wa.shape), full(wqk.shape), full(wvt.shape),
                  full(wg.shape), full((FORGET_ROWS, 1))],
        out_specs=out_specs,
        out_shape=out_shape,
        scratch_shapes=[pltpu.VMEM((FORGET_ROWS, LANES), F32)],
        compiler_params=_params(("arbitrary",)),
        name="inproj",
    )(x2d, g_mix, wa, wqk, wvt, wg, bf_col)


def _attn_kernel(q_ref, k_ref, v_ref, o_ref, s_ref, *, tq, tk, head_dim, heads):
    qi = pl.program_id(1)
    row = lax.broadcasted_iota(I32, (tk, tq), 0)
    col = lax.broadcasted_iota(I32, (tk, tq), 1)

    def score_tile(j, slot):
        off = pl.multiple_of(j * (tk // 2), tk // 2)
        for hh in range(heads):
            qa = _as_bf16(q_ref[:, hh * LANES:(hh + 1) * LANES])
            ka = _as_bf16(k_ref[pl.ds(off, tk // 2), hh * LANES:(hh + 1) * LANES])
            s_ref[slot, hh] = lax.dot_general(ka, qa, (((1,), (1,)), ((), ())), preferred_element_type=F32)

    def block(j, slot, carry, masked):
        if not masked:
            score_tile(j + 1, 1 - slot)
        stats = []
        for hh in range(heads):
            m, _ = carry[hh]
            s = s_ref[slot, hh]
            if masked:
                s = jnp.where(row <= col, s, -jnp.inf)
            m_new = jnp.maximum(m, jnp.max(s, axis=0, keepdims=True))
            stats.append((m_new, jnp.exp2(m - m_new), jnp.exp2(s - m_new).astype(BF16)))
        out = []
        for hh in range(heads):
            m_new, alpha, p = stats[hh]
            vth = _as_bf16(v_ref[j, hh * (vrows // 2):(hh + 1) * (vrows // 2), :])
            out.append((m_new, alpha * carry[hh][1] + jnp.dot(vth, p, preferred_element_type=F32)))
        return tuple(out)

    vrows = head_dim + V_ONES_ROWS
    init = tuple((jnp.full((1, tq), -jnp.inf, F32), jnp.zeros((vrows, tq), F32)) for _ in range(heads))
    score_tile(0, 0)

    def pair(p, c):
        return block(2 * p + 1, 1, block(2 * p, 0, c, False), False)

    carry = lax.fori_loop(0, qi // 2, pair, init)
    final = lax.cond(
        lax.rem(qi, 2) == 0,
        lambda c: block(qi, 0, c, True),
        lambda c: block(qi, 1, block(qi - 1, 0, c, False), True),
        carry)
    per_group = LANES // head_dim
    for g in range(heads // per_group):
        accs = [final[g * per_group + u][1] for u in range(per_group)]
        ot = jnp.concatenate([a[:head_dim] / a[head_dim:head_dim + 1] for a in accs],
                             axis=0)
        o_ref[:, g * LANES:(g + 1) * LANES] = _as_words(ot.T.astype(BF16))


def _attention(qa, ka, vt, *, batch, seq, heads, tq):
    n = 2 * qa.shape[0]
    vt_rows, tk = 2 * vt.shape[1], vt.shape[2]
    head_dim = vt_rows // heads - V_ONES_ROWS
    attn_dim = heads * head_dim
    assert LANES % head_dim == 0 and tq == tk
    nq = seq // tq
    nk = seq // tk
    kern = functools.partial(_attn_kernel, tq=tq, tk=tk, head_dim=head_dim, heads=heads)
    return pl.pallas_call(
        kern,
        grid=(batch, nq),
        in_specs=[
            pl.BlockSpec((tq // 2, heads * LANES), lambda b, qi: (b * nq + qi, 0)),
            pl.BlockSpec((seq // 2, heads * LANES), lambda b, qi: (b, 0)),
            pl.BlockSpec((nk, vt_rows // 2, tk), lambda b, qi: (b, 0, 0)),
        ],
        out_specs=pl.BlockSpec((tq // 2, attn_dim), lambda b, qi: (b * nq + qi, 0)),
        out_shape=jax.ShapeDtypeStruct((n // 2, attn_dim), U32),
        scratch_shapes=[pltpu.VMEM((2, heads, tk, tq), F32)],
        compiler_params=_params(("arbitrary", "arbitrary")),
        name="attn",
    )(qa, ka, vt)


def _mix_kernel(x_ref, cb_ref, u_ref, uh_ref, o_ref, sgc_ref, sga_ref, cw_ref, wco_ref, wao_ref, wout_ref,
                gffn_ref, wr_ref, br_ref,
                x1_ref, h2p_ref, topi_ref, gate_ref, rank_ref, cnt_ref, carry_ref, lg_scr,
                *, tiles_per_batch, tiles_per_group, n_experts, nt):
    tm, d_model = x_ref.shape
    i = pl.program_id(0)

    @pl.when(i == 0)
    def _():
        lg_scr[...] = jnp.zeros_like(lg_scr)

    @pl.when((i == 0) | (lax.rem(i - 1, tiles_per_group) == 0))
    def _():
        carry_ref[...] = jnp.zeros_like(carry_ref)

    lg = lg_scr[...]
    sub = lax.broadcasted_iota(I32, (n_experts, tm), 0)

    vals, idxs = [], []
    for _ in range(TOP_K):
        m = jnp.max(lg, axis=0, keepdims=True)
        idx = jnp.min(jnp.where(lg == m, sub, n_experts), axis=0, keepdims=True)
        vals.append(m)
        idxs.append(idx)
        lg = jnp.where(sub == idx, -jnp.inf, lg)
    es = [jnp.exp(vk - vals[0]) for vk in vals]
    denom = es[0] + es[1] + es[2] + es[3]

    sub8 = lax.broadcasted_iota(I32, (SUBLANES, tm), 0)
    subl = lax.broadcasted_iota(I32, (LANES, tm), 0)
    chosen = jnp.zeros((n_experts, tm), F32)
    topi = jnp.zeros((SUBLANES, tm), I32)
    gates_t = jnp.zeros((LANES, tm), F32)
    for kk in range(TOP_K):
        chosen = jnp.where(sub == idxs[kk], 1.0, chosen)
        topi = jnp.where(sub8 == kk, idxs[kk], topi)
        gates_t = jnp.where(subl == kk, es[kk] / denom, gates_t)
    topi_ref[...] = topi
    gate_ref[...] = gates_t.T

    first = lax.rem(jnp.minimum(i, nt - 1), tiles_per_batch) == 0

    u = _as_bf16(u_ref[...]).astype(F32)
    halo = jnp.where(first, 0.0, _as_bf16(uh_ref[...]).astype(F32))
    ext = jnp.concatenate([halo, u], axis=0)
    hs = halo.shape[0]
    u1 = pltpu.roll(ext, 1, 0)[hs:]
    u2 = pltpu.roll(ext, 2, 0)[hs:]
    cw = cw_ref[...]
    conv = cw[0:1, :] * u2 + cw[1:2, :] * u1 + cw[2:3, :] * u
    yc = (_as_bf16(cb_ref[...]).astype(F32) * conv).astype(BF16)
    y_conv = jnp.dot(yc, wco_ref[...], preferred_element_type=F32)
    y_attn = jnp.dot(_as_bf16(o_ref[...]), wao_ref[...], preferred_element_type=F32)
    mixed = (_as_bf16(sgc_ref[...]).astype(F32) * y_conv
             + _as_bf16(sga_ref[...]).astype(F32) * y_attn).astype(BF16)
    x1 = x_ref[...] + jnp.dot(mixed, wout_ref[...], preferred_element_type=F32)
    x1_ref[...] = x1

    ms = jnp.mean(x1 * x1, axis=-1, keepdims=True)
    h2 = (x1 * lax.rsqrt(ms + RMS_EPS)) * gffn_ref[...]
    hb = h2.astype(BF16)
    hbf = hb.astype(F32)
    half = d_model // 2
    h2p_ref[...] = _pack_bf16_pair(hbf[:, :half], hbf[:, half:])

    hlo = (h2 - hbf).astype(BF16)
    nt_dims = (((1,), (1,)), ((), ()))
    rt = (lax.dot_general(wr_ref[...], hb, nt_dims, preferred_element_type=F32)
          + lax.dot_general(wr_ref[...], hlo, nt_dims, preferred_element_type=F32))
    lg_scr[...] = rt[:n_experts] + rt[n_experts:] + br_ref[...]

    row = lax.broadcasted_iota(I32, (tm, tm), 0)
    col = lax.broadcasted_iota(I32, (tm, tm), 1)
    tri = jnp.where(row < col, 1.0, 0.0).astype(BF16)
    before = jnp.dot(chosen.astype(BF16), tri, preferred_element_type=F32) + carry_ref[:, 0:1]
    rank = jnp.zeros((SUBLANES, tm), F32)
    for kk in range(TOP_K):
        rk = jnp.sum(jnp.where(sub == idxs[kk], before, 0.0), axis=0, keepdims=True)
        rank = jnp.where(sub8 == kk, rk, rank)
    rank_ref[...] = rank.astype(I32)
    total = carry_ref[...] + jnp.sum(chosen, axis=1, keepdims=True)
    carry_ref[...] = total
    cnt_ref[0] = total


def _mix_kernel_into(x1_prior_ref, *refs, **static):
    del x1_prior_ref
    _mix_kernel(*refs, **static)


def _mix(x2d, cb, ucx, o, sgc, sga, conv_w, wco, wao, wout, g_ffn, wr, br, x1_prior=None, *, seq, n_experts, tm,
         tile0, nt, tiles_per_group):
    n_total, d_model = x2d.shape
    n = nt * tm
    conv_dim = cb.shape[1]
    attn_dim = o.shape[1]
    tiles_per_batch = seq // tm
    assert tile0 % tiles_per_batch == 0
    hb = tm // BF16_SUBLANES
    full = lambda shape: pl.BlockSpec(shape, lambda i: (0,) * len(shape))
    cur = lambda i: jnp.minimum(i, nt - 1)
    prev = lambda i: jnp.maximum(i - 1, 0)
    rows_in = lambda w: pl.BlockSpec((tm, w), lambda i: (cur(i) + tile0, 0))
    words_in = lambda w: pl.BlockSpec((tm // 2, w), lambda i: (cur(i) + tile0, 0))
    rows = lambda w: pl.BlockSpec((tm, w), lambda i: (cur(i), 0))
    prior = () if x1_prior is None else (x1_prior,)
    kern = functools.partial(_mix_kernel_into if prior else _mix_kernel, tiles_per_batch=tiles_per_batch,
                             tiles_per_group=tiles_per_group, n_experts=n_experts, nt=nt)
    out_shape = (
        jax.ShapeDtypeStruct((n_total, d_model), F32),
        jax.ShapeDtypeStruct((n, d_model // 2), U32),
        jax.ShapeDtypeStruct((SUBLANES, n), I32),
        jax.ShapeDtypeStruct((n, LANES), F32),
        jax.ShapeDtypeStruct((SUBLANES, n), I32),
        jax.ShapeDtypeStruct((nt, n_experts, LANES), F32),
    )
    cols = pl.BlockSpec((SUBLANES, tm), lambda i: (0, prev(i)))
    out_specs = (rows_in(d_model), rows(d_model // 2), cols, pl.BlockSpec((tm, LANES), lambda i: (prev(i), 0)), cols,
                 pl.BlockSpec((1, n_experts, LANES), lambda i: (prev(i), 0, 0)))
    return pl.pallas_call(
        kern,
        grid=(nt + 1,),
        in_specs=[pl.BlockSpec(memory_space=pl.ANY)] * len(prior) +
                 [rows_in(d_model), words_in(conv_dim), words_in(conv_dim),
                  pl.BlockSpec((BF16_SUBLANES // 2, conv_dim),
                               lambda i: (jnp.maximum((cur(i) + tile0) * hb - 1, 0), 0)),
                  words_in(attn_dim), words_in(d_model), words_in(d_model),
                  full(conv_w.shape), full(wco.shape), full(wao.shape), full(wout.shape),
                  full((1, d_model)), full(wr.shape), full((n_experts, 1))],
        out_specs=out_specs,
        out_shape=out_shape,
        scratch_shapes=[pltpu.VMEM((n_experts, LANES), F32), pltpu.VMEM((n_experts, tm), F32)],
        input_output_aliases={0: 0} if prior else {},
        compiler_params=_params(("arbitrary",)),
        name="mix",
    )(*prior, x2d, cb, ucx, ucx, o, sgc, sga, conv_w, wco, wao, wout, g_ffn, wr, br)


SC_CORES = 2
SC_SUBCORES = 16
SC_WORKERS = SC_CORES * SC_SUBCORES
SC_CHUNK = 64


def _sc_mesh():
    return plsc.VectorSubcoreMesh(core_axis_name="c", subcore_axis_name="s",
                                  num_cores=SC_CORES, num_subcores=SC_SUBCORES)


def _sc_worker():
    return lax.axis_index("s") * SC_CORES + lax.axis_index("c")


def _sc_scatter_rows(src, dest4, pad_idx, n_out, row0):
    width = src.shape[1]
    nchunks = dest4.shape[1]
    npad = pad_idx.shape[1]

    @functools.partial(
        pl.kernel, mesh=_sc_mesh(),
        out_type=jax.ShapeDtypeStruct((n_out, width), src.dtype),
        scratch_types=[pltpu.VMEM((TOP_K, SC_CHUNK), I32), pltpu.VMEM((SC_CHUNK, width), src.dtype),
                       pltpu.VMEM((npad, SC_CHUNK), I32), pltpu.SemaphoreType.DMA],
        name="sc_scatter_rows",
    )
    def k(src_hbm, dest_hbm, pad_hbm, out_hbm, idx_v, rows_v, pad_v, sem):
        wid = _sc_worker()
        base = row0 + wid * (nchunks * SC_CHUNK)

        @pl.loop(0, nchunks)
        def _(j):
            pltpu.sync_copy(src_hbm.at[pl.ds(base + j * SC_CHUNK, SC_CHUNK)], rows_v)
            pltpu.sync_copy(dest_hbm.at[wid, j], idx_v)
            copies = [pltpu.async_copy(rows_v, out_hbm.at[idx_v.at[kk]], sem) for kk in range(TOP_K)]
            for cp in copies:
                cp.wait()

        pltpu.sync_copy(pad_hbm.at[wid], pad_v)
        fills = [pltpu.async_copy(rows_v, out_hbm.at[pad_v.at[p]], sem) for p in range(npad)]
        for cp in fills:
            cp.wait()

    return k(src, dest4, pad_idx)


def _sc_gather_rows(src, dest4, n):
    width = src.shape[1]
    nchunks = dest4.shape[1]

    @functools.partial(
        pl.kernel, mesh=_sc_mesh(),
        out_type=jax.ShapeDtypeStruct((TOP_K, n, width), src.dtype),
        scratch_types=[pltpu.VMEM((TOP_K, SC_CHUNK), I32), pltpu.VMEM((SC_CHUNK, width), src.dtype),
                       pltpu.VMEM((SC_CHUNK, width), src.dtype), pltpu.SemaphoreType.DMA, pltpu.SemaphoreType.DMA],
        name="sc_gather_rows",
    )
    def k(src_hbm, dest_hbm, out_hbm, idx_v, rows_a, rows_b, sem_a, sem_b):
        wid = _sc_worker()
        base = wid * (nchunks * SC_CHUNK)
        bufs = ((rows_a, sem_a), (rows_b, sem_b))

        @pl.loop(0, nchunks)
        def _(j):
            pltpu.sync_copy(dest_hbm.at[wid, j], idx_v)
            pending = pltpu.async_copy(src_hbm.at[idx_v.at[0]], rows_a, sem_a)
            for kk in range(TOP_K):
                buf, _ = bufs[kk % 2]
                pending.wait()
                if kk + 1 < TOP_K:
                    nbuf, nsem = bufs[(kk + 1) % 2]
                    pending = pltpu.async_copy(src_hbm.at[idx_v.at[kk + 1]], nbuf, nsem)
                pltpu.sync_copy(buf, out_hbm.at[kk, pl.ds(base + j * SC_CHUNK, SC_CHUNK)])

    return k(src, dest4)


def _expert_kernel(be_ref, nu_ref, slot_ref, nxt_ref, nv_ref, xs_ref, wgu_hbm, bgu_ref, wd_hbm, bd_ref, ys_ref,
                   wgu_f32, wd_f32, wgu_bf, wd_bf, sem, *, d_ff, ff_chunk):
    i = pl.program_id(0)

    def weight_copies(e, s):
        return (pltpu.make_async_copy(wgu_hbm.at[e], wgu_f32.at[s], sem.at[s]),
                pltpu.make_async_copy(wd_hbm.at[e], wd_f32.at[s], sem.at[s]))

    @pl.when(i == 0)
    def _():
        for cp in weight_copies(be_ref[0], 0):
            cp.start()

    @pl.when((i < nu_ref[0]) & ((i == 0) | (be_ref[i] != be_ref[jnp.maximum(i - 1, 0)])))
    def _():
        s = slot_ref[i]
        for cp in weight_copies(be_ref[i], s):
            cp.wait()
        wgu_bf[...] = wgu_f32[s].astype(BF16)
        wd_bf[...] = wd_f32[s].astype(BF16)
        nxt = nxt_ref[i]

        @pl.when(nxt >= 0)
        def _():
            for cp in weight_copies(nxt, 1 - s):
                cp.start()

    def mlp_rows(rows):
        lo, hi = _unpack_bf16_pair(xs_ref[0:rows, :])
        xb = jnp.concatenate([lo.astype(BF16), hi.astype(BF16)], axis=1)
        chunks = range(0, d_ff, ff_chunk)
        gus = []
        for c0 in chunks:
            glu = jnp.dot(xb, wgu_bf[:, c0:c0 + ff_chunk], preferred_element_type=F32) + bgu_ref[0, :, c0:c0 + ff_chunk]
            lin = (jnp.dot(xb, wgu_bf[:, d_ff + c0:d_ff + c0 + ff_chunk], preferred_element_type=F32)
                   + bgu_ref[0, :, d_ff + c0:d_ff + c0 + ff_chunk])
            gus.append((glu, lin))
        y = bd_ref[0]
        for c0, (glu, lin) in zip(chunks, gus):
            glu = jnp.minimum(glu, SWIGLU_LIMIT)
            lin = jnp.clip(lin, -SWIGLU_LIMIT, SWIGLU_LIMIT)
            act = (glu * jax.nn.sigmoid(SWIGLU_ALPHA * glu)) * (lin + 1.0)
            y = y + jnp.dot(act.astype(BF16), wd_bf[c0:c0 + ff_chunk, :], preferred_element_type=F32)
        yb = y.astype(BF16).astype(F32)
        half = y.shape[1] // 2
        ys_ref[0:rows, :] = _pack_bf16_pair(yb[:, :half], yb[:, half:])

    tm = xs_ref.shape[0]
    used = i < nu_ref[0]
    part = tm // EXPERT_ROW_PARTS
    parts_needed = jnp.maximum((nv_ref[i] + part - 1) // part, 1)
    for k in range(1, EXPERT_ROW_PARTS + 1):
        @pl.when(used & (parts_needed == k))
        def _(k=k):
            mlp_rows(k * part)
            if k < EXPERT_ROW_PARTS:
                ys_ref[k * part:, :] = jnp.zeros((tm - k * part, ys_ref.shape[1]), U32)


def _experts(xs, block_e, n_used, slot, next_e, n_valid, wgu, bgu, wd, bd, *, tm, n_blocks):
    half = xs.shape[1]
    p = n_blocks * tm
    e, d_model, two_ff = wgu.shape
    d_ff = two_ff // 2
    kern = functools.partial(_expert_kernel, d_ff=d_ff, ff_chunk=min(256, d_ff))
    blk = lambda i, be, nu, sl, nx, nv: (jnp.minimum(i, nu[0] - 1), 0)
    bias = lambda i, be, nu, sl, nx, nv: (be[i], 0, 0)
    grid_spec = pltpu.PrefetchScalarGridSpec(
        num_scalar_prefetch=5,
        grid=(n_blocks,),
        in_specs=[
            pl.BlockSpec((tm, half), blk),
            pl.BlockSpec(memory_space=pl.ANY),
            pl.BlockSpec((1, 1, two_ff), bias),
            pl.BlockSpec(memory_space=pl.ANY),
            pl.BlockSpec((1, 1, d_model), bias),
        ],
        out_specs=pl.BlockSpec((tm, half), blk),
        scratch_shapes=[pltpu.VMEM((2, d_model, two_ff), F32), pltpu.VMEM((2, d_ff, d_model), F32),
                        pltpu.VMEM((d_model, two_ff), BF16), pltpu.VMEM((d_ff, d_model), BF16),
                        pltpu.SemaphoreType.DMA((2,))],
    )
    return pl.pallas_call(
        kern,
        grid_spec=grid_spec,
        out_shape=jax.ShapeDtypeStruct((p, half), U32),
        compiler_params=_params(("arbitrary",)),
        name="experts",
    )(block_e, n_used, slot, next_e, n_valid, xs, wgu, bgu.reshape(e, 1, two_ff), wd, bd.reshape(e, 1, d_model))


def _final_kernel(yk_ref, gate_ref, x1_ref, g_ref, out_ref):
    tm, d_model = x1_ref.shape
    half = d_model // 2
    gates = gate_ref[...]
    acc_lo = jnp.zeros((tm, half), F32)
    acc_hi = jnp.zeros((tm, half), F32)
    for kk in range(TOP_K):
        lo, hi = _unpack_bf16_pair(yk_ref[kk])
        gk = gates[:, kk:kk + 1]
        acc_lo = acc_lo + gk * lo
        acc_hi = acc_hi + gk * hi
    x2 = x1_ref[...] + jnp.concatenate([acc_lo, acc_hi], axis=1)
    ms = jnp.mean(x2 * x2, axis=-1, keepdims=True)
    out_ref[...] = (x2 * lax.rsqrt(ms + RMS_EPS)) * g_ref[...]


def _final(yk, gates, x1, g_final, *, tm, tile0):
    n, d_model = x1.shape
    rows = lambda w: pl.BlockSpec((tm, w), lambda i: (i + tile0, 0))
    return pl.pallas_call(
        _final_kernel,
        grid=(yk.shape[1] // tm,),
        in_specs=[pl.BlockSpec((TOP_K, tm, d_model // 2), lambda i: (0, i, 0)),
                  pl.BlockSpec((tm, LANES), lambda i: (i, 0)), rows(d_model),
                  pl.BlockSpec((1, d_model), lambda i: (0, 0))],
        out_specs=rows(d_model),
        out_shape=jax.ShapeDtypeStruct((n, d_model), F32),
        input_output_aliases={2: 0},
        compiler_params=_params(("arbitrary",)),
        name="final",
    )(yk, gates, x1, g_final)


def _tiles(seq):
    tm = min(512, seq)
    return dict(tm_proj=tm, t_attn=min(256, seq), tm_mix=tm, tm_expert=512, tm_final=tm, moe_groups=2)


def _forward(x, g_mix, w_in, conv_w, b_f, w_conv_o, w_attn_o, w_out, g_ffn, w_router, b_router,
             w_gate_up, b_gate_up, w_down, b_down, g_final, tiles):
    batch, seq, d_model = x.shape
    n = batch * seq
    conv_dim = conv_w.shape[1]
    attn_dim = w_attn_o.shape[0]
    heads = b_f.shape[0]
    head_dim = attn_dim // heads
    n_experts = w_router.shape[1]
    x2d = x.reshape(n, d_model)

    c0 = 3 * conv_dim
    a0 = c0 + 3 * attn_dim
    wa = w_in[:, :c0].astype(BF16)
    scale = LOG2E / (head_dim ** 0.5)
    wqk = jnp.concatenate([w_in[:, c0:c0 + attn_dim] * scale, w_in[:, c0 + attn_dim:c0 + 2 * attn_dim]],
                          axis=1).astype(BF16)
    wvt = jnp.pad(w_in[:, c0 + 2 * attn_dim:a0 + heads].T, ((0, FORGET_ROWS - heads), (0, 0))).astype(BF16)
    wg = w_in[:, a0 + heads:].astype(BF16)
    bf_col = jnp.pad(b_f, (0, FORGET_ROWS - heads)).reshape(FORGET_ROWS, 1)
    wr_hi = w_router.astype(BF16)
    wr_lo = (w_router - wr_hi.astype(F32)).astype(BF16)
    wr = jnp.concatenate([wr_hi, wr_lo], axis=1).T
    br = b_router.reshape(n_experts, 1)

    cb, ucx, qa, ka, vt, sgc, sga = _inproj(
        x2d, g_mix.reshape(1, d_model), wa, wqk, wvt, wg, bf_col,
        batch=batch, seq=seq, heads=heads, tm=tiles["tm_proj"], tk=tiles["t_attn"])
    o = _attention(qa, ka, vt, batch=batch, seq=seq, heads=heads, tq=tiles["t_attn"])
    groups = tiles["moe_groups"]
    ng = n // groups
    mix_tiles = ng // tiles["tm_mix"]
    wco, wao, wout = w_conv_o.astype(BF16), w_attn_o.astype(BF16), w_out.astype(BF16)

    tme = tiles["tm_expert"]
    n_blocks = (ng * TOP_K) // tme + n_experts
    p_rows = n_blocks * tme
    nchunks = ng // (SC_WORKERS * SC_CHUNK)
    jj = jnp.arange(tme, dtype=I32)[None, :]
    spare = p_rows + jnp.arange(n_experts, dtype=I32)[:, None] * tme + jj
    expert_ids = jnp.arange(n_experts, dtype=I32)[None, None, :]
    ys_groups, dest_groups, gate_groups = [], [], []
    x1 = None
    for g in range(groups):
        x1, h2p, topi, gates, rank, cnt = _mix(
            x2d, cb, ucx, o, sgc, sga, conv_w, wco, wao, wout, g_ffn.reshape(1, d_model), wr, br, x1,
            seq=seq, n_experts=n_experts, tm=tiles["tm_mix"], tile0=g * mix_tiles, nt=mix_tiles,
            tiles_per_group=mix_tiles)
        counts = cnt[mix_tiles - 1, :, 0].astype(I32)
        padded = ((counts + tme - 1) // tme) * tme
        pend = jnp.cumsum(padded)
        pstart = pend - padded
        n_used = (pend[-1] // tme).astype(I32).reshape(1)
        blk_start = jnp.minimum(jnp.arange(n_blocks, dtype=I32), n_used[0] - 1) * tme
        block_e = jnp.minimum(jnp.sum((pend[None, :] <= blk_start[:, None]).astype(I32), axis=1), n_experts - 1)
        onehot = topi[:TOP_K, :, None] == expert_ids
        dest = jnp.sum(jnp.where(onehot, pstart[None, None, :], 0), axis=2) + rank[:TOP_K]
        dest4 = dest.reshape(TOP_K, SC_WORKERS, nchunks, SC_CHUNK).transpose(1, 2, 0, 3)
        pad_idx = jnp.where(jj < (padded - counts)[:, None], (pstart + counts)[:, None] + jj, spare)
        pad_idx = pad_idx.reshape(SC_WORKERS, (n_experts * tme) // (SC_WORKERS * SC_CHUNK), SC_CHUNK)
        xs = _sc_scatter_rows(h2p, dest4, pad_idx, p_rows + n_experts * tme, 0)
        eid = jnp.arange(n_experts, dtype=I32)
        active = padded > 0
        run_tbl = jnp.sum((eid[None, :] < eid[:, None]) & active[None, :], axis=1)
        later = (eid[None, :] > eid[:, None]) & active[None, :]
        next_tbl = jnp.min(jnp.where(later, eid[None, :], n_experts), axis=1)
        next_tbl = jnp.where(next_tbl < n_experts, next_tbl, -1)
        is_e = block_e[:, None] == eid[None, :]
        slot = lax.rem(jnp.sum(jnp.where(is_e, run_tbl[None, :], 0), axis=1), 2).astype(I32)
        next_e = jnp.sum(jnp.where(is_e, next_tbl[None, :], 0), axis=1).astype(I32)
        row_in_e = jnp.arange(n_blocks, dtype=I32) * tme - jnp.sum(jnp.where(is_e, pstart[None, :], 0), axis=1)
        n_valid = jnp.clip(jnp.sum(jnp.where(is_e, counts[None, :], 0), axis=1) - row_in_e, 0, tme).astype(I32)
        ys_groups.append(_experts(xs, block_e, n_used, slot, next_e, n_valid, w_gate_up, b_gate_up, w_down,
                                  b_down, tm=tme, n_blocks=n_blocks))
        dest_groups.append(dest4)
        gate_groups.append(gates)

    out = x1
    for g in range(groups):
        yk = _sc_gather_rows(ys_groups[g], dest_groups[g], ng)
        out = _final(yk, gate_groups[g], out, g_final.reshape(1, d_model), tm=tiles["tm_final"],
                     tile0=g * (ng // tiles["tm_final"]))
    return out.reshape(batch, seq, d_model)


def kernel(x, g_mix, w_in, conv_w, b_f, w_conv_o, w_attn_o, w_out, g_ffn, w_router, b_router, w_gate_up,
           b_gate_up, w_down, b_down, g_final):
    return _forward(x, g_mix, w_in, conv_w, b_f, w_conv_o, w_attn_o, w_out, g_ffn, w_router, b_router,
                    w_gate_up, b_gate_up, w_down, b_down, g_final, _tiles(x.shape[1]))
```
